```python
import math
import jax, jax.numpy as jnp
from jax import lax
import numpy as np

D_MODEL = 1024
BATCH = 32
SEQ = 2048
DEPTH = 4

HEAD_DIM = 64
D_MIX = D_MODEL
N_GROUPS = 4
GROUP_WIDTH = D_MIX // N_GROUPS
HEADS_PER_GROUP = GROUP_WIDTH // HEAD_DIM
Q_BLOCK = 128
D_FF = 4 * D_MODEL
EPS = 1e-6
NEG_INF = -1e30

SB_HEADS = HEADS_PER_GROUP
NSA_HEADS = HEADS_PER_GROUP
NSA_KV_DIM = HEAD_DIM
CMP_LEN = 32
CMP_STRIDE = 16
CMP_HIDDEN = 2 * HEAD_DIM
SEL_LEN = 64
SEL_TOPN = 8
SEL_Q_BLOCK = 64
FORCE_SCORE = 1e9
NSA_WINDOW = 512
N_NSA_BRANCHES = 3
DIFF_HEADS = HEADS_PER_GROUP
DIFF_QK_DIM = HEAD_DIM // 2
DIFF_V_DIM = HEAD_DIM
SWA_HEADS = HEADS_PER_GROUP
SWA_KV_HEADS = 2
SWA_WINDOW = 128
N_ALIBI_HEADS = NSA_HEADS + DIFF_HEADS + SWA_HEADS

IN_SIZES = (
    SB_HEADS * HEAD_DIM, SB_HEADS * HEAD_DIM, SB_HEADS * HEAD_DIM,
    NSA_HEADS * HEAD_DIM, NSA_KV_DIM, NSA_KV_DIM, NSA_KV_DIM, NSA_KV_DIM, NSA_KV_DIM, NSA_KV_DIM,
    N_NSA_BRANCHES * NSA_HEADS,
    DIFF_HEADS * 2 * DIFF_QK_DIM, DIFF_HEADS * 2 * DIFF_QK_DIM, DIFF_HEADS * DIFF_V_DIM,
    SWA_HEADS * HEAD_DIM, SWA_KV_HEADS * HEAD_DIM, SWA_KV_HEADS * HEAD_DIM,
)
IN_COLS = sum(IN_SIZES)

kernel_name = "hybrid_parallel_heads_sb_nsa_diff_swa"


def _rmsnorm(x, gain=None):
    xf = x.astype(jnp.float32)
    y = xf * lax.rsqrt(jnp.mean(xf * xf, axis=-1, keepdims=True) + EPS)
    if gain is not None:
        y = y * gain.astype(jnp.float32)
    return y.astype(x.dtype)


def _alibi_slopes():
    m = 2.0 ** (-8.0 * np.arange(1, N_ALIBI_HEADS + 1) / N_ALIBI_HEADS)
    m = m.astype(np.float32).reshape(HEADS_PER_GROUP, 3)
    return jnp.asarray(m[:, 0]), jnp.asarray(m[:, 1]), jnp.asarray(m[:, 2])


def _sweep(block_fn, n_blocks):
    out = lax.map(block_fn, jnp.arange(n_blocks))
    nb, b, qb = out.shape[:3]
    return jnp.moveaxis(out, 0, 1).reshape((b, nb * qb) + out.shape[3:])


def _stick_breaking(q, k, v):
    B, S, H, d = q.shape
    scale = d ** -0.5
    kpos = jnp.arange(S)

    def block(i):
        start = i * Q_BLOCK
        qb = lax.dynamic_slice_in_dim(q, start, Q_BLOCK, axis=1)
        qpos = start + jnp.arange(Q_BLOCK)
        past = kpos[None, :] < qpos[:, None]
        z = jnp.einsum('bqhd,bkhd->bhqk', qb, k).astype(jnp.float32) * scale
        log_beta = jax.nn.log_sigmoid(z)
        log_keep = jnp.where(past, jax.nn.log_sigmoid(-z), 0.0)
        acc = lax.cumsum(log_keep, axis=3, reverse=True) - log_keep
        a = jnp.where(past, jnp.exp(log_beta + acc), 0.0)
        return jnp.einsum('bhqk,bkhd->bqhd', a.astype(v.dtype), v)

    return _sweep(block, S // Q_BLOCK)


def _banded_attention(q, k, v, slopes, window, sinks=None):
    B, S, H, d = q.shape
    kvh = k.shape[2]
    grp = H // kvh
    n_prev = -(-window // Q_BLOCK)
    pad = n_prev * Q_BLOCK
    band = pad + Q_BLOCK
    kp = jnp.pad(k, ((0, 0), (pad, 0), (0, 0), (0, 0)))
    vp = jnp.pad(v, ((0, 0), (pad, 0), (0, 0), (0, 0)))
    scale = d ** -0.5
    sl = slopes.reshape(kvh, grp)[None, :, :, None, None]

    def block(i):
        start = i * Q_BLOCK
        qb = lax.dynamic_slice_in_dim(q, start, Q_BLOCK, axis=1).reshape(B, Q_BLOCK, kvh, grp, d)
        kb = lax.dynamic_slice_in_dim(kp, start, band, axis=1)
        vb = lax.dynamic_slice_in_dim(vp, start, band, axis=1)
        qpos = start + jnp.arange(Q_BLOCK)
        kpos = start - pad + jnp.arange(band)
        dist = qpos[:, None] - kpos[None, :]
        mask = (dist >= 0) & (dist < window) & (kpos >= 0)[None, :]
        z = jnp.einsum('bqgrd,bkgd->bgrqk', qb, kb).astype(jnp.float32) * scale
        z = z - sl * dist.astype(jnp.float32)
        z = jnp.where(mask, z, NEG_INF)
        if sinks is not None:
            sk = jnp.broadcast_to(sinks.astype(jnp.float32).reshape(kvh, grp)[None, :, :, None, None],
                                  z.shape[:-1] + (1,))
            p = jax.nn.softmax(jnp.concatenate([z, sk], axis=-1), axis=-1)[..., :-1]
        else:
            p = jax.nn.softmax(z, axis=-1)
        o = jnp.einsum('bgrqk,bkgd->bqgrd', p.astype(vb.dtype), vb)
        return o.reshape(B, Q_BLOCK, H, d)

    return _sweep(block, S // Q_BLOCK)


def _native_sparse_attention(q, k_cmp, v_cmp, k_sel, v_sel, k_win, v_win, gate_logits,
                             pe_k, w1_k, w2_k, pe_v, w1_v, w2_v, slopes):
    B, S, H, d = q.shape
    scale = d ** -0.5
    tpos = jnp.arange(S)

    n_cmp = (S - CMP_LEN) // CMP_STRIDE + 1
    starts = CMP_STRIDE * np.arange(n_cmp)
    idx = starts[:, None] + np.arange(CMP_LEN)[None, :]

    def compress(t, pe, w1, w2):
        blk = (t[:, idx] + pe).reshape(B, n_cmp, CMP_LEN * t.shape[-1])
        return jax.nn.gelu(blk @ w1) @ w2

    kc = compress(k_cmp, pe_k, w1_k, w2_k)
    vc = compress(v_cmp, pe_v, w1_v, w2_v)
    dist_c = (tpos[:, None] - jnp.asarray(starts + CMP_LEN - 1)[None, :]).astype(jnp.float32)
    valid_c = dist_c >= 0
    z = jnp.einsum('bthd,bid->bhti', q, kc).astype(jnp.float32) * scale - slopes[:, None, None] * dist_c
    z = jnp.where(valid_c, z, NEG_INF)
    p_c = jax.nn.softmax(z, axis=-1) * valid_c
    o_cmp = jnp.einsum('bhti,bid->bthd', p_c.astype(vc.dtype), vc)

    n_sel = S // SEL_LEN
    n_top = min(SEL_TOPN, n_sel)
    to_sel = jax.nn.one_hot(starts // SEL_LEN, n_sel, dtype=jnp.float32)
    imp = jnp.einsum('bhti,ij->btj', p_c, to_sel)
    blk_id = jnp.arange(n_sel)[None, :]
    cur = (tpos // SEL_LEN)[:, None]
    forced = (blk_id == 0) | (blk_id == cur) | (blk_id == cur - 1)
    score = jnp.where(blk_id > cur, NEG_INF, jnp.where(forced, FORCE_SCORE, imp))
    _, sel = lax.top_k(score, n_top)
    ks_blk = k_sel.reshape(B, n_sel, SEL_LEN, d)
    vs_blk = v_sel.reshape(B, n_sel, SEL_LEN, d)
    sl5 = slopes[None, :, None, None, None]

    def sel_block(i):
        start = i * SEL_Q_BLOCK
        qb = lax.dynamic_slice_in_dim(q, start, SEL_Q_BLOCK, axis=1)
        sb = lax.dynamic_slice_in_dim(sel, start, SEL_Q_BLOCK, axis=1)
        kg = jax.vmap(lambda blk, s: blk[s])(ks_blk, sb)
        vg = jax.vmap(lambda blk, s: blk[s])(vs_blk, sb)
        qpos = start + jnp.arange(SEL_Q_BLOCK)
        kpos = sb[..., None] * SEL_LEN + jnp.arange(SEL_LEN)
        dist = (qpos[None, :, None, None] - kpos).astype(jnp.float32)[:, None]
        zs = jnp.einsum('bqhd,bqnld->bhqnl', qb, kg).astype(jnp.float32) * scale - sl5 * dist
        zs = jnp.where(dist >= 0, zs, NEG_INF)
        p = jax.nn.softmax(zs.reshape(B, H, SEL_Q_BLOCK, -1), axis=-1).reshape(zs.shape)
        return jnp.einsum('bhqnl,bqnld->bqhd', p.astype(vg.dtype), vg)

    o_sel = _sweep(sel_block, S // SEL_Q_BLOCK)

    o_win = _banded_attention(q, k_win[:, :, None], v_win[:, :, None], slopes, NSA_WINDOW)

    g = jax.nn.sigmoid(gate_logits.astype(jnp.float32))
    return (g[..., 0:1] * o_cmp + g[..., 1:2] * o_sel + g[..., 2:3] * o_win).astype(q.dtype)


def _differential_attention(q, k, v, slopes, lam):
    B, S, H, _, dqk = q.shape
    scale = dqk ** -0.5
    kpos = jnp.arange(S)
    sl = slopes[None, :, None, None, None]

    def block(i):
        start = i * Q_BLOCK
        qb = lax.dynamic_slice_in_dim(q, start, Q_BLOCK, axis=1)
        qpos = start + jnp.arange(Q_BLOCK)
        dist = (qpos[:, None] - kpos[None, :]).astype(jnp.float32)
        z = jnp.einsum('bqhcd,bkhcd->bhcqk', qb, k).astype(jnp.float32) * scale - sl * dist
        z = jnp.where(dist >= 0, z, NEG_INF)
        p = jax.nn.softmax(z, axis=-1)
        w = p[:, :, 0] - lam * p[:, :, 1]
        return jnp.einsum('bhqk,bkhd->bqhd', w.astype(v.dtype), v)

    return _sweep(block, S // Q_BLOCK)


def setup_inputs(seed: int = 0) -> dict:
    key = jax.random.key(seed)
    ks = jax.random.split(key, 24)
    f32 = jnp.float32

    def nrm(k, shape, scale):
        return jax.random.normal(k, shape, f32) * scale

    def gain(k, shape):
        return 1.0 + 0.01 * jax.random.normal(k, shape, f32)

    flat = CMP_LEN * NSA_KV_DIM
    return {
        "x": nrm(ks[0], (BATCH, SEQ, D_MODEL), 1.0),
        "norm_attn": gain(ks[1], (DEPTH, D_MODEL)),
        "w_in": nrm(ks[2], (DEPTH, D_MODEL, IN_COLS), D_MODEL ** -0.5),
        "cmp_pe_k": nrm(ks[3], (DEPTH, CMP_LEN, NSA_KV_DIM), 0.1),
        "cmp_w1_k": nrm(ks[4], (DEPTH, flat, CMP_HIDDEN), flat ** -0.5),
        "cmp_w2_k": nrm(ks[5], (DEPTH, CMP_HIDDEN, NSA_KV_DIM), CMP_HIDDEN ** -0.5),
        "cmp_pe_v": nrm(ks[6], (DEPTH, CMP_LEN, NSA_KV_DIM), 0.1),
        "cmp_w1_v": nrm(ks[7], (DEPTH, flat, CMP_HIDDEN), flat ** -0.5),
        "cmp_w2_v": nrm(ks[8], (DEPTH, CMP_HIDDEN, NSA_KV_DIM), CMP_HIDDEN ** -0.5),
        "diff_lq1": nrm(ks[9], (DEPTH, DIFF_QK_DIM), 0.1),
        "diff_lk1": nrm(ks[10], (DEPTH, DIFF_QK_DIM), 0.1),
        "diff_lq2": nrm(ks[11], (DEPTH, DIFF_QK_DIM), 0.1),
        "diff_lk2": nrm(ks[12], (DEPTH, DIFF_QK_DIM), 0.1),
        "sinks": nrm(ks[13], (DEPTH, SWA_HEADS), 0.5),
        "g_mix": gain(ks[14], (DEPTH, D_MIX)),
        "w_out": nrm(ks[15], (DEPTH, D_MIX, D_MODEL), D_MIX ** -0.5),
        "norm_mlp": gain(ks[16], (DEPTH, D_MODEL)),
        "w_up": nrm(ks[17], (DEPTH, D_MODEL, D_FF), D_MODEL ** -0.5),
        "w_down": nrm(ks[18], (DEPTH, D_FF, D_MODEL), 0.5 * D_FF ** -0.5),
        "norm_final": gain(ks[19], (D_MODEL,)),
    }


def reference(x, norm_attn, w_in, cmp_pe_k, cmp_w1_k, cmp_w2_k, cmp_pe_v, cmp_w1_v, cmp_w2_v,
              diff_lq1, diff_lk1, diff_lq2, diff_lk2, sinks, g_mix, w_out, norm_mlp, w_up, w_down,
              norm_final):
    B, S, _ = x.shape
    slopes_nsa, slopes_diff, slopes_swa = _alibi_slopes()
    split_at = [int(c) for c in np.cumsum(IN_SIZES)[:-1]]

    def heads(t, n):
        return t.reshape(B, S, n, -1)

    for l in range(DEPTH):
        h = _rmsnorm(x, norm_attn[l])
        proj = jnp.einsum('bsd,dc->bsc', h, w_in[l])
        (qa, ka, va, qn, kcn, vcn, ksn, vsn, kwn, vwn, gn,
         qc, kc, vc, qd, kd, vd) = jnp.split(proj, split_at, axis=-1)

        o_sb = _stick_breaking(heads(qa, SB_HEADS), heads(ka, SB_HEADS), heads(va, SB_HEADS))

        o_nsa = _native_sparse_attention(
            heads(qn, NSA_HEADS), kcn, vcn, ksn, vsn, kwn, vwn, heads(gn, NSA_HEADS),
            cmp_pe_k[l], cmp_w1_k[l], cmp_w2_k[l], cmp_pe_v[l], cmp_w1_v[l], cmp_w2_v[l], slopes_nsa)

        lam_init = 0.8 - 0.6 * math.exp(-0.3 * l)
        lam = (jnp.exp(jnp.sum(diff_lq1[l].astype(jnp.float32) * diff_lk1[l].astype(jnp.float32)))
               - jnp.exp(jnp.sum(diff_lq2[l].astype(jnp.float32) * diff_lk2[l].astype(jnp.float32)))
               + lam_init)
        o_diff = _differential_attention(qc.reshape(B, S, DIFF_HEADS, 2, DIFF_QK_DIM),
                                         kc.reshape(B, S, DIFF_HEADS, 2, DIFF_QK_DIM),
                                         heads(vc, DIFF_HEADS), slopes_diff, lam)
        o_diff = _rmsnorm(o_diff) * (1.0 - lam_init)

        o_swa = _banded_attention(heads(qd, SWA_HEADS), heads(kd, SWA_KV_HEADS), heads(vd, SWA_KV_HEADS),
                                  slopes_swa, SWA_WINDOW, sinks[l])

        mix = jnp.concatenate([_rmsnorm(o_sb.reshape(B, S, -1)),
                               _rmsnorm(o_nsa.reshape(B, S, -1)),
                               o_diff.reshape(B, S, -1),
                               _rmsnorm(o_swa.reshape(B, S, -1))], axis=-1)
        x = x + jnp.einsum('bsc,cd->bsd', mix * g_mix[l], w_out[l]).astype(x.dtype)

        h = _rmsnorm(x, norm_mlp[l])
        u = jax.nn.relu(jnp.einsum('bsd,df->bsf', h, w_up[l]))
        x = x + jnp.einsum('bsf,fd->bsd', u * u, w_down[l]).astype(x.dtype)

    return _rmsnorm(x, norm_final)
```

```python
import functools
import math

import numpy as np
import jax
import jax.numpy as jnp
from jax import lax
from jax.experimental import pallas as pl
from jax.experimental.pallas import tpu as pltpu

F32 = jnp.float32
BF16 = jnp.bfloat16

D_MODEL = 1024
DEPTH = 4
HEAD_DIM = 64
GROUP_WIDTH = 256
N_HEADS = 4
D_FF = 4 * D_MODEL
EPS = 1e-6
NEG_INF = -1e30
LANES = 128

CMP_LEN = 32
CMP_STRIDE = 16
CMP_HIDDEN = 128
SEL_LEN = 64
SEL_TOPN = 8
FORCE_SCORE = 1e9
NSA_WINDOW = 512
SWA_WINDOW = 128
DIFF_QK_DIM = 32

VMEM_LIMIT = 48 * 1024 * 1024

_ORIG = dict(qa=(0, 256), ka=(256, 256), va=(512, 256), qn=(768, 256), kcn=(1024, 64), vcn=(1088, 64),
             ksn=(1152, 64), vsn=(1216, 64), kwn=(1280, 64), vwn=(1344, 64), gn=(1408, 12),
             qc=(1420, 256), kc=(1676, 256), vc=(1932, 256), qd=(2188, 256), kd=(2444, 128), vd=(2572, 128))
IN_COLS = 2700


def _layout():
    cols = []

    def put(name, lo=0, n=None):
        start, size = _ORIG[name]
        n = size - lo if n is None else n
        cols.extend(range(start + lo, start + lo + n))

    def zeros(n):
        cols.extend([IN_COLS] * n)

    for nm in ("qa", "ka", "va", "qn", "kcn", "vcn"):
        put(nm)
    for nm in ("ksn", "vsn", "kwn", "vwn"):
        put(nm)
        put(nm)
    put("gn")
    zeros(LANES - 12)
    for nm in ("qc", "kc", "vc", "qd"):
        put(nm)
    for nm in ("kd", "vd"):
        put(nm, 0, 64)
        put(nm, 0, 64)
        put(nm, 64, 64)
        put(nm, 64, 64)
    return np.asarray(cols, np.int32)


_COLS = _layout()
NP = int(_COLS.shape[0])
COL_QA, COL_KA, COL_VA = 0, 1, 2
COL_QN = 3
COL_KVC = 8
COL_KS, COL_VS, COL_KW, COL_VW, COL_GN = 9, 10, 11, 12, 13
COL_QC, COL_KC, COL_VC = 7, 8, 9
COL_QD, COL_KD, COL_VD = 10, 11, 12


def _alibi_slopes():
    m = 2.0 ** (-8.0 * np.arange(1, 13) / 12.0)
    m = m.astype(np.float32).reshape(4, 3)
    return [float(v) for v in m[:, 0]], [float(v) for v in m[:, 1]], [float(v) for v in m[:, 2]]


SLOPES_NSA, SLOPES_DIFF, SLOPES_SWA = _alibi_slopes()


def _nt(a, b):
    return lax.dot_general(a, b, (((1,), (1,)), ((), ())), preferred_element_type=F32)


def _nn(a, b):
    return jnp.dot(a, b, preferred_element_type=F32)


def _cparams(*sem):
    return pltpu.CompilerParams(dimension_semantics=sem, vmem_limit_bytes=VMEM_LIMIT)


def _rms(x):
    return x * lax.rsqrt(jnp.mean(x * x, axis=-1, keepdims=True) + EPS)


def _inproj_kernel(x_ref, g_ref, w_ref, o_ref):
    h = _rms(x_ref[...]) * g_ref[...]
    o_ref[...] = _nn(h.astype(BF16), w_ref[...]).astype(BF16)


def _inproj(x2, gain, w, tm=256):
    m = x2.shape[0]
    return pl.pallas_call(
        _inproj_kernel,
        grid=(m // tm,),
        in_specs=[pl.BlockSpec((tm, D_MODEL), lambda i: (i, 0)),
                  pl.BlockSpec((1, D_MODEL), lambda i: (0, 0)),
                  pl.BlockSpec((D_MODEL, NP), lambda i: (0, 0))],
        out_specs=pl.BlockSpec((tm, NP), lambda i: (i, 0)),
        out_shape=jax.ShapeDtypeStruct((m, NP), BF16),
        compiler_params=_cparams("parallel"),
        name="inproj",
    )(x2, gain, w)


def _compress_kernel(c_ref, pea_ref, peb_ref, w1a_ref, w1b_ref, w2_ref, o_ref):
    c = c_ref[0].astype(F32)
    p = _nn((c + pea_ref[...]).astype(BF16), w1a_ref[...])
    r = _nn((c + peb_ref[...]).astype(BF16), w1b_ref[...])
    n = c.shape[0]
    r_next = pltpu.roll(r, n - 1, 0)
    hid = jax.nn.gelu(p + r_next)
    o_ref[0] = _nn(hid.astype(BF16), w2_ref[...]).astype(BF16)


def _compress(chunks, pea, peb, w1a, w1b, w2):
    b, n, w = chunks.shape
    full = lambda a: pl.BlockSpec(a.shape, lambda i: (0,) * a.ndim)
    return pl.pallas_call(
        _compress_kernel,
        grid=(b,),
        in_specs=[pl.BlockSpec((1, n, w), lambda i: (i, 0, 0)), full(pea), full(peb), full(w1a), full(w1b),
                  full(w2)],
        out_specs=pl.BlockSpec((1, n, 2 * LANES), lambda i: (i, 0, 0)),
        out_shape=jax.ShapeDtypeStruct((b, n, 2 * LANES), BF16),
        compiler_params=_cparams("parallel"),
        name="nsa_compress",
    )(chunks, pea, peb, w1a, w1b, w2)


def _half_mask(shape, half):
    lane = lax.broadcasted_iota(jnp.int32, shape, len(shape) - 1)
    return (lane // HEAD_DIM) == half


def _group_norm_store(o_ref, gm_ref, o):
    o_ref[0] = (_rms(o) * gm_ref[...]).astype(o_ref.dtype)


def _sb_kernel(q_ref, k_ref, v_ref, gm_ref, o_ref, acc_ref, carry_ref, *, blk):
    i = pl.program_id(1)
    q = q_ref[0]
    row = lax.broadcasted_iota(jnp.int32, (blk, blk), 0)
    col = lax.broadcasted_iota(jnp.int32, (blk, blk), 1)
    past = col < row
    rr = lax.broadcasted_iota(jnp.int32, (blk, 2 * blk), 0)
    cc = lax.broadcasted_iota(jnp.int32, (blk, 2 * blk), 1)
    u = jnp.where((rr > cc) | (cc >= blk), 1.0, 0.0).astype(BF16)
    qm = []
    for h in range(N_HEADS):
        slab = q[:, (h // 2) * LANES:(h // 2 + 1) * LANES]
        qm.append(jnp.where(_half_mask(slab.shape, h % 2), slab, jnp.zeros_like(slab)))
    acc_ref[...] = jnp.zeros_like(acc_ref)
    carry_ref[...] = jnp.zeros_like(carry_ref)

    def block(j, diag):
        start = pl.multiple_of(j * blk, blk)
        for h in range(N_HEADS):
            lanes = slice((h // 2) * LANES, (h // 2 + 1) * LANES)
            kb = k_ref[0, pl.ds(start, blk), lanes]
            vb = v_ref[0, pl.ds(start, blk), lanes]
            z = _nt(qm[h], kb) * (HEAD_DIM ** -0.5)
            sp = jnp.log1p(jnp.exp(-jnp.abs(z)))
            lb = jnp.minimum(z, 0.0) - sp
            lk = lb - z
            if diag:
                lk = jnp.where(past, lk, 0.0)
            hi = lk.astype(BF16)
            lo = (lk - hi.astype(F32)).astype(BF16)
            cs = _nn(hi, u) + _nn(lo, u)
            a = jnp.exp(lb + cs[:, :blk] + carry_ref[h])
            if diag:
                a = jnp.where(past, a, 0.0)
            acc_ref[h] += _nn(a.astype(BF16), vb)
            carry_ref[h] += cs[:, blk:]

    block(i, True)

    def body(jj, c):
        block(i - 1 - jj, False)
        return c

    lax.fori_loop(0, i, body, 0)
    lo_half = _half_mask((blk, LANES), 0)
    o = jnp.concatenate([jnp.where(lo_half, acc_ref[0], acc_ref[1]),
                         jnp.where(lo_half, acc_ref[2], acc_ref[3])], axis=1)
    _group_norm_store(o_ref, gm_ref, o)


def _stick_breaking(proj, gm, blk=128):
    b, s, _ = proj.shape
    return pl.pallas_call(
        functools.partial(_sb_kernel, blk=blk),
        grid=(b, s // blk),
        in_specs=[pl.BlockSpec((1, blk, 256), lambda bi, i: (bi, i, COL_QA)),
                  pl.BlockSpec((1, s, 256), lambda bi, i: (bi, 0, COL_KA)),
                  pl.BlockSpec((1, s, 256), lambda bi, i: (bi, 0, COL_VA)),
                  pl.BlockSpec((1, 256), lambda bi, i: (0, 0))],
        out_specs=pl.BlockSpec((1, blk, 256), lambda bi, i: (bi, i, 0)),
        out_shape=jax.ShapeDtypeStruct((b, s, 256), BF16),
        scratch_shapes=[pltpu.VMEM((N_HEADS, blk, LANES), F32), pltpu.VMEM((N_HEADS, blk, LANES), F32)],
        compiler_params=_cparams("parallel", "arbitrary"),
        name="stick_breaking",
    )(proj, proj, proj, gm)


def _diff_kernel(q_ref, k_ref, v_ref, dl_ref, gm_ref, o_ref, acc_ref, m_ref, l_ref, *, blk, lam_init):
    i = pl.program_id(1)
    q = q_ref[0]
    row = lax.broadcasted_iota(jnp.int32, (blk, blk), 0)
    col = lax.broadcasted_iota(jnp.int32, (blk, blk), 1)
    rel = (row - col).astype(F32)
    qm = []
    for h in range(N_HEADS):
        slab = q[:, (h // 2) * LANES:(h // 2 + 1) * LANES]
        lane = lax.broadcasted_iota(jnp.int32, slab.shape, 1)
        for c in range(2):
            qm.append(jnp.where((lane // DIFF_QK_DIM) == (h % 2) * 2 + c, slab, jnp.zeros_like(slab)))
    acc_ref[...] = jnp.zeros_like(acc_ref)
    l_ref[...] = jnp.zeros_like(l_ref)
    m_ref[...] = jnp.full(m_ref.shape, NEG_INF, F32)

    def block(j, diag):
        start = pl.multiple_of(j * blk, blk)
        off = ((i - j) * blk).astype(F32)
        for h in range(N_HEADS):
            lanes = slice((h // 2) * LANES, (h // 2 + 1) * LANES)
            kb = k_ref[0, pl.ds(start, blk), lanes]
            vb = v_ref[0, pl.ds(start, blk), lanes]
            for c in range(2):
                n = 2 * h + c
                z = _nt(qm[n], kb) * (DIFF_QK_DIM ** -0.5) - SLOPES_DIFF[h] * (rel + off)
                if diag:
                    z = jnp.where(rel >= 0, z, NEG_INF)
                m_prev = m_ref[n]
                m_new = jnp.maximum(m_prev, jnp.max(z, axis=-1, keepdims=True))
                alpha = jnp.exp(m_prev - m_new)
                p = jnp.exp(z - m_new)
                l_ref[n] = alpha * l_ref[n] + jnp.sum(p, axis=-1, keepdims=True)
                acc_ref[n] = alpha * acc_ref[n] + _nn(p.astype(BF16), vb)
                m_ref[n] = m_new

    def body(j, c):
        block(j, False)
        return c

    lax.fori_loop(0, i, body, 0)
    block(i, True)

    dl = dl_ref[...]
    s1 = jnp.sum(dl[0:1] * dl[1:2], axis=-1, keepdims=True)
    s2 = jnp.sum(dl[2:3] * dl[3:4], axis=-1, keepdims=True)
    lam = jnp.exp(s1) - jnp.exp(s2) + lam_init
    outs = []
    for h in range(N_HEADS):
        d = acc_ref[2 * h] / l_ref[2 * h] - lam * (acc_ref[2 * h + 1] / l_ref[2 * h + 1])
        mine = _half_mask(d.shape, h % 2)
        ms = jnp.sum(jnp.where(mine, d * d, 0.0), axis=-1, keepdims=True) * (1.0 / HEAD_DIM)
        outs.append(d * lax.rsqrt(ms + EPS) * (1.0 - lam_init))
    lo_half = _half_mask((blk, LANES), 0)
    o = jnp.concatenate([jnp.where(lo_half, outs[0], outs[1]), jnp.where(lo_half, outs[2], outs[3])], axis=1)
    o_ref[0] = (o * gm_ref[...]).astype(o_ref.dtype)


def _differential(proj, dl, gm, lam_init, blk=128):
    b, s, _ = proj.shape
    return pl.pallas_call(
        functools.partial(_diff_kernel, blk=blk, lam_init=lam_init),
        grid=(b, s // blk),
        in_specs=[pl.BlockSpec((1, blk, 256), lambda bi, i: (bi, i, COL_QC)),
                  pl.BlockSpec((1, s, 256), lambda bi, i: (bi, 0, COL_KC)),
                  pl.BlockSpec((1, s, 256), lambda bi, i: (bi, 0, COL_VC)),
                  pl.BlockSpec((4, DIFF_QK_DIM), lambda bi, i: (0, 0)),
                  pl.BlockSpec((1, 256), lambda bi, i: (0, 0))],
        out_specs=pl.BlockSpec((1, blk, 256), lambda bi, i: (bi, i, 0)),
        out_shape=jax.ShapeDtypeStruct((b, s, 256), BF16),
        scratch_shapes=[pltpu.VMEM((2 * N_HEADS, blk, LANES), F32), pltpu.VMEM((2 * N_HEADS, blk, 1), F32),
                        pltpu.VMEM((2 * N_HEADS, blk, 1), F32)],
        compiler_params=_cparams("parallel", "arbitrary"),
        name="differential",
    )(proj, proj, proj, dl, gm)


def _swa_kernel(q_ref, k_ref, v_ref, sink_ref, gm_ref, o_ref, *, blk):
    i = pl.program_id(1)
    q = q_ref[0]
    first = jnp.maximum(i - 1, 0)
    start = pl.multiple_of(first * blk, blk)
    row = lax.broadcasted_iota(jnp.int32, (blk, 2 * blk), 0)
    col = lax.broadcasted_iota(jnp.int32, (blk, 2 * blk), 1)
    dist = (row - col) + (i - first) * blk
    mask = (dist >= 0) & (dist < SWA_WINDOW)
    distf = dist.astype(F32)
    outs = []
    for h in range(N_HEADS):
        lanes = slice((h // 2) * LANES, (h // 2 + 1) * LANES)
        slab = q[:, lanes]
        qm = jnp.where(_half_mask(slab.shape, h % 2), slab, jnp.zeros_like(slab))
        kb = k_ref[0, pl.ds(start, 2 * blk), lanes]
        vb = v_ref[0, pl.ds(start, 2 * blk), lanes]
        z = _nt(qm, kb) * (HEAD_DIM ** -0.5) - SLOPES_SWA[h] * distf
        z = jnp.where(mask, z, NEG_INF)
        sink = sink_ref[h]
        m = jnp.maximum(jnp.max(z, axis=-1, keepdims=True), sink)
        p = jnp.exp(z - m)
        l = jnp.sum(p, axis=-1, keepdims=True) + jnp.exp(sink - m)
        outs.append(_nn(p.astype(BF16), vb) / l)
    lo_half = _half_mask((blk, LANES), 0)
    o = jnp.concatenate([jnp.where(lo_half, outs[0], outs[1]), jnp.where(lo_half, outs[2], outs[3])], axis=1)
    _group_norm_store(o_ref, gm_ref, o)


def _sliding_window(proj, sinks, gm, blk=128):
    b, s, _ = proj.shape
    return pl.pallas_call(
        functools.partial(_swa_kernel, blk=blk),
        grid=(b, s // blk),
        in_specs=[pl.BlockSpec((1, blk, 256), lambda bi, i: (bi, i, COL_QD)),
                  pl.BlockSpec((1, s, 256), lambda bi, i: (bi, 0, COL_KD)),
                  pl.BlockSpec((1, s, 256), lambda bi, i: (bi, 0, COL_VD)),
                  pl.BlockSpec(memory_space=pltpu.SMEM),
                  pl.BlockSpec((1, 256), lambda bi, i: (0, 0))],
        out_specs=pl.BlockSpec((1, blk, 256), lambda bi, i: (bi, i, 0)),
        out_shape=jax.ShapeDtypeStruct((b, s, 256), BF16),
        compiler_params=_cparams("parallel", "arbitrary"),
        name="sliding_window",
    )(proj, proj, proj, sinks, gm)


def _nsa_kernel(q_ref, g_ref, cmp_ref, ks_ref, vs_ref, kw_ref, vw_ref, gm_ref, o_ref, acc_ref, m_ref, l_ref,
                *, blk, n_cmp, n_sel):
    i = pl.program_id(1)
    q = q_ref[0]
    rows = N_HEADS * blk
    qs = []
    for h in range(N_HEADS):
        slab = q[:, (h // 2) * LANES:(h // 2 + 1) * LANES]
        qs.append(jnp.where(_half_mask(slab.shape, h % 2), slab, jnp.zeros_like(slab)))
    qs = jnp.concatenate(qs, axis=0)
    scale = HEAD_DIM ** -0.5

    def head_bias(mask, dist):
        distf = dist.astype(F32)
        return jnp.concatenate([jnp.where(mask, -SLOPES_NSA[h] * distf, NEG_INF) for h in range(N_HEADS)], axis=0)

    tq = i * blk + lax.broadcasted_iota(jnp.int32, (blk, LANES), 0)
    lane = lax.broadcasted_iota(jnp.int32, (blk, LANES), 1)

    kc = cmp_ref[0, :, 0:LANES]
    vc = cmp_ref[0, :, LANES:2 * LANES]
    dist_c = tq - (CMP_STRIDE * lane + CMP_LEN - 1)
    bias_c = head_bias((dist_c >= 0) & (lane < n_cmp), dist_c)
    z = _nt(qs, kc) * scale + bias_c
    p = jnp.exp(z - jnp.max(z, axis=-1, keepdims=True))
    p = p / jnp.sum(p, axis=-1, keepdims=True)
    p = jnp.where(bias_c > 0.5 * NEG_INF, p, 0.0).astype(BF16)
    o_cmp = _nn(p, vc)

    ci = lax.broadcasted_iota(jnp.int32, (LANES, LANES), 0)
    cj = lax.broadcasted_iota(jnp.int32, (LANES, LANES), 1)
    to_sel = jnp.where((ci * CMP_STRIDE) // SEL_LEN == cj, 1.0, 0.0).astype(BF16)
    imp = _nn(p[0:blk], to_sel)
    for h in range(1, N_HEADS):
        imp = imp + _nn(p[h * blk:(h + 1) * blk], to_sel)
    cur = tq // SEL_LEN
    forced = (lane == 0) | (lane == cur) | (lane == cur - 1)
    score = jnp.where(lane > cur, NEG_INF, jnp.where(forced, FORCE_SCORE, imp))
    score = jnp.where(lane < n_sel, score, -3.0e38)
    sel = jnp.zeros((blk, LANES), F32)
    lanef = lane.astype(F32)
    for _ in range(min(SEL_TOPN, n_sel)):
        best = jnp.max(score, axis=-1, keepdims=True)
        idx = jnp.min(jnp.where(score == best, lanef, float(LANES)), axis=-1, keepdims=True)
        hit = lanef == idx
        sel = jnp.where(hit, 1.0, sel)
        score = jnp.where(hit, -3.4e38, score)
    sel = sel.astype(BF16)

    row = lax.broadcasted_iota(jnp.int32, (blk, blk), 0)
    col = lax.broadcasted_iota(jnp.int32, (blk, blk), 1)
    rel = row - col
    ej = lax.broadcasted_iota(jnp.int32, (LANES, blk), 0)
    ec = lax.broadcasted_iota(jnp.int32, (LANES, blk), 1)

    def flash(k_ref, v_ref, j_lo, mask_fn):
        acc_ref[...] = jnp.zeros_like(acc_ref)
        l_ref[...] = jnp.zeros_like(l_ref)
        m_ref[...] = jnp.full(m_ref.shape, NEG_INF, F32)

        def body(j, c):
            start = pl.multiple_of(j * blk, blk)
            kb = k_ref[0, pl.ds(start, blk), :]
            vb = v_ref[0, pl.ds(start, blk), :]
            dist = rel + (i - j) * blk
            bias = head_bias(mask_fn(j, dist), dist)
            zz = _nt(qs, kb) * scale + bias
            m_prev = m_ref[...]
            m_new = jnp.maximum(m_prev, jnp.max(zz, axis=-1, keepdims=True))
            alpha = jnp.exp(m_prev - m_new)
            pp = jnp.where(bias > 0.5 * NEG_INF, jnp.exp(zz - m_new), 0.0)
            l_ref[...] = alpha * l_ref[...] + jnp.sum(pp, axis=-1, keepdims=True)
            acc_ref[...] = alpha * acc_ref[...] + _nn(pp.astype(BF16), vb)
            m_ref[...] = m_new
            return c

        lax.fori_loop(j_lo, i + 1, body, 0)
        return acc_ref[...] / l_ref[...]

    def sel_mask(j, dist):
        expand = jnp.where(ej == (j * blk + ec) // SEL_LEN, 1.0, 0.0).astype(BF16)
        picked = _nn(sel, expand) > 0.5
        return picked & (dist >= 0)

    def win_mask(j, dist):
        return (dist >= 0) & (dist < NSA_WINDOW)

    o_sel = flash(ks_ref, vs_ref, 0, sel_mask)
    o_win = flash(kw_ref, vw_ref, jnp.maximum(i - NSA_WINDOW // blk, 0), win_mask)

    g = jax.nn.sigmoid(g_ref[0].astype(F32))
    outs = []
    for h in range(N_HEADS):
        r = slice(h * blk, (h + 1) * blk)
        outs.append(g[:, 3 * h:3 * h + 1] * o_cmp[r] + g[:, 3 * h + 1:3 * h + 2] * o_sel[r]
                    + g[:, 3 * h + 2:3 * h + 3] * o_win[r])
    lo_half = _half_mask((blk, LANES), 0)
    o = jnp.concatenate([jnp.where(lo_half, outs[0], outs[1]), jnp.where(lo_half, outs[2], outs[3])], axis=1)
    _group_norm_store(o_ref, gm_ref, o)


def _native_sparse(proj, cmp_kv, gm, blk=128):
    b, s, _ = proj.shape
    n_cmp = (s - CMP_LEN) // CMP_STRIDE + 1
    n_sel = s // SEL_LEN
    assert cmp_kv.shape[1] == LANES and n_cmp <= LANES and n_sel <= LANES
    slab = lambda c: pl.BlockSpec((1, s, LANES), lambda bi, i: (bi, 0, c))
    rows = N_HEADS * blk
    return pl.pallas_call(
        functools.partial(_nsa_kernel, blk=blk, n_cmp=n_cmp, n_sel=n_sel),
        grid=(b, s // blk),
        in_specs=[pl.BlockSpec((1, blk, 256), lambda bi, i: (bi, i, COL_QN)),
                  pl.BlockSpec((1, blk, LANES), lambda bi, i: (bi, i, COL_GN)),
                  pl.BlockSpec((1, LANES, 2 * LANES), lambda bi, i: (bi, 0, 0)),
                  slab(COL_KS), slab(COL_VS), slab(COL_KW), slab(COL_VW),
                  pl.BlockSpec((1, 256), lambda bi, i: (0, 0))],
        out_specs=pl.BlockSpec((1, blk, 256), lambda bi, i: (bi, i, 0)),
        out_shape=jax.ShapeDtypeStruct((b, s, 256), BF16),
        scratch_shapes=[pltpu.VMEM((rows, LANES), F32), pltpu.VMEM((rows, 1), F32), pltpu.VMEM((rows, 1), F32)],
        compiler_params=_cparams("parallel", "arbitrary"),
        name="native_sparse",
    )(proj, proj, cmp_kv, proj, proj, proj, proj, gm)


def _outproj_kernel(a_ref, b_ref, c_ref, d_ref, w_ref, x_ref, o_ref):
    acc = x_ref[...]
    for g, r in enumerate((a_ref, b_ref, c_ref, d_ref)):
        acc = acc + _nn(r[...], w_ref[g * GROUP_WIDTH:(g + 1) * GROUP_WIDTH, :])
    o_ref[...] = acc


def _outproj(mixes, w, x2, tm=512):
    m = x2.shape[0]
    grp = pl.BlockSpec((tm, GROUP_WIDTH), lambda i: (i, 0))
    return pl.pallas_call(
        _outproj_kernel,
        grid=(m // tm,),
        in_specs=[grp, grp, grp, grp, pl.BlockSpec((D_MODEL, D_MODEL), lambda i: (0, 0)),
                  pl.BlockSpec((tm, D_MODEL), lambda i: (i, 0))],
        out_specs=pl.BlockSpec((tm, D_MODEL), lambda i: (i, 0)),
        out_shape=jax.ShapeDtypeStruct((m, D_MODEL), F32),
        compiler_params=_cparams("parallel"),
        name="outproj",
    )(*mixes, w, x2)


def _mlp_kernel(x_ref, g_ref, wu_ref, wd_ref, gf_ref, o_ref, h_ref, acc_ref, *, final_norm):
    f = pl.program_id(1)

    @pl.when(f == 0)
    def _():
        h_ref[...] = (_rms(x_ref[...]) * g_ref[...]).astype(BF16)
        acc_ref[...] = jnp.zeros_like(acc_ref)

    u = jnp.maximum(_nn(h_ref[...], wu_ref[...]), 0.0)
    acc_ref[...] += _nn((u * u).astype(BF16), wd_ref[...])

    @pl.when(f == pl.num_programs(1) - 1)
    def _():
        y = x_ref[...] + acc_ref[...]
        if final_norm:
            y = _rms(y) * gf_ref[...]
        o_ref[...] = y


def _mlp(x2, gain, wu, wd, gfinal, final_norm, tm=1024, tf=512):
    m = x2.shape[0]
    return pl.pallas_call(
        functools.partial(_mlp_kernel, final_norm=final_norm),
        grid=(m // tm, D_FF // tf),
        in_specs=[pl.BlockSpec((tm, D_MODEL), lambda i, f: (i, 0)),
                  pl.BlockSpec((1, D_MODEL), lambda i, f: (0, 0)),
                  pl.BlockSpec((D_MODEL, tf), lambda i, f: (0, f)),
                  pl.BlockSpec((tf, D_MODEL), lambda i, f: (f, 0)),
                  pl.BlockSpec((1, D_MODEL), lambda i, f: (0, 0))],
        out_specs=pl.BlockSpec((tm, D_MODEL), lambda i, f: (i, 0)),
        out_shape=jax.ShapeDtypeStruct((m, D_MODEL), F32),
        scratch_shapes=[pltpu.VMEM((tm, D_MODEL), BF16), pltpu.VMEM((tm, D_MODEL), F32)],
        compiler_params=_cparams("parallel", "arbitrary"),
        name="mlp",
    )(x2, gain, wu, wd, gfinal)


def _compress_weights(pe_k, w1_k, w2_k, pe_v, w1_v, w2_v):
    half = CMP_STRIDE
    d = HEAD_DIM

    def halves(w1):
        w = w1.reshape(CMP_LEN, d, CMP_HIDDEN)
        return w[:half], w[half:]

    ka, kb = halves(w1_k)
    va, vb = halves(w1_v)
    zero = jnp.zeros_like(ka)

    def merge(wk, wv):
        top = jnp.concatenate([wk, zero], axis=-1)
        bot = jnp.concatenate([zero, wv], axis=-1)
        return jnp.concatenate([top, bot], axis=1).reshape(half * 2 * d, 2 * CMP_HIDDEN).astype(BF16)

    w1a, w1b = merge(ka, va), merge(kb, vb)
    pe = jnp.concatenate([pe_k, pe_v], axis=-1)
    pea = pe[:half].reshape(1, half * 2 * d)
    peb = pe[half:].reshape(1, half * 2 * d)
    zk = jnp.zeros_like(w2_k)
    w2 = jnp.concatenate([jnp.concatenate([w2_k, w2_k, zk, zk], axis=1),
                          jnp.concatenate([zk, zk, w2_v, w2_v], axis=1)], axis=0).astype(BF16)
    return pea, peb, w1a, w1b, w2


def kernel(x, norm_attn, w_in, cmp_pe_k, cmp_w1_k, cmp_w2_k, cmp_pe_v, cmp_w1_v, cmp_w2_v, diff_lq1, diff_lk1,
           diff_lq2, diff_lk2, sinks, g_mix, w_out, norm_mlp, w_up, w_down, norm_final):
    b, s, d = x.shape
    m = b * s
    depth = w_in.shape[0]
    cols = jnp.asarray(_COLS)
    w_in_p = jnp.take(jnp.concatenate([w_in, jnp.zeros((depth, d, 1), w_in.dtype)], axis=2), cols,
                      axis=2).astype(BF16)
    w_out_b = w_out.astype(BF16)
    w_up_b = w_up.astype(BF16)
    w_down_b = w_down.astype(BF16)
    gfinal = norm_final.reshape(1, d)

    x2 = x.reshape(m, d)
    for l in range(depth):
        proj = _inproj(x2, norm_attn[l].reshape(1, d), w_in_p[l]).reshape(b, s, NP)
        gm = g_mix[l].reshape(N_HEADS, 1, GROUP_WIDTH)

        chunks = proj[:, :, COL_KVC * LANES:(COL_KVC + 1) * LANES].reshape(b, s // CMP_STRIDE, CMP_STRIDE * LANES)
        cmp_kv = _compress(chunks, *_compress_weights(cmp_pe_k[l], cmp_w1_k[l], cmp_w2_k[l],
                                                      cmp_pe_v[l], cmp_w1_v[l], cmp_w2_v[l]))

        o_sb = _stick_breaking(proj, gm[0])
        o_nsa = _native_sparse(proj, cmp_kv, gm[1])
        lam_init = 0.8 - 0.6 * math.exp(-0.3 * l)
        dl = jnp.stack([diff_lq1[l], diff_lk1[l], diff_lq2[l], diff_lk2[l]]).astype(F32)
        o_diff = _differential(proj, dl, gm[2], lam_init)
        o_swa = _sliding_window(proj, sinks[l].astype(F32), gm[3])

        mixes = [o.reshape(m, GROUP_WIDTH) for o in (o_sb, o_nsa, o_diff, o_swa)]
        x2 = _outproj(mixes, w_out_b[l], x2)
        x2 = _mlp(x2, norm_mlp[l].reshape(1, d), w_up_b[l], w_down_b[l], gfinal, final_norm=(l == depth - 1))
    return x2.reshape(b, s, d)
```

```python
import functools
import math

import ml_dtypes
import numpy as np
import jax
import jax.numpy as jnp
from jax import lax
from jax.experimental import pallas as pl
from jax.experimental.pallas import tpu as pltpu

F32 = jnp.float32
BF16 = jnp.bfloat16

D_MODEL = 1024
DEPTH = 4
HEAD_DIM = 64
GROUP_WIDTH = 256
N_HEADS = 4
D_FF = 4 * D_MODEL
EPS = 1e-6
NEG_INF = -1e30
LOG2E = math.log2(math.e)
LANES = 128

CMP_LEN = 32
CMP_STRIDE = 16
CMP_HIDDEN = 128
SEL_LEN = 64
SEL_TOPN = 8
FORCE_SCORE = 1e9
NSA_WINDOW = 512
SWA_WINDOW = 128
DIFF_QK_DIM = 32

VMEM_LIMIT = 48 * 1024 * 1024

_ORIG = dict(qa=(0, 256), ka=(256, 256), va=(512, 256), qn=(768, 256), kcn=(1024, 64), vcn=(1088, 64),
             ksn=(1152, 64), vsn=(1216, 64), kwn=(1280, 64), vwn=(1344, 64), gn=(1408, 12),
             qc=(1420, 256), kc=(1676, 256), vc=(1932, 256), qd=(2188, 256), kd=(2444, 128), vd=(2572, 128))
IN_COLS = 2700


def _layout():
    cols = []

    def put(name, lo=0, n=None):
        start, size = _ORIG[name]
        n = size - lo if n is None else n
        cols.extend(range(start + lo, start + lo + n))

    def zeros(n):
        cols.extend([IN_COLS] * n)

    for nm in ("qa", "ka", "va", "qn", "kcn", "vcn"):
        put(nm)
    for nm in ("ksn", "vsn", "kwn", "vwn"):
        put(nm)
        put(nm)
    put("gn")
    zeros(LANES - 12)
    for nm in ("qc", "kc", "vc", "qd"):
        put(nm)
    for nm in ("kd", "vd"):
        put(nm, 0, 64)
        put(nm, 0, 64)
        put(nm, 64, 64)
        put(nm, 64, 64)
    return np.asarray(cols, np.int32)


_COLS = _layout()
NP = int(_COLS.shape[0])


def _column_scales():
    cs = np.ones((1, NP), np.float32)
    cs[0, 0:256] = HEAD_DIM ** -0.5 * LOG2E
    cs[0, 1792:2048] = DIFF_QK_DIM ** -0.5 * LOG2E
    return cs


COL_QA, COL_KA, COL_VA = 0, 1, 2
COL_QN = 3
COL_KVC = 8
COL_KS, COL_VS, COL_KW, COL_VW, COL_GN = 9, 10, 11, 12, 13
COL_QC, COL_KC, COL_VC = 7, 8, 9
COL_QD, COL_KD, COL_VD = 10, 11, 12


def _alibi_slopes():
    m = 2.0 ** (-8.0 * np.arange(1, 13) / 12.0)
    m = m.astype(np.float32).reshape(4, 3)
    return [float(v) for v in m[:, 0]], [float(v) for v in m[:, 1]], [float(v) for v in m[:, 2]]


SLOPES_NSA, SLOPES_DIFF, SLOPES_SWA = _alibi_slopes()


def _nt(a, b):
    return lax.dot_general(a, b, (((1,), (1,)), ((), ())), preferred_element_type=F32)


def _nn(a, b):
    return jnp.dot(a, b, preferred_element_type=F32)


def _cparams(*sem):
    return pltpu.CompilerParams(dimension_semantics=sem, vmem_limit_bytes=VMEM_LIMIT)


def _rms(x):
    return x * lax.rsqrt(jnp.mean(x * x, axis=-1, keepdims=True) + EPS)


def _inproj_kernel(x_ref, g_ref, w_ref, cs_ref, o_ref):
    h = _rms(x_ref[...]) * g_ref[...]
    o_ref[...] = (_nn(h.astype(BF16), w_ref[...]) * cs_ref[...]).astype(BF16)


def _inproj(x2, gain, w, colscale, tm=256):
    m = x2.shape[0]
    return pl.pallas_call(
        _inproj_kernel,
        grid=(m // tm,),
        in_specs=[pl.BlockSpec((tm, D_MODEL), lambda i: (i, 0)),
                  pl.BlockSpec((1, D_MODEL), lambda i: (0, 0)),
                  pl.BlockSpec((D_MODEL, NP), lambda i: (0, 0)),
                  pl.BlockSpec((1, NP), lambda i: (0, 0))],
        out_specs=pl.BlockSpec((tm, NP), lambda i: (i, 0)),
        out_shape=jax.ShapeDtypeStruct((m, NP), BF16),
        compiler_params=_cparams("parallel"),
        name="inproj",
    )(x2, gain, w, colscale)


def _compress_kernel(c_ref, pea_ref, peb_ref, w1a_ref, w1b_ref, w2_ref, o_ref):
    c = c_ref[0].astype(F32)
    p = _nn((c + pea_ref[...]).astype(BF16), w1a_ref[...])
    r = _nn((c + peb_ref[...]).astype(BF16), w1b_ref[...])
    n = c.shape[0]
    r_next = pltpu.roll(r, n - 1, 0)
    hid = jax.nn.gelu(p + r_next)
    o_ref[0] = _nn(hid.astype(BF16), w2_ref[...]).astype(BF16)


def _compress(chunks, pea, peb, w1a, w1b, w2):
    b, n, w = chunks.shape
    full = lambda a: pl.BlockSpec(a.shape, lambda i: (0,) * a.ndim)
    return pl.pallas_call(
        _compress_kernel,
        grid=(b,),
        in_specs=[pl.BlockSpec((1, n, w), lambda i: (i, 0, 0)), full(pea), full(peb), full(w1a), full(w1b),
                  full(w2)],
        out_specs=pl.BlockSpec((1, n, 2 * LANES), lambda i: (i, 0, 0)),
        out_shape=jax.ShapeDtypeStruct((b, n, 2 * LANES), BF16),
        compiler_params=_cparams("parallel"),
        name="nsa_compress",
    )(chunks, pea, peb, w1a, w1b, w2)


def _half_mask(shape, half):
    lane = lax.broadcasted_iota(jnp.int32, shape, len(shape) - 1)
    return (lane // HEAD_DIM) == half


def _group_norm_store(o_ref, gm_ref, o):
    o_ref[0] = (_rms(o) * gm_ref[...]).astype(o_ref.dtype)


def _sb_kernel(q_ref, k_ref, v_ref, gm_ref, o_ref, acc_ref, carry_ref, *, blk):
    i = pl.program_id(1)
    q = q_ref[0]
    row = lax.broadcasted_iota(jnp.int32, (blk, blk), 0)
    col = lax.broadcasted_iota(jnp.int32, (blk, blk), 1)
    past = col < row
    u = jnp.where(row > col, 1.0, 0.0).astype(BF16)
    qm = []
    for h in range(N_HEADS):
        slab = q[:, (h // 2) * LANES:(h // 2 + 1) * LANES]
        qm.append(jnp.where(_half_mask(slab.shape, h % 2), slab, jnp.zeros_like(slab)))
    acc_ref[...] = jnp.zeros_like(acc_ref)
    carry_ref[...] = jnp.zeros_like(carry_ref)

    def block(j, diag):
        start = pl.multiple_of(j * blk, blk)
        for h in range(N_HEADS):
            lanes = slice((h // 2) * LANES, (h // 2 + 1) * LANES)
            kb = k_ref[0, pl.ds(start, blk), lanes]
            vb = v_ref[0, pl.ds(start, blk), lanes]
            z = _nt(qm[h], kb)
            sp = jnp.log2(1.0 + jnp.exp2(-jnp.abs(z)))
            lb = jnp.minimum(z, 0.0) - sp
            lk = lb - z
            if diag:
                lk = jnp.where(past, lk, 0.0)
            hi = lk.astype(BF16)
            lo = (lk - hi.astype(F32)).astype(BF16)
            cs = _nn(hi, u) + _nn(lo, u)
            carry = carry_ref[h]
            a = jnp.exp2(lb + cs + jnp.concatenate([carry] * (blk // LANES), axis=1))
            if diag:
                a = jnp.where(past, a, 0.0)
            acc_ref[h] += _nn(a.astype(BF16), vb)
            carry_ref[h] = carry + jnp.sum(lk, axis=-1, keepdims=True)

    block(i, True)

    def body(jj, c):
        block(i - 1 - jj, False)
        return c

    lax.fori_loop(0, i, body, 0)
    lo_half = _half_mask((blk, LANES), 0)
    o = jnp.concatenate([jnp.where(lo_half, acc_ref[0], acc_ref[1]),
                         jnp.where(lo_half, acc_ref[2], acc_ref[3])], axis=1)
    _group_norm_store(o_ref, gm_ref, o)


def _stick_breaking(proj, gm, blk=256):
    b, s, _ = proj.shape
    return pl.pallas_call(
        functools.partial(_sb_kernel, blk=blk),
        grid=(b, s // blk),
        in_specs=[pl.BlockSpec((1, blk, 256), lambda bi, i: (bi, i, COL_QA)),
                  pl.BlockSpec((1, s, 256), lambda bi, i: (bi, 0, COL_KA)),
                  pl.BlockSpec((1, s, 256), lambda bi, i: (bi, 0, COL_VA)),
                  pl.BlockSpec((1, 256), lambda bi, i: (0, 0))],
        out_specs=pl.BlockSpec((1, blk, 256), lambda bi, i: (bi, i, 0)),
        out_shape=jax.ShapeDtypeStruct((b, s, 256), BF16),
        scratch_shapes=[pltpu.VMEM((N_HEADS, blk, LANES), F32), pltpu.VMEM((N_HEADS, blk, LANES), F32)],
        compiler_params=_cparams("parallel", "arbitrary"),
        name="stick_breaking",
    )(proj, proj, proj, gm)


N_MAPS = 2 * N_HEADS
POS_LOCAL, POS_BLOCK, POS_ONE = 0, 3, 6


def _bf16_pieces(x, n):
    out = []
    r = np.float32(x)
    for _ in range(n):
        p = np.float32(r.astype(ml_dtypes.bfloat16))
        out.append(float(p))
        r = np.float32(r - p)
    return out


def _diff_tables(s, blk):
    coef = np.zeros((N_HEADS, LANES), np.float32)
    for h in range(N_HEADS):
        pieces = _bf16_pieces(SLOPES_DIFF[h] * LOG2E, 3)
        coef[h, POS_LOCAL:POS_LOCAL + 3] = pieces
        coef[h, POS_BLOCK:POS_BLOCK + 3] = [blk * p for p in pieces]
    pos = np.zeros((s, LANES), np.float32)
    idx = np.arange(s)
    pos[:, POS_LOCAL:POS_LOCAL + 3] = (idx % blk)[:, None]
    pos[:, POS_BLOCK:POS_BLOCK + 3] = (idx // blk)[:, None]
    pos[:, POS_ONE:POS_ONE + 2] = 1.0
    return jnp.asarray(coef), jnp.asarray(pos, BF16)


def _diff_kernel(q_ref, k_ref, v_ref, pos_ref, coef_ref, dl_ref, gm_ref, o_ref, qa_ref, mx_ref, acc_ref,
                 *, blk, lam_init):
    i = pl.program_id(1)
    q = q_ref[0]
    row = lax.broadcasted_iota(jnp.int32, (blk, blk), 0)
    col = lax.broadcasted_iota(jnp.int32, (blk, blk), 1)
    causal = col <= row
    lane = lax.broadcasted_iota(jnp.int32, (blk, LANES), 1)
    for h in range(N_HEADS):
        slab = q[:, (h // 2) * LANES:(h // 2 + 1) * LANES]
        coef = jnp.broadcast_to(coef_ref[h:h + 1, :], (blk, LANES)).astype(BF16)
        for c in range(2):
            n = 2 * h + c
            qa_ref[n, :, 0:LANES] = jnp.where((lane // DIFF_QK_DIM) == (h % 2) * 2 + c, slab, jnp.zeros_like(slab))
            qa_ref[n, :, LANES:2 * LANES] = coef
    mx_ref[...] = jnp.full(mx_ref.shape, NEG_INF, F32)
    acc_ref[...] = jnp.zeros_like(acc_ref)
    ones = jnp.ones((blk, LANES), BF16)

    def sweep(j, diag, second):
        rows = pl.ds(pl.multiple_of(j * blk, blk), blk)
        posb = pos_ref[rows, :]
        for pr in range(2):
            lanes = slice(pr * LANES, (pr + 1) * LANES)
            ka = jnp.concatenate([k_ref[0, rows, lanes], posb], axis=1)
            if second:
                vo = jnp.concatenate([v_ref[0, rows, lanes], ones], axis=1)
            for n in range(4 * pr, 4 * pr + 4):
                z = _nt(qa_ref[n], ka)
                if second:
                    p = jnp.exp2(z)
                    if diag:
                        p = jnp.where(causal, p, 0.0)
                    acc_ref[n] += _nn(p.astype(BF16), vo)
                else:
                    if diag:
                        z = jnp.where(causal, z, NEG_INF)
                    zm = z[:, 0:LANES]
                    for g in range(1, blk // LANES):
                        zm = jnp.maximum(zm, z[:, g * LANES:(g + 1) * LANES])
                    mx_ref[n] = jnp.maximum(mx_ref[n], zm)

    def run(second):
        def body(j, c):
            sweep(j, False, second)
            return c
        lax.fori_loop(0, i, body, 0)
        sweep(i, True, second)

    run(False)
    for n in range(N_MAPS):
        m = jnp.max(mx_ref[n], axis=-1, keepdims=True)
        m_hi = m.astype(BF16)
        m_lo = (m - m_hi.astype(F32)).astype(BF16)
        old = qa_ref[n, :, LANES:2 * LANES]
        qa_ref[n, :, LANES:2 * LANES] = jnp.where(lane == POS_ONE, -m_hi, jnp.where(lane == POS_ONE + 1, -m_lo, old))
    run(True)

    dl = dl_ref[...]
    s1 = jnp.sum(dl[0:1] * dl[1:2], axis=-1, keepdims=True)
    s2 = jnp.sum(dl[2:3] * dl[3:4], axis=-1, keepdims=True)
    lam = jnp.exp(s1) - jnp.exp(s2) + lam_init
    outs = []
    for h in range(N_HEADS):
        a1, a2 = acc_ref[2 * h], acc_ref[2 * h + 1]
        d = a1[:, :LANES] / a1[:, LANES:] - lam * (a2[:, :LANES] / a2[:, LANES:])
        mine = _half_mask(d.shape, h % 2)
        ms = jnp.sum(jnp.where(mine, d * d, 0.0), axis=-1, keepdims=True) * (1.0 / HEAD_DIM)
        outs.append(d * lax.rsqrt(ms + EPS) * (1.0 - lam_init))
    lo_half = _half_mask((blk, LANES), 0)
    o = jnp.concatenate([jnp.where(lo_half, outs[0], outs[1]), jnp.where(lo_half, outs[2], outs[3])], axis=1)
    o_ref[0] = (o * gm_ref[...]).astype(o_ref.dtype)


def _differential(proj, dl, gm, lam_init, blk=256):
    b, s, _ = proj.shape
    coef, pos = _diff_tables(s, blk)
    return pl.pallas_call(
        functools.partial(_diff_kernel, blk=blk, lam_init=lam_init),
        grid=(b, s // blk),
        in_specs=[pl.BlockSpec((1, blk, 256), lambda bi, i: (bi, i, COL_QC)),
                  pl.BlockSpec((1, s, 256), lambda bi, i: (bi, 0, COL_KC)),
                  pl.BlockSpec((1, s, 256), lambda bi, i: (bi, 0, COL_VC)),
                  pl.BlockSpec((s, LANES), lambda bi, i: (0, 0)),
                  pl.BlockSpec((N_HEADS, LANES), lambda bi, i: (0, 0)),
                  pl.BlockSpec((4, DIFF_QK_DIM), lambda bi, i: (0, 0)),
                  pl.BlockSpec((1, 256), lambda bi, i: (0, 0))],
        out_specs=pl.BlockSpec((1, blk, 256), lambda bi, i: (bi, i, 0)),
        out_shape=jax.ShapeDtypeStruct((b, s, 256), BF16),
        scratch_shapes=[pltpu.VMEM((N_MAPS, blk, 2 * LANES), BF16), pltpu.VMEM((N_MAPS, blk, LANES), F32),
                        pltpu.VMEM((N_MAPS, blk, 2 * LANES), F32)],
        compiler_params=_cparams("parallel", "arbitrary"),
        name="differential",
    )(proj, proj, proj, pos, coef, dl, gm)


def _swa_kernel(q_ref, k_ref, v_ref, sink_ref, gm_ref, o_ref, *, blk):
    i = pl.program_id(1)
    q = q_ref[0]
    first = jnp.maximum(i - 1, 0)
    start = pl.multiple_of(first * blk, blk)
    row = lax.broadcasted_iota(jnp.int32, (blk, 2 * blk), 0)
    col = lax.broadcasted_iota(jnp.int32, (blk, 2 * blk), 1)
    dist = (row - col) + (i - first) * blk
    mask = (dist >= 0) & (dist < SWA_WINDOW)
    distf = dist.astype(F32)
    outs = []
    for h in range(N_HEADS):
        lanes = slice((h // 2) * LANES, (h // 2 + 1) * LANES)
        slab = q[:, lanes]
        qm = jnp.where(_half_mask(slab.shape, h % 2), slab, jnp.zeros_like(slab))
        kb = k_ref[0, pl.ds(start, 2 * blk), lanes]
        vb = v_ref[0, pl.ds(start, 2 * blk), lanes]
        z = _nt(qm, kb) * (HEAD_DIM ** -0.5) - SLOPES_SWA[h] * distf
        z = jnp.where(mask, z, NEG_INF)
        sink = sink_ref[h]
        m = jnp.maximum(jnp.max(z, axis=-1, keepdims=True), sink)
        p = jnp.exp(z - m)
        l = jnp.sum(p, axis=-1, keepdims=True) + jnp.exp(sink - m)
        outs.append(_nn(p.astype(BF16), vb) / l)
    lo_half = _half_mask((blk, LANES), 0)
    o = jnp.concatenate([jnp.where(lo_half, outs[0], outs[1]), jnp.where(lo_half, outs[2], outs[3])], axis=1)
    _group_norm_store(o_ref, gm_ref, o)


def _sliding_window(proj, sinks, gm, blk=128):
    b, s, _ = proj.shape
    return pl.pallas_call(
        functools.partial(_swa_kernel, blk=blk),
        grid=(b, s // blk),
        in_specs=[pl.BlockSpec((1, blk, 256), lambda bi, i: (bi, i, COL_QD)),
                  pl.BlockSpec((1, s, 256), lambda bi, i: (bi, 0, COL_KD)),
                  pl.BlockSpec((1, s, 256), lambda bi, i: (bi, 0, COL_VD)),
                  pl.BlockSpec(memory_space=pltpu.SMEM),
                  pl.BlockSpec((1, 256), lambda bi, i: (0, 0))],
        out_specs=pl.BlockSpec((1, blk, 256), lambda bi, i: (bi, i, 0)),
        out_shape=jax.ShapeDtypeStruct((b, s, 256), BF16),
        compiler_params=_cparams("parallel", "arbitrary"),
        name="sliding_window",
    )(proj, proj, proj, sinks, gm)


def _nsa_kernel(q_ref, g_ref, cmp_ref, ks_ref, vs_ref, kw_ref, vw_ref, gm_ref, o_ref, acc_ref, m_ref, l_ref,
                *, blk, n_cmp, n_sel):
    i = pl.program_id(1)
    q = q_ref[0]
    rows = N_HEADS * blk
    qs = []
    for h in range(N_HEADS):
        slab = q[:, (h // 2) * LANES:(h // 2 + 1) * LANES]
        qs.append(jnp.where(_half_mask(slab.shape, h % 2), slab, jnp.zeros_like(slab)))
    qs = jnp.concatenate(qs, axis=0)
    scale = HEAD_DIM ** -0.5

    def head_bias(mask, dist):
        distf = dist.astype(F32)
        return jnp.concatenate([jnp.where(mask, -SLOPES_NSA[h] * distf, NEG_INF) for h in range(N_HEADS)], axis=0)

    tq = i * blk + lax.broadcasted_iota(jnp.int32, (blk, LANES), 0)
    lane = lax.broadcasted_iota(jnp.int32, (blk, LANES), 1)

    kc = cmp_ref[0, :, 0:LANES]
    vc = cmp_ref[0, :, LANES:2 * LANES]
    dist_c = tq - (CMP_STRIDE * lane + CMP_LEN - 1)
    bias_c = head_bias((dist_c >= 0) & (lane < n_cmp), dist_c)
    z = _nt(qs, kc) * scale + bias_c
    p = jnp.exp(z - jnp.max(z, axis=-1, keepdims=True))
    p = p / jnp.sum(p, axis=-1, keepdims=True)
    p = jnp.where(bias_c > 0.5 * NEG_INF, p, 0.0).astype(BF16)
    o_cmp = _nn(p, vc)

    ci = lax.broadcasted_iota(jnp.int32, (LANES, LANES), 0)
    cj = lax.broadcasted_iota(jnp.int32, (LANES, LANES), 1)
    to_sel = jnp.where((ci * CMP_STRIDE) // SEL_LEN == cj, 1.0, 0.0).astype(BF16)
    imp = _nn(p[0:blk], to_sel)
    for h in range(1, N_HEADS):
        imp = imp + _nn(p[h * blk:(h + 1) * blk], to_sel)
    cur = tq // SEL_LEN
    forced = (lane == 0) | (lane == cur) | (lane == cur - 1)
    score = jnp.where(lane > cur, NEG_INF, jnp.where(forced, FORCE_SCORE, imp))
    score = jnp.where(lane < n_sel, score, -3.0e38)
    sel = jnp.zeros((blk, LANES), F32)
    lanef = lane.astype(F32)
    for _ in range(min(SEL_TOPN, n_sel)):
        best = jnp.max(score, axis=-1, keepdims=True)
        idx = jnp.min(jnp.where(score == best, lanef, float(LANES)), axis=-1, keepdims=True)
        hit = lanef == idx
        sel = jnp.where(hit, 1.0, sel)
        score = jnp.where(hit, -3.4e38, score)
    sel = sel.astype(BF16)

    row = lax.broadcasted_iota(jnp.int32, (blk, blk), 0)
    col = lax.broadcasted_iota(jnp.int32, (blk, blk), 1)
    rel = row - col
    ej = lax.broadcasted_iota(jnp.int32, (LANES, blk), 0)
    ec = lax.broadcasted_iota(jnp.int32, (LANES, blk), 1)

    def flash(k_ref, v_ref, j_lo, mask_fn):
        acc_ref[...] = jnp.zeros_like(acc_ref)
        l_ref[...] = jnp.zeros_like(l_ref)
        m_ref[...] = jnp.full(m_ref.shape, NEG_INF, F32)

        def body(j, c):
            start = pl.multiple_of(j * blk, blk)
            kb = k_ref[0, pl.ds(start, blk), :]
            vb = v_ref[0, pl.ds(start, blk), :]
            dist = rel + (i - j) * blk
            bias = head_bias(mask_fn(j, dist), dist)
            zz = _nt(qs, kb) * scale + bias
            m_prev = m_ref[...]
            m_new = jnp.maximum(m_prev, jnp.max(zz, axis=-1, keepdims=True))
            alpha = jnp.exp(m_prev - m_new)
            pp = jnp.where(bias > 0.5 * NEG_INF, jnp.exp(zz - m_new), 0.0)
            l_ref[...] = alpha * l_ref[...] + jnp.sum(pp, axis=-1, keepdims=True)
            acc_ref[...] = alpha * acc_ref[...] + _nn(pp.astype(BF16), vb)
            m_ref[...] = m_new
            return c

        lax.fori_loop(j_lo, i + 1, body, 0)
        return acc_ref[...] / l_ref[...]

    def sel_mask(j, dist):
        expand = jnp.where(ej == (j * blk + ec) // SEL_LEN, 1.0, 0.0).astype(BF16)
        picked = _nn(sel, expand) > 0.5
        return picked & (dist >= 0)

    def win_mask(j, dist):
        return (dist >= 0) & (dist < NSA_WINDOW)

    o_sel = flash(ks_ref, vs_ref, 0, sel_mask)
    o_win = flash(kw_ref, vw_ref, jnp.maximum(i - NSA_WINDOW // blk, 0), win_mask)

    g = jax.nn.sigmoid(g_ref[0].astype(F32))
    outs = []
    for h in range(N_HEADS):
        r = slice(h * blk, (h + 1) * blk)
        outs.append(g[:, 3 * h:3 * h + 1] * o_cmp[r] + g[:, 3 * h + 1:3 * h + 2] * o_sel[r]
                    + g[:, 3 * h + 2:3 * h + 3] * o_win[r])
    lo_half = _half_mask((blk, LANES), 0)
    o = jnp.concatenate([jnp.where(lo_half, outs[0], outs[1]), jnp.where(lo_half, outs[2], outs[3])], axis=1)
    _group_norm_store(o_ref, gm_ref, o)


def _native_sparse(proj, cmp_kv, gm, blk=128):
    b, s, _ = proj.shape
    n_cmp = (s - CMP_LEN) // CMP_STRIDE + 1
    n_sel = s // SEL_LEN
    assert cmp_kv.shape[1] == LANES and n_cmp <= LANES and n_sel <= LANES
    slab = lambda c: pl.BlockSpec((1, s, LANES), lambda bi, i: (bi, 0, c))
    rows = N_HEADS * blk
    return pl.pallas_call(
        functools.partial(_nsa_kernel, blk=blk, n_cmp=n_cmp, n_sel=n_sel),
        grid=(b, s // blk),
        in_specs=[pl.BlockSpec((1, blk, 256), lambda bi, i: (bi, i, COL_QN)),
                  pl.BlockSpec((1, blk, LANES), lambda bi, i: (bi, i, COL_GN)),
                  pl.BlockSpec((1, LANES, 2 * LANES), lambda bi, i: (bi, 0, 0)),
                  slab(COL_KS), slab(COL_VS), slab(COL_KW), slab(COL_VW),
                  pl.BlockSpec((1, 256), lambda bi, i: (0, 0))],
        out_specs=pl.BlockSpec((1, blk, 256), lambda bi, i: (bi, i, 0)),
        out_shape=jax.ShapeDtypeStruct((b, s, 256), BF16),
        scratch_shapes=[pltpu.VMEM((rows, LANES), F32), pltpu.VMEM((rows, 1), F32), pltpu.VMEM((rows, 1), F32)],
        compiler_params=_cparams("parallel", "arbitrary"),
        name="native_sparse",
    )(proj, proj, cmp_kv, proj, proj, proj, proj, gm)


def _outproj_kernel(a_ref, b_ref, c_ref, d_ref, w_ref, x_ref, o_ref):
    acc = x_ref[...]
    for g, r in enumerate((a_ref, b_ref, c_ref, d_ref)):
        acc = acc + _nn(r[...], w_ref[g * GROUP_WIDTH:(g + 1) * GROUP_WIDTH, :])
    o_ref[...] = acc


def _outproj(mixes, w, x2, tm=512):
    m = x2.shape[0]
    grp = pl.BlockSpec((tm, GROUP_WIDTH), lambda i: (i, 0))
    return pl.pallas_call(
        _outproj_kernel,
        grid=(m // tm,),
        in_specs=[grp, grp, grp, grp, pl.BlockSpec((D_MODEL, D_MODEL), lambda i: (0, 0)),
                  pl.BlockSpec((tm, D_MODEL), lambda i: (i, 0))],
        out_specs=pl.BlockSpec((tm, D_MODEL), lambda i: (i, 0)),
        out_shape=jax.ShapeDtypeStruct((m, D_MODEL), F32),
        compiler_params=_cparams("parallel"),
        name="outproj",
    )(*mixes, w, x2)


def _mlp_kernel(x_ref, g_ref, wu_ref, wd_ref, gf_ref, o_ref, h_ref, acc_ref, *, final_norm):
    f = pl.program_id(1)

    @pl.when(f == 0)
    def _():
        h_ref[...] = (_rms(x_ref[...]) * g_ref[...]).astype(BF16)
        acc_ref[...] = jnp.zeros_like(acc_ref)

    u = jnp.maximum(_nn(h_ref[...], wu_ref[...]), 0.0)
    acc_ref[...] += _nn((u * u).astype(BF16), wd_ref[...])

    @pl.when(f == pl.num_programs(1) - 1)
    def _():
        y = x_ref[...] + acc_ref[...]
        if final_norm:
            y = _rms(y) * gf_ref[...]
        o_ref[...] = y


def _mlp(x2, gain, wu, wd, gfinal, final_norm, tm=1024, tf=512):
    m = x2.shape[0]
    return pl.pallas_call(
        functools.partial(_mlp_kernel, final_norm=final_norm),
        grid=(m // tm, D_FF // tf),
        in_specs=[pl.BlockSpec((tm, D_MODEL), lambda i, f: (i, 0)),
                  pl.BlockSpec((1, D_MODEL), lambda i, f: (0, 0)),
                  pl.BlockSpec((D_MODEL, tf), lambda i, f: (0, f)),
                  pl.BlockSpec((tf, D_MODEL), lambda i, f: (f, 0)),
                  pl.BlockSpec((1, D_MODEL), lambda i, f: (0, 0))],
        out_specs=pl.BlockSpec((tm, D_MODEL), lambda i, f: (i, 0)),
        out_shape=jax.ShapeDtypeStruct((m, D_MODEL), F32),
        scratch_shapes=[pltpu.VMEM((tm, D_MODEL), BF16), pltpu.VMEM((tm, D_MODEL), F32)],
        compiler_params=_cparams("parallel", "arbitrary"),
        name="mlp",
    )(x2, gain, wu, wd, gfinal)


def _compress_weights(pe_k, w1_k, w2_k, pe_v, w1_v, w2_v):
    half = CMP_STRIDE
    d = HEAD_DIM

    def halves(w1):
        w = w1.reshape(CMP_LEN, d, CMP_HIDDEN)
        return w[:half], w[half:]

    ka, kb = halves(w1_k)
    va, vb = halves(w1_v)
    zero = jnp.zeros_like(ka)

    def merge(wk, wv):
        top = jnp.concatenate([wk, zero], axis=-1)
        bot = jnp.concatenate([zero, wv], axis=-1)
        return jnp.concatenate([top, bot], axis=1).reshape(half * 2 * d, 2 * CMP_HIDDEN).astype(BF16)

    w1a, w1b = merge(ka, va), merge(kb, vb)
    pe = jnp.concatenate([pe_k, pe_v], axis=-1)
    pea = pe[:half].reshape(1, half * 2 * d)
    peb = pe[half:].reshape(1, half * 2 * d)
    zk = jnp.zeros_like(w2_k)
    w2 = jnp.concatenate([jnp.concatenate([w2_k, w2_k, zk, zk], axis=1),
                          jnp.concatenate([zk, zk, w2_v, w2_v], axis=1)], axis=0).astype(BF16)
    return pea, peb, w1a, w1b, w2


def kernel(x, norm_attn, w_in, cmp_pe_k, cmp_w1_k, cmp_w2_k, cmp_pe_v, cmp_w1_v, cmp_w2_v, diff_lq1, diff_lk1,
           diff_lq2, diff_lk2, sinks, g_mix, w_out, norm_mlp, w_up, w_down, norm_final):
    b, s, d = x.shape
    m = b * s
    depth = w_in.shape[0]
    cols = jnp.asarray(_COLS)
    w_in_p = jnp.take(jnp.concatenate([w_in, jnp.zeros((depth, d, 1), w_in.dtype)], axis=2), cols,
                      axis=2).astype(BF16)
    w_out_b = w_out.astype(BF16)
    w_up_b = w_up.astype(BF16)
    w_down_b = w_down.astype(BF16)
    gfinal = norm_final.reshape(1, d)

    colscale = jnp.asarray(_column_scales())

    x2 = x.reshape(m, d)
    for l in range(depth):
        proj = _inproj(x2, norm_attn[l].reshape(1, d), w_in_p[l], colscale).reshape(b, s, NP)
        gm = g_mix[l].reshape(N_HEADS, 1, GROUP_WIDTH)

        chunks = proj[:, :, COL_KVC * LANES:(COL_KVC + 1) * LANES].reshape(b, s // CMP_STRIDE, CMP_STRIDE * LANES)
        cmp_kv = _compress(chunks, *_compress_weights(cmp_pe_k[l], cmp_w1_k[l], cmp_w2_k[l],
                                                      cmp_pe_v[l], cmp_w1_v[l], cmp_w2_v[l]))

        o_sb = _stick_breaking(proj, gm[0])
        o_nsa = _native_sparse(proj, cmp_kv, gm[1])
        lam_init = 0.8 - 0.6 * math.exp(-0.3 * l)
        dl = jnp.stack([diff_lq1[l], diff_lk1[l], diff_lq2[l], diff_lk2[l]]).astype(F32)
        o_diff = _differential(proj, dl, gm[2], lam_init)
        o_swa = _sliding_window(proj, sinks[l].astype(F32), gm[3])

        mixes = [o.reshape(m, GROUP_WIDTH) for o in (o_sb, o_nsa, o_diff, o_swa)]
        x2 = _outproj(mixes, w_out_b[l], x2)
        x2 = _mlp(x2, norm_mlp[l].reshape(1, d), w_up_b[l], w_down_b[l], gfinal, final_norm=(l == depth - 1))
    return x2.reshape(b, s, d)
```

```python
import functools
import math

import ml_dtypes
import numpy as np
import jax
import jax.numpy as jnp
from jax import lax
from jax.experimental import pallas as pl
from jax.experimental.pallas import tpu as pltpu

F32 = jnp.float32
BF16 = jnp.bfloat16

D_MODEL = 1024
DEPTH = 4
HEAD_DIM = 64
GROUP_WIDTH = 256
N_HEADS = 4
D_FF = 4 * D_MODEL
EPS = 1e-6
NEG_INF = -1e30
LOG2E = math.log2(math.e)
LANES = 128

CMP_LEN = 32
CMP_STRIDE = 16
CMP_HIDDEN = 128
SEL_LEN = 64
SEL_TOPN = 8
FORCE_SCORE = 1e9
NSA_WINDOW = 512
SWA_WINDOW = 128
DIFF_QK_DIM = 32

VMEM_LIMIT = 48 * 1024 * 1024

_ORIG = dict(qa=(0, 256), ka=(256, 256), va=(512, 256), qn=(768, 256), kcn=(1024, 64), vcn=(1088, 64),
             ksn=(1152, 64), vsn=(1216, 64), kwn=(1280, 64), vwn=(1344, 64), gn=(1408, 12),
             qc=(1420, 256), kc=(1676, 256), vc=(1932, 256), qd=(2188, 256), kd=(2444, 128), vd=(2572, 128))
IN_COLS = 2700


def _layout():
    cols = []

    def put(name, lo=0, n=None):
        start, size = _ORIG[name]
        n = size - lo if n is None else n
        cols.extend(range(start + lo, start + lo + n))

    def zeros(n):
        cols.extend([IN_COLS] * n)

    for nm in ("qa", "ka", "va", "qn", "kcn", "vcn"):
        put(nm)
    for nm in ("ksn", "vsn", "kwn", "vwn"):
        put(nm)
        put(nm)
    put("gn")
    zeros(LANES - 12)
    for nm in ("qc", "kc", "vc", "qd"):
        put(nm)
    for nm in ("kd", "vd"):
        put(nm, 0, 64)
        put(nm, 0, 64)
        put(nm, 64, 64)
        put(nm, 64, 64)
    return np.asarray(cols, np.int32)


_COLS = _layout()
NP = int(_COLS.shape[0])


def _column_scales():
    cs = np.ones((1, NP), np.float32)
    cs[0, 0:256] = HEAD_DIM ** -0.5 * LOG2E
    cs[0, 768:1024] = HEAD_DIM ** -0.5 * LOG2E
    cs[0, 1792:2048] = DIFF_QK_DIM ** -0.5 * LOG2E
    return cs


COL_QA, COL_KA, COL_VA = 0, 1, 2
COL_QN = 3
COL_KVC = 8
COL_KS, COL_VS, COL_KW, COL_VW, COL_GN = 9, 10, 11, 12, 13
COL_QC, COL_KC, COL_VC = 7, 8, 9
COL_QD, COL_KD, COL_VD = 10, 11, 12


def _alibi_slopes():
    m = 2.0 ** (-8.0 * np.arange(1, 13) / 12.0)
    m = m.astype(np.float32).reshape(4, 3)
    return [float(v) for v in m[:, 0]], [float(v) for v in m[:, 1]], [float(v) for v in m[:, 2]]


SLOPES_NSA, SLOPES_DIFF, SLOPES_SWA = _alibi_slopes()


def _nt(a, b):
    return lax.dot_general(a, b, (((1,), (1,)), ((), ())), preferred_element_type=F32)


def _nn(a, b):
    return jnp.dot(a, b, preferred_element_type=F32)


def _cparams(*sem):
    return pltpu.CompilerParams(dimension_semantics=sem, vmem_limit_bytes=VMEM_LIMIT)


def _rms(x):
    return x * lax.rsqrt(jnp.mean(x * x, axis=-1, keepdims=True) + EPS)


def _inproj_kernel(x_ref, g_ref, w_ref, cs_ref, o_ref):
    h = _rms(x_ref[...]) * g_ref[...]
    o_ref[...] = (_nn(h.astype(BF16), w_ref[...]) * cs_ref[...]).astype(BF16)


def _inproj(x2, gain, w, colscale, tm=256):
    m = x2.shape[0]
    return pl.pallas_call(
        _inproj_kernel,
        grid=(m // tm,),
        in_specs=[pl.BlockSpec((tm, D_MODEL), lambda i: (i, 0)),
                  pl.BlockSpec((1, D_MODEL), lambda i: (0, 0)),
                  pl.BlockSpec((D_MODEL, NP), lambda i: (0, 0)),
                  pl.BlockSpec((1, NP), lambda i: (0, 0))],
        out_specs=pl.BlockSpec((tm, NP), lambda i: (i, 0)),
        out_shape=jax.ShapeDtypeStruct((m, NP), BF16),
        compiler_params=_cparams("parallel"),
        name="inproj",
    )(x2, gain, w, colscale)


def _compress_kernel(c_ref, pea_ref, peb_ref, w1a_ref, w1b_ref, w2_ref, o_ref):
    c = c_ref[0].astype(F32)
    p = _nn((c + pea_ref[...]).astype(BF16), w1a_ref[...])
    r = _nn((c + peb_ref[...]).astype(BF16), w1b_ref[...])
    n = c.shape[0]
    r_next = pltpu.roll(r, n - 1, 0)
    hid = jax.nn.gelu(p + r_next)
    o_ref[0] = _nn(hid.astype(BF16), w2_ref[...]).astype(BF16)


def _compress(chunks, pea, peb, w1a, w1b, w2):
    b, n, w = chunks.shape
    full = lambda a: pl.BlockSpec(a.shape, lambda i: (0,) * a.ndim)
    return pl.pallas_call(
        _compress_kernel,
        grid=(b,),
        in_specs=[pl.BlockSpec((1, n, w), lambda i: (i, 0, 0)), full(pea), full(peb), full(w1a), full(w1b),
                  full(w2)],
        out_specs=pl.BlockSpec((1, n, 2 * LANES), lambda i: (i, 0, 0)),
        out_shape=jax.ShapeDtypeStruct((b, n, 2 * LANES), BF16),
        compiler_params=_cparams("parallel"),
        name="nsa_compress",
    )(chunks, pea, peb, w1a, w1b, w2)


def _half_mask(shape, half):
    lane = lax.broadcasted_iota(jnp.int32, shape, len(shape) - 1)
    return (lane // HEAD_DIM) == half


def _group_norm_store(o_ref, gm_ref, o):
    o_ref[0] = (_rms(o) * gm_ref[...]).astype(o_ref.dtype)


def _sb_kernel(q_ref, k_ref, v_ref, gm_ref, o_ref, acc_ref, carry_ref, *, blk):
    i = pl.program_id(1)
    q = q_ref[0]
    row = lax.broadcasted_iota(jnp.int32, (blk, blk), 0)
    col = lax.broadcasted_iota(jnp.int32, (blk, blk), 1)
    past = col < row
    u = jnp.where(row > col, 1.0, 0.0).astype(BF16)
    qm = []
    for h in range(N_HEADS):
        slab = q[:, (h // 2) * LANES:(h // 2 + 1) * LANES]
        qm.append(jnp.where(_half_mask(slab.shape, h % 2), slab, jnp.zeros_like(slab)))
    acc_ref[...] = jnp.zeros_like(acc_ref)
    carry_ref[...] = jnp.zeros_like(carry_ref)

    def block(j, diag):
        start = pl.multiple_of(j * blk, blk)
        for h in range(N_HEADS):
            lanes = slice((h // 2) * LANES, (h // 2 + 1) * LANES)
            kb = k_ref[0, pl.ds(start, blk), lanes]
            vb = v_ref[0, pl.ds(start, blk), lanes]
            z = _nt(qm[h], kb)
            sp = jnp.log2(1.0 + jnp.exp2(-jnp.abs(z)))
            lb = jnp.minimum(z, 0.0) - sp
            lk = lb - z
            if diag:
                lk = jnp.where(past, lk, 0.0)
            hi = lk.astype(BF16)
            lo = (lk - hi.astype(F32)).astype(BF16)
            cs = _nn(hi, u) + _nn(lo, u)
            carry = carry_ref[h]
            a = jnp.exp2(lb + cs + jnp.concatenate([carry] * (blk // LANES), axis=1))
            if diag:
                a = jnp.where(past, a, 0.0)
            acc_ref[h] += _nn(a.astype(BF16), vb)
            carry_ref[h] = carry + jnp.sum(lk, axis=-1, keepdims=True)

    block(i, True)

    def body(jj, c):
        block(i - 1 - jj, False)
        return c

    lax.fori_loop(0, i, body, 0)
    lo_half = _half_mask((blk, LANES), 0)
    o = jnp.concatenate([jnp.where(lo_half, acc_ref[0], acc_ref[1]),
                         jnp.where(lo_half, acc_ref[2], acc_ref[3])], axis=1)
    _group_norm_store(o_ref, gm_ref, o)


def _stick_breaking(proj, gm, blk=256):
    b, s, _ = proj.shape
    return pl.pallas_call(
        functools.partial(_sb_kernel, blk=blk),
        grid=(b, s // blk),
        in_specs=[pl.BlockSpec((1, blk, 256), lambda bi, i: (bi, i, COL_QA)),
                  pl.BlockSpec((1, s, 256), lambda bi, i: (bi, 0, COL_KA)),
                  pl.BlockSpec((1, s, 256), lambda bi, i: (bi, 0, COL_VA)),
                  pl.BlockSpec((1, 256), lambda bi, i: (0, 0))],
        out_specs=pl.BlockSpec((1, blk, 256), lambda bi, i: (bi, i, 0)),
        out_shape=jax.ShapeDtypeStruct((b, s, 256), BF16),
        scratch_shapes=[pltpu.VMEM((N_HEADS, blk, LANES), F32), pltpu.VMEM((N_HEADS, blk, LANES), F32)],
        compiler_params=_cparams("parallel", "arbitrary"),
        name="stick_breaking",
    )(proj, proj, proj, gm)


N_MAPS = 2 * N_HEADS
POS_LOCAL, POS_BLOCK, POS_ONE = 0, 3, 6
SEL_LANE = 8
MASK_BIAS = -2.0 ** 100


def _bf16_pieces(x, n):
    out = []
    r = np.float32(x)
    for _ in range(n):
        p = np.float32(r.astype(ml_dtypes.bfloat16))
        out.append(float(p))
        r = np.float32(r - p)
    return out


def _position_tables(slopes, s, blk, sel_blocks=False):
    coef = np.zeros((N_HEADS, LANES), np.float32)
    for h in range(N_HEADS):
        pieces = _bf16_pieces(slopes[h] * LOG2E, 3)
        coef[h, POS_LOCAL:POS_LOCAL + 3] = pieces
        coef[h, POS_BLOCK:POS_BLOCK + 3] = [blk * p for p in pieces]
    pos = np.zeros((s, LANES), np.float32)
    idx = np.arange(s)
    pos[:, POS_LOCAL:POS_LOCAL + 3] = (idx % blk)[:, None]
    pos[:, POS_BLOCK:POS_BLOCK + 3] = (idx // blk)[:, None]
    pos[:, POS_ONE:POS_ONE + 2] = 1.0
    if sel_blocks:
        pos[idx, SEL_LANE + idx // SEL_LEN] = 1.0
    return jnp.asarray(coef), jnp.asarray(pos, BF16)


def _diff_kernel(q_ref, k_ref, v_ref, pos_ref, coef_ref, dl_ref, gm_ref, o_ref, qa_ref, mx_ref, acc_ref,
                 *, blk, lam_init):
    i = pl.program_id(1)
    q = q_ref[0]
    row = lax.broadcasted_iota(jnp.int32, (blk, blk), 0)
    col = lax.broadcasted_iota(jnp.int32, (blk, blk), 1)
    causal = col <= row
    lane = lax.broadcasted_iota(jnp.int32, (blk, LANES), 1)
    for h in range(N_HEADS):
        slab = q[:, (h // 2) * LANES:(h // 2 + 1) * LANES]
        coef = jnp.broadcast_to(coef_ref[h:h + 1, :], (blk, LANES)).astype(BF16)
        for c in range(2):
            n = 2 * h + c
            qa_ref[n, :, 0:LANES] = jnp.where((lane // DIFF_QK_DIM) == (h % 2) * 2 + c, slab, jnp.zeros_like(slab))
            qa_ref[n, :, LANES:2 * LANES] = coef
    mx_ref[...] = jnp.full(mx_ref.shape, NEG_INF, F32)
    acc_ref[...] = jnp.zeros_like(acc_ref)
    ones = jnp.ones((blk, LANES), BF16)

    def sweep(j, diag, second):
        rows = pl.ds(pl.multiple_of(j * blk, blk), blk)
        posb = pos_ref[rows, :]
        for pr in range(2):
            lanes = slice(pr * LANES, (pr + 1) * LANES)
            ka = jnp.concatenate([k_ref[0, rows, lanes], posb], axis=1)
            if second:
                vo = jnp.concatenate([v_ref[0, rows, lanes], ones], axis=1)
            for n in range(4 * pr, 4 * pr + 4):
                z = _nt(qa_ref[n], ka)
                if second:
                    p = jnp.exp2(z)
                    if diag:
                        p = jnp.where(causal, p, 0.0)
                    acc_ref[n] += _nn(p.astype(BF16), vo)
                else:
                    if diag:
                        z = jnp.where(causal, z, NEG_INF)
                    zm = z[:, 0:LANES]
                    for g in range(1, blk // LANES):
                        zm = jnp.maximum(zm, z[:, g * LANES:(g + 1) * LANES])
                    mx_ref[n] = jnp.maximum(mx_ref[n], zm)

    def run(second):
        def body(j, c):
            sweep(j, False, second)
            return c
        lax.fori_loop(0, i, body, 0)
        sweep(i, True, second)

    run(False)
    for n in range(N_MAPS):
        m = jnp.max(mx_ref[n], axis=-1, keepdims=True)
        m_hi = m.astype(BF16)
        m_lo = (m - m_hi.astype(F32)).astype(BF16)
        old = qa_ref[n, :, LANES:2 * LANES]
        qa_ref[n, :, LANES:2 * LANES] = jnp.where(lane == POS_ONE, -m_hi, jnp.where(lane == POS_ONE + 1, -m_lo, old))
    run(True)

    dl = dl_ref[...]
    s1 = jnp.sum(dl[0:1] * dl[1:2], axis=-1, keepdims=True)
    s2 = jnp.sum(dl[2:3] * dl[3:4], axis=-1, keepdims=True)
    lam = jnp.exp(s1) - jnp.exp(s2) + lam_init
    outs = []
    for h in range(N_HEADS):
        a1, a2 = acc_ref[2 * h], acc_ref[2 * h + 1]
        d = a1[:, :LANES] / a1[:, LANES:] - lam * (a2[:, :LANES] / a2[:, LANES:])
        mine = _half_mask(d.shape, h % 2)
        ms = jnp.sum(jnp.where(mine, d * d, 0.0), axis=-1, keepdims=True) * (1.0 / HEAD_DIM)
        outs.append(d * lax.rsqrt(ms + EPS) * (1.0 - lam_init))
    lo_half = _half_mask((blk, LANES), 0)
    o = jnp.concatenate([jnp.where(lo_half, outs[0], outs[1]), jnp.where(lo_half, outs[2], outs[3])], axis=1)
    o_ref[0] = (o * gm_ref[...]).astype(o_ref.dtype)


def _differential(proj, dl, gm, lam_init, blk=256):
    b, s, _ = proj.shape
    coef, pos = _position_tables(SLOPES_DIFF, s, blk)
    return pl.pallas_call(
        functools.partial(_diff_kernel, blk=blk, lam_init=lam_init),
        grid=(b, s // blk),
        in_specs=[pl.BlockSpec((1, blk, 256), lambda bi, i: (bi, i, COL_QC)),
                  pl.BlockSpec((1, s, 256), lambda bi, i: (bi, 0, COL_KC)),
                  pl.BlockSpec((1, s, 256), lambda bi, i: (bi, 0, COL_VC)),
                  pl.BlockSpec((s, LANES), lambda bi, i: (0, 0)),
                  pl.BlockSpec((N_HEADS, LANES), lambda bi, i: (0, 0)),
                  pl.BlockSpec((4, DIFF_QK_DIM), lambda bi, i: (0, 0)),
                  pl.BlockSpec((1, 256), lambda bi, i: (0, 0))],
        out_specs=pl.BlockSpec((1, blk, 256), lambda bi, i: (bi, i, 0)),
        out_shape=jax.ShapeDtypeStruct((b, s, 256), BF16),
        scratch_shapes=[pltpu.VMEM((N_MAPS, blk, 2 * LANES), BF16), pltpu.VMEM((N_MAPS, blk, LANES), F32),
                        pltpu.VMEM((N_MAPS, blk, 2 * LANES), F32)],
        compiler_params=_cparams("parallel", "arbitrary"),
        name="differential",
    )(proj, proj, proj, pos, coef, dl, gm)


def _swa_kernel(q_ref, k_ref, v_ref, sink_ref, gm_ref, o_ref, *, blk):
    i = pl.program_id(1)
    q = q_ref[0]
    first = jnp.maximum(i - 1, 0)
    start = pl.multiple_of(first * blk, blk)
    row = lax.broadcasted_iota(jnp.int32, (blk, 2 * blk), 0)
    col = lax.broadcasted_iota(jnp.int32, (blk, 2 * blk), 1)
    dist = (row - col) + (i - first) * blk
    mask = (dist >= 0) & (dist < SWA_WINDOW)
    distf = dist.astype(F32)
    outs = []
    for h in range(N_HEADS):
        lanes = slice((h // 2) * LANES, (h // 2 + 1) * LANES)
        slab = q[:, lanes]
        qm = jnp.where(_half_mask(slab.shape, h % 2), slab, jnp.zeros_like(slab))
        kb = k_ref[0, pl.ds(start, 2 * blk), lanes]
        vb = v_ref[0, pl.ds(start, 2 * blk), lanes]
        z = _nt(qm, kb) * (HEAD_DIM ** -0.5) - SLOPES_SWA[h] * distf
        z = jnp.where(mask, z, NEG_INF)
        sink = sink_ref[h]
        m = jnp.maximum(jnp.max(z, axis=-1, keepdims=True), sink)
        p = jnp.exp(z - m)
        l = jnp.sum(p, axis=-1, keepdims=True) + jnp.exp(sink - m)
        outs.append(_nn(p.astype(BF16), vb) / l)
    lo_half = _half_mask((blk, LANES), 0)
    o = jnp.concatenate([jnp.where(lo_half, outs[0], outs[1]), jnp.where(lo_half, outs[2], outs[3])], axis=1)
    _group_norm_store(o_ref, gm_ref, o)


def _sliding_window(proj, sinks, gm, blk=128):
    b, s, _ = proj.shape
    return pl.pallas_call(
        functools.partial(_swa_kernel, blk=blk),
        grid=(b, s // blk),
        in_specs=[pl.BlockSpec((1, blk, 256), lambda bi, i: (bi, i, COL_QD)),
                  pl.BlockSpec((1, s, 256), lambda bi, i: (bi, 0, COL_KD)),
                  pl.BlockSpec((1, s, 256), lambda bi, i: (bi, 0, COL_VD)),
                  pl.BlockSpec(memory_space=pltpu.SMEM),
                  pl.BlockSpec((1, 256), lambda bi, i: (0, 0))],
        out_specs=pl.BlockSpec((1, blk, 256), lambda bi, i: (bi, i, 0)),
        out_shape=jax.ShapeDtypeStruct((b, s, 256), BF16),
        compiler_params=_cparams("parallel", "arbitrary"),
        name="sliding_window",
    )(proj, proj, proj, sinks, gm)


def _nsa_kernel(q_ref, g_ref, cmp_ref, ks_ref, vs_ref, kw_ref, vw_ref, pos_ref, coef_ref, gm_ref, o_ref,
                qa_ref, mx_ref, acc_ref, *, blk, n_cmp, n_sel):
    i = pl.program_id(1)
    q = q_ref[0]
    lane = lax.broadcasted_iota(jnp.int32, (blk, LANES), 1)
    for h in range(N_HEADS):
        slab = q[:, (h // 2) * LANES:(h // 2 + 1) * LANES]
        qa_ref[h * blk:(h + 1) * blk, 0:LANES] = jnp.where(_half_mask(slab.shape, h % 2), slab, jnp.zeros_like(slab))
    qs = qa_ref[:, 0:LANES]
    coefs = [jnp.broadcast_to(coef_ref[h:h + 1, :], (blk, LANES)) for h in range(N_HEADS)]

    tq = i * blk + lax.broadcasted_iota(jnp.int32, (blk, LANES), 0)

    kc = cmp_ref[0, :, 0:LANES]
    vc = cmp_ref[0, :, LANES:2 * LANES]
    dist_c = tq - (CMP_STRIDE * lane + CMP_LEN - 1)
    ok_c = (dist_c >= 0) & (lane < n_cmp)
    dist_cf = dist_c.astype(F32)
    bias_c = jnp.concatenate([jnp.where(ok_c, (-SLOPES_NSA[h] * LOG2E) * dist_cf, NEG_INF)
                              for h in range(N_HEADS)], axis=0)
    z = _nt(qs, kc) + bias_c
    p = jnp.exp2(z - jnp.max(z, axis=-1, keepdims=True))
    p = p / jnp.sum(p, axis=-1, keepdims=True)
    p = jnp.where(bias_c > 0.5 * NEG_INF, p, 0.0).astype(BF16)
    o_cmp = _nn(p, vc)

    ci = lax.broadcasted_iota(jnp.int32, (LANES, LANES), 0)
    cj = lax.broadcasted_iota(jnp.int32, (LANES, LANES), 1)
    to_sel = jnp.where((ci * CMP_STRIDE) // SEL_LEN + SEL_LANE == cj, 1.0, 0.0).astype(BF16)
    imp = _nn(p[0:blk], to_sel)
    for h in range(1, N_HEADS):
        imp = imp + _nn(p[h * blk:(h + 1) * blk], to_sel)
    blk_id = lane - SEL_LANE
    in_sel = (blk_id >= 0) & (blk_id < n_sel)
    cur = tq // SEL_LEN
    forced = (blk_id == 0) | (blk_id == cur) | (blk_id == cur - 1)
    score = jnp.where(blk_id > cur, NEG_INF, jnp.where(forced, FORCE_SCORE, imp))
    score = jnp.where(in_sel, score, -3.0e38)
    sel = jnp.zeros((blk, LANES), F32)
    lanef = lane.astype(F32)
    for _ in range(min(SEL_TOPN, n_sel)):
        best = jnp.max(score, axis=-1, keepdims=True)
        idx = jnp.min(jnp.where(score == best, lanef, float(LANES)), axis=-1, keepdims=True)
        hit = lanef == idx
        sel = jnp.where(hit, 1.0, sel)
        score = jnp.where(hit, -3.4e38, score)
    sel_bias = jnp.where(sel > 0.5, 0.0, MASK_BIAS)

    row = lax.broadcasted_iota(jnp.int32, (blk, blk), 0)
    col = lax.broadcasted_iota(jnp.int32, (blk, blk), 1)
    causal = col <= row
    in_window = col > row
    ones = jnp.ones((blk, LANES), BF16)

    def process(k_ref, v_ref, j, mask, second):
        rows = pl.ds(pl.multiple_of(j * blk, blk), blk)
        ka = jnp.concatenate([k_ref[0, rows, :], pos_ref[rows, :]], axis=1)
        zz = _nt(qa_ref[...], ka)
        if second:
            pp = jnp.exp2(zz)
            if mask is not None:
                pp = jnp.concatenate([jnp.where(mask, pp[h * blk:(h + 1) * blk], 0.0) for h in range(N_HEADS)], axis=0)
            vo = jnp.concatenate([v_ref[0, rows, :], ones], axis=1)
            acc_ref[...] += _nn(pp.astype(BF16), vo)
        else:
            if mask is not None:
                zz = jnp.concatenate([jnp.where(mask, zz[h * blk:(h + 1) * blk], NEG_INF) for h in range(N_HEADS)],
                                     axis=0)
            zm = zz[:, 0:LANES]
            for g in range(1, blk // LANES):
                zm = jnp.maximum(zm, zz[:, g * LANES:(g + 1) * LANES])
            mx_ref[...] = jnp.maximum(mx_ref[...], zm)

    def attend(sweep, aug):
        for h in range(N_HEADS):
            qa_ref[h * blk:(h + 1) * blk, LANES:2 * LANES] = aug[h].astype(BF16)
        mx_ref[...] = jnp.full(mx_ref.shape, NEG_INF, F32)
        acc_ref[...] = jnp.zeros_like(acc_ref)
        sweep(False)
        m = jnp.max(mx_ref[...], axis=-1, keepdims=True)
        m_hi = m.astype(BF16)
        m_lo = (m - m_hi.astype(F32)).astype(BF16)
        lane4 = lax.broadcasted_iota(jnp.int32, (N_HEADS * blk, LANES), 1)
        old = qa_ref[:, LANES:2 * LANES]
        qa_ref[:, LANES:2 * LANES] = jnp.where(lane4 == POS_ONE, -m_hi, jnp.where(lane4 == POS_ONE + 1, -m_lo, old))
        sweep(True)
        acc = acc_ref[...]
        return acc[:, 0:LANES] / acc[:, LANES:2 * LANES]

    def sel_sweep(second):
        def body(j, c):
            process(ks_ref, vs_ref, j, None, second)
            return c
        lax.fori_loop(0, i, body, 0)
        process(ks_ref, vs_ref, i, causal, second)

    def win_sweep(second):
        @pl.when(i >= 2)
        def _():
            process(kw_ref, vw_ref, i - 2, in_window, second)

        @pl.when(i >= 1)
        def _():
            process(kw_ref, vw_ref, i - 1, None, second)

        process(kw_ref, vw_ref, i, causal, second)

    o_sel = attend(sel_sweep, [jnp.where(in_sel, sel_bias, coefs[h]) for h in range(N_HEADS)])
    o_win = attend(win_sweep, coefs)

    g = jax.nn.sigmoid(g_ref[0].astype(F32))
    outs = []
    for h in range(N_HEADS):
        r = slice(h * blk, (h + 1) * blk)
        outs.append(g[:, 3 * h:3 * h + 1] * o_cmp[r] + g[:, 3 * h + 1:3 * h + 2] * o_sel[r]
                    + g[:, 3 * h + 2:3 * h + 3] * o_win[r])
    lo_half = _half_mask((blk, LANES), 0)
    o = jnp.concatenate([jnp.where(lo_half, outs[0], outs[1]), jnp.where(lo_half, outs[2], outs[3])], axis=1)
    _group_norm_store(o_ref, gm_ref, o)


def _native_sparse(proj, cmp_kv, gm, blk=256):
    b, s, _ = proj.shape
    n_cmp = (s - CMP_LEN) // CMP_STRIDE + 1
    n_sel = s // SEL_LEN
    assert cmp_kv.shape[1] == LANES and n_cmp <= LANES and SEL_LANE + n_sel <= LANES
    assert NSA_WINDOW == 2 * blk
    coef, pos = _position_tables(SLOPES_NSA, s, blk, sel_blocks=True)
    slab = lambda c: pl.BlockSpec((1, s, LANES), lambda bi, i: (bi, 0, c))
    rows = N_HEADS * blk
    return pl.pallas_call(
        functools.partial(_nsa_kernel, blk=blk, n_cmp=n_cmp, n_sel=n_sel),
        grid=(b, s // blk),
        in_specs=[pl.BlockSpec((1, blk, 256), lambda bi, i: (bi, i, COL_QN)),
                  pl.BlockSpec((1, blk, LANES), lambda bi, i: (bi, i, COL_GN)),
                  pl.BlockSpec((1, LANES, 2 * LANES), lambda bi, i: (bi, 0, 0)),
                  slab(COL_KS), slab(COL_VS), slab(COL_KW), slab(COL_VW),
                  pl.BlockSpec((s, LANES), lambda bi, i: (0, 0)),
                  pl.BlockSpec((N_HEADS, LANES), lambda bi, i: (0, 0)),
                  pl.BlockSpec((1, 256), lambda bi, i: (0, 0))],
        out_specs=pl.BlockSpec((1, blk, 256), lambda bi, i: (bi, i, 0)),
        out_shape=jax.ShapeDtypeStruct((b, s, 256), BF16),
        scratch_shapes=[pltpu.VMEM((rows, 2 * LANES), BF16), pltpu.VMEM((rows, LANES), F32),
                        pltpu.VMEM((rows, 2 * LANES), F32)],
        compiler_params=_cparams("parallel", "arbitrary"),
        name="native_sparse",
    )(proj, proj, cmp_kv, proj, proj, proj, proj, pos, coef, gm)


def _outproj_kernel(a_ref, b_ref, c_ref, d_ref, w_ref, x_ref, o_ref):
    acc = x_ref[...]
    for g, r in enumerate((a_ref, b_ref, c_ref, d_ref)):
        acc = acc + _nn(r[...], w_ref[g * GROUP_WIDTH:(g + 1) * GROUP_WIDTH, :])
    o_ref[...] = acc


def _outproj(mixes, w, x2, tm=512):
    m = x2.shape[0]
    grp = pl.BlockSpec((tm, GROUP_WIDTH), lambda i: (i, 0))
    return pl.pallas_call(
        _outproj_kernel,
        grid=(m // tm,),
        in_specs=[grp, grp, grp, grp, pl.BlockSpec((D_MODEL, D_MODEL), lambda i: (0, 0)),
                  pl.BlockSpec((tm, D_MODEL), lambda i: (i, 0))],
        out_specs=pl.BlockSpec((tm, D_MODEL), lambda i: (i, 0)),
        out_shape=jax.ShapeDtypeStruct((m, D_MODEL), F32),
        compiler_params=_cparams("parallel"),
        name="outproj",
    )(*mixes, w, x2)


def _mlp_kernel(x_ref, g_ref, wu_ref, wd_ref, gf_ref, o_ref, h_ref, acc_ref, *, final_norm):
    f = pl.program_id(1)

    @pl.when(f == 0)
    def _():
        h_ref[...] = (_rms(x_ref[...]) * g_ref[...]).astype(BF16)
        acc_ref[...] = jnp.zeros_like(acc_ref)

    u = jnp.maximum(_nn(h_ref[...], wu_ref[...]), 0.0)
    acc_ref[...] += _nn((u * u).astype(BF16), wd_ref[...])

    @pl.when(f == pl.num_programs(1) - 1)
    def _():
        y = x_ref[...] + acc_ref[...]
        if final_norm:
            y = _rms(y) * gf_ref[...]
        o_ref[...] = y


def _mlp(x2, gain, wu, wd, gfinal, final_norm, tm=1024, tf=512):
    m = x2.shape[0]
    return pl.pallas_call(
        functools.partial(_mlp_kernel, final_norm=final_norm),
        grid=(m // tm, D_FF // tf),
        in_specs=[pl.BlockSpec((tm, D_MODEL), lambda i, f: (i, 0)),
                  pl.BlockSpec((1, D_MODEL), lambda i, f: (0, 0)),
                  pl.BlockSpec((D_MODEL, tf), lambda i, f: (0, f)),
                  pl.BlockSpec((tf, D_MODEL), lambda i, f: (f, 0)),
                  pl.BlockSpec((1, D_MODEL), lambda i, f: (0, 0))],
        out_specs=pl.BlockSpec((tm, D_MODEL), lambda i, f: (i, 0)),
        out_shape=jax.ShapeDtypeStruct((m, D_MODEL), F32),
        scratch_shapes=[pltpu.VMEM((tm, D_MODEL), BF16), pltpu.VMEM((tm, D_MODEL), F32)],
        compiler_params=_cparams("parallel", "arbitrary"),
        name="mlp",
    )(x2, gain, wu, wd, gfinal)


def _compress_weights(pe_k, w1_k, w2_k, pe_v, w1_v, w2_v):
    half = CMP_STRIDE
    d = HEAD_DIM

    def halves(w1):
        w = w1.reshape(CMP_LEN, d, CMP_HIDDEN)
        return w[:half], w[half:]

    ka, kb = halves(w1_k)
    va, vb = halves(w1_v)
    zero = jnp.zeros_like(ka)

    def merge(wk, wv):
        top = jnp.concatenate([wk, zero], axis=-1)
        bot = jnp.concatenate([zero, wv], axis=-1)
        return jnp.concatenate([top, bot], axis=1).reshape(half * 2 * d, 2 * CMP_HIDDEN).astype(BF16)

    w1a, w1b = merge(ka, va), merge(kb, vb)
    pe = jnp.concatenate([pe_k, pe_v], axis=-1)
    pea = pe[:half].reshape(1, half * 2 * d)
    peb = pe[half:].reshape(1, half * 2 * d)
    zk = jnp.zeros_like(w2_k)
    w2 = jnp.concatenate([jnp.concatenate([w2_k, w2_k, zk, zk], axis=1),
                          jnp.concatenate([zk, zk, w2_v, w2_v], axis=1)], axis=0).astype(BF16)
    return pea, peb, w1a, w1b, w2


def kernel(x, norm_attn, w_in, cmp_pe_k, cmp_w1_k, cmp_w2_k, cmp_pe_v, cmp_w1_v, cmp_w2_v, diff_lq1, diff_lk1,
           diff_lq2, diff_lk2, sinks, g_mix, w_out, norm_mlp, w_up, w_down, norm_final):
    b, s, d = x.shape
    m = b * s
    depth = w_in.shape[0]
    cols = jnp.asarray(_COLS)
    w_in_p = jnp.take(jnp.concatenate([w_in, jnp.zeros((depth, d, 1), w_in.dtype)], axis=2), cols,
                      axis=2).astype(BF16)
    w_out_b = w_out.astype(BF16)
    w_up_b = w_up.astype(BF16)
    w_down_b = w_down.astype(BF16)
    gfinal = norm_final.reshape(1, d)

    colscale = jnp.asarray(_column_scales())

    x2 = x.reshape(m, d)
    for l in range(depth):
        proj = _inproj(x2, norm_attn[l].reshape(1, d), w_in_p[l], colscale).reshape(b, s, NP)
        gm = g_mix[l].reshape(N_HEADS, 1, GROUP_WIDTH)

        chunks = proj[:, :, COL_KVC * LANES:(COL_KVC + 1) * LANES].reshape(b, s // CMP_STRIDE, CMP_STRIDE * LANES)
        cmp_kv = _compress(chunks, *_compress_weights(cmp_pe_k[l], cmp_w1_k[l], cmp_w2_k[l],
                                                      cmp_pe_v[l], cmp_w1_v[l], cmp_w2_v[l]))

        o_sb = _stick_breaking(proj, gm[0])
        o_nsa = _native_sparse(proj, cmp_kv, gm[1])
        lam_init = 0.8 - 0.6 * math.exp(-0.3 * l)
        dl = jnp.stack([diff_lq1[l], diff_lk1[l], diff_lq2[l], diff_lk2[l]]).astype(F32)
        o_diff = _differential(proj, dl, gm[2], lam_init)
        o_swa = _sliding_window(proj, sinks[l].astype(F32), gm[3])

        mixes = [o.reshape(m, GROUP_WIDTH) for o in (o_sb, o_nsa, o_diff, o_swa)]
        x2 = _outproj(mixes, w_out_b[l], x2)
        x2 = _mlp(x2, norm_mlp[l].reshape(1, d), w_up_b[l], w_down_b[l], gfinal, final_norm=(l == depth - 1))
    return x2.reshape(b, s, d)
```

```python
import functools
import math

import ml_dtypes
import numpy as np
import jax
import jax.numpy as jnp
from jax import lax
from jax.experimental import pallas as pl
from jax.experimental.pallas import tpu as pltpu

F32 = jnp.float32
BF16 = jnp.bfloat16

D_MODEL = 1024
DEPTH = 4
HEAD_DIM = 64
GROUP_WIDTH = 256
N_HEADS = 4
D_FF = 4 * D_MODEL
EPS = 1e-6
NEG_INF = -1e30
LOG2E = math.log2(math.e)
LANES = 128

CMP_LEN = 32
CMP_STRIDE = 16
CMP_HIDDEN = 128
SEL_LEN = 64
SEL_TOPN = 8
FORCE_SCORE = 1e9
NSA_WINDOW = 512
SWA_WINDOW = 128
DIFF_QK_DIM = 32

VMEM_LIMIT = 48 * 1024 * 1024

_ORIG = dict(qa=(0, 256), ka=(256, 256), va=(512, 256), qn=(768, 256), kcn=(1024, 64), vcn=(1088, 64),
             ksn=(1152, 64), vsn=(1216, 64), kwn=(1280, 64), vwn=(1344, 64), gn=(1408, 12),
             qc=(1420, 256), kc=(1676, 256), vc=(1932, 256), qd=(2188, 256), kd=(2444, 128), vd=(2572, 128))
IN_COLS = 2700


def _layout():
    cols = []

    def put(name, lo=0, n=None):
        start, size = _ORIG[name]
        n = size - lo if n is None else n
        cols.extend(range(start + lo, start + lo + n))

    def zeros(n):
        cols.extend([IN_COLS] * n)

    for nm in ("qa", "ka", "va", "qn", "kcn", "vcn"):
        put(nm)
    for nm in ("ksn", "vsn", "kwn", "vwn"):
        put(nm)
        put(nm)
    put("gn")
    zeros(LANES - 12)
    for nm in ("qc", "kc", "vc", "qd"):
        put(nm)
    for nm in ("kd", "vd"):
        put(nm, 0, 64)
        put(nm, 0, 64)
        put(nm, 64, 64)
        put(nm, 64, 64)
    return np.asarray(cols, np.int32)


_COLS = _layout()
NP = int(_COLS.shape[0])


def _column_scales():
    cs = np.ones((1, NP), np.float32)
    cs[0, 0:256] = HEAD_DIM ** -0.5 * LOG2E
    cs[0, 768:1024] = HEAD_DIM ** -0.5 * LOG2E
    cs[0, 1792:2048] = DIFF_QK_DIM ** -0.5 * LOG2E
    return cs


COL_QA, COL_KA, COL_VA = 0, 1, 2
COL_QN = 3
COL_KVC = 8
COL_KS, COL_VS, COL_KW, COL_VW, COL_GN = 9, 10, 11, 12, 13
COL_QC, COL_KC, COL_VC = 7, 8, 9
COL_QD, COL_KD, COL_VD = 10, 11, 12


def _alibi_slopes():
    m = 2.0 ** (-8.0 * np.arange(1, 13) / 12.0)
    m = m.astype(np.float32).reshape(4, 3)
    return [float(v) for v in m[:, 0]], [float(v) for v in m[:, 1]], [float(v) for v in m[:, 2]]


SLOPES_NSA, SLOPES_DIFF, SLOPES_SWA = _alibi_slopes()


def _nt(a, b):
    return lax.dot_general(a, b, (((1,), (1,)), ((), ())), preferred_element_type=F32)


def _nn(a, b):
    return jnp.dot(a, b, preferred_element_type=F32)


def _cparams(*sem):
    return pltpu.CompilerParams(dimension_semantics=sem, vmem_limit_bytes=VMEM_LIMIT)


def _rms(x):
    return x * lax.rsqrt(jnp.mean(x * x, axis=-1, keepdims=True) + EPS)


def _inproj_kernel(x_ref, g_ref, w_ref, cs_ref, o_ref):
    h = _rms(x_ref[...]) * g_ref[...]
    o_ref[...] = (_nn(h.astype(BF16), w_ref[...]) * cs_ref[...]).astype(BF16)


def _inproj(x2, gain, w, colscale, tm=512):
    m = x2.shape[0]
    return pl.pallas_call(
        _inproj_kernel,
        grid=(m // tm,),
        in_specs=[pl.BlockSpec((tm, D_MODEL), lambda i: (i, 0)),
                  pl.BlockSpec((1, D_MODEL), lambda i: (0, 0)),
                  pl.BlockSpec((D_MODEL, NP), lambda i: (0, 0)),
                  pl.BlockSpec((1, NP), lambda i: (0, 0))],
        out_specs=pl.BlockSpec((tm, NP), lambda i: (i, 0)),
        out_shape=jax.ShapeDtypeStruct((m, NP), BF16),
        compiler_params=_cparams("parallel"),
        name="inproj",
    )(x2, gain, w, colscale)


def _compress_kernel(c_ref, pea_ref, peb_ref, w1a_ref, w1b_ref, w2_ref, o_ref):
    c = c_ref[0].astype(F32)
    p = _nn((c + pea_ref[...]).astype(BF16), w1a_ref[...])
    r = _nn((c + peb_ref[...]).astype(BF16), w1b_ref[...])
    n = c.shape[0]
    r_next = pltpu.roll(r, n - 1, 0)
    hid = jax.nn.gelu(p + r_next)
    o_ref[0] = _nn(hid.astype(BF16), w2_ref[...]).astype(BF16)


def _compress(chunks, pea, peb, w1a, w1b, w2):
    b, n, w = chunks.shape
    full = lambda a: pl.BlockSpec(a.shape, lambda i: (0,) * a.ndim)
    return pl.pallas_call(
        _compress_kernel,
        grid=(b,),
        in_specs=[pl.BlockSpec((1, n, w), lambda i: (i, 0, 0)), full(pea), full(peb), full(w1a), full(w1b),
                  full(w2)],
        out_specs=pl.BlockSpec((1, n, 2 * LANES), lambda i: (i, 0, 0)),
        out_shape=jax.ShapeDtypeStruct((b, n, 2 * LANES), BF16),
        compiler_params=_cparams("parallel"),
        name="nsa_compress",
    )(chunks, pea, peb, w1a, w1b, w2)


def _half_mask(shape, half):
    lane = lax.broadcasted_iota(jnp.int32, shape, len(shape) - 1)
    return (lane // HEAD_DIM) == half


def _group_norm_store(o_ref, gm_ref, o):
    o_ref[0] = (_rms(o) * gm_ref[...]).astype(o_ref.dtype)


def _sb_kernel(q_ref, k_ref, v_ref, gm_ref, o_ref, acc_ref, carry_ref, lb_ref, hl_ref, tot_ref, *, blk):
    i = pl.program_id(1)
    q = q_ref[0]
    row = lax.broadcasted_iota(jnp.int32, (blk, blk), 0)
    col = lax.broadcasted_iota(jnp.int32, (blk, blk), 1)
    past = col < row
    u = jnp.where(row > col, 1.0, 0.0).astype(BF16)
    uu = jnp.concatenate([u, u], axis=0)
    qm = []
    for h in range(N_HEADS):
        slab = q[:, (h // 2) * LANES:(h // 2 + 1) * LANES]
        qm.append(jnp.where(_half_mask(slab.shape, h % 2), slab, jnp.zeros_like(slab)))
    qm = [jnp.concatenate(qm[0:2], axis=0), jnp.concatenate(qm[2:4], axis=0)]
    acc_ref[...] = jnp.zeros_like(acc_ref)
    carry_ref[...] = jnp.zeros_like(carry_ref)

    def per_head(x, fn):
        return jnp.concatenate([fn(x[k * blk:(k + 1) * blk]) for k in range(x.shape[0] // blk)], axis=0)

    def stage_a(j, slot, diag):
        rows = pl.ds(pl.multiple_of(j * blk, blk), blk)
        for pr in range(2):
            r2 = slice(2 * pr * blk, 2 * (pr + 1) * blk)
            z = _nt(qm[pr], k_ref[0, rows, pr * LANES:(pr + 1) * LANES])
            sp = jnp.log2(1.0 + jnp.exp2(-jnp.abs(z)))
            lb = jnp.minimum(z, 0.0) - sp
            lk = lb - z
            if diag:
                lk = per_head(lk, lambda t: jnp.where(past, t, 0.0))
                lb = per_head(lb, lambda t: jnp.where(past, t, NEG_INF))
            hi = lk.astype(BF16)
            lb_ref[slot, r2, :] = lb
            hl_ref[slot, r2, 0:blk] = hi
            hl_ref[slot, r2, blk:2 * blk] = (lk - hi.astype(F32)).astype(BF16)
            tot_ref[slot, r2, :] = jnp.broadcast_to(jnp.sum(lk, axis=-1, keepdims=True), (2 * blk, LANES))

    def stage_b(j, slot):
        rows = pl.ds(pl.multiple_of(j * blk, blk), blk)
        cs = _nn(hl_ref[slot], uu)
        carry = carry_ref[...]
        a = jnp.exp2(lb_ref[slot] + cs + jnp.concatenate([carry] * (blk // LANES), axis=1)).astype(BF16)
        for pr in range(2):
            r2 = slice(2 * pr * blk, 2 * (pr + 1) * blk)
            acc_ref[r2, :] += _nn(a[r2], v_ref[0, rows, pr * LANES:(pr + 1) * LANES])
        carry_ref[...] = carry + tot_ref[slot]

    stage_a(i, 0, True)

    def step(t, slot):
        stage_b(i - t, slot)
        stage_a(i - 1 - t, 1 - slot, False)

    def body(tt, c):
        step(2 * tt, 0)
        step(2 * tt + 1, 1)
        return c

    lax.fori_loop(0, i // 2, body, 0)

    @pl.when(i % 2 == 1)
    def _():
        step(i - 1, 0)
        stage_b(0, 1)

    @pl.when(i % 2 == 0)
    def _():
        stage_b(0, 0)

    lo_half = _half_mask((blk, LANES), 0)
    acc = [acc_ref[h * blk:(h + 1) * blk, :] for h in range(N_HEADS)]
    o = jnp.concatenate([jnp.where(lo_half, acc[0], acc[1]), jnp.where(lo_half, acc[2], acc[3])], axis=1)
    _group_norm_store(o_ref, gm_ref, o)


def _stick_breaking(proj, gm, blk=256):
    b, s, _ = proj.shape
    stage = lambda w, dt: pltpu.VMEM((2, N_HEADS * blk, w), dt)
    return pl.pallas_call(
        functools.partial(_sb_kernel, blk=blk),
        grid=(b, s // blk),
        in_specs=[pl.BlockSpec((1, blk, 256), lambda bi, i: (bi, i, COL_QA)),
                  pl.BlockSpec((1, s, 256), lambda bi, i: (bi, 0, COL_KA)),
                  pl.BlockSpec((1, s, 256), lambda bi, i: (bi, 0, COL_VA)),
                  pl.BlockSpec((1, 256), lambda bi, i: (0, 0))],
        out_specs=pl.BlockSpec((1, blk, 256), lambda bi, i: (bi, i, 0)),
        out_shape=jax.ShapeDtypeStruct((b, s, 256), BF16),
        scratch_shapes=[pltpu.VMEM((N_HEADS * blk, LANES), F32), pltpu.VMEM((N_HEADS * blk, LANES), F32),
                        stage(blk, F32), stage(2 * blk, BF16), stage(LANES, F32)],
        compiler_params=_cparams("parallel", "arbitrary"),
        name="stick_breaking",
    )(proj, proj, proj, gm)


N_MAPS = 2 * N_HEADS
POS_LOCAL, POS_BLOCK, POS_ONE = 0, 3, 6
SEL_LANE = 8
MASK_BIAS = -2.0 ** 100


def _bf16_pieces(x, n):
    out = []
    r = np.float32(x)
    for _ in range(n):
        p = np.float32(r.astype(ml_dtypes.bfloat16))
        out.append(float(p))
        r = np.float32(r - p)
    return out


def _position_tables(slopes, s, blk, sel_blocks=False):
    coef = np.zeros((N_HEADS, LANES), np.float32)
    for h in range(N_HEADS):
        pieces = _bf16_pieces(slopes[h] * LOG2E, 3)
        coef[h, POS_LOCAL:POS_LOCAL + 3] = pieces
        coef[h, POS_BLOCK:POS_BLOCK + 3] = [blk * p for p in pieces]
    pos = np.zeros((s, LANES), np.float32)
    idx = np.arange(s)
    pos[:, POS_LOCAL:POS_LOCAL + 3] = (idx % blk)[:, None]
    pos[:, POS_BLOCK:POS_BLOCK + 3] = (idx // blk)[:, None]
    pos[:, POS_ONE:POS_ONE + 2] = 1.0
    if sel_blocks:
        pos[idx, SEL_LANE + idx // SEL_LEN] = 1.0
    return jnp.asarray(coef), jnp.asarray(pos, BF16)


def _diff_kernel(q_ref, k_ref, v_ref, pos_ref, coef_ref, dl_ref, gm_ref, o_ref, qa_ref, mx_ref, acc_ref,
                 *, blk, lam_init):
    i = pl.program_id(1)
    q = q_ref[0]
    row = lax.broadcasted_iota(jnp.int32, (blk, blk), 0)
    col = lax.broadcasted_iota(jnp.int32, (blk, blk), 1)
    causal = col <= row
    lane = lax.broadcasted_iota(jnp.int32, (blk, LANES), 1)
    for h in range(N_HEADS):
        slab = q[:, (h // 2) * LANES:(h // 2 + 1) * LANES]
        coef = jnp.broadcast_to(coef_ref[h:h + 1, :], (blk, LANES)).astype(BF16)
        for c in range(2):
            r = slice(((2 * h + c) % 4) * blk, ((2 * h + c) % 4 + 1) * blk)
            qa_ref[h // 2, r, 0:LANES] = jnp.where((lane // DIFF_QK_DIM) == (h % 2) * 2 + c, slab,
                                                   jnp.zeros_like(slab))
            qa_ref[h // 2, r, LANES:2 * LANES] = coef
    ones = jnp.ones((blk, LANES), BF16)

    def per_map(x, fn):
        return jnp.concatenate([fn(x[k * blk:(k + 1) * blk]) for k in range(4)], axis=0)

    def sweep(j, diag, second):
        rows = pl.ds(pl.multiple_of(j * blk, blk), blk)
        posb = pos_ref[rows, :]
        for pr in range(2):
            lanes = slice(pr * LANES, (pr + 1) * LANES)
            ka = jnp.concatenate([k_ref[0, rows, lanes], posb], axis=1)
            z = _nt(qa_ref[pr], ka)
            if second:
                p = jnp.exp2(z)
                if diag:
                    p = per_map(p, lambda t: jnp.where(causal, t, 0.0))
                vo = jnp.concatenate([v_ref[0, rows, lanes], ones], axis=1)
                upd = _nn(p.astype(BF16), vo)
                acc_ref[pr] = upd if diag else acc_ref[pr] + upd
            else:
                if diag:
                    z = per_map(z, lambda t: jnp.where(causal, t, NEG_INF))
                zm = z[:, 0:LANES]
                for g in range(1, blk // LANES):
                    zm = jnp.maximum(zm, z[:, g * LANES:(g + 1) * LANES])
                mx_ref[pr] = zm if diag else jnp.maximum(mx_ref[pr], zm)

    def run(second):
        sweep(i, True, second)

        def body(j, c):
            sweep(j, False, second)
            return c
        lax.fori_loop(0, i, body, 0)

    run(False)
    lane4 = lax.broadcasted_iota(jnp.int32, (4 * blk, LANES), 1)
    for pr in range(2):
        m = jnp.max(mx_ref[pr], axis=-1, keepdims=True)
        m_hi = m.astype(BF16)
        m_lo = (m - m_hi.astype(F32)).astype(BF16)
        old = qa_ref[pr, :, LANES:2 * LANES]
        qa_ref[pr, :, LANES:2 * LANES] = jnp.where(lane4 == POS_ONE, -m_hi, jnp.where(lane4 == POS_ONE + 1, -m_lo, old))
    run(True)

    dl = dl_ref[...]
    s1 = jnp.sum(dl[0:1] * dl[1:2], axis=-1, keepdims=True)
    s2 = jnp.sum(dl[2:3] * dl[3:4], axis=-1, keepdims=True)
    lam = jnp.exp(s1) - jnp.exp(s2) + lam_init
    outs = []
    for h in range(N_HEADS):
        k1 = (2 * h) % 4
        a1 = acc_ref[h // 2, k1 * blk:(k1 + 1) * blk, :]
        a2 = acc_ref[h // 2, (k1 + 1) * blk:(k1 + 2) * blk, :]
        d = a1[:, :LANES] / a1[:, LANES:] - lam * (a2[:, :LANES] / a2[:, LANES:])
        mine = _half_mask(d.shape, h % 2)
        ms = jnp.sum(jnp.where(mine, d * d, 0.0), axis=-1, keepdims=True) * (1.0 / HEAD_DIM)
        outs.append(d * lax.rsqrt(ms + EPS) * (1.0 - lam_init))
    lo_half = _half_mask((blk, LANES), 0)
    o = jnp.concatenate([jnp.where(lo_half, outs[0], outs[1]), jnp.where(lo_half, outs[2], outs[3])], axis=1)
    o_ref[0] = (o * gm_ref[...]).astype(o_ref.dtype)


def _differential(proj, dl, gm, lam_init, blk=256):
    b, s, _ = proj.shape
    coef, pos = _position_tables(SLOPES_DIFF, s, blk)
    return pl.pallas_call(
        functools.partial(_diff_kernel, blk=blk, lam_init=lam_init),
        grid=(b, s // blk),
        in_specs=[pl.BlockSpec((1, blk, 256), lambda bi, i: (bi, i, COL_QC)),
                  pl.BlockSpec((1, s, 256), lambda bi, i: (bi, 0, COL_KC)),
                  pl.BlockSpec((1, s, 256), lambda bi, i: (bi, 0, COL_VC)),
                  pl.BlockSpec((s, LANES), lambda bi, i: (0, 0)),
                  pl.BlockSpec((N_HEADS, LANES), lambda bi, i: (0, 0)),
                  pl.BlockSpec((4, DIFF_QK_DIM), lambda bi, i: (0, 0)),
                  pl.BlockSpec((1, 256), lambda bi, i: (0, 0))],
        out_specs=pl.BlockSpec((1, blk, 256), lambda bi, i: (bi, i, 0)),
        out_shape=jax.ShapeDtypeStruct((b, s, 256), BF16),
        scratch_shapes=[pltpu.VMEM((2, 4 * blk, 2 * LANES), BF16), pltpu.VMEM((2, 4 * blk, LANES), F32),
                        pltpu.VMEM((2, 4 * blk, 2 * LANES), F32)],
        compiler_params=_cparams("parallel", "arbitrary"),
        name="differential",
    )(proj, proj, proj, pos, coef, dl, gm)


def _swa_kernel(q_ref, k_ref, v_ref, sink_ref, gm_ref, o_ref, *, blk):
    i = pl.program_id(1)
    q = q_ref[0]
    first = jnp.maximum(i - 1, 0)
    start = pl.multiple_of(first * blk, blk)
    row = lax.broadcasted_iota(jnp.int32, (blk, 2 * blk), 0)
    col = lax.broadcasted_iota(jnp.int32, (blk, 2 * blk), 1)
    dist = (row - col) + (i - first) * blk
    mask = (dist >= 0) & (dist < SWA_WINDOW)
    distf = dist.astype(F32)
    outs = []
    for h in range(N_HEADS):
        lanes = slice((h // 2) * LANES, (h // 2 + 1) * LANES)
        slab = q[:, lanes]
        qm = jnp.where(_half_mask(slab.shape, h % 2), slab, jnp.zeros_like(slab))
        kb = k_ref[0, pl.ds(start, 2 * blk), lanes]
        vb = v_ref[0, pl.ds(start, 2 * blk), lanes]
        z = _nt(qm, kb) * (HEAD_DIM ** -0.5) - SLOPES_SWA[h] * distf
        z = jnp.where(mask, z, NEG_INF)
        sink = sink_ref[h]
        m = jnp.maximum(jnp.max(z, axis=-1, keepdims=True), sink)
        p = jnp.exp(z - m)
        l = jnp.sum(p, axis=-1, keepdims=True) + jnp.exp(sink - m)
        outs.append(_nn(p.astype(BF16), vb) / l)
    lo_half = _half_mask((blk, LANES), 0)
    o = jnp.concatenate([jnp.where(lo_half, outs[0], outs[1]), jnp.where(lo_half, outs[2], outs[3])], axis=1)
    _group_norm_store(o_ref, gm_ref, o)


def _sliding_window(proj, sinks, gm, blk=128):
    b, s, _ = proj.shape
    return pl.pallas_call(
        functools.partial(_swa_kernel, blk=blk),
        grid=(b, s // blk),
        in_specs=[pl.BlockSpec((1, blk, 256), lambda bi, i: (bi, i, COL_QD)),
                  pl.BlockSpec((1, s, 256), lambda bi, i: (bi, 0, COL_KD)),
                  pl.BlockSpec((1, s, 256), lambda bi, i: (bi, 0, COL_VD)),
                  pl.BlockSpec(memory_space=pltpu.SMEM),
                  pl.BlockSpec((1, 256), lambda bi, i: (0, 0))],
        out_specs=pl.BlockSpec((1, blk, 256), lambda bi, i: (bi, i, 0)),
        out_shape=jax.ShapeDtypeStruct((b, s, 256), BF16),
        compiler_params=_cparams("parallel", "arbitrary"),
        name="sliding_window",
    )(proj, proj, proj, sinks, gm)


def _nsa_kernel(q_ref, g_ref, cmp_ref, ks_ref, vs_ref, kw_ref, vw_ref, pos_ref, coef_ref, gm_ref, o_ref,
                qa_ref, mx_ref, acc_ref, *, blk, n_cmp, n_sel):
    i = pl.program_id(1)
    q = q_ref[0]
    lane = lax.broadcasted_iota(jnp.int32, (blk, LANES), 1)
    for h in range(N_HEADS):
        slab = q[:, (h // 2) * LANES:(h // 2 + 1) * LANES]
        qa_ref[h * blk:(h + 1) * blk, 0:LANES] = jnp.where(_half_mask(slab.shape, h % 2), slab, jnp.zeros_like(slab))
    qs = qa_ref[:, 0:LANES]
    coefs = [jnp.broadcast_to(coef_ref[h:h + 1, :], (blk, LANES)) for h in range(N_HEADS)]

    tq = i * blk + lax.broadcasted_iota(jnp.int32, (blk, LANES), 0)

    kc = cmp_ref[0, :, 0:LANES]
    vc = cmp_ref[0, :, LANES:2 * LANES]
    dist_c = tq - (CMP_STRIDE * lane + CMP_LEN - 1)
    ok_c = (dist_c >= 0) & (lane < n_cmp)
    dist_cf = dist_c.astype(F32)
    bias_c = jnp.concatenate([jnp.where(ok_c, (-SLOPES_NSA[h] * LOG2E) * dist_cf, NEG_INF)
                              for h in range(N_HEADS)], axis=0)
    z = _nt(qs, kc) + bias_c
    p = jnp.exp2(z - jnp.max(z, axis=-1, keepdims=True))
    p = p / jnp.sum(p, axis=-1, keepdims=True)
    p = jnp.where(bias_c > 0.5 * NEG_INF, p, 0.0).astype(BF16)
    o_cmp = _nn(p, vc)

    ci = lax.broadcasted_iota(jnp.int32, (LANES, LANES), 0)
    cj = lax.broadcasted_iota(jnp.int32, (LANES, LANES), 1)
    to_sel = jnp.where((ci * CMP_STRIDE) // SEL_LEN + SEL_LANE == cj, 1.0, 0.0).astype(BF16)
    imp = _nn(p[0:blk], to_sel)
    for h in range(1, N_HEADS):
        imp = imp + _nn(p[h * blk:(h + 1) * blk], to_sel)
    blk_id = lane - SEL_LANE
    in_sel = (blk_id >= 0) & (blk_id < n_sel)
    cur = tq // SEL_LEN
    forced = (blk_id == 0) | (blk_id == cur) | (blk_id == cur - 1)
    score = jnp.where(blk_id > cur, NEG_INF, jnp.where(forced, FORCE_SCORE, imp))
    score = jnp.where(in_sel, score, -3.0e38)
    sel = jnp.zeros((blk, LANES), F32)
    lanef = lane.astype(F32)
    for _ in range(min(SEL_TOPN, n_sel)):
        best = jnp.max(score, axis=-1, keepdims=True)
        idx = jnp.min(jnp.where(score == best, lanef, float(LANES)), axis=-1, keepdims=True)
        hit = lanef == idx
        sel = jnp.where(hit, 1.0, sel)
        score = jnp.where(hit, -3.4e38, score)
    sel_bias = jnp.where(sel > 0.5, 0.0, MASK_BIAS)

    row = lax.broadcasted_iota(jnp.int32, (blk, blk), 0)
    col = lax.broadcasted_iota(jnp.int32, (blk, blk), 1)
    causal = col <= row
    in_window = col > row
    ones = jnp.ones((blk, LANES), BF16)

    def process(k_ref, v_ref, j, mask, second):
        rows = pl.ds(pl.multiple_of(j * blk, blk), blk)
        ka = jnp.concatenate([k_ref[0, rows, :], pos_ref[rows, :]], axis=1)
        zz = _nt(qa_ref[...], ka)
        if second:
            pp = jnp.exp2(zz)
            if mask is not None:
                pp = jnp.concatenate([jnp.where(mask, pp[h * blk:(h + 1) * blk], 0.0) for h in range(N_HEADS)], axis=0)
            vo = jnp.concatenate([v_ref[0, rows, :], ones], axis=1)
            acc_ref[...] += _nn(pp.astype(BF16), vo)
        else:
            if mask is not None:
                zz = jnp.concatenate([jnp.where(mask, zz[h * blk:(h + 1) * blk], NEG_INF) for h in range(N_HEADS)],
                                     axis=0)
            zm = zz[:, 0:LANES]
            for g in range(1, blk // LANES):
                zm = jnp.maximum(zm, zz[:, g * LANES:(g + 1) * LANES])
            mx_ref[...] = jnp.maximum(mx_ref[...], zm)

    def attend(sweep, aug):
        for h in range(N_HEADS):
            qa_ref[h * blk:(h + 1) * blk, LANES:2 * LANES] = aug[h].astype(BF16)
        mx_ref[...] = jnp.full(mx_ref.shape, NEG_INF, F32)
        acc_ref[...] = jnp.zeros_like(acc_ref)
        sweep(False)
        m = jnp.max(mx_ref[...], axis=-1, keepdims=True)
        m_hi = m.astype(BF16)
        m_lo = (m - m_hi.astype(F32)).astype(BF16)
        lane4 = lax.broadcasted_iota(jnp.int32, (N_HEADS * blk, LANES), 1)
        old = qa_ref[:, LANES:2 * LANES]
        qa_ref[:, LANES:2 * LANES] = jnp.where(lane4 == POS_ONE, -m_hi, jnp.where(lane4 == POS_ONE + 1, -m_lo, old))
        sweep(True)
        acc = acc_ref[...]
        return acc[:, 0:LANES] / acc[:, LANES:2 * LANES]

    def sel_sweep(second):
        def body(j, c):
            process(ks_ref, vs_ref, j, None, second)
            return c
        lax.fori_loop(0, i, body, 0)
        process(ks_ref, vs_ref, i, causal, second)

    def win_sweep(second):
        @pl.when(i >= 2)
        def _():
            process(kw_ref, vw_ref, i - 2, in_window, second)

        @pl.when(i >= 1)
        def _():
            process(kw_ref, vw_ref, i - 1, None, second)

        process(kw_ref, vw_ref, i, causal, second)

    o_sel = attend(sel_sweep, [jnp.where(in_sel, sel_bias, coefs[h]) for h in range(N_HEADS)])
    o_win = attend(win_sweep, coefs)

    g = jax.nn.sigmoid(g_ref[0].astype(F32))
    outs = []
    for h in range(N_HEADS):
        r = slice(h * blk, (h + 1) * blk)
        outs.append(g[:, 3 * h:3 * h + 1] * o_cmp[r] + g[:, 3 * h + 1:3 * h + 2] * o_sel[r]
                    + g[:, 3 * h + 2:3 * h + 3] * o_win[r])
    lo_half = _half_mask((blk, LANES), 0)
    o = jnp.concatenate([jnp.where(lo_half, outs[0], outs[1]), jnp.where(lo_half, outs[2], outs[3])], axis=1)
    _group_norm_store(o_ref, gm_ref, o)


def _native_sparse(proj, cmp_kv, gm, blk=256):
    b, s, _ = proj.shape
    n_cmp = (s - CMP_LEN) // CMP_STRIDE + 1
    n_sel = s // SEL_LEN
    assert cmp_kv.shape[1] == LANES and n_cmp <= LANES and SEL_LANE + n_sel <= LANES
    assert NSA_WINDOW == 2 * blk
    coef, pos = _position_tables(SLOPES_NSA, s, blk, sel_blocks=True)
    slab = lambda c: pl.BlockSpec((1, s, LANES), lambda bi, i: (bi, 0, c))
    rows = N_HEADS * blk
    return pl.pallas_call(
        functools.partial(_nsa_kernel, blk=blk, n_cmp=n_cmp, n_sel=n_sel),
        grid=(b, s // blk),
        in_specs=[pl.BlockSpec((1, blk, 256), lambda bi, i: (bi, i, COL_QN)),
                  pl.BlockSpec((1, blk, LANES), lambda bi, i: (bi, i, COL_GN)),
                  pl.BlockSpec((1, LANES, 2 * LANES), lambda bi, i: (bi, 0, 0)),
                  slab(COL_KS), slab(COL_VS), slab(COL_KW), slab(COL_VW),
                  pl.BlockSpec((s, LANES), lambda bi, i: (0, 0)),
                  pl.BlockSpec((N_HEADS, LANES), lambda bi, i: (0, 0)),
                  pl.BlockSpec((1, 256), lambda bi, i: (0, 0))],
        out_specs=pl.BlockSpec((1, blk, 256), lambda bi, i: (bi, i, 0)),
        out_shape=jax.ShapeDtypeStruct((b, s, 256), BF16),
        scratch_shapes=[pltpu.VMEM((rows, 2 * LANES), BF16), pltpu.VMEM((rows, LANES), F32),
                        pltpu.VMEM((rows, 2 * LANES), F32)],
        compiler_params=_cparams("parallel", "arbitrary"),
        name="native_sparse",
    )(proj, proj, cmp_kv, proj, proj, proj, proj, pos, coef, gm)


def _outproj_kernel(a_ref, b_ref, c_ref, d_ref, w_ref, x_ref, o_ref):
    acc = x_ref[...]
    for g, r in enumerate((a_ref, b_ref, c_ref, d_ref)):
        acc = acc + _nn(r[...], w_ref[g * GROUP_WIDTH:(g + 1) * GROUP_WIDTH, :])
    o_ref[...] = acc


def _outproj(mixes, w, x2, tm=512):
    m = x2.shape[0]
    grp = pl.BlockSpec((tm, GROUP_WIDTH), lambda i: (i, 0))
    return pl.pallas_call(
        _outproj_kernel,
        grid=(m // tm,),
        in_specs=[grp, grp, grp, grp, pl.BlockSpec((D_MODEL, D_MODEL), lambda i: (0, 0)),
                  pl.BlockSpec((tm, D_MODEL), lambda i: (i, 0))],
        out_specs=pl.BlockSpec((tm, D_MODEL), lambda i: (i, 0)),
        out_shape=jax.ShapeDtypeStruct((m, D_MODEL), F32),
        compiler_params=_cparams("parallel"),
        name="outproj",
    )(*mixes, w, x2)


def _mlp_kernel(x_ref, g_ref, wu_ref, wd_ref, gf_ref, o_ref, h_ref, acc_ref, *, final_norm):
    f = pl.program_id(1)

    @pl.when(f == 0)
    def _():
        h_ref[...] = (_rms(x_ref[...]) * g_ref[...]).astype(BF16)
        acc_ref[...] = jnp.zeros_like(acc_ref)

    u = jnp.maximum(_nn(h_ref[...], wu_ref[...]), 0.0)
    acc_ref[...] += _nn((u * u).astype(BF16), wd_ref[...])

    @pl.when(f == pl.num_programs(1) - 1)
    def _():
        y = x_ref[...] + acc_ref[...]
        if final_norm:
            y = _rms(y) * gf_ref[...]
        o_ref[...] = y


def _mlp(x2, gain, wu, wd, gfinal, final_norm, tm=1024, tf=1024):
    m = x2.shape[0]
    return pl.pallas_call(
        functools.partial(_mlp_kernel, final_norm=final_norm),
        grid=(m // tm, D_FF // tf),
        in_specs=[pl.BlockSpec((tm, D_MODEL), lambda i, f: (i, 0)),
                  pl.BlockSpec((1, D_MODEL), lambda i, f: (0, 0)),
                  pl.BlockSpec((D_MODEL, tf), lambda i, f: (0, f)),
                  pl.BlockSpec((tf, D_MODEL), lambda i, f: (f, 0)),
                  pl.BlockSpec((1, D_MODEL), lambda i, f: (0, 0))],
        out_specs=pl.BlockSpec((tm, D_MODEL), lambda i, f: (i, 0)),
        out_shape=jax.ShapeDtypeStruct((m, D_MODEL), F32),
        scratch_shapes=[pltpu.VMEM((tm, D_MODEL), BF16), pltpu.VMEM((tm, D_MODEL), F32)],
        compiler_params=_cparams("parallel", "arbitrary"),
        name="mlp",
    )(x2, gain, wu, wd, gfinal)


def _compress_weights(pe_k, w1_k, w2_k, pe_v, w1_v, w2_v):
    half = CMP_STRIDE
    d = HEAD_DIM

    def halves(w1):
        w = w1.reshape(CMP_LEN, d, CMP_HIDDEN)
        return w[:half], w[half:]

    ka, kb = halves(w1_k)
    va, vb = halves(w1_v)
    zero = jnp.zeros_like(ka)

    def merge(wk, wv):
        top = jnp.concatenate([wk, zero], axis=-1)
        bot = jnp.concatenate([zero, wv], axis=-1)
        return jnp.concatenate([top, bot], axis=1).reshape(half * 2 * d, 2 * CMP_HIDDEN).astype(BF16)

    w1a, w1b = merge(ka, va), merge(kb, vb)
    pe = jnp.concatenate([pe_k, pe_v], axis=-1)
    pea = pe[:half].reshape(1, half * 2 * d)
    peb = pe[half:].reshape(1, half * 2 * d)
    zk = jnp.zeros_like(w2_k)
    w2 = jnp.concatenate([jnp.concatenate([w2_k, w2_k, zk, zk], axis=1),
                          jnp.concatenate([zk, zk, w2_v, w2_v], axis=1)], axis=0).astype(BF16)
    return pea, peb, w1a, w1b, w2


def kernel(x, norm_attn, w_in, cmp_pe_k, cmp_w1_k, cmp_w2_k, cmp_pe_v, cmp_w1_v, cmp_w2_v, diff_lq1, diff_lk1,
           diff_lq2, diff_lk2, sinks, g_mix, w_out, norm_mlp, w_up, w_down, norm_final):
    b, s, d = x.shape
    m = b * s
    depth = w_in.shape[0]
    cols = jnp.asarray(_COLS)
    w_in_p = jnp.take(jnp.concatenate([w_in, jnp.zeros((depth, d, 1), w_in.dtype)], axis=2), cols,
                      axis=2).astype(BF16)
    w_out_b = w_out.astype(BF16)
    w_up_b = w_up.astype(BF16)
    w_down_b = w_down.astype(BF16)
    gfinal = norm_final.reshape(1, d)

    colscale = jnp.asarray(_column_scales())

    x2 = x.reshape(m, d)
    for l in range(depth):
        proj = _inproj(x2, norm_attn[l].reshape(1, d), w_in_p[l], colscale).reshape(b, s, NP)
        gm = g_mix[l].reshape(N_HEADS, 1, GROUP_WIDTH)

        chunks = proj[:, :, COL_KVC * LANES:(COL_KVC + 1) * LANES].reshape(b, s // CMP_STRIDE, CMP_STRIDE * LANES)
        cmp_kv = _compress(chunks, *_compress_weights(cmp_pe_k[l], cmp_w1_k[l], cmp_w2_k[l],
                                                      cmp_pe_v[l], cmp_w1_v[l], cmp_w2_v[l]))

        o_sb = _stick_breaking(proj, gm[0])
        o_nsa = _native_sparse(proj, cmp_kv, gm[1])
        lam_init = 0.8 - 0.6 * math.exp(-0.3 * l)
        dl = jnp.stack([diff_lq1[l], diff_lk1[l], diff_lq2[l], diff_lk2[l]]).astype(F32)
        o_diff = _differential(proj, dl, gm[2], lam_init)
        o_swa = _sliding_window(proj, sinks[l].astype(F32), gm[3])

        mixes = [o.reshape(m, GROUP_WIDTH) for o in (o_sb, o_nsa, o_diff, o_swa)]
        x2 = _outproj(mixes, w_out_b[l], x2)
        x2 = _mlp(x2, norm_mlp[l].reshape(1, d), w_up_b[l], w_down_b[l], gfinal, final_norm=(l == depth - 1))
    return x2.reshape(b, s, d)
```

```python
import functools
import math

import ml_dtypes
import numpy as np
import jax
import jax.numpy as jnp
from jax import lax
from jax.experimental import pallas as pl
from jax.experimental.pallas import tpu as pltpu

F32 = jnp.float32
BF16 = jnp.bfloat16

D_MODEL = 1024
DEPTH = 4
HEAD_DIM = 64
GROUP_WIDTH = 256
N_HEADS = 4
D_FF = 4 * D_MODEL
EPS = 1e-6
NEG_INF = -1e30
LOG2E = math.log2(math.e)
LANES = 128

CMP_LEN = 32
CMP_STRIDE = 16
CMP_HIDDEN = 128
SEL_LEN = 64
SEL_TOPN = 8
FORCE_SCORE = 1e9
NSA_WINDOW = 512
SWA_WINDOW = 128
DIFF_QK_DIM = 32

VMEM_LIMIT = 48 * 1024 * 1024

_ORIG = dict(qa=(0, 256), ka=(256, 256), va=(512, 256), qn=(768, 256), kcn=(1024, 64), vcn=(1088, 64),
             ksn=(1152, 64), vsn=(1216, 64), kwn=(1280, 64), vwn=(1344, 64), gn=(1408, 12),
             qc=(1420, 256), kc=(1676, 256), vc=(1932, 256), qd=(2188, 256), kd=(2444, 128), vd=(2572, 128))
IN_COLS = 2700


def _layout():
    cols = []

    def put(name, lo=0, n=None):
        start, size = _ORIG[name]
        n = size - lo if n is None else n
        cols.extend(range(start + lo, start + lo + n))

    def zeros(n):
        cols.extend([IN_COLS] * n)

    for nm in ("qa", "ka", "va", "qn", "kcn", "vcn"):
        put(nm)
    for nm in ("ksn", "vsn", "kwn", "vwn"):
        put(nm)
        put(nm)
    put("gn")
    zeros(LANES - 12)
    for nm in ("qc", "kc", "vc", "qd"):
        put(nm)
    for nm in ("kd", "vd"):
        put(nm, 0, 64)
        put(nm, 0, 64)
        put(nm, 64, 64)
        put(nm, 64, 64)
    return np.asarray(cols, np.int32)


_COLS = _layout()
NP = int(_COLS.shape[0])


def _column_scales():
    cs = np.ones((1, NP), np.float32)
    cs[0, 0:256] = HEAD_DIM ** -0.5 * LOG2E
    cs[0, 768:1024] = HEAD_DIM ** -0.5 * LOG2E
    cs[0, 1792:2048] = DIFF_QK_DIM ** -0.5 * LOG2E
    return cs


COL_QA, COL_KA, COL_VA = 0, 1, 2
COL_QN = 3
COL_KVC = 8
COL_KS, COL_VS, COL_KW, COL_VW, COL_GN = 9, 10, 11, 12, 13
COL_QC, COL_KC, COL_VC = 7, 8, 9
COL_QD, COL_KD, COL_VD = 10, 11, 12


def _alibi_slopes():
    m = 2.0 ** (-8.0 * np.arange(1, 13) / 12.0)
    m = m.astype(np.float32).reshape(4, 3)
    return [float(v) for v in m[:, 0]], [float(v) for v in m[:, 1]], [float(v) for v in m[:, 2]]


SLOPES_NSA, SLOPES_DIFF, SLOPES_SWA = _alibi_slopes()


def _nt(a, b):
    return lax.dot_general(a, b, (((1,), (1,)), ((), ())), preferred_element_type=F32)


def _nn(a, b):
    return jnp.dot(a, b, preferred_element_type=F32)


def _cparams(*sem):
    return pltpu.CompilerParams(dimension_semantics=sem, vmem_limit_bytes=VMEM_LIMIT)


def _rms(x):
    return x * lax.rsqrt(jnp.mean(x * x, axis=-1, keepdims=True) + EPS)


def _inproj_kernel(x_ref, g_ref, w_ref, cs_ref, o_ref):
    h = _rms(x_ref[...]) * g_ref[...]
    o_ref[...] = (_nn(h.astype(BF16), w_ref[...]) * cs_ref[...]).astype(BF16)


def _inproj(x2, gain, w, colscale, tm=512):
    m = x2.shape[0]
    return pl.pallas_call(
        _inproj_kernel,
        grid=(m // tm,),
        in_specs=[pl.BlockSpec((tm, D_MODEL), lambda i: (i, 0)),
                  pl.BlockSpec((1, D_MODEL), lambda i: (0, 0)),
                  pl.BlockSpec((D_MODEL, NP), lambda i: (0, 0)),
                  pl.BlockSpec((1, NP), lambda i: (0, 0))],
        out_specs=pl.BlockSpec((tm, NP), lambda i: (i, 0)),
        out_shape=jax.ShapeDtypeStruct((m, NP), BF16),
        compiler_params=_cparams("parallel"),
        name="inproj",
    )(x2, gain, w, colscale)


def _compress_kernel(c_ref, pea_ref, peb_ref, w1a_ref, w1b_ref, w2_ref, o_ref):
    c = c_ref[0].astype(F32)
    p = _nn((c + pea_ref[...]).astype(BF16), w1a_ref[...])
    r = _nn((c + peb_ref[...]).astype(BF16), w1b_ref[...])
    n = c.shape[0]
    r_next = pltpu.roll(r, n - 1, 0)
    hid = jax.nn.gelu(p + r_next)
    o_ref[0] = _nn(hid.astype(BF16), w2_ref[...]).astype(BF16)


def _compress(chunks, pea, peb, w1a, w1b, w2):
    b, n, w = chunks.shape
    full = lambda a: pl.BlockSpec(a.shape, lambda i: (0,) * a.ndim)
    return pl.pallas_call(
        _compress_kernel,
        grid=(b,),
        in_specs=[pl.BlockSpec((1, n, w), lambda i: (i, 0, 0)), full(pea), full(peb), full(w1a), full(w1b),
                  full(w2)],
        out_specs=pl.BlockSpec((1, n, 2 * LANES), lambda i: (i, 0, 0)),
        out_shape=jax.ShapeDtypeStruct((b, n, 2 * LANES), BF16),
        compiler_params=_cparams("parallel"),
        name="nsa_compress",
    )(chunks, pea, peb, w1a, w1b, w2)


def _half_mask(shape, half):
    lane = lax.broadcasted_iota(jnp.int32, shape, len(shape) - 1)
    return (lane // HEAD_DIM) == half


def _group_norm_store(o_ref, gm_ref, o):
    o_ref[0] = (_rms(o) * gm_ref[...]).astype(o_ref.dtype)


def _sb_kernel(q_ref, k_ref, v_ref, gm_ref, o_ref, acc_ref, carry_ref, lb_ref, hl_ref, tot_ref, *, blk):
    i = pl.program_id(1)
    q = q_ref[0]
    row = lax.broadcasted_iota(jnp.int32, (blk, blk), 0)
    col = lax.broadcasted_iota(jnp.int32, (blk, blk), 1)
    past = col < row
    u = jnp.where(row > col, 1.0, 0.0).astype(BF16)
    uu = jnp.concatenate([u, u], axis=0)
    qm = []
    for h in range(N_HEADS):
        slab = q[:, (h // 2) * LANES:(h // 2 + 1) * LANES]
        qm.append(jnp.where(_half_mask(slab.shape, h % 2), slab, jnp.zeros_like(slab)))
    qm = [jnp.concatenate(qm[0:2], axis=0), jnp.concatenate(qm[2:4], axis=0)]
    acc_ref[...] = jnp.zeros_like(acc_ref)
    carry_ref[...] = jnp.zeros_like(carry_ref)

    def per_head(x, fn):
        return jnp.concatenate([fn(x[k * blk:(k + 1) * blk]) for k in range(x.shape[0] // blk)], axis=0)

    def stage_a(j, slot, diag):
        rows = pl.ds(pl.multiple_of(j * blk, blk), blk)
        for pr in range(2):
            r2 = slice(2 * pr * blk, 2 * (pr + 1) * blk)
            z = _nt(qm[pr], k_ref[0, rows, pr * LANES:(pr + 1) * LANES])
            sp = jnp.log2(1.0 + jnp.exp2(-jnp.abs(z)))
            lb = jnp.minimum(z, 0.0) - sp
            lk = lb - z
            if diag:
                lk = per_head(lk, lambda t: jnp.where(past, t, 0.0))
                lb = per_head(lb, lambda t: jnp.where(past, t, NEG_INF))
            hi = lk.astype(BF16)
            lb_ref[slot, r2, :] = lb
            hl_ref[slot, r2, 0:blk] = hi
            hl_ref[slot, r2, blk:2 * blk] = (lk - hi.astype(F32)).astype(BF16)
            tot_ref[slot, r2, :] = jnp.broadcast_to(jnp.sum(lk, axis=-1, keepdims=True), (2 * blk, LANES))

    def stage_b(j, slot):
        rows = pl.ds(pl.multiple_of(j * blk, blk), blk)
        cs = _nn(hl_ref[slot], uu)
        carry = carry_ref[...]
        a = jnp.exp2(lb_ref[slot] + cs + jnp.concatenate([carry] * (blk // LANES), axis=1)).astype(BF16)
        for pr in range(2):
            r2 = slice(2 * pr * blk, 2 * (pr + 1) * blk)
            acc_ref[r2, :] += _nn(a[r2], v_ref[0, rows, pr * LANES:(pr + 1) * LANES])
        carry_ref[...] = carry + tot_ref[slot]

    stage_a(i, 0, True)

    def step(t, slot):
        stage_b(i - t, slot)
        stage_a(i - 1 - t, 1 - slot, False)

    def body(tt, c):
        step(2 * tt, 0)
        step(2 * tt + 1, 1)
        return c

    lax.fori_loop(0, i // 2, body, 0)

    @pl.when(i % 2 == 1)
    def _():
        step(i - 1, 0)
        stage_b(0, 1)

    @pl.when(i % 2 == 0)
    def _():
        stage_b(0, 0)

    lo_half = _half_mask((blk, LANES), 0)
    acc = [acc_ref[h * blk:(h + 1) * blk, :] for h in range(N_HEADS)]
    o = jnp.concatenate([jnp.where(lo_half, acc[0], acc[1]), jnp.where(lo_half, acc[2], acc[3])], axis=1)
    _group_norm_store(o_ref, gm_ref, o)


def _stick_breaking(proj, gm, blk=256):
    b, s, _ = proj.shape
    stage = lambda w, dt: pltpu.VMEM((2, N_HEADS * blk, w), dt)
    return pl.pallas_call(
        functools.partial(_sb_kernel, blk=blk),
        grid=(b, s // blk),
        in_specs=[pl.BlockSpec((1, blk, 256), lambda bi, i: (bi, i, COL_QA)),
                  pl.BlockSpec((1, s, 256), lambda bi, i: (bi, 0, COL_KA)),
                  pl.BlockSpec((1, s, 256), lambda bi, i: (bi, 0, COL_VA)),
                  pl.BlockSpec((1, 256), lambda bi, i: (0, 0))],
        out_specs=pl.BlockSpec((1, blk, 256), lambda bi, i: (bi, i, 0)),
        out_shape=jax.ShapeDtypeStruct((b, s, 256), BF16),
        scratch_shapes=[pltpu.VMEM((N_HEADS * blk, LANES), F32), pltpu.VMEM((N_HEADS * blk, LANES), F32),
                        stage(blk, F32), stage(2 * blk, BF16), stage(LANES, F32)],
        compiler_params=_cparams("parallel", "arbitrary"),
        name="stick_breaking",
    )(proj, proj, proj, gm)


N_MAPS = 2 * N_HEADS
POS_LOCAL, POS_BLOCK, POS_ONE = 0, 3, 6
SEL_LANE = 8
MASK_BIAS = -2.0 ** 100


def _bf16_pieces(x, n):
    out = []
    r = np.float32(x)
    for _ in range(n):
        p = np.float32(r.astype(ml_dtypes.bfloat16))
        out.append(float(p))
        r = np.float32(r - p)
    return out


def _position_tables(slopes, s, blk, sel_blocks=False):
    coef = np.zeros((N_HEADS, LANES), np.float32)
    for h in range(N_HEADS):
        pieces = _bf16_pieces(slopes[h] * LOG2E, 3)
        coef[h, POS_LOCAL:POS_LOCAL + 3] = pieces
        coef[h, POS_BLOCK:POS_BLOCK + 3] = [blk * p for p in pieces]
    pos = np.zeros((s, LANES), np.float32)
    idx = np.arange(s)
    pos[:, POS_LOCAL:POS_LOCAL + 3] = (idx % blk)[:, None]
    pos[:, POS_BLOCK:POS_BLOCK + 3] = (idx // blk)[:, None]
    pos[:, POS_ONE:POS_ONE + 2] = 1.0
    if sel_blocks:
        pos[idx, SEL_LANE + idx // SEL_LEN] = 1.0
    return jnp.asarray(coef), jnp.asarray(pos, BF16)


def _diff_kernel(q_ref, k_ref, v_ref, pos_ref, coef_ref, dl_ref, gm_ref, o_ref, qa_ref, mx_ref, acc_ref,
                 *, blk, lam_init):
    i = pl.program_id(1)
    q = q_ref[0]
    row = lax.broadcasted_iota(jnp.int32, (blk, blk), 0)
    col = lax.broadcasted_iota(jnp.int32, (blk, blk), 1)
    causal = col <= row
    lane = lax.broadcasted_iota(jnp.int32, (blk, LANES), 1)
    for h in range(N_HEADS):
        slab = q[:, (h // 2) * LANES:(h // 2 + 1) * LANES]
        coef = jnp.broadcast_to(coef_ref[h:h + 1, :], (blk, LANES)).astype(BF16)
        for c in range(2):
            r = slice(((2 * h + c) % 4) * blk, ((2 * h + c) % 4 + 1) * blk)
            qa_ref[h // 2, r, 0:LANES] = jnp.where((lane // DIFF_QK_DIM) == (h % 2) * 2 + c, slab,
                                                   jnp.zeros_like(slab))
            qa_ref[h // 2, r, LANES:2 * LANES] = coef
    ones = jnp.ones((blk, LANES), BF16)

    def per_map(x, fn):
        return jnp.concatenate([fn(x[k * blk:(k + 1) * blk]) for k in range(4)], axis=0)

    def contribution(pr, j, diag, second):
        rows = pl.ds(pl.multiple_of(j * blk, blk), blk)
        lanes = slice(pr * LANES, (pr + 1) * LANES)
        ka = jnp.concatenate([k_ref[0, rows, lanes], pos_ref[rows, :]], axis=1)
        z = _nt(qa_ref[pr], ka)
        if second:
            p = jnp.exp2(z)
            if diag:
                p = per_map(p, lambda t: jnp.where(causal, t, 0.0))
            vo = jnp.concatenate([v_ref[0, rows, lanes], ones], axis=1)
            return _nn(p.astype(BF16), vo)
        if diag:
            z = per_map(z, lambda t: jnp.where(causal, t, NEG_INF))
        zm = z[:, 0:LANES]
        for g in range(1, blk // LANES):
            zm = jnp.maximum(zm, z[:, g * LANES:(g + 1) * LANES])
        return zm

    def sweep(blocks, second, first):
        ref, combine = (acc_ref, jnp.add) if second else (mx_ref, jnp.maximum)
        for pr in range(2):
            tot = None
            for j, diag in blocks:
                c = contribution(pr, j, diag, second)
                tot = c if tot is None else combine(tot, c)
            ref[pr] = tot if first else combine(ref[pr], tot)

    def run(second):
        @pl.when(i % 2 == 1)
        def _():
            sweep([(i - 1, False), (i, True)], second, True)

        @pl.when(i % 2 == 0)
        def _():
            sweep([(i, True)], second, True)

        def body(tt, c):
            sweep([(2 * tt, False), (2 * tt + 1, False)], second, False)
            return c
        lax.fori_loop(0, i // 2, body, 0)

    run(False)
    lane4 = lax.broadcasted_iota(jnp.int32, (4 * blk, LANES), 1)
    for pr in range(2):
        m = jnp.max(mx_ref[pr], axis=-1, keepdims=True)
        m_hi = m.astype(BF16)
        m_lo = (m - m_hi.astype(F32)).astype(BF16)
        old = qa_ref[pr, :, LANES:2 * LANES]
        qa_ref[pr, :, LANES:2 * LANES] = jnp.where(lane4 == POS_ONE, -m_hi, jnp.where(lane4 == POS_ONE + 1, -m_lo, old))
    run(True)

    dl = dl_ref[...]
    s1 = jnp.sum(dl[0:1] * dl[1:2], axis=-1, keepdims=True)
    s2 = jnp.sum(dl[2:3] * dl[3:4], axis=-1, keepdims=True)
    lam = jnp.exp(s1) - jnp.exp(s2) + lam_init
    outs = []
    for h in range(N_HEADS):
        k1 = (2 * h) % 4
        a1 = acc_ref[h // 2, k1 * blk:(k1 + 1) * blk, :]
        a2 = acc_ref[h // 2, (k1 + 1) * blk:(k1 + 2) * blk, :]
        d = a1[:, :LANES] / a1[:, LANES:] - lam * (a2[:, :LANES] / a2[:, LANES:])
        mine = _half_mask(d.shape, h % 2)
        ms = jnp.sum(jnp.where(mine, d * d, 0.0), axis=-1, keepdims=True) * (1.0 / HEAD_DIM)
        outs.append(d * lax.rsqrt(ms + EPS) * (1.0 - lam_init))
    lo_half = _half_mask((blk, LANES), 0)
    o = jnp.concatenate([jnp.where(lo_half, outs[0], outs[1]), jnp.where(lo_half, outs[2], outs[3])], axis=1)
    o_ref[0] = (o * gm_ref[...]).astype(o_ref.dtype)


def _differential(proj, dl, gm, lam_init, blk=256):
    b, s, _ = proj.shape
    coef, pos = _position_tables(SLOPES_DIFF, s, blk)
    return pl.pallas_call(
        functools.partial(_diff_kernel, blk=blk, lam_init=lam_init),
        grid=(b, s // blk),
        in_specs=[pl.BlockSpec((1, blk, 256), lambda bi, i: (bi, i, COL_QC)),
                  pl.BlockSpec((1, s, 256), lambda bi, i: (bi, 0, COL_KC)),
                  pl.BlockSpec((1, s, 256), lambda bi, i: (bi, 0, COL_VC)),
                  pl.BlockSpec((s, LANES), lambda bi, i: (0, 0)),
                  pl.BlockSpec((N_HEADS, LANES), lambda bi, i: (0, 0)),
                  pl.BlockSpec((4, DIFF_QK_DIM), lambda bi, i: (0, 0)),
                  pl.BlockSpec((1, 256), lambda bi, i: (0, 0))],
        out_specs=pl.BlockSpec((1, blk, 256), lambda bi, i: (bi, i, 0)),
        out_shape=jax.ShapeDtypeStruct((b, s, 256), BF16),
        scratch_shapes=[pltpu.VMEM((2, 4 * blk, 2 * LANES), BF16), pltpu.VMEM((2, 4 * blk, LANES), F32),
                        pltpu.VMEM((2, 4 * blk, 2 * LANES), F32)],
        compiler_params=_cparams("parallel", "arbitrary"),
        name="differential",
    )(proj, proj, proj, pos, coef, dl, gm)


def _swa_kernel(q_ref, k_ref, v_ref, sink_ref, gm_ref, o_ref, *, blk):
    i = pl.program_id(1)
    q = q_ref[0]
    first = jnp.maximum(i - 1, 0)
    start = pl.multiple_of(first * blk, blk)
    row = lax.broadcasted_iota(jnp.int32, (blk, 2 * blk), 0)
    col = lax.broadcasted_iota(jnp.int32, (blk, 2 * blk), 1)
    dist = (row - col) + (i - first) * blk
    mask = (dist >= 0) & (dist < SWA_WINDOW)
    distf = dist.astype(F32)
    outs = []
    for h in range(N_HEADS):
        lanes = slice((h // 2) * LANES, (h // 2 + 1) * LANES)
        slab = q[:, lanes]
        qm = jnp.where(_half_mask(slab.shape, h % 2), slab, jnp.zeros_like(slab))
        kb = k_ref[0, pl.ds(start, 2 * blk), lanes]
        vb = v_ref[0, pl.ds(start, 2 * blk), lanes]
        z = _nt(qm, kb) * (HEAD_DIM ** -0.5) - SLOPES_SWA[h] * distf
        z = jnp.where(mask, z, NEG_INF)
        sink = sink_ref[h]
        m = jnp.maximum(jnp.max(z, axis=-1, keepdims=True), sink)
        p = jnp.exp(z - m)
        l = jnp.sum(p, axis=-1, keepdims=True) + jnp.exp(sink - m)
        outs.append(_nn(p.astype(BF16), vb) / l)
    lo_half = _half_mask((blk, LANES), 0)
    o = jnp.concatenate([jnp.where(lo_half, outs[0], outs[1]), jnp.where(lo_half, outs[2], outs[3])], axis=1)
    _group_norm_store(o_ref, gm_ref, o)


def _sliding_window(proj, sinks, gm, blk=128):
    b, s, _ = proj.shape
    return pl.pallas_call(
        functools.partial(_swa_kernel, blk=blk),
        grid=(b, s // blk),
        in_specs=[pl.BlockSpec((1, blk, 256), lambda bi, i: (bi, i, COL_QD)),
                  pl.BlockSpec((1, s, 256), lambda bi, i: (bi, 0, COL_KD)),
                  pl.BlockSpec((1, s, 256), lambda bi, i: (bi, 0, COL_VD)),
                  pl.BlockSpec(memory_space=pltpu.SMEM),
                  pl.BlockSpec((1, 256), lambda bi, i: (0, 0))],
        out_specs=pl.BlockSpec((1, blk, 256), lambda bi, i: (bi, i, 0)),
        out_shape=jax.ShapeDtypeStruct((b, s, 256), BF16),
        compiler_params=_cparams("parallel", "arbitrary"),
        name="sliding_window",
    )(proj, proj, proj, sinks, gm)


def _nsa_kernel(q_ref, g_ref, cmp_ref, ks_ref, vs_ref, kw_ref, vw_ref, pos_ref, coef_ref, gm_ref, o_ref,
                qa_ref, mx_ref, acc_ref, *, blk, n_cmp, n_sel):
    i = pl.program_id(1)
    q = q_ref[0]
    lane = lax.broadcasted_iota(jnp.int32, (blk, LANES), 1)
    for h in range(N_HEADS):
        slab = q[:, (h // 2) * LANES:(h // 2 + 1) * LANES]
        qa_ref[h * blk:(h + 1) * blk, 0:LANES] = jnp.where(_half_mask(slab.shape, h % 2), slab, jnp.zeros_like(slab))
    qs = qa_ref[:, 0:LANES]
    coefs = [jnp.broadcast_to(coef_ref[h:h + 1, :], (blk, LANES)) for h in range(N_HEADS)]

    tq = i * blk + lax.broadcasted_iota(jnp.int32, (blk, LANES), 0)

    kc = cmp_ref[0, :, 0:LANES]
    vc = cmp_ref[0, :, LANES:2 * LANES]
    dist_c = tq - (CMP_STRIDE * lane + CMP_LEN - 1)
    ok_c = (dist_c >= 0) & (lane < n_cmp)
    dist_cf = dist_c.astype(F32)
    bias_c = jnp.concatenate([jnp.where(ok_c, (-SLOPES_NSA[h] * LOG2E) * dist_cf, NEG_INF)
                              for h in range(N_HEADS)], axis=0)
    z = _nt(qs, kc) + bias_c
    p = jnp.exp2(z - jnp.max(z, axis=-1, keepdims=True))
    p = p / jnp.sum(p, axis=-1, keepdims=True)
    p = jnp.where(bias_c > 0.5 * NEG_INF, p, 0.0).astype(BF16)
    o_cmp = _nn(p, vc)

    ci = lax.broadcasted_iota(jnp.int32, (LANES, LANES), 0)
    cj = lax.broadcasted_iota(jnp.int32, (LANES, LANES), 1)
    to_sel = jnp.where((ci * CMP_STRIDE) // SEL_LEN + SEL_LANE == cj, 1.0, 0.0).astype(BF16)
    imp = _nn(p[0:blk], to_sel)
    for h in range(1, N_HEADS):
        imp = imp + _nn(p[h * blk:(h + 1) * blk], to_sel)
    blk_id = lane - SEL_LANE
    in_sel = (blk_id >= 0) & (blk_id < n_sel)
    cur = tq // SEL_LEN
    forced = (blk_id == 0) | (blk_id == cur) | (blk_id == cur - 1)
    score = jnp.where(blk_id > cur, NEG_INF, jnp.where(forced, FORCE_SCORE, imp))
    score = jnp.where(in_sel, score, -3.0e38)
    sel = jnp.zeros((blk, LANES), F32)
    lanef = lane.astype(F32)
    for _ in range(min(SEL_TOPN, n_sel)):
        best = jnp.max(score, axis=-1, keepdims=True)
        idx = jnp.min(jnp.where(score == best, lanef, float(LANES)), axis=-1, keepdims=True)
        hit = lanef == idx
        sel = jnp.where(hit, 1.0, sel)
        score = jnp.where(hit, -3.4e38, score)
    sel_bias = jnp.where(sel > 0.5, 0.0, MASK_BIAS)

    row = lax.broadcasted_iota(jnp.int32, (blk, blk), 0)
    col = lax.broadcasted_iota(jnp.int32, (blk, blk), 1)
    causal = col <= row
    in_window = col > row
    ones = jnp.ones((blk, LANES), BF16)

    def contribution(k_ref, v_ref, j, mask, second):
        rows = pl.ds(pl.multiple_of(j * blk, blk), blk)
        ka = jnp.concatenate([k_ref[0, rows, :], pos_ref[rows, :]], axis=1)
        if second:
            vo = jnp.concatenate([v_ref[0, rows, :], ones], axis=1)
            outs = []
            for hp in range(2):
                pp = jnp.exp2(_nt(qa_ref[2 * hp * blk:2 * (hp + 1) * blk, :], ka))
                if mask is not None:
                    pp = jnp.concatenate([jnp.where(mask, pp[h * blk:(h + 1) * blk], 0.0) for h in range(2)], axis=0)
                outs.append(_nn(pp.astype(BF16), vo))
            return jnp.concatenate(outs, axis=0)
        zz = _nt(qa_ref[...], ka)
        if mask is not None:
            zz = jnp.concatenate([jnp.where(mask, zz[h * blk:(h + 1) * blk], NEG_INF) for h in range(N_HEADS)], axis=0)
        zm = zz[:, 0:LANES]
        for g in range(1, blk // LANES):
            zm = jnp.maximum(zm, zz[:, g * LANES:(g + 1) * LANES])
        return zm

    def process(k_ref, v_ref, blocks, second):
        ref, combine = (acc_ref, jnp.add) if second else (mx_ref, jnp.maximum)
        tot = None
        for j, mask in blocks:
            c = contribution(k_ref, v_ref, j, mask, second)
            tot = c if tot is None else combine(tot, c)
        ref[...] = combine(ref[...], tot)

    def attend(sweep, aug):
        for h in range(N_HEADS):
            qa_ref[h * blk:(h + 1) * blk, LANES:2 * LANES] = aug[h].astype(BF16)
        mx_ref[...] = jnp.full(mx_ref.shape, NEG_INF, F32)
        acc_ref[...] = jnp.zeros_like(acc_ref)
        sweep(False)
        m = jnp.max(mx_ref[...], axis=-1, keepdims=True)
        m_hi = m.astype(BF16)
        m_lo = (m - m_hi.astype(F32)).astype(BF16)
        lane4 = lax.broadcasted_iota(jnp.int32, (N_HEADS * blk, LANES), 1)
        old = qa_ref[:, LANES:2 * LANES]
        qa_ref[:, LANES:2 * LANES] = jnp.where(lane4 == POS_ONE, -m_hi, jnp.where(lane4 == POS_ONE + 1, -m_lo, old))
        sweep(True)
        acc = acc_ref[...]
        return acc[:, 0:LANES] / acc[:, LANES:2 * LANES]

    def sel_sweep(second):
        def body(tt, c):
            process(ks_ref, vs_ref, [(2 * tt, None), (2 * tt + 1, None)], second)
            return c
        lax.fori_loop(0, i // 2, body, 0)

        @pl.when(i % 2 == 1)
        def _():
            process(ks_ref, vs_ref, [(i - 1, None), (i, causal)], second)

        @pl.when(i % 2 == 0)
        def _():
            process(ks_ref, vs_ref, [(i, causal)], second)

    def win_sweep(second):
        process(kw_ref, vw_ref, [(jnp.maximum(i - 2, 0), in_window & (i >= 2)),
                                 (jnp.maximum(i - 1, 0), jnp.broadcast_to(i >= 1, (blk, blk))),
                                 (i, causal)], second)

    o_sel = attend(sel_sweep, [jnp.where(in_sel, sel_bias, coefs[h]) for h in range(N_HEADS)])
    o_win = attend(win_sweep, coefs)

    g = jax.nn.sigmoid(g_ref[0].astype(F32))
    outs = []
    for h in range(N_HEADS):
        r = slice(h * blk, (h + 1) * blk)
        outs.append(g[:, 3 * h:3 * h + 1] * o_cmp[r] + g[:, 3 * h + 1:3 * h + 2] * o_sel[r]
                    + g[:, 3 * h + 2:3 * h + 3] * o_win[r])
    lo_half = _half_mask((blk, LANES), 0)
    o = jnp.concatenate([jnp.where(lo_half, outs[0], outs[1]), jnp.where(lo_half, outs[2], outs[3])], axis=1)
    _group_norm_store(o_ref, gm_ref, o)


def _native_sparse(proj, cmp_kv, gm, blk=256):
    b, s, _ = proj.shape
    n_cmp = (s - CMP_LEN) // CMP_STRIDE + 1
    n_sel = s // SEL_LEN
    assert cmp_kv.shape[1] == LANES and n_cmp <= LANES and SEL_LANE + n_sel <= LANES
    assert NSA_WINDOW == 2 * blk
    coef, pos = _position_tables(SLOPES_NSA, s, blk, sel_blocks=True)
    slab = lambda c: pl.BlockSpec((1, s, LANES), lambda bi, i: (bi, 0, c))
    rows = N_HEADS * blk
    return pl.pallas_call(
        functools.partial(_nsa_kernel, blk=blk, n_cmp=n_cmp, n_sel=n_sel),
        grid=(b, s // blk),
        in_specs=[pl.BlockSpec((1, blk, 256), lambda bi, i: (bi, i, COL_QN)),
                  pl.BlockSpec((1, blk, LANES), lambda bi, i: (bi, i, COL_GN)),
                  pl.BlockSpec((1, LANES, 2 * LANES), lambda bi, i: (bi, 0, 0)),
                  slab(COL_KS), slab(COL_VS), slab(COL_KW), slab(COL_VW),
                  pl.BlockSpec((s, LANES), lambda bi, i: (0, 0)),
                  pl.BlockSpec((N_HEADS, LANES), lambda bi, i: (0, 0)),
                  pl.BlockSpec((1, 256), lambda bi, i: (0, 0))],
        out_specs=pl.BlockSpec((1, blk, 256), lambda bi, i: (bi, i, 0)),
        out_shape=jax.ShapeDtypeStruct((b, s, 256), BF16),
        scratch_shapes=[pltpu.VMEM((rows, 2 * LANES), BF16), pltpu.VMEM((rows, LANES), F32),
                        pltpu.VMEM((rows, 2 * LANES), F32)],
        compiler_params=_cparams("parallel", "arbitrary"),
        name="native_sparse",
    )(proj, proj, cmp_kv, proj, proj, proj, proj, pos, coef, gm)


def _outproj_kernel(a_ref, b_ref, c_ref, d_ref, w_ref, x_ref, o_ref):
    acc = x_ref[...]
    for g, r in enumerate((a_ref, b_ref, c_ref, d_ref)):
        acc = acc + _nn(r[...], w_ref[g * GROUP_WIDTH:(g + 1) * GROUP_WIDTH, :])
    o_ref[...] = acc


def _outproj(mixes, w, x2, tm=512):
    m = x2.shape[0]
    grp = pl.BlockSpec((tm, GROUP_WIDTH), lambda i: (i, 0))
    return pl.pallas_call(
        _outproj_kernel,
        grid=(m // tm,),
        in_specs=[grp, grp, grp, grp, pl.BlockSpec((D_MODEL, D_MODEL), lambda i: (0, 0)),
                  pl.BlockSpec((tm, D_MODEL), lambda i: (i, 0))],
        out_specs=pl.BlockSpec((tm, D_MODEL), lambda i: (i, 0)),
        out_shape=jax.ShapeDtypeStruct((m, D_MODEL), F32),
        compiler_params=_cparams("parallel"),
        name="outproj",
    )(*mixes, w, x2)


def _mlp_kernel(x_ref, g_ref, wu_ref, wd_ref, gf_ref, o_ref, h_ref, acc_ref, *, final_norm):
    f = pl.program_id(1)

    @pl.when(f == 0)
    def _():
        h_ref[...] = (_rms(x_ref[...]) * g_ref[...]).astype(BF16)
        acc_ref[...] = jnp.zeros_like(acc_ref)

    u = jnp.maximum(_nn(h_ref[...], wu_ref[...]), 0.0)
    acc_ref[...] += _nn((u * u).astype(BF16), wd_ref[...])

    @pl.when(f == pl.num_programs(1) - 1)
    def _():
        y = x_ref[...] + acc_ref[...]
        if final_norm:
            y = _rms(y) * gf_ref[...]
        o_ref[...] = y


def _mlp(x2, gain, wu, wd, gfinal, final_norm, tm=1024, tf=1024):
    m = x2.shape[0]
    return pl.pallas_call(
        functools.partial(_mlp_kernel, final_norm=final_norm),
        grid=(m // tm, D_FF // tf),
        in_specs=[pl.BlockSpec((tm, D_MODEL), lambda i, f: (i, 0)),
                  pl.BlockSpec((1, D_MODEL), lambda i, f: (0, 0)),
                  pl.BlockSpec((D_MODEL, tf), lambda i, f: (0, f)),
                  pl.BlockSpec((tf, D_MODEL), lambda i, f: (f, 0)),
                  pl.BlockSpec((1, D_MODEL), lambda i, f: (0, 0))],
        out_specs=pl.BlockSpec((tm, D_MODEL), lambda i, f: (i, 0)),
        out_shape=jax.ShapeDtypeStruct((m, D_MODEL), F32),
        scratch_shapes=[pltpu.VMEM((tm, D_MODEL), BF16), pltpu.VMEM((tm, D_MODEL), F32)],
        compiler_params=_cparams("parallel", "arbitrary"),
        name="mlp",
    )(x2, gain, wu, wd, gfinal)


def _compress_weights(pe_k, w1_k, w2_k, pe_v, w1_v, w2_v):
    half = CMP_STRIDE
    d = HEAD_DIM

    def halves(w1):
        w = w1.reshape(CMP_LEN, d, CMP_HIDDEN)
        return w[:half], w[half:]

    ka, kb = halves(w1_k)
    va, vb = halves(w1_v)
    zero = jnp.zeros_like(ka)

    def merge(wk, wv):
        top = jnp.concatenate([wk, zero], axis=-1)
        bot = jnp.concatenate([zero, wv], axis=-1)
        return jnp.concatenate([top, bot], axis=1).reshape(half * 2 * d, 2 * CMP_HIDDEN).astype(BF16)

    w1a, w1b = merge(ka, va), merge(kb, vb)
    pe = jnp.concatenate([pe_k, pe_v], axis=-1)
    pea = pe[:half].reshape(1, half * 2 * d)
    peb = pe[half:].reshape(1, half * 2 * d)
    zk = jnp.zeros_like(w2_k)
    w2 = jnp.concatenate([jnp.concatenate([w2_k, w2_k, zk, zk], axis=1),
                          jnp.concatenate([zk, zk, w2_v, w2_v], axis=1)], axis=0).astype(BF16)
    return pea, peb, w1a, w1b, w2


def kernel(x, norm_attn, w_in, cmp_pe_k, cmp_w1_k, cmp_w2_k, cmp_pe_v, cmp_w1_v, cmp_w2_v, diff_lq1, diff_lk1,
           diff_lq2, diff_lk2, sinks, g_mix, w_out, norm_mlp, w_up, w_down, norm_final):
    b, s, d = x.shape
    m = b * s
    depth = w_in.shape[0]
    cols = jnp.asarray(_COLS)
    w_in_p = jnp.take(jnp.concatenate([w_in, jnp.zeros((depth, d, 1), w_in.dtype)], axis=2), cols,
                      axis=2).astype(BF16)
    w_out_b = w_out.astype(BF16)
    w_up_b = w_up.astype(BF16)
    w_down_b = w_down.astype(BF16)
    gfinal = norm_final.reshape(1, d)

    colscale = jnp.asarray(_column_scales())

    x2 = x.reshape(m, d)
    for l in range(depth):
        proj = _inproj(x2, norm_attn[l].reshape(1, d), w_in_p[l], colscale).reshape(b, s, NP)
        gm = g_mix[l].reshape(N_HEADS, 1, GROUP_WIDTH)

        chunks = proj[:, :, COL_KVC * LANES:(COL_KVC + 1) * LANES].reshape(b, s // CMP_STRIDE, CMP_STRIDE * LANES)
        cmp_kv = _compress(chunks, *_compress_weights(cmp_pe_k[l], cmp_w1_k[l], cmp_w2_k[l],
                                                      cmp_pe_v[l], cmp_w1_v[l], cmp_w2_v[l]))

        o_sb = _stick_breaking(proj, gm[0])
        o_nsa = _native_sparse(proj, cmp_kv, gm[1])
        lam_init = 0.8 - 0.6 * math.exp(-0.3 * l)
        dl = jnp.stack([diff_lq1[l], diff_lk1[l], diff_lq2[l], diff_lk2[l]]).astype(F32)
        o_diff = _differential(proj, dl, gm[2], lam_init)
        o_swa = _sliding_window(proj, sinks[l].astype(F32), gm[3])

        mixes = [o.reshape(m, GROUP_WIDTH) for o in (o_sb, o_nsa, o_diff, o_swa)]
        x2 = _outproj(mixes, w_out_b[l], x2)
        x2 = _mlp(x2, norm_mlp[l].reshape(1, d), w_up_b[l], w_down_b[l], gfinal, final_norm=(l == depth - 1))
    return x2.reshape(b, s, d)
```

```python
import functools
import math

import ml_dtypes
import numpy as np
import jax
import jax.numpy as jnp
from jax import lax
from jax.experimental import pallas as pl
from jax.experimental.pallas import tpu as pltpu

F32 = jnp.float32
BF16 = jnp.bfloat16

D_MODEL = 1024
DEPTH = 4
HEAD_DIM = 64
GROUP_WIDTH = 256
N_HEADS = 4
D_FF = 4 * D_MODEL
EPS = 1e-6
NEG_INF = -1e30
LOG2E = math.log2(math.e)
LANES = 128

CMP_LEN = 32
CMP_STRIDE = 16
CMP_HIDDEN = 128
SEL_LEN = 64
SEL_TOPN = 8
FORCE_SCORE = 1e9
NSA_WINDOW = 512
SWA_WINDOW = 128
DIFF_QK_DIM = 32

VMEM_LIMIT = 48 * 1024 * 1024

_ORIG = dict(qa=(0, 256), ka=(256, 256), va=(512, 256), qn=(768, 256), kcn=(1024, 64), vcn=(1088, 64),
             ksn=(1152, 64), vsn=(1216, 64), kwn=(1280, 64), vwn=(1344, 64), gn=(1408, 12),
             qc=(1420, 256), kc=(1676, 256), vc=(1932, 256), qd=(2188, 256), kd=(2444, 128), vd=(2572, 128))
IN_COLS = 2700


def _layout():
    cols = []

    def put(name, lo=0, n=None):
        start, size = _ORIG[name]
        n = size - lo if n is None else n
        cols.extend(range(start + lo, start + lo + n))

    def zeros(n):
        cols.extend([IN_COLS] * n)

    for nm in ("qa", "ka", "va", "qn", "kcn", "vcn"):
        put(nm)
    for nm in ("ksn", "vsn", "kwn", "vwn"):
        put(nm)
        put(nm)
    put("gn")
    zeros(LANES - 12)
    for nm in ("qc", "kc", "vc", "qd"):
        put(nm)
    for nm in ("kd", "vd"):
        put(nm, 0, 64)
        put(nm, 0, 64)
        put(nm, 64, 64)
        put(nm, 64, 64)
    return np.asarray(cols, np.int32)


_COLS = _layout()
NP = int(_COLS.shape[0])


def _column_scales():
    cs = np.ones((1, NP), np.float32)
    cs[0, 0:256] = HEAD_DIM ** -0.5 * LOG2E
    cs[0, 768:1024] = HEAD_DIM ** -0.5 * LOG2E
    cs[0, 1792:2048] = DIFF_QK_DIM ** -0.5 * LOG2E
    return cs


COL_QA, COL_KA, COL_VA = 0, 1, 2
COL_QN = 3
COL_KVC = 8
COL_KS, COL_VS, COL_KW, COL_VW, COL_GN = 9, 10, 11, 12, 13
COL_QC, COL_KC, COL_VC = 7, 8, 9
COL_QD, COL_KD, COL_VD = 10, 11, 12


def _alibi_slopes():
    m = 2.0 ** (-8.0 * np.arange(1, 13) / 12.0)
    m = m.astype(np.float32).reshape(4, 3)
    return [float(v) for v in m[:, 0]], [float(v) for v in m[:, 1]], [float(v) for v in m[:, 2]]


SLOPES_NSA, SLOPES_DIFF, SLOPES_SWA = _alibi_slopes()


def _nt(a, b):
    return lax.dot_general(a, b, (((1,), (1,)), ((), ())), preferred_element_type=F32)


def _nn(a, b):
    return jnp.dot(a, b, preferred_element_type=F32)


def _cparams(*sem):
    return pltpu.CompilerParams(dimension_semantics=sem, vmem_limit_bytes=VMEM_LIMIT)


def _rms(x):
    return x * lax.rsqrt(jnp.mean(x * x, axis=-1, keepdims=True) + EPS)


def _inproj_kernel(x_ref, g_ref, w_ref, cs_ref, o_ref):
    h = _rms(x_ref[...]) * g_ref[...]
    o_ref[...] = (_nn(h.astype(BF16), w_ref[...]) * cs_ref[...]).astype(BF16)


def _inproj(x2, gain, w, colscale, tm=512):
    m = x2.shape[0]
    return pl.pallas_call(
        _inproj_kernel,
        grid=(m // tm,),
        in_specs=[pl.BlockSpec((tm, D_MODEL), lambda i: (i, 0)),
                  pl.BlockSpec((1, D_MODEL), lambda i: (0, 0)),
                  pl.BlockSpec((D_MODEL, NP), lambda i: (0, 0)),
                  pl.BlockSpec((1, NP), lambda i: (0, 0))],
        out_specs=pl.BlockSpec((tm, NP), lambda i: (i, 0)),
        out_shape=jax.ShapeDtypeStruct((m, NP), BF16),
        compiler_params=_cparams("parallel"),
        name="inproj",
    )(x2, gain, w, colscale)


def _compress_kernel(c_ref, pea_ref, peb_ref, w1a_ref, w1b_ref, w2_ref, o_ref):
    c = c_ref[0].astype(F32)
    p = _nn((c + pea_ref[...]).astype(BF16), w1a_ref[...])
    r = _nn((c + peb_ref[...]).astype(BF16), w1b_ref[...])
    n = c.shape[0]
    r_next = pltpu.roll(r, n - 1, 0)
    hid = jax.nn.gelu(p + r_next)
    o_ref[0] = _nn(hid.astype(BF16), w2_ref[...]).astype(BF16)


def _compress(chunks, pea, peb, w1a, w1b, w2):
    b, n, w = chunks.shape
    full = lambda a: pl.BlockSpec(a.shape, lambda i: (0,) * a.ndim)
    return pl.pallas_call(
        _compress_kernel,
        grid=(b,),
        in_specs=[pl.BlockSpec((1, n, w), lambda i: (i, 0, 0)), full(pea), full(peb), full(w1a), full(w1b),
                  full(w2)],
        out_specs=pl.BlockSpec((1, n, 2 * LANES), lambda i: (i, 0, 0)),
        out_shape=jax.ShapeDtypeStruct((b, n, 2 * LANES), BF16),
        compiler_params=_cparams("parallel"),
        name="nsa_compress",
    )(chunks, pea, peb, w1a, w1b, w2)


def _half_mask(shape, half):
    lane = lax.broadcasted_iota(jnp.int32, shape, len(shape) - 1)
    return (lane // HEAD_DIM) == half


def _group_norm_store(o_ref, gm_ref, o):
    o_ref[0] = (_rms(o) * gm_ref[...]).astype(o_ref.dtype)


def _sb_kernel(q_ref, k_ref, v_ref, gm_ref, o_ref, acc_ref, carry_ref, lb_ref, hl_ref, tot_ref, *, blk):
    i = pl.program_id(1)
    q = q_ref[0]
    row = lax.broadcasted_iota(jnp.int32, (blk, blk), 0)
    col = lax.broadcasted_iota(jnp.int32, (blk, blk), 1)
    past = col < row
    u = jnp.where(row > col, 1.0, 0.0).astype(BF16)
    uu = jnp.concatenate([u, u], axis=0)
    qm = []
    for h in range(N_HEADS):
        slab = q[:, (h // 2) * LANES:(h // 2 + 1) * LANES]
        qm.append(jnp.where(_half_mask(slab.shape, h % 2), slab, jnp.zeros_like(slab)))
    qm = [jnp.concatenate(qm[0:2], axis=0), jnp.concatenate(qm[2:4], axis=0)]
    acc_ref[...] = jnp.zeros_like(acc_ref)
    carry_ref[...] = jnp.zeros_like(carry_ref)

    def per_head(x, fn):
        return jnp.concatenate([fn(x[k * blk:(k + 1) * blk]) for k in range(x.shape[0] // blk)], axis=0)

    def a_logits(j):
        rows = pl.ds(pl.multiple_of(j * blk, blk), blk)
        return [_nt(qm[pr], k_ref[0, rows, pr * LANES:(pr + 1) * LANES]) for pr in range(2)]

    def a_finish(zs, slot, diag):
        for pr, z in enumerate(zs):
            r2 = slice(2 * pr * blk, 2 * (pr + 1) * blk)
            sp = jnp.log2(1.0 + jnp.exp2(-jnp.abs(z)))
            lb = jnp.minimum(z, 0.0) - sp
            lk = lb - z
            if diag:
                lk = per_head(lk, lambda t: jnp.where(past, t, 0.0))
                lb = per_head(lb, lambda t: jnp.where(past, t, NEG_INF))
            hi = lk.astype(BF16)
            lb_ref[slot, r2, :] = lb
            hl_ref[slot, r2, 0:blk] = hi
            hl_ref[slot, r2, blk:2 * blk] = (lk - hi.astype(F32)).astype(BF16)
            tot_ref[slot, r2, :] = jnp.broadcast_to(jnp.sum(lk, axis=-1, keepdims=True), (2 * blk, LANES))

    def b_suffix(slot):
        return _nn(hl_ref[slot], uu)

    def b_finish(cs, j, slot):
        rows = pl.ds(pl.multiple_of(j * blk, blk), blk)
        carry = carry_ref[...]
        a = jnp.exp2(lb_ref[slot] + cs + jnp.concatenate([carry] * (blk // LANES), axis=1)).astype(BF16)
        for pr in range(2):
            r2 = slice(2 * pr * blk, 2 * (pr + 1) * blk)
            acc_ref[r2, :] += _nn(a[r2], v_ref[0, rows, pr * LANES:(pr + 1) * LANES])
        carry_ref[...] = carry + tot_ref[slot]

    a_finish(a_logits(i), 0, True)

    def step(t, slot):
        zs = a_logits(i - 1 - t)
        cs = b_suffix(slot)
        a_finish(zs, 1 - slot, False)
        b_finish(cs, i - t, slot)

    def body(tt, c):
        step(2 * tt, 0)
        step(2 * tt + 1, 1)
        return c

    lax.fori_loop(0, i // 2, body, 0)

    @pl.when(i % 2 == 1)
    def _():
        step(i - 1, 0)
        b_finish(b_suffix(1), 0, 1)

    @pl.when(i % 2 == 0)
    def _():
        b_finish(b_suffix(0), 0, 0)

    lo_half = _half_mask((blk, LANES), 0)
    acc = [acc_ref[h * blk:(h + 1) * blk, :] for h in range(N_HEADS)]
    o = jnp.concatenate([jnp.where(lo_half, acc[0], acc[1]), jnp.where(lo_half, acc[2], acc[3])], axis=1)
    _group_norm_store(o_ref, gm_ref, o)


def _stick_breaking(proj, gm, blk=256):
    b, s, _ = proj.shape
    stage = lambda w, dt: pltpu.VMEM((2, N_HEADS * blk, w), dt)
    return pl.pallas_call(
        functools.partial(_sb_kernel, blk=blk),
        grid=(b, s // blk),
        in_specs=[pl.BlockSpec((1, blk, 256), lambda bi, i: (bi, i, COL_QA)),
                  pl.BlockSpec((1, s, 256), lambda bi, i: (bi, 0, COL_KA)),
                  pl.BlockSpec((1, s, 256), lambda bi, i: (bi, 0, COL_VA)),
                  pl.BlockSpec((1, 256), lambda bi, i: (0, 0))],
        out_specs=pl.BlockSpec((1, blk, 256), lambda bi, i: (bi, i, 0)),
        out_shape=jax.ShapeDtypeStruct((b, s, 256), BF16),
        scratch_shapes=[pltpu.VMEM((N_HEADS * blk, LANES), F32), pltpu.VMEM((N_HEADS * blk, LANES), F32),
                        stage(blk, F32), stage(2 * blk, BF16), stage(LANES, F32)],
        compiler_params=_cparams("parallel", "arbitrary"),
        name="stick_breaking",
    )(proj, proj, proj, gm)


N_MAPS = 2 * N_HEADS
POS_LOCAL, POS_BLOCK, POS_ONE = 0, 3, 6
SEL_LANE = 8
MASK_BIAS = -2.0 ** 100


def _bf16_pieces(x, n):
    out = []
    r = np.float32(x)
    for _ in range(n):
        p = np.float32(r.astype(ml_dtypes.bfloat16))
        out.append(float(p))
        r = np.float32(r - p)
    return out


def _position_tables(slopes, s, blk, sel_blocks=False):
    coef = np.zeros((N_HEADS, LANES), np.float32)
    for h in range(N_HEADS):
        pieces = _bf16_pieces(slopes[h] * LOG2E, 3)
        coef[h, POS_LOCAL:POS_LOCAL + 3] = pieces
        coef[h, POS_BLOCK:POS_BLOCK + 3] = [blk * p for p in pieces]
    pos = np.zeros((s, LANES), np.float32)
    idx = np.arange(s)
    pos[:, POS_LOCAL:POS_LOCAL + 3] = (idx % blk)[:, None]
    pos[:, POS_BLOCK:POS_BLOCK + 3] = (idx // blk)[:, None]
    pos[:, POS_ONE:POS_ONE + 2] = 1.0
    if sel_blocks:
        pos[idx, SEL_LANE + idx // SEL_LEN] = 1.0
    return jnp.asarray(coef), jnp.asarray(pos, BF16)


def _diff_kernel(q_ref, k_ref, v_ref, pos_ref, coef_ref, dl_ref, gm_ref, o_ref, qa_ref, mx_ref, acc_ref,
                 *, blk, lam_init):
    i = pl.program_id(1)
    q = q_ref[0]
    row = lax.broadcasted_iota(jnp.int32, (blk, blk), 0)
    col = lax.broadcasted_iota(jnp.int32, (blk, blk), 1)
    causal = col <= row
    lane = lax.broadcasted_iota(jnp.int32, (blk, LANES), 1)
    for h in range(N_HEADS):
        slab = q[:, (h // 2) * LANES:(h // 2 + 1) * LANES]
        coef = jnp.broadcast_to(coef_ref[h:h + 1, :], (blk, LANES)).astype(BF16)
        for c in range(2):
            r = slice(((2 * h + c) % 4) * blk, ((2 * h + c) % 4 + 1) * blk)
            qa_ref[h // 2, r, 0:LANES] = jnp.where((lane // DIFF_QK_DIM) == (h % 2) * 2 + c, slab,
                                                   jnp.zeros_like(slab))
            qa_ref[h // 2, r, LANES:2 * LANES] = coef
    ones = jnp.ones((blk, LANES), BF16)

    def per_map(x, fn):
        return jnp.concatenate([fn(x[k * blk:(k + 1) * blk]) for k in range(4)], axis=0)

    def sweep(blocks, second, first):
        chains = [(pr, pl.ds(pl.multiple_of(j * blk, blk), blk), diag) for pr in range(2) for j, diag in blocks]
        zs = []
        for pr, rows, diag in chains:
            ka = jnp.concatenate([k_ref[0, rows, pr * LANES:(pr + 1) * LANES], pos_ref[rows, :]], axis=1)
            zs.append(_nt(qa_ref[pr], ka))
        outs = []
        if second:
            ps = []
            for (pr, rows, diag), z in zip(chains, zs):
                p = jnp.exp2(z)
                if diag:
                    p = per_map(p, lambda t: jnp.where(causal, t, 0.0))
                ps.append(p.astype(BF16))
            for (pr, rows, diag), p in zip(chains, ps):
                vo = jnp.concatenate([v_ref[0, rows, pr * LANES:(pr + 1) * LANES], ones], axis=1)
                outs.append(_nn(p, vo))
        else:
            for (pr, rows, diag), z in zip(chains, zs):
                if diag:
                    z = per_map(z, lambda t: jnp.where(causal, t, NEG_INF))
                zm = z[:, 0:LANES]
                for g in range(1, blk // LANES):
                    zm = jnp.maximum(zm, z[:, g * LANES:(g + 1) * LANES])
                outs.append(zm)
        ref, combine = (acc_ref, jnp.add) if second else (mx_ref, jnp.maximum)
        for pr in range(2):
            mine = [o for (cp, _, _), o in zip(chains, outs) if cp == pr]
            tot = mine[0]
            for o in mine[1:]:
                tot = combine(tot, o)
            ref[pr] = tot if first else combine(ref[pr], tot)

    def run(second):
        @pl.when(i % 2 == 1)
        def _():
            sweep([(i - 1, False), (i, True)], second, True)

        @pl.when(i % 2 == 0)
        def _():
            sweep([(i, True)], second, True)

        def body(tt, c):
            sweep([(2 * tt, False), (2 * tt + 1, False)], second, False)
            return c
        lax.fori_loop(0, i // 2, body, 0)

    run(False)
    lane4 = lax.broadcasted_iota(jnp.int32, (4 * blk, LANES), 1)
    for pr in range(2):
        m = jnp.max(mx_ref[pr], axis=-1, keepdims=True)
        m_hi = m.astype(BF16)
        m_lo = (m - m_hi.astype(F32)).astype(BF16)
        old = qa_ref[pr, :, LANES:2 * LANES]
        qa_ref[pr, :, LANES:2 * LANES] = jnp.where(lane4 == POS_ONE, -m_hi, jnp.where(lane4 == POS_ONE + 1, -m_lo, old))
    run(True)

    dl = dl_ref[...]
    s1 = jnp.sum(dl[0:1] * dl[1:2], axis=-1, keepdims=True)
    s2 = jnp.sum(dl[2:3] * dl[3:4], axis=-1, keepdims=True)
    lam = jnp.exp(s1) - jnp.exp(s2) + lam_init
    outs = []
    for h in range(N_HEADS):
        k1 = (2 * h) % 4
        a1 = acc_ref[h // 2, k1 * blk:(k1 + 1) * blk, :]
        a2 = acc_ref[h // 2, (k1 + 1) * blk:(k1 + 2) * blk, :]
        d = a1[:, :LANES] / a1[:, LANES:] - lam * (a2[:, :LANES] / a2[:, LANES:])
        mine = _half_mask(d.shape, h % 2)
        ms = jnp.sum(jnp.where(mine, d * d, 0.0), axis=-1, keepdims=True) * (1.0 / HEAD_DIM)
        outs.append(d * lax.rsqrt(ms + EPS) * (1.0 - lam_init))
    lo_half = _half_mask((blk, LANES), 0)
    o = jnp.concatenate([jnp.where(lo_half, outs[0], outs[1]), jnp.where(lo_half, outs[2], outs[3])], axis=1)
    o_ref[0] = (o * gm_ref[...]).astype(o_ref.dtype)


def _differential(proj, dl, gm, lam_init, blk=256):
    b, s, _ = proj.shape
    coef, pos = _position_tables(SLOPES_DIFF, s, blk)
    return pl.pallas_call(
        functools.partial(_diff_kernel, blk=blk, lam_init=lam_init),
        grid=(b, s // blk),
        in_specs=[pl.BlockSpec((1, blk, 256), lambda bi, i: (bi, i, COL_QC)),
                  pl.BlockSpec((1, s, 256), lambda bi, i: (bi, 0, COL_KC)),
                  pl.BlockSpec((1, s, 256), lambda bi, i: (bi, 0, COL_VC)),
                  pl.BlockSpec((s, LANES), lambda bi, i: (0, 0)),
                  pl.BlockSpec((N_HEADS, LANES), lambda bi, i: (0, 0)),
                  pl.BlockSpec((4, DIFF_QK_DIM), lambda bi, i: (0, 0)),
                  pl.BlockSpec((1, 256), lambda bi, i: (0, 0))],
        out_specs=pl.BlockSpec((1, blk, 256), lambda bi, i: (bi, i, 0)),
        out_shape=jax.ShapeDtypeStruct((b, s, 256), BF16),
        scratch_shapes=[pltpu.VMEM((2, 4 * blk, 2 * LANES), BF16), pltpu.VMEM((2, 4 * blk, LANES), F32),
                        pltpu.VMEM((2, 4 * blk, 2 * LANES), F32)],
        compiler_params=_cparams("parallel", "arbitrary"),
        name="differential",
    )(proj, proj, proj, pos, coef, dl, gm)


def _swa_kernel(q_ref, k_ref, v_ref, sink_ref, gm_ref, o_ref, *, blk, sub):
    i = pl.program_id(1)
    row = lax.broadcasted_iota(jnp.int32, (blk, 2 * blk), 0)
    col = lax.broadcasted_iota(jnp.int32, (blk, 2 * blk), 1)
    lo_half = _half_mask((blk, LANES), 0)
    blocks = []
    for s_ in range(sub):
        g = i * sub + s_
        first = jnp.maximum(g - 1, 0)
        start = pl.multiple_of(first * blk, blk)
        dist = (row - col) + (g - first) * blk
        mask = (dist >= 0) & (dist < SWA_WINDOW)
        distf = dist.astype(F32)
        q = q_ref[0, s_ * blk:(s_ + 1) * blk, :]
        outs = []
        for h in range(N_HEADS):
            lanes = slice((h // 2) * LANES, (h // 2 + 1) * LANES)
            slab = q[:, lanes]
            qm = jnp.where(_half_mask(slab.shape, h % 2), slab, jnp.zeros_like(slab))
            kb = k_ref[0, pl.ds(start, 2 * blk), lanes]
            vb = v_ref[0, pl.ds(start, 2 * blk), lanes]
            z = _nt(qm, kb) * (HEAD_DIM ** -0.5) - SLOPES_SWA[h] * distf
            z = jnp.where(mask, z, NEG_INF)
            sink = sink_ref[h]
            m = jnp.maximum(jnp.max(z, axis=-1, keepdims=True), sink)
            p = jnp.exp(z - m)
            l = jnp.sum(p, axis=-1, keepdims=True) + jnp.exp(sink - m)
            outs.append(_nn(p.astype(BF16), vb) / l)
        blocks.append(jnp.concatenate([jnp.where(lo_half, outs[0], outs[1]), jnp.where(lo_half, outs[2], outs[3])],
                                      axis=1))
    _group_norm_store(o_ref, gm_ref, jnp.concatenate(blocks, axis=0))


def _sliding_window(proj, sinks, gm, blk=128, sub=4):
    b, s, _ = proj.shape
    rows = blk * sub
    return pl.pallas_call(
        functools.partial(_swa_kernel, blk=blk, sub=sub),
        grid=(b, s // rows),
        in_specs=[pl.BlockSpec((1, rows, 256), lambda bi, i: (bi, i, COL_QD)),
                  pl.BlockSpec((1, s, 256), lambda bi, i: (bi, 0, COL_KD)),
                  pl.BlockSpec((1, s, 256), lambda bi, i: (bi, 0, COL_VD)),
                  pl.BlockSpec(memory_space=pltpu.SMEM),
                  pl.BlockSpec((1, 256), lambda bi, i: (0, 0))],
        out_specs=pl.BlockSpec((1, rows, 256), lambda bi, i: (bi, i, 0)),
        out_shape=jax.ShapeDtypeStruct((b, s, 256), BF16),
        compiler_params=_cparams("parallel", "arbitrary"),
        name="sliding_window",
    )(proj, proj, proj, sinks, gm)


def _nsa_kernel(q_ref, g_ref, cmp_ref, ks_ref, vs_ref, kw_ref, vw_ref, pos_ref, coef_ref, gm_ref, o_ref,
                qa_ref, mx_ref, acc_ref, *, blk, n_cmp, n_sel):
    i = pl.program_id(1)
    q = q_ref[0]
    lane = lax.broadcasted_iota(jnp.int32, (blk, LANES), 1)
    for h in range(N_HEADS):
        slab = q[:, (h // 2) * LANES:(h // 2 + 1) * LANES]
        qa_ref[h * blk:(h + 1) * blk, 0:LANES] = jnp.where(_half_mask(slab.shape, h % 2), slab, jnp.zeros_like(slab))
    qs = qa_ref[:, 0:LANES]
    coefs = [jnp.broadcast_to(coef_ref[h:h + 1, :], (blk, LANES)) for h in range(N_HEADS)]

    tq = i * blk + lax.broadcasted_iota(jnp.int32, (blk, LANES), 0)

    kc = cmp_ref[0, :, 0:LANES]
    vc = cmp_ref[0, :, LANES:2 * LANES]
    dist_c = tq - (CMP_STRIDE * lane + CMP_LEN - 1)
    ok_c = (dist_c >= 0) & (lane < n_cmp)
    dist_cf = dist_c.astype(F32)
    bias_c = jnp.concatenate([jnp.where(ok_c, (-SLOPES_NSA[h] * LOG2E) * dist_cf, NEG_INF)
                              for h in range(N_HEADS)], axis=0)
    z = _nt(qs, kc) + bias_c
    p = jnp.exp2(z - jnp.max(z, axis=-1, keepdims=True))
    p = p / jnp.sum(p, axis=-1, keepdims=True)
    p = jnp.where(bias_c > 0.5 * NEG_INF, p, 0.0).astype(BF16)
    o_cmp = _nn(p, vc)

    ci = lax.broadcasted_iota(jnp.int32, (LANES, LANES), 0)
    cj = lax.broadcasted_iota(jnp.int32, (LANES, LANES), 1)
    to_sel = jnp.where((ci * CMP_STRIDE) // SEL_LEN + SEL_LANE == cj, 1.0, 0.0).astype(BF16)
    imp = _nn(p[0:blk], to_sel)
    for h in range(1, N_HEADS):
        imp = imp + _nn(p[h * blk:(h + 1) * blk], to_sel)
    blk_id = lane - SEL_LANE
    in_sel = (blk_id >= 0) & (blk_id < n_sel)
    cur = tq // SEL_LEN
    forced = (blk_id == 0) | (blk_id == cur) | (blk_id == cur - 1)
    score = jnp.where(blk_id > cur, NEG_INF, jnp.where(forced, FORCE_SCORE, imp))
    score = jnp.where(in_sel, score, -3.0e38)
    sel = jnp.zeros((blk, LANES), F32)
    lanef = lane.astype(F32)
    for _ in range(min(SEL_TOPN, n_sel)):
        best = jnp.max(score, axis=-1, keepdims=True)
        idx = jnp.min(jnp.where(score == best, lanef, float(LANES)), axis=-1, keepdims=True)
        hit = lanef == idx
        sel = jnp.where(hit, 1.0, sel)
        score = jnp.where(hit, -3.4e38, score)
    sel_bias = jnp.where(sel > 0.5, 0.0, MASK_BIAS)

    row = lax.broadcasted_iota(jnp.int32, (blk, blk), 0)
    col = lax.broadcasted_iota(jnp.int32, (blk, blk), 1)
    causal = col <= row
    in_window = col > row
    ones = jnp.ones((blk, LANES), BF16)

    def process(k_ref, v_ref, blocks, second):
        groups = (slice(0, 2 * blk), slice(2 * blk, 4 * blk))
        chains = [(pl.ds(pl.multiple_of(j * blk, blk), blk), mask, g) for j, mask in blocks for g in groups]
        zs = []
        for rows, mask, g in chains:
            ka = jnp.concatenate([k_ref[0, rows, :], pos_ref[rows, :]], axis=1)
            zs.append(_nt(qa_ref[g, :], ka))

        def masked(x, mask, fill):
            if mask is None:
                return x
            return jnp.concatenate([jnp.where(mask, x[r * blk:(r + 1) * blk], fill) for r in range(x.shape[0] // blk)],
                                   axis=0)

        if second:
            ps = [masked(jnp.exp2(z), mask, 0.0).astype(BF16) for (rows, mask, g), z in zip(chains, zs)]
            outs = []
            for (rows, mask, g), p in zip(chains, ps):
                vo = jnp.concatenate([v_ref[0, rows, :], ones], axis=1)
                outs.append(_nn(p, vo))
            for g in groups:
                mine = [o for (_, _, cg), o in zip(chains, outs) if cg == g]
                acc_ref[g, :] += functools.reduce(jnp.add, mine)
        else:
            outs = []
            for (rows, mask, g), z in zip(chains, zs):
                z = masked(z, mask, NEG_INF)
                zm = z[:, 0:LANES]
                for c in range(1, blk // LANES):
                    zm = jnp.maximum(zm, z[:, c * LANES:(c + 1) * LANES])
                outs.append(zm)
            for g in groups:
                mine = [o for (_, _, cg), o in zip(chains, outs) if cg == g]
                mx_ref[g, :] = jnp.maximum(mx_ref[g, :], functools.reduce(jnp.maximum, mine))

    def attend(sweep, aug):
        for h in range(N_HEADS):
            qa_ref[h * blk:(h + 1) * blk, LANES:2 * LANES] = aug[h].astype(BF16)
        mx_ref[...] = jnp.full(mx_ref.shape, NEG_INF, F32)
        acc_ref[...] = jnp.zeros_like(acc_ref)
        sweep(False)
        m = jnp.max(mx_ref[...], axis=-1, keepdims=True)
        m_hi = m.astype(BF16)
        m_lo = (m - m_hi.astype(F32)).astype(BF16)
        lane4 = lax.broadcasted_iota(jnp.int32, (N_HEADS * blk, LANES), 1)
        old = qa_ref[:, LANES:2 * LANES]
        qa_ref[:, LANES:2 * LANES] = jnp.where(lane4 == POS_ONE, -m_hi, jnp.where(lane4 == POS_ONE + 1, -m_lo, old))
        sweep(True)
        acc = acc_ref[...]
        return acc[:, 0:LANES] / acc[:, LANES:2 * LANES]

    def sel_sweep(second):
        def body(tt, c):
            process(ks_ref, vs_ref, [(2 * tt, None), (2 * tt + 1, None)], second)
            return c
        lax.fori_loop(0, i // 2, body, 0)

        @pl.when(i % 2 == 1)
        def _():
            process(ks_ref, vs_ref, [(i - 1, None), (i, causal)], second)

        @pl.when(i % 2 == 0)
        def _():
            process(ks_ref, vs_ref, [(i, causal)], second)

    def win_sweep(second):
        process(kw_ref, vw_ref, [(jnp.maximum(i - 2, 0), in_window & (i >= 2)),
                                 (jnp.maximum(i - 1, 0), jnp.broadcast_to(i >= 1, (blk, blk))),
                                 (i, causal)], second)

    o_sel = attend(sel_sweep, [jnp.where(in_sel, sel_bias, coefs[h]) for h in range(N_HEADS)])
    o_win = attend(win_sweep, coefs)

    g = jax.nn.sigmoid(g_ref[0].astype(F32))
    outs = []
    for h in range(N_HEADS):
        r = slice(h * blk, (h + 1) * blk)
        outs.append(g[:, 3 * h:3 * h + 1] * o_cmp[r] + g[:, 3 * h + 1:3 * h + 2] * o_sel[r]
                    + g[:, 3 * h + 2:3 * h + 3] * o_win[r])
    lo_half = _half_mask((blk, LANES), 0)
    o = jnp.concatenate([jnp.where(lo_half, outs[0], outs[1]), jnp.where(lo_half, outs[2], outs[3])], axis=1)
    _group_norm_store(o_ref, gm_ref, o)


def _native_sparse(proj, cmp_kv, gm, blk=256):
    b, s, _ = proj.shape
    n_cmp = (s - CMP_LEN) // CMP_STRIDE + 1
    n_sel = s // SEL_LEN
    assert cmp_kv.shape[1] == LANES and n_cmp <= LANES and SEL_LANE + n_sel <= LANES
    assert NSA_WINDOW == 2 * blk
    coef, pos = _position_tables(SLOPES_NSA, s, blk, sel_blocks=True)
    slab = lambda c: pl.BlockSpec((1, s, LANES), lambda bi, i: (bi, 0, c))
    rows = N_HEADS * blk
    return pl.pallas_call(
        functools.partial(_nsa_kernel, blk=blk, n_cmp=n_cmp, n_sel=n_sel),
        grid=(b, s // blk),
        in_specs=[pl.BlockSpec((1, blk, 256), lambda bi, i: (bi, i, COL_QN)),
                  pl.BlockSpec((1, blk, LANES), lambda bi, i: (bi, i, COL_GN)),
                  pl.BlockSpec((1, LANES, 2 * LANES), lambda bi, i: (bi, 0, 0)),
                  slab(COL_KS), slab(COL_VS), slab(COL_KW), slab(COL_VW),
                  pl.BlockSpec((s, LANES), lambda bi, i: (0, 0)),
                  pl.BlockSpec((N_HEADS, LANES), lambda bi, i: (0, 0)),
                  pl.BlockSpec((1, 256), lambda bi, i: (0, 0))],
        out_specs=pl.BlockSpec((1, blk, 256), lambda bi, i: (bi, i, 0)),
        out_shape=jax.ShapeDtypeStruct((b, s, 256), BF16),
        scratch_shapes=[pltpu.VMEM((rows, 2 * LANES), BF16), pltpu.VMEM((rows, LANES), F32),
                        pltpu.VMEM((rows, 2 * LANES), F32)],
        compiler_params=_cparams("parallel", "arbitrary"),
        name="native_sparse",
    )(proj, proj, cmp_kv, proj, proj, proj, proj, pos, coef, gm)


def _outproj_kernel(a_ref, b_ref, c_ref, d_ref, w_ref, x_ref, o_ref):
    acc = x_ref[...]
    for g, r in enumerate((a_ref, b_ref, c_ref, d_ref)):
        acc = acc + _nn(r[...], w_ref[g * GROUP_WIDTH:(g + 1) * GROUP_WIDTH, :])
    o_ref[...] = acc


def _outproj(mixes, w, x2, tm=512):
    m = x2.shape[0]
    grp = pl.BlockSpec((tm, GROUP_WIDTH), lambda i: (i, 0))
    return pl.pallas_call(
        _outproj_kernel,
        grid=(m // tm,),
        in_specs=[grp, grp, grp, grp, pl.BlockSpec((D_MODEL, D_MODEL), lambda i: (0, 0)),
                  pl.BlockSpec((tm, D_MODEL), lambda i: (i, 0))],
        out_specs=pl.BlockSpec((tm, D_MODEL), lambda i: (i, 0)),
        out_shape=jax.ShapeDtypeStruct((m, D_MODEL), F32),
        compiler_params=_cparams("parallel"),
        name="outproj",
    )(*mixes, w, x2)


def _mlp_kernel(x_ref, g_ref, wu_ref, wd_ref, gf_ref, o_ref, h_ref, acc_ref, *, final_norm):
    f = pl.program_id(1)

    @pl.when(f == 0)
    def _():
        h_ref[...] = (_rms(x_ref[...]) * g_ref[...]).astype(BF16)
        acc_ref[...] = jnp.zeros_like(acc_ref)

    u = jnp.maximum(_nn(h_ref[...], wu_ref[...]), 0.0)
    acc_ref[...] += _nn((u * u).astype(BF16), wd_ref[...])

    @pl.when(f == pl.num_programs(1) - 1)
    def _():
        y = x_ref[...] + acc_ref[...]
        if final_norm:
            y = _rms(y) * gf_ref[...]
        o_ref[...] = y


def _mlp(x2, gain, wu, wd, gfinal, final_norm, tm=1024, tf=1024):
    m = x2.shape[0]
    return pl.pallas_call(
        functools.partial(_mlp_kernel, final_norm=final_norm),
        grid=(m // tm, D_FF // tf),
        in_specs=[pl.BlockSpec((tm, D_MODEL), lambda i, f: (i, 0)),
                  pl.BlockSpec((1, D_MODEL), lambda i, f: (0, 0)),
                  pl.BlockSpec((D_MODEL, tf), lambda i, f: (0, f)),
                  pl.BlockSpec((tf, D_MODEL), lambda i, f: (f, 0)),
                  pl.BlockSpec((1, D_MODEL), lambda i, f: (0, 0))],
        out_specs=pl.BlockSpec((tm, D_MODEL), lambda i, f: (i, 0)),
        out_shape=jax.ShapeDtypeStruct((m, D_MODEL), F32),
        scratch_shapes=[pltpu.VMEM((tm, D_MODEL), BF16), pltpu.VMEM((tm, D_MODEL), F32)],
        compiler_params=_cparams("parallel", "arbitrary"),
        name="mlp",
    )(x2, gain, wu, wd, gfinal)


def _compress_weights(pe_k, w1_k, w2_k, pe_v, w1_v, w2_v):
    half = CMP_STRIDE
    d = HEAD_DIM

    def halves(w1):
        w = w1.reshape(CMP_LEN, d, CMP_HIDDEN)
        return w[:half], w[half:]

    ka, kb = halves(w1_k)
    va, vb = halves(w1_v)
    zero = jnp.zeros_like(ka)

    def merge(wk, wv):
        top = jnp.concatenate([wk, zero], axis=-1)
        bot = jnp.concatenate([zero, wv], axis=-1)
        return jnp.concatenate([top, bot], axis=1).reshape(half * 2 * d, 2 * CMP_HIDDEN).astype(BF16)

    w1a, w1b = merge(ka, va), merge(kb, vb)
    pe = jnp.concatenate([pe_k, pe_v], axis=-1)
    pea = pe[:half].reshape(1, half * 2 * d)
    peb = pe[half:].reshape(1, half * 2 * d)
    zk = jnp.zeros_like(w2_k)
    w2 = jnp.concatenate([jnp.concatenate([w2_k, w2_k, zk, zk], axis=1),
                          jnp.concatenate([zk, zk, w2_v, w2_v], axis=1)], axis=0).astype(BF16)
    return pea, peb, w1a, w1b, w2


def kernel(x, norm_attn, w_in, cmp_pe_k, cmp_w1_k, cmp_w2_k, cmp_pe_v, cmp_w1_v, cmp_w2_v, diff_lq1, diff_lk1,
           diff_lq2, diff_lk2, sinks, g_mix, w_out, norm_mlp, w_up, w_down, norm_final):
    b, s, d = x.shape
    m = b * s
    depth = w_in.shape[0]
    cols = jnp.asarray(_COLS)
    w_in_p = jnp.take(jnp.concatenate([w_in, jnp.zeros((depth, d, 1), w_in.dtype)], axis=2), cols,
                      axis=2).astype(BF16)
    w_out_b = w_out.astype(BF16)
    w_up_b = w_up.astype(BF16)
    w_down_b = w_down.astype(BF16)
    gfinal = norm_final.reshape(1, d)

    colscale = jnp.asarray(_column_scales())

    x2 = x.reshape(m, d)
    for l in range(depth):
        proj = _inproj(x2, norm_attn[l].reshape(1, d), w_in_p[l], colscale).reshape(b, s, NP)
        gm = g_mix[l].reshape(N_HEADS, 1, GROUP_WIDTH)

        chunks = proj[:, :, COL_KVC * LANES:(COL_KVC + 1) * LANES].reshape(b, s // CMP_STRIDE, CMP_STRIDE * LANES)
        cmp_kv = _compress(chunks, *_compress_weights(cmp_pe_k[l], cmp_w1_k[l], cmp_w2_k[l],
                                                      cmp_pe_v[l], cmp_w1_v[l], cmp_w2_v[l]))

        o_sb = _stick_breaking(proj, gm[0])
        o_nsa = _native_sparse(proj, cmp_kv, gm[1])
        lam_init = 0.8 - 0.6 * math.exp(-0.3 * l)
        dl = jnp.stack([diff_lq1[l], diff_lk1[l], diff_lq2[l], diff_lk2[l]]).astype(F32)
        o_diff = _differential(proj, dl, gm[2], lam_init)
        o_swa = _sliding_window(proj, sinks[l].astype(F32), gm[3])

        mixes = [o.reshape(m, GROUP_WIDTH) for o in (o_sb, o_nsa, o_diff, o_swa)]
        x2 = _outproj(mixes, w_out_b[l], x2)
        x2 = _mlp(x2, norm_mlp[l].reshape(1, d), w_up_b[l], w_down_b[l], gfinal, final_norm=(l == depth - 1))
    return x2.reshape(b, s, d)
```

```python
import functools
import math

import ml_dtypes
import numpy as np
import jax
import jax.numpy as jnp
from jax import lax
from jax.experimental import pallas as pl
from jax.experimental.pallas import tpu as pltpu

F32 = jnp.float32
BF16 = jnp.bfloat16

D_MODEL = 1024
DEPTH = 4
HEAD_DIM = 64
GROUP_WIDTH = 256
N_HEADS = 4
D_FF = 4 * D_MODEL
EPS = 1e-6
NEG_INF = -1e30
LOG2E = math.log2(math.e)
LANES = 128

CMP_LEN = 32
CMP_STRIDE = 16
CMP_HIDDEN = 128
SEL_LEN = 64
SEL_TOPN = 8
FORCE_SCORE = 1e9
NSA_WINDOW = 512
SWA_WINDOW = 128
DIFF_QK_DIM = 32

VMEM_LIMIT = 48 * 1024 * 1024

_ORIG = dict(qa=(0, 256), ka=(256, 256), va=(512, 256), qn=(768, 256), kcn=(1024, 64), vcn=(1088, 64),
             ksn=(1152, 64), vsn=(1216, 64), kwn=(1280, 64), vwn=(1344, 64), gn=(1408, 12),
             qc=(1420, 256), kc=(1676, 256), vc=(1932, 256), qd=(2188, 256), kd=(2444, 128), vd=(2572, 128))
IN_COLS = 2700


def _layout():
    cols = []

    def put(name, lo=0, n=None):
        start, size = _ORIG[name]
        n = size - lo if n is None else n
        cols.extend(range(start + lo, start + lo + n))

    def zeros(n):
        cols.extend([IN_COLS] * n)

    for nm in ("qa", "ka", "va", "qn", "kcn", "vcn"):
        put(nm)
    for nm in ("ksn", "vsn", "kwn", "vwn"):
        put(nm)
        put(nm)
    put("gn")
    zeros(LANES - 12)
    for nm in ("qc", "kc", "vc", "qd"):
        put(nm)
    for nm in ("kd", "vd"):
        put(nm, 0, 64)
        put(nm, 0, 64)
        put(nm, 64, 64)
        put(nm, 64, 64)
    return np.asarray(cols, np.int32)


_COLS = _layout()
NP = int(_COLS.shape[0])


def _column_scales():
    cs = np.ones((1, NP), np.float32)
    cs[0, 0:256] = HEAD_DIM ** -0.5 * LOG2E
    cs[0, 768:1024] = HEAD_DIM ** -0.5 * LOG2E
    cs[0, 1792:2048] = DIFF_QK_DIM ** -0.5 * LOG2E
    return cs


COL_QA, COL_KA, COL_VA = 0, 1, 2
COL_QN = 3
COL_KVC = 8
COL_KS, COL_VS, COL_KW, COL_VW, COL_GN = 9, 10, 11, 12, 13
COL_QC, COL_KC, COL_VC = 7, 8, 9
COL_QD, COL_KD, COL_VD = 10, 11, 12


def _alibi_slopes():
    m = 2.0 ** (-8.0 * np.arange(1, 13) / 12.0)
    m = m.astype(np.float32).reshape(4, 3)
    return [float(v) for v in m[:, 0]], [float(v) for v in m[:, 1]], [float(v) for v in m[:, 2]]


SLOPES_NSA, SLOPES_DIFF, SLOPES_SWA = _alibi_slopes()


def _nt(a, b):
    return lax.dot_general(a, b, (((1,), (1,)), ((), ())), preferred_element_type=F32)


def _nn(a, b):
    return jnp.dot(a, b, preferred_element_type=F32)


def _cparams(*sem):
    return pltpu.CompilerParams(dimension_semantics=sem, vmem_limit_bytes=VMEM_LIMIT)


def _rms(x):
    return x * lax.rsqrt(jnp.mean(x * x, axis=-1, keepdims=True) + EPS)


def _inproj_kernel(x_ref, g_ref, w_ref, cs_ref, o_ref):
    h = _rms(x_ref[...]) * g_ref[...]
    o_ref[...] = (_nn(h.astype(BF16), w_ref[...]) * cs_ref[...]).astype(BF16)


def _inproj(x2, gain, w, colscale, tm=512):
    m = x2.shape[0]
    return pl.pallas_call(
        _inproj_kernel,
        grid=(m // tm,),
        in_specs=[pl.BlockSpec((tm, D_MODEL), lambda i: (i, 0)),
                  pl.BlockSpec((1, D_MODEL), lambda i: (0, 0)),
                  pl.BlockSpec((D_MODEL, NP), lambda i: (0, 0)),
                  pl.BlockSpec((1, NP), lambda i: (0, 0))],
        out_specs=pl.BlockSpec((tm, NP), lambda i: (i, 0)),
        out_shape=jax.ShapeDtypeStruct((m, NP), BF16),
        compiler_params=_cparams("parallel"),
        name="inproj",
    )(x2, gain, w, colscale)


def _compress_kernel(c_ref, pea_ref, peb_ref, w1a_ref, w1b_ref, w2_ref, o_ref):
    c = c_ref[0].astype(F32)
    p = _nn((c + pea_ref[...]).astype(BF16), w1a_ref[...])
    r = _nn((c + peb_ref[...]).astype(BF16), w1b_ref[...])
    n = c.shape[0]
    r_next = pltpu.roll(r, n - 1, 0)
    hid = jax.nn.gelu(p + r_next)
    o_ref[0] = _nn(hid.astype(BF16), w2_ref[...]).astype(BF16)


def _compress(chunks, pea, peb, w1a, w1b, w2):
    b, n, w = chunks.shape
    full = lambda a: pl.BlockSpec(a.shape, lambda i: (0,) * a.ndim)
    return pl.pallas_call(
        _compress_kernel,
        grid=(b,),
        in_specs=[pl.BlockSpec((1, n, w), lambda i: (i, 0, 0)), full(pea), full(peb), full(w1a), full(w1b),
                  full(w2)],
        out_specs=pl.BlockSpec((1, n, 2 * LANES), lambda i: (i, 0, 0)),
        out_shape=jax.ShapeDtypeStruct((b, n, 2 * LANES), BF16),
        compiler_params=_cparams("parallel"),
        name="nsa_compress",
    )(chunks, pea, peb, w1a, w1b, w2)


def _half_mask(shape, half):
    lane = lax.broadcasted_iota(jnp.int32, shape, len(shape) - 1)
    return (lane // HEAD_DIM) == half


def _group_norm_store(o_ref, gm_ref, o):
    o_ref[0] = (_rms(o) * gm_ref[...]).astype(o_ref.dtype)


def _sb_kernel(q_ref, k_ref, v_ref, gm_ref, o_ref, acc_ref, carry_ref, lb_ref, hl_ref, tot_ref, *, blk):
    i = pl.program_id(1)
    q = q_ref[0]
    row = lax.broadcasted_iota(jnp.int32, (blk, blk), 0)
    col = lax.broadcasted_iota(jnp.int32, (blk, blk), 1)
    past = col < row
    u = jnp.where(row > col, 1.0, 0.0).astype(BF16)
    uu = jnp.concatenate([u, u], axis=0)
    qm = []
    for h in range(N_HEADS):
        slab = q[:, (h // 2) * LANES:(h // 2 + 1) * LANES]
        qm.append(jnp.where(_half_mask(slab.shape, h % 2), slab, jnp.zeros_like(slab)))
    qm = [jnp.concatenate(qm[0:2], axis=0), jnp.concatenate(qm[2:4], axis=0)]
    acc_ref[...] = jnp.zeros_like(acc_ref)
    carry_ref[...] = jnp.zeros_like(carry_ref)

    def per_head(x, fn):
        return jnp.concatenate([fn(x[k * blk:(k + 1) * blk]) for k in range(x.shape[0] // blk)], axis=0)

    def a_logits(j):
        rows = pl.ds(pl.multiple_of(j * blk, blk), blk)
        return [_nt(qm[pr], k_ref[0, rows, pr * LANES:(pr + 1) * LANES]) for pr in range(2)]

    def a_finish(zs, slot, diag):
        for pr, z in enumerate(zs):
            r2 = slice(2 * pr * blk, 2 * (pr + 1) * blk)
            sp = jnp.log2(1.0 + jnp.exp2(-jnp.abs(z)))
            lb = jnp.minimum(z, 0.0) - sp
            lk = lb - z
            if diag:
                lk = per_head(lk, lambda t: jnp.where(past, t, 0.0))
                lb = per_head(lb, lambda t: jnp.where(past, t, NEG_INF))
            hi = lk.astype(BF16)
            lb_ref[slot, r2, :] = lb
            hl_ref[slot, r2, 0:blk] = hi
            hl_ref[slot, r2, blk:2 * blk] = (lk - hi.astype(F32)).astype(BF16)
            tot_ref[slot, r2, :] = jnp.broadcast_to(jnp.sum(lk, axis=-1, keepdims=True), (2 * blk, LANES))

    def b_suffix(slot):
        return _nn(hl_ref[slot], uu)

    def b_finish(cs, j, slot):
        rows = pl.ds(pl.multiple_of(j * blk, blk), blk)
        carry = carry_ref[...]
        a = jnp.exp2(lb_ref[slot] + cs + jnp.concatenate([carry] * (blk // LANES), axis=1)).astype(BF16)
        for pr in range(2):
            r2 = slice(2 * pr * blk, 2 * (pr + 1) * blk)
            acc_ref[r2, :] += _nn(a[r2], v_ref[0, rows, pr * LANES:(pr + 1) * LANES])
        carry_ref[...] = carry + tot_ref[slot]

    a_finish(a_logits(i), 0, True)

    def step(t, slot):
        zs = a_logits(i - 1 - t)
        cs = b_suffix(slot)
        a_finish(zs, 1 - slot, False)
        b_finish(cs, i - t, slot)

    def body(tt, c):
        step(2 * tt, 0)
        step(2 * tt + 1, 1)
        return c

    lax.fori_loop(0, i // 2, body, 0)

    @pl.when(i % 2 == 1)
    def _():
        step(i - 1, 0)
        b_finish(b_suffix(1), 0, 1)

    @pl.when(i % 2 == 0)
    def _():
        b_finish(b_suffix(0), 0, 0)

    lo_half = _half_mask((blk, LANES), 0)
    acc = [acc_ref[h * blk:(h + 1) * blk, :] for h in range(N_HEADS)]
    o = jnp.concatenate([jnp.where(lo_half, acc[0], acc[1]), jnp.where(lo_half, acc[2], acc[3])], axis=1)
    _group_norm_store(o_ref, gm_ref, o)


def _stick_breaking(proj, gm, blk=256):
    b, s, _ = proj.shape
    stage = lambda w, dt: pltpu.VMEM((2, N_HEADS * blk, w), dt)
    return pl.pallas_call(
        functools.partial(_sb_kernel, blk=blk),
        grid=(b, s // blk),
        in_specs=[pl.BlockSpec((1, blk, 256), lambda bi, i: (bi, i, COL_QA)),
                  pl.BlockSpec((1, s, 256), lambda bi, i: (bi, 0, COL_KA)),
                  pl.BlockSpec((1, s, 256), lambda bi, i: (bi, 0, COL_VA)),
                  pl.BlockSpec((1, 256), lambda bi, i: (0, 0))],
        out_specs=pl.BlockSpec((1, blk, 256), lambda bi, i: (bi, i, 0)),
        out_shape=jax.ShapeDtypeStruct((b, s, 256), BF16),
        scratch_shapes=[pltpu.VMEM((N_HEADS * blk, LANES), F32), pltpu.VMEM((N_HEADS * blk, LANES), F32),
                        stage(blk, F32), stage(2 * blk, BF16), stage(LANES, F32)],
        compiler_params=_cparams("parallel", "arbitrary"),
        name="stick_breaking",
    )(proj, proj, proj, gm)


N_MAPS = 2 * N_HEADS
POS_LOCAL, POS_BLOCK, POS_ONE = 0, 3, 6
SEL_LANE = 8
MASK_BIAS = -2.0 ** 100


def _bf16_pieces(x, n):
    out = []
    r = np.float32(x)
    for _ in range(n):
        p = np.float32(r.astype(ml_dtypes.bfloat16))
        out.append(float(p))
        r = np.float32(r - p)
    return out


def _position_tables(slopes, s, blk, sel_blocks=False):
    coef = np.zeros((N_HEADS, LANES), np.float32)
    for h in range(N_HEADS):
        pieces = _bf16_pieces(slopes[h] * LOG2E, 3)
        coef[h, POS_LOCAL:POS_LOCAL + 3] = pieces
        coef[h, POS_BLOCK:POS_BLOCK + 3] = [blk * p for p in pieces]
    pos = np.zeros((s, LANES), np.float32)
    idx = np.arange(s)
    pos[:, POS_LOCAL:POS_LOCAL + 3] = (idx % blk)[:, None]
    pos[:, POS_BLOCK:POS_BLOCK + 3] = (idx // blk)[:, None]
    pos[:, POS_ONE:POS_ONE + 2] = 1.0
    if sel_blocks:
        pos[idx, SEL_LANE + idx // SEL_LEN] = 1.0
    return jnp.asarray(coef), jnp.asarray(pos, BF16)


def _diff_kernel(q_ref, k_ref, v_ref, pos_ref, coef_ref, dl_ref, gm_ref, o_ref, qa_ref, mx_ref, acc_ref,
                 *, blk, lam_init):
    i = pl.program_id(1)
    q = q_ref[0]
    row = lax.broadcasted_iota(jnp.int32, (blk, blk), 0)
    col = lax.broadcasted_iota(jnp.int32, (blk, blk), 1)
    causal = col <= row
    lane = lax.broadcasted_iota(jnp.int32, (blk, LANES), 1)
    for h in range(N_HEADS):
        slab = q[:, (h // 2) * LANES:(h // 2 + 1) * LANES]
        coef = jnp.broadcast_to(coef_ref[h:h + 1, :], (blk, LANES)).astype(BF16)
        for c in range(2):
            r = slice(((2 * h + c) % 4) * blk, ((2 * h + c) % 4 + 1) * blk)
            qa_ref[h // 2, r, 0:LANES] = jnp.where((lane // DIFF_QK_DIM) == (h % 2) * 2 + c, slab,
                                                   jnp.zeros_like(slab))
            qa_ref[h // 2, r, LANES:2 * LANES] = coef
    ones = jnp.ones((blk, LANES), BF16)

    def per_map(x, fn):
        return jnp.concatenate([fn(x[k * blk:(k + 1) * blk]) for k in range(4)], axis=0)

    def sweep(blocks, second, first):
        chains = [(pr, pl.ds(pl.multiple_of(j * blk, blk), blk), diag) for pr in range(2) for j, diag in blocks]
        zs = []
        for pr, rows, diag in chains:
            ka = jnp.concatenate([k_ref[0, rows, pr * LANES:(pr + 1) * LANES], pos_ref[rows, :]], axis=1)
            zs.append(_nt(qa_ref[pr], ka))
        outs = []
        if second:
            ps = []
            for (pr, rows, diag), z in zip(chains, zs):
                p = jnp.exp2(z)
                if diag:
                    p = per_map(p, lambda t: jnp.where(causal, t, 0.0))
                ps.append(p.astype(BF16))
            for (pr, rows, diag), p in zip(chains, ps):
                vo = jnp.concatenate([v_ref[0, rows, pr * LANES:(pr + 1) * LANES], ones], axis=1)
                outs.append(_nn(p, vo))
        else:
            for (pr, rows, diag), z in zip(chains, zs):
                if diag:
                    z = per_map(z, lambda t: jnp.where(causal, t, NEG_INF))
                zm = z[:, 0:LANES]
                for g in range(1, blk // LANES):
                    zm = jnp.maximum(zm, z[:, g * LANES:(g + 1) * LANES])
                outs.append(zm)
        ref, combine = (acc_ref, jnp.add) if second else (mx_ref, jnp.maximum)
        for pr in range(2):
            mine = [o for (cp, _, _), o in zip(chains, outs) if cp == pr]
            tot = mine[0]
            for o in mine[1:]:
                tot = combine(tot, o)
            ref[pr] = tot if first else combine(ref[pr], tot)

    def run(second):
        @pl.when(i % 2 == 1)
        def _():
            sweep([(i - 1, False), (i, True)], second, True)

        @pl.when(i % 2 == 0)
        def _():
            sweep([(i, True)], second, True)

        def body(tt, c):
            sweep([(2 * tt, False), (2 * tt + 1, False)], second, False)
            return c
        lax.fori_loop(0, i // 2, body, 0)

    run(False)
    lane4 = lax.broadcasted_iota(jnp.int32, (4 * blk, LANES), 1)
    for pr in range(2):
        m = jnp.max(mx_ref[pr], axis=-1, keepdims=True)
        m_hi = m.astype(BF16)
        m_lo = (m - m_hi.astype(F32)).astype(BF16)
        old = qa_ref[pr, :, LANES:2 * LANES]
        qa_ref[pr, :, LANES:2 * LANES] = jnp.where(lane4 == POS_ONE, -m_hi, jnp.where(lane4 == POS_ONE + 1, -m_lo, old))
    run(True)

    dl = dl_ref[...]
    s1 = jnp.sum(dl[0:1] * dl[1:2], axis=-1, keepdims=True)
    s2 = jnp.sum(dl[2:3] * dl[3:4], axis=-1, keepdims=True)
    lam = jnp.exp(s1) - jnp.exp(s2) + lam_init
    outs = []
    for h in range(N_HEADS):
        k1 = (2 * h) % 4
        a1 = acc_ref[h // 2, k1 * blk:(k1 + 1) * blk, :]
        a2 = acc_ref[h // 2, (k1 + 1) * blk:(k1 + 2) * blk, :]
        d = a1[:, :LANES] / a1[:, LANES:] - lam * (a2[:, :LANES] / a2[:, LANES:])
        mine = _half_mask(d.shape, h % 2)
        ms = jnp.sum(jnp.where(mine, d * d, 0.0), axis=-1, keepdims=True) * (1.0 / HEAD_DIM)
        outs.append(d * lax.rsqrt(ms + EPS) * (1.0 - lam_init))
    lo_half = _half_mask((blk, LANES), 0)
    o = jnp.concatenate([jnp.where(lo_half, outs[0], outs[1]), jnp.where(lo_half, outs[2], outs[3])], axis=1)
    o_ref[0] = (o * gm_ref[...]).astype(o_ref.dtype)


def _differential(proj, dl, gm, lam_init, blk=256):
    b, s, _ = proj.shape
    coef, pos = _position_tables(SLOPES_DIFF, s, blk)
    return pl.pallas_call(
        functools.partial(_diff_kernel, blk=blk, lam_init=lam_init),
        grid=(b, s // blk),
        in_specs=[pl.BlockSpec((1, blk, 256), lambda bi, i: (bi, i, COL_QC)),
                  pl.BlockSpec((1, s, 256), lambda bi, i: (bi, 0, COL_KC)),
                  pl.BlockSpec((1, s, 256), lambda bi, i: (bi, 0, COL_VC)),
                  pl.BlockSpec((s, LANES), lambda bi, i: (0, 0)),
                  pl.BlockSpec((N_HEADS, LANES), lambda bi, i: (0, 0)),
                  pl.BlockSpec((4, DIFF_QK_DIM), lambda bi, i: (0, 0)),
                  pl.BlockSpec((1, 256), lambda bi, i: (0, 0))],
        out_specs=pl.BlockSpec((1, blk, 256), lambda bi, i: (bi, i, 0)),
        out_shape=jax.ShapeDtypeStruct((b, s, 256), BF16),
        scratch_shapes=[pltpu.VMEM((2, 4 * blk, 2 * LANES), BF16), pltpu.VMEM((2, 4 * blk, LANES), F32),
                        pltpu.VMEM((2, 4 * blk, 2 * LANES), F32)],
        compiler_params=_cparams("parallel", "arbitrary"),
        name="differential",
    )(proj, proj, proj, pos, coef, dl, gm)


def _swa_kernel(q_ref, k_ref, v_ref, sink_ref, gm_ref, o_ref, *, blk, sub):
    i = pl.program_id(1)
    row = lax.broadcasted_iota(jnp.int32, (blk, 2 * blk), 0)
    col = lax.broadcasted_iota(jnp.int32, (blk, 2 * blk), 1)
    lo_half = _half_mask((blk, LANES), 0)
    blocks = []
    for s_ in range(sub):
        g = i * sub + s_
        first = jnp.maximum(g - 1, 0)
        start = pl.multiple_of(first * blk, blk)
        dist = (row - col) + (g - first) * blk
        mask = (dist >= 0) & (dist < SWA_WINDOW)
        distf = dist.astype(F32)
        q = q_ref[0, s_ * blk:(s_ + 1) * blk, :]
        outs = []
        for h in range(N_HEADS):
            lanes = slice((h // 2) * LANES, (h // 2 + 1) * LANES)
            slab = q[:, lanes]
            qm = jnp.where(_half_mask(slab.shape, h % 2), slab, jnp.zeros_like(slab))
            kb = k_ref[0, pl.ds(start, 2 * blk), lanes]
            vb = v_ref[0, pl.ds(start, 2 * blk), lanes]
            z = _nt(qm, kb) * (HEAD_DIM ** -0.5) - SLOPES_SWA[h] * distf
            z = jnp.where(mask, z, NEG_INF)
            sink = sink_ref[h]
            m = jnp.maximum(jnp.max(z, axis=-1, keepdims=True), sink)
            p = jnp.exp(z - m)
            l = jnp.sum(p, axis=-1, keepdims=True) + jnp.exp(sink - m)
            outs.append(_nn(p.astype(BF16), vb) / l)
        blocks.append(jnp.concatenate([jnp.where(lo_half, outs[0], outs[1]), jnp.where(lo_half, outs[2], outs[3])],
                                      axis=1))
    _group_norm_store(o_ref, gm_ref, jnp.concatenate(blocks, axis=0))


def _sliding_window(proj, sinks, gm, blk=128, sub=4):
    b, s, _ = proj.shape
    rows = blk * sub
    return pl.pallas_call(
        functools.partial(_swa_kernel, blk=blk, sub=sub),
        grid=(b, s // rows),
        in_specs=[pl.BlockSpec((1, rows, 256), lambda bi, i: (bi, i, COL_QD)),
                  pl.BlockSpec((1, s, 256), lambda bi, i: (bi, 0, COL_KD)),
                  pl.BlockSpec((1, s, 256), lambda bi, i: (bi, 0, COL_VD)),
                  pl.BlockSpec(memory_space=pltpu.SMEM),
                  pl.BlockSpec((1, 256), lambda bi, i: (0, 0))],
        out_specs=pl.BlockSpec((1, rows, 256), lambda bi, i: (bi, i, 0)),
        out_shape=jax.ShapeDtypeStruct((b, s, 256), BF16),
        compiler_params=_cparams("parallel", "arbitrary"),
        name="sliding_window",
    )(proj, proj, proj, sinks, gm)


def _nsa_kernel(q_ref, g_ref, cmp_ref, ks_ref, vs_ref, kw_ref, vw_ref, pos_ref, coef_ref, gm_ref, o_ref,
                qa_ref, mx_ref, acc_ref, *, blk, n_cmp, n_sel):
    i = pl.program_id(1)
    q = q_ref[0]
    lane = lax.broadcasted_iota(jnp.int32, (blk, LANES), 1)
    for h in range(N_HEADS):
        slab = q[:, (h // 2) * LANES:(h // 2 + 1) * LANES]
        qa_ref[h * blk:(h + 1) * blk, 0:LANES] = jnp.where(_half_mask(slab.shape, h % 2), slab, jnp.zeros_like(slab))
    qs = qa_ref[:, 0:LANES]
    coefs = [jnp.broadcast_to(coef_ref[h:h + 1, :], (blk, LANES)) for h in range(N_HEADS)]

    tq = i * blk + lax.broadcasted_iota(jnp.int32, (blk, LANES), 0)

    row = lax.broadcasted_iota(jnp.int32, (blk, blk), 0)
    col = lax.broadcasted_iota(jnp.int32, (blk, blk), 1)
    causal = col <= row
    in_window = col > row
    ones = jnp.ones((blk, LANES), BF16)

    def process(k_ref, v_ref, blocks, second):
        groups = (slice(0, 2 * blk), slice(2 * blk, 4 * blk))
        chains = [(pl.ds(pl.multiple_of(j * blk, blk), blk), mask, g) for j, mask in blocks for g in groups]
        zs = []
        for rows, mask, g in chains:
            ka = jnp.concatenate([k_ref[0, rows, :], pos_ref[rows, :]], axis=1)
            zs.append(_nt(qa_ref[g, :], ka))

        def masked(x, mask, fill):
            if mask is None:
                return x
            return jnp.concatenate([jnp.where(mask, x[r * blk:(r + 1) * blk], fill) for r in range(x.shape[0] // blk)],
                                   axis=0)

        if second:
            ps = [masked(jnp.exp2(z), mask, 0.0).astype(BF16) for (rows, mask, g), z in zip(chains, zs)]
            outs = []
            for (rows, mask, g), p in zip(chains, ps):
                vo = jnp.concatenate([v_ref[0, rows, :], ones], axis=1)
                outs.append(_nn(p, vo))
            for g in groups:
                mine = [o for (_, _, cg), o in zip(chains, outs) if cg == g]
                acc_ref[g, :] += functools.reduce(jnp.add, mine)
        else:
            outs = []
            for (rows, mask, g), z in zip(chains, zs):
                z = masked(z, mask, NEG_INF)
                zm = z[:, 0:LANES]
                for c in range(1, blk // LANES):
                    zm = jnp.maximum(zm, z[:, c * LANES:(c + 1) * LANES])
                outs.append(zm)
            for g in groups:
                mine = [o for (_, _, cg), o in zip(chains, outs) if cg == g]
                mx_ref[g, :] = jnp.maximum(mx_ref[g, :], functools.reduce(jnp.maximum, mine))

    def attend(sweep, aug):
        for h in range(N_HEADS):
            qa_ref[h * blk:(h + 1) * blk, LANES:2 * LANES] = aug[h].astype(BF16)
        mx_ref[...] = jnp.full(mx_ref.shape, NEG_INF, F32)
        acc_ref[...] = jnp.zeros_like(acc_ref)
        sweep(False)
        m = jnp.max(mx_ref[...], axis=-1, keepdims=True)
        m_hi = m.astype(BF16)
        m_lo = (m - m_hi.astype(F32)).astype(BF16)
        lane4 = lax.broadcasted_iota(jnp.int32, (N_HEADS * blk, LANES), 1)
        old = qa_ref[:, LANES:2 * LANES]
        qa_ref[:, LANES:2 * LANES] = jnp.where(lane4 == POS_ONE, -m_hi, jnp.where(lane4 == POS_ONE + 1, -m_lo, old))
        sweep(True)
        acc = acc_ref[...]
        return acc[:, 0:LANES] / acc[:, LANES:2 * LANES]

    def sel_sweep(second):
        def body(tt, c):
            process(ks_ref, vs_ref, [(2 * tt, None), (2 * tt + 1, None)], second)
            return c
        lax.fori_loop(0, i // 2, body, 0)

        @pl.when(i % 2 == 1)
        def _():
            process(ks_ref, vs_ref, [(i - 1, None), (i, causal)], second)

        @pl.when(i % 2 == 0)
        def _():
            process(ks_ref, vs_ref, [(i, causal)], second)

    def win_sweep(second):
        process(kw_ref, vw_ref, [(jnp.maximum(i - 2, 0), in_window & (i >= 2)),
                                 (jnp.maximum(i - 1, 0), jnp.broadcast_to(i >= 1, (blk, blk))),
                                 (i, causal)], second)

    o_win = attend(win_sweep, coefs)

    kc = cmp_ref[0, :, 0:LANES]
    vc = cmp_ref[0, :, LANES:2 * LANES]
    dist_c = tq - (CMP_STRIDE * lane + CMP_LEN - 1)
    ok_c = (dist_c >= 0) & (lane < n_cmp)
    dist_cf = dist_c.astype(F32)
    bias_c = jnp.concatenate([jnp.where(ok_c, (-SLOPES_NSA[h] * LOG2E) * dist_cf, NEG_INF)
                              for h in range(N_HEADS)], axis=0)
    z = _nt(qs, kc) + bias_c
    p = jnp.exp2(z - jnp.max(z, axis=-1, keepdims=True))
    p = p / jnp.sum(p, axis=-1, keepdims=True)
    p = jnp.where(bias_c > 0.5 * NEG_INF, p, 0.0).astype(BF16)
    o_cmp = _nn(p, vc)

    sel_rows = -(-(SEL_LANE + n_sel) // 8) * 8
    rj = lax.broadcasted_iota(jnp.int32, (sel_rows, LANES), 0)
    ri = lax.broadcasted_iota(jnp.int32, (sel_rows, LANES), 1)
    to_sel = jnp.where((ri * CMP_STRIDE) // SEL_LEN + SEL_LANE == rj, 1.0, 0.0).astype(BF16)
    imp = _nt(to_sel, p[0:blk])
    for h in range(1, N_HEADS):
        imp = imp + _nt(to_sel, p[h * blk:(h + 1) * blk])
    srow = lax.broadcasted_iota(jnp.int32, (sel_rows, blk), 0)
    blk_id = srow - SEL_LANE
    cur = (i * blk + lax.broadcasted_iota(jnp.int32, (sel_rows, blk), 1)) // SEL_LEN
    forced = (blk_id == 0) | (blk_id == cur) | (blk_id == cur - 1)
    score = jnp.where(blk_id > cur, NEG_INF, jnp.where(forced, FORCE_SCORE, imp))
    score = jnp.where((blk_id >= 0) & (blk_id < n_sel), score, -3.0e38)
    sel = jnp.zeros((sel_rows, blk), F32)
    srowf = srow.astype(F32)
    for _ in range(min(SEL_TOPN, n_sel)):
        best = jnp.max(score, axis=0, keepdims=True)
        idx = jnp.min(jnp.where(score == best, srowf, float(LANES)), axis=0, keepdims=True)
        hit = srowf == idx
        sel = jnp.where(hit, 1.0, sel)
        score = jnp.where(hit, -3.4e38, score)
    sel_bias = jnp.where(sel > 0.5, 0.0, MASK_BIAS)
    sel_bias = jnp.concatenate([sel_bias, jnp.zeros((LANES - sel_rows, blk), F32)], axis=0).T
    in_sel = (lane >= SEL_LANE) & (lane < SEL_LANE + n_sel)

    o_sel = attend(sel_sweep, [jnp.where(in_sel, sel_bias, coefs[h]) for h in range(N_HEADS)])

    g = jax.nn.sigmoid(g_ref[0].astype(F32))
    outs = []
    for h in range(N_HEADS):
        r = slice(h * blk, (h + 1) * blk)
        outs.append(g[:, 3 * h:3 * h + 1] * o_cmp[r] + g[:, 3 * h + 1:3 * h + 2] * o_sel[r]
                    + g[:, 3 * h + 2:3 * h + 3] * o_win[r])
    lo_half = _half_mask((blk, LANES), 0)
    o = jnp.concatenate([jnp.where(lo_half, outs[0], outs[1]), jnp.where(lo_half, outs[2], outs[3])], axis=1)
    _group_norm_store(o_ref, gm_ref, o)


def _native_sparse(proj, cmp_kv, gm, blk=256):
    b, s, _ = proj.shape
    n_cmp = (s - CMP_LEN) // CMP_STRIDE + 1
    n_sel = s // SEL_LEN
    assert cmp_kv.shape[1] == LANES and n_cmp <= LANES and SEL_LANE + n_sel <= LANES
    assert NSA_WINDOW == 2 * blk
    coef, pos = _position_tables(SLOPES_NSA, s, blk, sel_blocks=True)
    slab = lambda c: pl.BlockSpec((1, s, LANES), lambda bi, i: (bi, 0, c))
    rows = N_HEADS * blk
    return pl.pallas_call(
        functools.partial(_nsa_kernel, blk=blk, n_cmp=n_cmp, n_sel=n_sel),
        grid=(b, s // blk),
        in_specs=[pl.BlockSpec((1, blk, 256), lambda bi, i: (bi, i, COL_QN)),
                  pl.BlockSpec((1, blk, LANES), lambda bi, i: (bi, i, COL_GN)),
                  pl.BlockSpec((1, LANES, 2 * LANES), lambda bi, i: (bi, 0, 0)),
                  slab(COL_KS), slab(COL_VS), slab(COL_KW), slab(COL_VW),
                  pl.BlockSpec((s, LANES), lambda bi, i: (0, 0)),
                  pl.BlockSpec((N_HEADS, LANES), lambda bi, i: (0, 0)),
                  pl.BlockSpec((1, 256), lambda bi, i: (0, 0))],
        out_specs=pl.BlockSpec((1, blk, 256), lambda bi, i: (bi, i, 0)),
        out_shape=jax.ShapeDtypeStruct((b, s, 256), BF16),
        scratch_shapes=[pltpu.VMEM((rows, 2 * LANES), BF16), pltpu.VMEM((rows, LANES), F32),
                        pltpu.VMEM((rows, 2 * LANES), F32)],
        compiler_params=_cparams("parallel", "arbitrary"),
        name="native_sparse",
    )(proj, proj, cmp_kv, proj, proj, proj, proj, pos, coef, gm)


def _mix_mlp_kernel(x_ref, a_ref, b_ref, c_ref, d_ref, wo_ref, g_ref, wu_ref, wd_ref, gf_ref, o_ref, h_ref,
                    *, final_norm):
    f = pl.program_id(1)

    @pl.when(f == 0)
    def _():
        mix = jnp.concatenate([a_ref[...], b_ref[...], c_ref[...], d_ref[...]], axis=1)
        x1 = x_ref[...] + _nn(mix, wo_ref[...])
        o_ref[...] = x1
        h_ref[...] = (_rms(x1) * g_ref[...]).astype(BF16)

    u = jnp.maximum(_nn(h_ref[...], wu_ref[...]), 0.0)
    o_ref[...] += _nn((u * u).astype(BF16), wd_ref[...])

    if final_norm:
        @pl.when(f == pl.num_programs(1) - 1)
        def _():
            o_ref[...] = _rms(o_ref[...]) * gf_ref[...]


def _mix_mlp(x2, mixes, wo, gain, wu, wd, gfinal, final_norm, tm=1024, tf=1024):
    m = x2.shape[0]
    grp = pl.BlockSpec((tm, GROUP_WIDTH), lambda i, f: (i, 0))
    row = pl.BlockSpec((1, D_MODEL), lambda i, f: (0, 0))
    return pl.pallas_call(
        functools.partial(_mix_mlp_kernel, final_norm=final_norm),
        grid=(m // tm, D_FF // tf),
        in_specs=[pl.BlockSpec((tm, D_MODEL), lambda i, f: (i, 0)), grp, grp, grp, grp,
                  pl.BlockSpec((D_MODEL, D_MODEL), lambda i, f: (0, 0)), row,
                  pl.BlockSpec((D_MODEL, tf), lambda i, f: (0, f)),
                  pl.BlockSpec((tf, D_MODEL), lambda i, f: (f, 0)), row],
        out_specs=pl.BlockSpec((tm, D_MODEL), lambda i, f: (i, 0)),
        out_shape=jax.ShapeDtypeStruct((m, D_MODEL), F32),
        scratch_shapes=[pltpu.VMEM((tm, D_MODEL), BF16)],
        compiler_params=_cparams("parallel", "arbitrary"),
        name="mix_mlp",
    )(x2, *mixes, wo, gain, wu, wd, gfinal)


def _compress_weights(pe_k, w1_k, w2_k, pe_v, w1_v, w2_v):
    half = CMP_STRIDE
    d = HEAD_DIM

    def halves(w1):
        w = w1.reshape(CMP_LEN, d, CMP_HIDDEN)
        return w[:half], w[half:]

    ka, kb = halves(w1_k)
    va, vb = halves(w1_v)
    zero = jnp.zeros_like(ka)

    def merge(wk, wv):
        top = jnp.concatenate([wk, zero], axis=-1)
        bot = jnp.concatenate([zero, wv], axis=-1)
        return jnp.concatenate([top, bot], axis=1).reshape(half * 2 * d, 2 * CMP_HIDDEN).astype(BF16)

    w1a, w1b = merge(ka, va), merge(kb, vb)
    pe = jnp.concatenate([pe_k, pe_v], axis=-1)
    pea = pe[:half].reshape(1, half * 2 * d)
    peb = pe[half:].reshape(1, half * 2 * d)
    zk = jnp.zeros_like(w2_k)
    w2 = jnp.concatenate([jnp.concatenate([w2_k, w2_k, zk, zk], axis=1),
                          jnp.concatenate([zk, zk, w2_v, w2_v], axis=1)], axis=0).astype(BF16)
    return pea, peb, w1a, w1b, w2


def kernel(x, norm_attn, w_in, cmp_pe_k, cmp_w1_k, cmp_w2_k, cmp_pe_v, cmp_w1_v, cmp_w2_v, diff_lq1, diff_lk1,
           diff_lq2, diff_lk2, sinks, g_mix, w_out, norm_mlp, w_up, w_down, norm_final):
    b, s, d = x.shape
    m = b * s
    depth = w_in.shape[0]
    cols = jnp.asarray(_COLS)
    w_in_p = jnp.take(jnp.concatenate([w_in, jnp.zeros((depth, d, 1), w_in.dtype)], axis=2), cols,
                      axis=2).astype(BF16)
    w_out_b = w_out.astype(BF16)
    w_up_b = w_up.astype(BF16)
    w_down_b = w_down.astype(BF16)
    gfinal = norm_final.reshape(1, d)

    colscale = jnp.asarray(_column_scales())

    x2 = x.reshape(m, d)
    for l in range(depth):
        proj = _inproj(x2, norm_attn[l].reshape(1, d), w_in_p[l], colscale).reshape(b, s, NP)
        gm = g_mix[l].reshape(N_HEADS, 1, GROUP_WIDTH)

        chunks = proj[:, :, COL_KVC * LANES:(COL_KVC + 1) * LANES].reshape(b, s // CMP_STRIDE, CMP_STRIDE * LANES)
        cmp_kv = _compress(chunks, *_compress_weights(cmp_pe_k[l], cmp_w1_k[l], cmp_w2_k[l],
                                                      cmp_pe_v[l], cmp_w1_v[l], cmp_w2_v[l]))

        o_sb = _stick_breaking(proj, gm[0])
        o_nsa = _native_sparse(proj, cmp_kv, gm[1])
        lam_init = 0.8 - 0.6 * math.exp(-0.3 * l)
        dl = jnp.stack([diff_lq1[l], diff_lk1[l], diff_lq2[l], diff_lk2[l]]).astype(F32)
        o_diff = _differential(proj, dl, gm[2], lam_init)
        o_swa = _sliding_window(proj, sinks[l].astype(F32), gm[3])

        mixes = [o.reshape(m, GROUP_WIDTH) for o in (o_sb, o_nsa, o_diff, o_swa)]
        x2 = _mix_mlp(x2, mixes, w_out_b[l], norm_mlp[l].reshape(1, d), w_up_b[l], w_down_b[l], gfinal,
                      final_norm=(l == depth - 1))
    return x2.reshape(b, s, d)
```

```python
import functools
import math

import ml_dtypes
import numpy as np
import jax
import jax.numpy as jnp
from jax import lax
from jax.experimental import pallas as pl
from jax.experimental.pallas import tpu as pltpu

F32 = jnp.float32
BF16 = jnp.bfloat16

D_MODEL = 1024
DEPTH = 4
HEAD_DIM = 64
GROUP_WIDTH = 256
N_HEADS = 4
D_FF = 4 * D_MODEL
EPS = 1e-6
NEG_INF = -1e30
LOG2E = math.log2(math.e)
LANES = 128

CMP_LEN = 32
CMP_STRIDE = 16
CMP_HIDDEN = 128
SEL_LEN = 64
SEL_TOPN = 8
FORCE_SCORE = 1e9
NSA_WINDOW = 512
SWA_WINDOW = 128
DIFF_QK_DIM = 32

VMEM_LIMIT = 48 * 1024 * 1024

_ORIG = dict(qa=(0, 256), ka=(256, 256), va=(512, 256), qn=(768, 256), kcn=(1024, 64), vcn=(1088, 64),
             ksn=(1152, 64), vsn=(1216, 64), kwn=(1280, 64), vwn=(1344, 64), gn=(1408, 12),
             qc=(1420, 256), kc=(1676, 256), vc=(1932, 256), qd=(2188, 256), kd=(2444, 128), vd=(2572, 128))
IN_COLS = 2700


def _layout():
    cols = []

    def put(name, lo=0, n=None):
        start, size = _ORIG[name]
        n = size - lo if n is None else n
        cols.extend(range(start + lo, start + lo + n))

    def zeros(n):
        cols.extend([IN_COLS] * n)

    for nm in ("qa", "ka", "va", "qn", "kcn", "vcn"):
        put(nm)
    for nm in ("ksn", "vsn", "kwn", "vwn"):
        put(nm)
        put(nm)
    put("gn")
    zeros(LANES - 12)
    for nm in ("qc", "kc", "vc", "qd"):
        put(nm)
    for nm in ("kd", "vd"):
        put(nm, 0, 64)
        put(nm, 0, 64)
        put(nm, 64, 64)
        put(nm, 64, 64)
    return np.asarray(cols, np.int32)


_COLS = _layout()
NP = int(_COLS.shape[0])


def _column_scales():
    cs = np.ones((1, NP), np.float32)
    cs[0, 0:256] = HEAD_DIM ** -0.5 * LOG2E
    cs[0, 768:1024] = HEAD_DIM ** -0.5 * LOG2E
    cs[0, 1792:2048] = DIFF_QK_DIM ** -0.5 * LOG2E
    return cs


COL_QA, COL_KA, COL_VA = 0, 1, 2
COL_QN = 3
COL_KVC = 8
COL_KS, COL_VS, COL_KW, COL_VW, COL_GN = 9, 10, 11, 12, 13
COL_QC, COL_KC, COL_VC = 7, 8, 9
COL_QD, COL_KD, COL_VD = 10, 11, 12


def _alibi_slopes():
    m = 2.0 ** (-8.0 * np.arange(1, 13) / 12.0)
    m = m.astype(np.float32).reshape(4, 3)
    return [float(v) for v in m[:, 0]], [float(v) for v in m[:, 1]], [float(v) for v in m[:, 2]]


SLOPES_NSA, SLOPES_DIFF, SLOPES_SWA = _alibi_slopes()


def _nt(a, b):
    return lax.dot_general(a, b, (((1,), (1,)), ((), ())), preferred_element_type=F32)


def _nn(a, b):
    return jnp.dot(a, b, preferred_element_type=F32)


def _cparams(*sem):
    return pltpu.CompilerParams(dimension_semantics=sem, vmem_limit_bytes=VMEM_LIMIT)


def _rms(x):
    return x * lax.rsqrt(jnp.mean(x * x, axis=-1, keepdims=True) + EPS)


def _inproj_kernel(x_ref, g_ref, w_ref, cs_ref, o_ref):
    h = _rms(x_ref[...]) * g_ref[...]
    o_ref[...] = (_nn(h.astype(BF16), w_ref[...]) * cs_ref[...]).astype(BF16)


def _inproj(x2, gain, w, colscale, tm=512):
    m = x2.shape[0]
    return pl.pallas_call(
        _inproj_kernel,
        grid=(m // tm,),
        in_specs=[pl.BlockSpec((tm, D_MODEL), lambda i: (i, 0)),
                  pl.BlockSpec((1, D_MODEL), lambda i: (0, 0)),
                  pl.BlockSpec((D_MODEL, NP), lambda i: (0, 0)),
                  pl.BlockSpec((1, NP), lambda i: (0, 0))],
        out_specs=pl.BlockSpec((tm, NP), lambda i: (i, 0)),
        out_shape=jax.ShapeDtypeStruct((m, NP), BF16),
        compiler_params=_cparams("parallel"),
        name="inproj",
    )(x2, gain, w, colscale)


def _compress_kernel(c_ref, pea_ref, peb_ref, w1a_ref, w1b_ref, w2_ref, o_ref):
    c = c_ref[0].astype(F32)
    p = _nn((c + pea_ref[...]).astype(BF16), w1a_ref[...])
    r = _nn((c + peb_ref[...]).astype(BF16), w1b_ref[...])
    n = c.shape[0]
    r_next = pltpu.roll(r, n - 1, 0)
    hid = jax.nn.gelu(p + r_next)
    o_ref[0] = _nn(hid.astype(BF16), w2_ref[...]).astype(BF16)


def _compress(chunks, pea, peb, w1a, w1b, w2):
    b, n, w = chunks.shape
    full = lambda a: pl.BlockSpec(a.shape, lambda i: (0,) * a.ndim)
    return pl.pallas_call(
        _compress_kernel,
        grid=(b,),
        in_specs=[pl.BlockSpec((1, n, w), lambda i: (i, 0, 0)), full(pea), full(peb), full(w1a), full(w1b),
                  full(w2)],
        out_specs=pl.BlockSpec((1, n, 2 * LANES), lambda i: (i, 0, 0)),
        out_shape=jax.ShapeDtypeStruct((b, n, 2 * LANES), BF16),
        compiler_params=_cparams("parallel"),
        name="nsa_compress",
    )(chunks, pea, peb, w1a, w1b, w2)


def _half_mask(shape, half):
    lane = lax.broadcasted_iota(jnp.int32, shape, len(shape) - 1)
    return (lane // HEAD_DIM) == half


def _group_norm_store(o_ref, rows, gm_ref, o):
    o_ref[0, rows, :] = (_rms(o) * gm_ref[...]).astype(o_ref.dtype)


def _query_rows(i, blk):
    return pl.ds(pl.multiple_of(i * blk, blk), blk)


def _over_query_blocks(step, blk):
    def kernel(*refs):
        def body(i, c):
            step(i, *refs)
            return c
        lax.fori_loop(0, refs[0].shape[1] // blk, body, 0)
    return kernel


def _sb_step(i, q_ref, k_ref, v_ref, gm_ref, o_ref, acc_ref, carry_ref, lb_ref, hl_ref, tot_ref, *, blk):
    q = q_ref[0, _query_rows(i, blk), :]
    row = lax.broadcasted_iota(jnp.int32, (blk, blk), 0)
    col = lax.broadcasted_iota(jnp.int32, (blk, blk), 1)
    past = col < row
    u = jnp.where(row > col, 1.0, 0.0).astype(BF16)
    uu = jnp.concatenate([u, u], axis=0)
    qm = []
    for h in range(N_HEADS):
        slab = q[:, (h // 2) * LANES:(h // 2 + 1) * LANES]
        qm.append(jnp.where(_half_mask(slab.shape, h % 2), slab, jnp.zeros_like(slab)))
    qm = [jnp.concatenate(qm[0:2], axis=0), jnp.concatenate(qm[2:4], axis=0)]
    acc_ref[...] = jnp.zeros_like(acc_ref)
    carry_ref[...] = jnp.zeros_like(carry_ref)

    def per_head(x, fn):
        return jnp.concatenate([fn(x[k * blk:(k + 1) * blk]) for k in range(x.shape[0] // blk)], axis=0)

    def a_logits(j):
        rows = pl.ds(pl.multiple_of(j * blk, blk), blk)
        return [_nt(qm[pr], k_ref[0, rows, pr * LANES:(pr + 1) * LANES]) for pr in range(2)]

    def a_finish(zs, slot, diag):
        for pr, z in enumerate(zs):
            r2 = slice(2 * pr * blk, 2 * (pr + 1) * blk)
            sp = jnp.log2(1.0 + jnp.exp2(-jnp.abs(z)))
            lb = jnp.minimum(z, 0.0) - sp
            lk = lb - z
            if diag:
                lk = per_head(lk, lambda t: jnp.where(past, t, 0.0))
                lb = per_head(lb, lambda t: jnp.where(past, t, NEG_INF))
            hi = lk.astype(BF16)
            lb_ref[slot, r2, :] = lb
            hl_ref[slot, r2, 0:blk] = hi
            hl_ref[slot, r2, blk:2 * blk] = (lk - hi.astype(F32)).astype(BF16)
            tot_ref[slot, r2, :] = jnp.broadcast_to(jnp.sum(lk, axis=-1, keepdims=True), (2 * blk, LANES))

    def b_suffix(slot):
        return _nn(hl_ref[slot], uu)

    def b_finish(cs, j, slot):
        rows = pl.ds(pl.multiple_of(j * blk, blk), blk)
        carry = carry_ref[...]
        a = jnp.exp2(lb_ref[slot] + cs + jnp.concatenate([carry] * (blk // LANES), axis=1)).astype(BF16)
        for pr in range(2):
            r2 = slice(2 * pr * blk, 2 * (pr + 1) * blk)
            acc_ref[r2, :] += _nn(a[r2], v_ref[0, rows, pr * LANES:(pr + 1) * LANES])
        carry_ref[...] = carry + tot_ref[slot]

    a_finish(a_logits(i), 0, True)

    def step(t, slot):
        zs = a_logits(i - 1 - t)
        cs = b_suffix(slot)
        a_finish(zs, 1 - slot, False)
        b_finish(cs, i - t, slot)

    def body(tt, c):
        step(2 * tt, 0)
        step(2 * tt + 1, 1)
        return c

    lax.fori_loop(0, i // 2, body, 0)

    @pl.when(i % 2 == 1)
    def _():
        step(i - 1, 0)
        b_finish(b_suffix(1), 0, 1)

    @pl.when(i % 2 == 0)
    def _():
        b_finish(b_suffix(0), 0, 0)

    lo_half = _half_mask((blk, LANES), 0)
    acc = [acc_ref[h * blk:(h + 1) * blk, :] for h in range(N_HEADS)]
    o = jnp.concatenate([jnp.where(lo_half, acc[0], acc[1]), jnp.where(lo_half, acc[2], acc[3])], axis=1)
    _group_norm_store(o_ref, _query_rows(i, blk), gm_ref, o)


def _stick_breaking(proj, gm, blk=256):
    b, s, _ = proj.shape
    stage = lambda w, dt: pltpu.VMEM((2, N_HEADS * blk, w), dt)
    return pl.pallas_call(
        _over_query_blocks(functools.partial(_sb_step, blk=blk), blk),
        grid=(b,),
        in_specs=[pl.BlockSpec((1, s, 256), lambda bi: (bi, 0, COL_QA)),
                  pl.BlockSpec((1, s, 256), lambda bi: (bi, 0, COL_KA)),
                  pl.BlockSpec((1, s, 256), lambda bi: (bi, 0, COL_VA)),
                  pl.BlockSpec((1, 256), lambda bi: (0, 0))],
        out_specs=pl.BlockSpec((1, s, 256), lambda bi: (bi, 0, 0)),
        out_shape=jax.ShapeDtypeStruct((b, s, 256), BF16),
        scratch_shapes=[pltpu.VMEM((N_HEADS * blk, LANES), F32), pltpu.VMEM((N_HEADS * blk, LANES), F32),
                        stage(blk, F32), stage(2 * blk, BF16), stage(LANES, F32)],
        compiler_params=_cparams("parallel"),
        name="stick_breaking",
    )(proj, proj, proj, gm)


N_MAPS = 2 * N_HEADS
POS_LOCAL, POS_BLOCK, POS_ONE = 0, 3, 6
SEL_LANE = 8
MASK_BIAS = -2.0 ** 100


def _bf16_pieces(x, n):
    out = []
    r = np.float32(x)
    for _ in range(n):
        p = np.float32(r.astype(ml_dtypes.bfloat16))
        out.append(float(p))
        r = np.float32(r - p)
    return out


def _position_tables(slopes, s, blk, sel_blocks=False):
    coef = np.zeros((N_HEADS, LANES), np.float32)
    for h in range(N_HEADS):
        pieces = _bf16_pieces(slopes[h] * LOG2E, 3)
        coef[h, POS_LOCAL:POS_LOCAL + 3] = pieces
        coef[h, POS_BLOCK:POS_BLOCK + 3] = [blk * p for p in pieces]
    pos = np.zeros((s, LANES), np.float32)
    idx = np.arange(s)
    pos[:, POS_LOCAL:POS_LOCAL + 3] = (idx % blk)[:, None]
    pos[:, POS_BLOCK:POS_BLOCK + 3] = (idx // blk)[:, None]
    pos[:, POS_ONE:POS_ONE + 2] = 1.0
    if sel_blocks:
        pos[idx, SEL_LANE + idx // SEL_LEN] = 1.0
    return jnp.asarray(coef), jnp.asarray(pos, BF16)


def _diff_step(i, q_ref, k_ref, v_ref, pos_ref, coef_ref, dl_ref, gm_ref, o_ref, qa_ref, mx_ref, acc_ref,
               *, blk, lam_init):
    q = q_ref[0, _query_rows(i, blk), :]
    row = lax.broadcasted_iota(jnp.int32, (blk, blk), 0)
    col = lax.broadcasted_iota(jnp.int32, (blk, blk), 1)
    causal = col <= row
    lane = lax.broadcasted_iota(jnp.int32, (blk, LANES), 1)
    for h in range(N_HEADS):
        slab = q[:, (h // 2) * LANES:(h // 2 + 1) * LANES]
        coef = jnp.broadcast_to(coef_ref[h:h + 1, :], (blk, LANES)).astype(BF16)
        for c in range(2):
            r = slice(((2 * h + c) % 4) * blk, ((2 * h + c) % 4 + 1) * blk)
            qa_ref[h // 2, r, 0:LANES] = jnp.where((lane // DIFF_QK_DIM) == (h % 2) * 2 + c, slab,
                                                   jnp.zeros_like(slab))
            qa_ref[h // 2, r, LANES:2 * LANES] = coef
    ones = jnp.ones((blk, LANES), BF16)

    def per_map(x, fn):
        return jnp.concatenate([fn(x[k * blk:(k + 1) * blk]) for k in range(4)], axis=0)

    def sweep(blocks, second, first):
        chains = [(pr, pl.ds(pl.multiple_of(j * blk, blk), blk), diag) for pr in range(2) for j, diag in blocks]
        zs = []
        for pr, rows, diag in chains:
            ka = jnp.concatenate([k_ref[0, rows, pr * LANES:(pr + 1) * LANES], pos_ref[rows, :]], axis=1)
            zs.append(_nt(qa_ref[pr], ka))
        outs = []
        if second:
            ps = []
            for (pr, rows, diag), z in zip(chains, zs):
                p = jnp.exp2(z)
                if diag:
                    p = per_map(p, lambda t: jnp.where(causal, t, 0.0))
                ps.append(p.astype(BF16))
            for (pr, rows, diag), p in zip(chains, ps):
                vo = jnp.concatenate([v_ref[0, rows, pr * LANES:(pr + 1) * LANES], ones], axis=1)
                outs.append(_nn(p, vo))
        else:
            for (pr, rows, diag), z in zip(chains, zs):
                if diag:
                    z = per_map(z, lambda t: jnp.where(causal, t, NEG_INF))
                zm = z[:, 0:LANES]
                for g in range(1, blk // LANES):
                    zm = jnp.maximum(zm, z[:, g * LANES:(g + 1) * LANES])
                outs.append(zm)
        ref, combine = (acc_ref, jnp.add) if second else (mx_ref, jnp.maximum)
        for pr in range(2):
            mine = [o for (cp, _, _), o in zip(chains, outs) if cp == pr]
            tot = mine[0]
            for o in mine[1:]:
                tot = combine(tot, o)
            ref[pr] = tot if first else combine(ref[pr], tot)

    def run(second):
        @pl.when(i % 2 == 1)
        def _():
            sweep([(i - 1, False), (i, True)], second, True)

        @pl.when(i % 2 == 0)
        def _():
            sweep([(i, True)], second, True)

        def body(tt, c):
            sweep([(2 * tt, False), (2 * tt + 1, False)], second, False)
            return c
        lax.fori_loop(0, i // 2, body, 0)

    run(False)
    lane4 = lax.broadcasted_iota(jnp.int32, (4 * blk, LANES), 1)
    for pr in range(2):
        m = jnp.max(mx_ref[pr], axis=-1, keepdims=True)
        m_hi = m.astype(BF16)
        m_lo = (m - m_hi.astype(F32)).astype(BF16)
        old = qa_ref[pr, :, LANES:2 * LANES]
        qa_ref[pr, :, LANES:2 * LANES] = jnp.where(lane4 == POS_ONE, -m_hi, jnp.where(lane4 == POS_ONE + 1, -m_lo, old))
    run(True)

    dl = dl_ref[...]
    s1 = jnp.sum(dl[0:1] * dl[1:2], axis=-1, keepdims=True)
    s2 = jnp.sum(dl[2:3] * dl[3:4], axis=-1, keepdims=True)
    lam = jnp.exp(s1) - jnp.exp(s2) + lam_init
    outs = []
    for h in range(N_HEADS):
        k1 = (2 * h) % 4
        a1 = acc_ref[h // 2, k1 * blk:(k1 + 1) * blk, :]
        a2 = acc_ref[h // 2, (k1 + 1) * blk:(k1 + 2) * blk, :]
        d = a1[:, :LANES] / a1[:, LANES:] - lam * (a2[:, :LANES] / a2[:, LANES:])
        mine = _half_mask(d.shape, h % 2)
        ms = jnp.sum(jnp.where(mine, d * d, 0.0), axis=-1, keepdims=True) * (1.0 / HEAD_DIM)
        outs.append(d * lax.rsqrt(ms + EPS) * (1.0 - lam_init))
    lo_half = _half_mask((blk, LANES), 0)
    o = jnp.concatenate([jnp.where(lo_half, outs[0], outs[1]), jnp.where(lo_half, outs[2], outs[3])], axis=1)
    o_ref[0, _query_rows(i, blk), :] = (o * gm_ref[...]).astype(o_ref.dtype)


def _differential(proj, dl, gm, lam_init, blk=256):
    b, s, _ = proj.shape
    coef, pos = _position_tables(SLOPES_DIFF, s, blk)
    return pl.pallas_call(
        _over_query_blocks(functools.partial(_diff_step, blk=blk, lam_init=lam_init), blk),
        grid=(b,),
        in_specs=[pl.BlockSpec((1, s, 256), lambda bi: (bi, 0, COL_QC)),
                  pl.BlockSpec((1, s, 256), lambda bi: (bi, 0, COL_KC)),
                  pl.BlockSpec((1, s, 256), lambda bi: (bi, 0, COL_VC)),
                  pl.BlockSpec((s, LANES), lambda bi: (0, 0)),
                  pl.BlockSpec((N_HEADS, LANES), lambda bi: (0, 0)),
                  pl.BlockSpec((4, DIFF_QK_DIM), lambda bi: (0, 0)),
                  pl.BlockSpec((1, 256), lambda bi: (0, 0))],
        out_specs=pl.BlockSpec((1, s, 256), lambda bi: (bi, 0, 0)),
        out_shape=jax.ShapeDtypeStruct((b, s, 256), BF16),
        scratch_shapes=[pltpu.VMEM((2, 4 * blk, 2 * LANES), BF16), pltpu.VMEM((2, 4 * blk, LANES), F32),
                        pltpu.VMEM((2, 4 * blk, 2 * LANES), F32)],
        compiler_params=_cparams("parallel"),
        name="differential",
    )(proj, proj, proj, pos, coef, dl, gm)


def _swa_step(i, q_ref, k_ref, v_ref, sink_ref, gm_ref, o_ref, *, blk, sub):
    row = lax.broadcasted_iota(jnp.int32, (blk, 2 * blk), 0)
    col = lax.broadcasted_iota(jnp.int32, (blk, 2 * blk), 1)
    lo_half = _half_mask((blk, LANES), 0)
    blocks = []
    for s_ in range(sub):
        g = i * sub + s_
        first = jnp.maximum(g - 1, 0)
        start = pl.multiple_of(first * blk, blk)
        dist = (row - col) + (g - first) * blk
        mask = (dist >= 0) & (dist < SWA_WINDOW)
        distf = dist.astype(F32)
        q = q_ref[0, _query_rows(g, blk), :]
        outs = []
        for h in range(N_HEADS):
            lanes = slice((h // 2) * LANES, (h // 2 + 1) * LANES)
            slab = q[:, lanes]
            qm = jnp.where(_half_mask(slab.shape, h % 2), slab, jnp.zeros_like(slab))
            kb = k_ref[0, pl.ds(start, 2 * blk), lanes]
            vb = v_ref[0, pl.ds(start, 2 * blk), lanes]
            z = _nt(qm, kb) * (HEAD_DIM ** -0.5) - SLOPES_SWA[h] * distf
            z = jnp.where(mask, z, NEG_INF)
            sink = sink_ref[h]
            m = jnp.maximum(jnp.max(z, axis=-1, keepdims=True), sink)
            p = jnp.exp(z - m)
            l = jnp.sum(p, axis=-1, keepdims=True) + jnp.exp(sink - m)
            outs.append(_nn(p.astype(BF16), vb) / l)
        blocks.append(jnp.concatenate([jnp.where(lo_half, outs[0], outs[1]), jnp.where(lo_half, outs[2], outs[3])],
                                      axis=1))
    _group_norm_store(o_ref, _query_rows(i, sub * blk), gm_ref, jnp.concatenate(blocks, axis=0))


def _sliding_window(proj, sinks, gm, blk=128, sub=4):
    b, s, _ = proj.shape
    rows = blk * sub
    return pl.pallas_call(
        _over_query_blocks(functools.partial(_swa_step, blk=blk, sub=sub), rows),
        grid=(b,),
        in_specs=[pl.BlockSpec((1, s, 256), lambda bi: (bi, 0, COL_QD)),
                  pl.BlockSpec((1, s, 256), lambda bi: (bi, 0, COL_KD)),
                  pl.BlockSpec((1, s, 256), lambda bi: (bi, 0, COL_VD)),
                  pl.BlockSpec(memory_space=pltpu.SMEM),
                  pl.BlockSpec((1, 256), lambda bi: (0, 0))],
        out_specs=pl.BlockSpec((1, s, 256), lambda bi: (bi, 0, 0)),
        out_shape=jax.ShapeDtypeStruct((b, s, 256), BF16),
        compiler_params=_cparams("parallel"),
        name="sliding_window",
    )(proj, proj, proj, sinks, gm)


def _nsa_step(i, q_ref, g_ref, cmp_ref, ks_ref, vs_ref, kw_ref, vw_ref, pos_ref, coef_ref, gm_ref, o_ref,
              qa_ref, mx_ref, acc_ref, *, blk, n_cmp, n_sel):
    q = q_ref[0, _query_rows(i, blk), :]
    lane = lax.broadcasted_iota(jnp.int32, (blk, LANES), 1)
    for h in range(N_HEADS):
        slab = q[:, (h // 2) * LANES:(h // 2 + 1) * LANES]
        qa_ref[h * blk:(h + 1) * blk, 0:LANES] = jnp.where(_half_mask(slab.shape, h % 2), slab, jnp.zeros_like(slab))
    qs = qa_ref[:, 0:LANES]
    coefs = [jnp.broadcast_to(coef_ref[h:h + 1, :], (blk, LANES)) for h in range(N_HEADS)]

    tq = i * blk + lax.broadcasted_iota(jnp.int32, (blk, LANES), 0)

    row = lax.broadcasted_iota(jnp.int32, (blk, blk), 0)
    col = lax.broadcasted_iota(jnp.int32, (blk, blk), 1)
    causal = col <= row
    in_window = col > row
    ones = jnp.ones((blk, LANES), BF16)

    def process(k_ref, v_ref, blocks, second):
        groups = (slice(0, 2 * blk), slice(2 * blk, 4 * blk))
        chains = [(pl.ds(pl.multiple_of(j * blk, blk), blk), mask, g) for j, mask in blocks for g in groups]
        zs = []
        for rows, mask, g in chains:
            ka = jnp.concatenate([k_ref[0, rows, :], pos_ref[rows, :]], axis=1)
            zs.append(_nt(qa_ref[g, :], ka))

        def masked(x, mask, fill):
            if mask is None:
                return x
            return jnp.concatenate([jnp.where(mask, x[r * blk:(r + 1) * blk], fill) for r in range(x.shape[0] // blk)],
                                   axis=0)

        if second:
            ps = [masked(jnp.exp2(z), mask, 0.0).astype(BF16) for (rows, mask, g), z in zip(chains, zs)]
            outs = []
            for (rows, mask, g), p in zip(chains, ps):
                vo = jnp.concatenate([v_ref[0, rows, :], ones], axis=1)
                outs.append(_nn(p, vo))
            for g in groups:
                mine = [o for (_, _, cg), o in zip(chains, outs) if cg == g]
                acc_ref[g, :] += functools.reduce(jnp.add, mine)
        else:
            outs = []
            for (rows, mask, g), z in zip(chains, zs):
                z = masked(z, mask, NEG_INF)
                zm = z[:, 0:LANES]
                for c in range(1, blk // LANES):
                    zm = jnp.maximum(zm, z[:, c * LANES:(c + 1) * LANES])
                outs.append(zm)
            for g in groups:
                mine = [o for (_, _, cg), o in zip(chains, outs) if cg == g]
                mx_ref[g, :] = jnp.maximum(mx_ref[g, :], functools.reduce(jnp.maximum, mine))

    def attend(sweep, aug):
        for h in range(N_HEADS):
            qa_ref[h * blk:(h + 1) * blk, LANES:2 * LANES] = aug[h].astype(BF16)
        mx_ref[...] = jnp.full(mx_ref.shape, NEG_INF, F32)
        acc_ref[...] = jnp.zeros_like(acc_ref)
        sweep(False)
        m = jnp.max(mx_ref[...], axis=-1, keepdims=True)
        m_hi = m.astype(BF16)
        m_lo = (m - m_hi.astype(F32)).astype(BF16)
        lane4 = lax.broadcasted_iota(jnp.int32, (N_HEADS * blk, LANES), 1)
        old = qa_ref[:, LANES:2 * LANES]
        qa_ref[:, LANES:2 * LANES] = jnp.where(lane4 == POS_ONE, -m_hi, jnp.where(lane4 == POS_ONE + 1, -m_lo, old))
        sweep(True)
        acc = acc_ref[...]
        return acc[:, 0:LANES] / acc[:, LANES:2 * LANES]

    def sel_sweep(second):
        def body(tt, c):
            process(ks_ref, vs_ref, [(2 * tt, None), (2 * tt + 1, None)], second)
            return c
        lax.fori_loop(0, i // 2, body, 0)

        @pl.when(i % 2 == 1)
        def _():
            process(ks_ref, vs_ref, [(i - 1, None), (i, causal)], second)

        @pl.when(i % 2 == 0)
        def _():
            process(ks_ref, vs_ref, [(i, causal)], second)

    def win_sweep(second):
        process(kw_ref, vw_ref, [(jnp.maximum(i - 2, 0), in_window & (i >= 2)),
                                 (jnp.maximum(i - 1, 0), jnp.broadcast_to(i >= 1, (blk, blk))),
                                 (i, causal)], second)

    o_win = attend(win_sweep, coefs)

    kc = cmp_ref[0, :, 0:LANES]
    vc = cmp_ref[0, :, LANES:2 * LANES]
    dist_c = tq - (CMP_STRIDE * lane + CMP_LEN - 1)
    ok_c = (dist_c >= 0) & (lane < n_cmp)
    dist_cf = dist_c.astype(F32)
    bias_c = jnp.concatenate([jnp.where(ok_c, (-SLOPES_NSA[h] * LOG2E) * dist_cf, NEG_INF)
                              for h in range(N_HEADS)], axis=0)
    z = _nt(qs, kc) + bias_c
    p = jnp.exp2(z - jnp.max(z, axis=-1, keepdims=True))
    p = p / jnp.sum(p, axis=-1, keepdims=True)
    p = jnp.where(bias_c > 0.5 * NEG_INF, p, 0.0).astype(BF16)
    o_cmp = _nn(p, vc)

    sel_rows = -(-(SEL_LANE + n_sel) // 8) * 8
    rj = lax.broadcasted_iota(jnp.int32, (sel_rows, LANES), 0)
    ri = lax.broadcasted_iota(jnp.int32, (sel_rows, LANES), 1)
    to_sel = jnp.where((ri * CMP_STRIDE) // SEL_LEN + SEL_LANE == rj, 1.0, 0.0).astype(BF16)
    imp = _nt(to_sel, p[0:blk])
    for h in range(1, N_HEADS):
        imp = imp + _nt(to_sel, p[h * blk:(h + 1) * blk])
    srow = lax.broadcasted_iota(jnp.int32, (sel_rows, blk), 0)
    blk_id = srow - SEL_LANE
    cur = (i * blk + lax.broadcasted_iota(jnp.int32, (sel_rows, blk), 1)) // SEL_LEN
    forced = (blk_id == 0) | (blk_id == cur) | (blk_id == cur - 1)
    score = jnp.where(blk_id > cur, NEG_INF, jnp.where(forced, FORCE_SCORE, imp))
    score = jnp.where((blk_id >= 0) & (blk_id < n_sel), score, -3.0e38)
    sel = jnp.zeros((sel_rows, blk), F32)
    srowf = srow.astype(F32)
    for _ in range(min(SEL_TOPN, n_sel)):
        best = jnp.max(score, axis=0, keepdims=True)
        idx = jnp.min(jnp.where(score == best, srowf, float(LANES)), axis=0, keepdims=True)
        hit = srowf == idx
        sel = jnp.where(hit, 1.0, sel)
        score = jnp.where(hit, -3.4e38, score)
    sel_bias = jnp.where(sel > 0.5, 0.0, MASK_BIAS)
    sel_bias = jnp.concatenate([sel_bias, jnp.zeros((LANES - sel_rows, blk), F32)], axis=0).T
    in_sel = (lane >= SEL_LANE) & (lane < SEL_LANE + n_sel)

    o_sel = attend(sel_sweep, [jnp.where(in_sel, sel_bias, coefs[h]) for h in range(N_HEADS)])

    g = jax.nn.sigmoid(g_ref[0, _query_rows(i, blk), :].astype(F32))
    outs = []
    for h in range(N_HEADS):
        r = slice(h * blk, (h + 1) * blk)
        outs.append(g[:, 3 * h:3 * h + 1] * o_cmp[r] + g[:, 3 * h + 1:3 * h + 2] * o_sel[r]
                    + g[:, 3 * h + 2:3 * h + 3] * o_win[r])
    lo_half = _half_mask((blk, LANES), 0)
    o = jnp.concatenate([jnp.where(lo_half, outs[0], outs[1]), jnp.where(lo_half, outs[2], outs[3])], axis=1)
    _group_norm_store(o_ref, _query_rows(i, blk), gm_ref, o)


def _native_sparse(proj, cmp_kv, gm, blk=256):
    b, s, _ = proj.shape
    n_cmp = (s - CMP_LEN) // CMP_STRIDE + 1
    n_sel = s // SEL_LEN
    assert cmp_kv.shape[1] == LANES and n_cmp <= LANES and SEL_LANE + n_sel <= LANES
    assert NSA_WINDOW == 2 * blk
    coef, pos = _position_tables(SLOPES_NSA, s, blk, sel_blocks=True)
    slab = lambda c: pl.BlockSpec((1, s, LANES), lambda bi: (bi, 0, c))
    rows = N_HEADS * blk
    return pl.pallas_call(
        _over_query_blocks(functools.partial(_nsa_step, blk=blk, n_cmp=n_cmp, n_sel=n_sel), blk),
        grid=(b,),
        in_specs=[pl.BlockSpec((1, s, 256), lambda bi: (bi, 0, COL_QN)),
                  slab(COL_GN),
                  pl.BlockSpec((1, LANES, 2 * LANES), lambda bi: (bi, 0, 0)),
                  slab(COL_KS), slab(COL_VS), slab(COL_KW), slab(COL_VW),
                  pl.BlockSpec((s, LANES), lambda bi: (0, 0)),
                  pl.BlockSpec((N_HEADS, LANES), lambda bi: (0, 0)),
                  pl.BlockSpec((1, 256), lambda bi: (0, 0))],
        out_specs=pl.BlockSpec((1, s, 256), lambda bi: (bi, 0, 0)),
        out_shape=jax.ShapeDtypeStruct((b, s, 256), BF16),
        scratch_shapes=[pltpu.VMEM((rows, 2 * LANES), BF16), pltpu.VMEM((rows, LANES), F32),
                        pltpu.VMEM((rows, 2 * LANES), F32)],
        compiler_params=_cparams("parallel"),
        name="native_sparse",
    )(proj, proj, cmp_kv, proj, proj, proj, proj, pos, coef, gm)


def _mix_mlp_kernel(x_ref, a_ref, b_ref, c_ref, d_ref, wo_ref, g_ref, wu_ref, wd_ref, gf_ref, o_ref, h_ref,
                    *, final_norm):
    f = pl.program_id(1)

    @pl.when(f == 0)
    def _():
        mix = jnp.concatenate([a_ref[...], b_ref[...], c_ref[...], d_ref[...]], axis=1)
        x1 = x_ref[...] + _nn(mix, wo_ref[...])
        o_ref[...] = x1
        h_ref[...] = (_rms(x1) * g_ref[...]).astype(BF16)

    u = jnp.maximum(_nn(h_ref[...], wu_ref[...]), 0.0)
    o_ref[...] += _nn((u * u).astype(BF16), wd_ref[...])

    if final_norm:
        @pl.when(f == pl.num_programs(1) - 1)
        def _():
            o_ref[...] = _rms(o_ref[...]) * gf_ref[...]


def _mix_mlp(x2, mixes, wo, gain, wu, wd, gfinal, final_norm, tm=1024, tf=1024):
    m = x2.shape[0]
    grp = pl.BlockSpec((tm, GROUP_WIDTH), lambda i, f: (i, 0))
    row = pl.BlockSpec((1, D_MODEL), lambda i, f: (0, 0))
    return pl.pallas_call(
        functools.partial(_mix_mlp_kernel, final_norm=final_norm),
        grid=(m // tm, D_FF // tf),
        in_specs=[pl.BlockSpec((tm, D_MODEL), lambda i, f: (i, 0)), grp, grp, grp, grp,
                  pl.BlockSpec((D_MODEL, D_MODEL), lambda i, f: (0, 0)), row,
                  pl.BlockSpec((D_MODEL, tf), lambda i, f: (0, f)),
                  pl.BlockSpec((tf, D_MODEL), lambda i, f: (f, 0)), row],
        out_specs=pl.BlockSpec((tm, D_MODEL), lambda i, f: (i, 0)),
        out_shape=jax.ShapeDtypeStruct((m, D_MODEL), F32),
        scratch_shapes=[pltpu.VMEM((tm, D_MODEL), BF16)],
        compiler_params=_cparams("parallel", "arbitrary"),
        name="mix_mlp",
    )(x2, *mixes, wo, gain, wu, wd, gfinal)


def _compress_weights(pe_k, w1_k, w2_k, pe_v, w1_v, w2_v):
    half = CMP_STRIDE
    d = HEAD_DIM

    def halves(w1):
        w = w1.reshape(CMP_LEN, d, CMP_HIDDEN)
        return w[:half], w[half:]

    ka, kb = halves(w1_k)
    va, vb = halves(w1_v)
    zero = jnp.zeros_like(ka)

    def merge(wk, wv):
        top = jnp.concatenate([wk, zero], axis=-1)
        bot = jnp.concatenate([zero, wv], axis=-1)
        return jnp.concatenate([top, bot], axis=1).reshape(half * 2 * d, 2 * CMP_HIDDEN).astype(BF16)

    w1a, w1b = merge(ka, va), merge(kb, vb)
    pe = jnp.concatenate([pe_k, pe_v], axis=-1)
    pea = pe[:half].reshape(1, half * 2 * d)
    peb = pe[half:].reshape(1, half * 2 * d)
    zk = jnp.zeros_like(w2_k)
    w2 = jnp.concatenate([jnp.concatenate([w2_k, w2_k, zk, zk], axis=1),
                          jnp.concatenate([zk, zk, w2_v, w2_v], axis=1)], axis=0).astype(BF16)
    return pea, peb, w1a, w1b, w2


def kernel(x, norm_attn, w_in, cmp_pe_k, cmp_w1_k, cmp_w2_k, cmp_pe_v, cmp_w1_v, cmp_w2_v, diff_lq1, diff_lk1,
           diff_lq2, diff_lk2, sinks, g_mix, w_out, norm_mlp, w_up, w_down, norm_final):
    b, s, d = x.shape
    m = b * s
    depth = w_in.shape[0]
    cols = jnp.asarray(_COLS)
    w_in_p = jnp.take(jnp.concatenate([w_in, jnp.zeros((depth, d, 1), w_in.dtype)], axis=2), cols,
                      axis=2).astype(BF16)
    w_out_b = w_out.astype(BF16)
    w_up_b = w_up.astype(BF16)
    w_down_b = w_down.astype(BF16)
    gfinal = norm_final.reshape(1, d)

    colscale = jnp.asarray(_column_scales())

    x2 = x.reshape(m, d)
    for l in range(depth):
        proj = _inproj(x2, norm_attn[l].reshape(1, d), w_in_p[l], colscale).reshape(b, s, NP)
        gm = g_mix[l].reshape(N_HEADS, 1, GROUP_WIDTH)

        chunks = proj[:, :, COL_KVC * LANES:(COL_KVC + 1) * LANES].reshape(b, s // CMP_STRIDE, CMP_STRIDE * LANES)
        cmp_kv = _compress(chunks, *_compress_weights(cmp_pe_k[l], cmp_w1_k[l], cmp_w2_k[l],
                                                      cmp_pe_v[l], cmp_w1_v[l], cmp_w2_v[l]))

        o_sb = _stick_breaking(proj, gm[0])
        o_nsa = _native_sparse(proj, cmp_kv, gm[1])
        lam_init = 0.8 - 0.6 * math.exp(-0.3 * l)
        dl = jnp.stack([diff_lq1[l], diff_lk1[l], diff_lq2[l], diff_lk2[l]]).astype(F32)
        o_diff = _differential(proj, dl, gm[2], lam_init)
        o_swa = _sliding_window(proj, sinks[l].astype(F32), gm[3])

        mixes = [o.reshape(m, GROUP_WIDTH) for o in (o_sb, o_nsa, o_diff, o_swa)]
        x2 = _mix_mlp(x2, mixes, w_out_b[l], norm_mlp[l].reshape(1, d), w_up_b[l], w_down_b[l], gfinal,
                      final_norm=(l == depth - 1))
    return x2.reshape(b, s, d)
```

```python
import functools
import math

import ml_dtypes
import numpy as np
import jax
import jax.numpy as jnp
from jax import lax
from jax.experimental import pallas as pl
from jax.experimental.pallas import tpu as pltpu

F32 = jnp.float32
BF16 = jnp.bfloat16

D_MODEL = 1024
DEPTH = 4
HEAD_DIM = 64
GROUP_WIDTH = 256
N_HEADS = 4
D_FF = 4 * D_MODEL
EPS = 1e-6
NEG_INF = -1e30
LOG2E = math.log2(math.e)
LANES = 128

CMP_LEN = 32
CMP_STRIDE = 16
CMP_HIDDEN = 128
SEL_LEN = 64
SEL_TOPN = 8
FORCE_SCORE = 1e9
NSA_WINDOW = 512
SWA_WINDOW = 128
DIFF_QK_DIM = 32

VMEM_LIMIT = 48 * 1024 * 1024

_ORIG = dict(qa=(0, 256), ka=(256, 256), va=(512, 256), qn=(768, 256), kcn=(1024, 64), vcn=(1088, 64),
             ksn=(1152, 64), vsn=(1216, 64), kwn=(1280, 64), vwn=(1344, 64), gn=(1408, 12),
             qc=(1420, 256), kc=(1676, 256), vc=(1932, 256), qd=(2188, 256), kd=(2444, 128), vd=(2572, 128))
IN_COLS = 2700


def _layout():
    cols = []

    def put(name, lo=0, n=None):
        start, size = _ORIG[name]
        n = size - lo if n is None else n
        cols.extend(range(start + lo, start + lo + n))

    def zeros(n):
        cols.extend([IN_COLS] * n)

    for nm in ("qa", "ka", "va", "qn", "kcn", "vcn"):
        put(nm)
    for nm in ("ksn", "vsn", "kwn", "vwn"):
        put(nm)
        put(nm)
    put("gn")
    zeros(LANES - 12)
    for nm in ("qc", "kc", "vc", "qd"):
        put(nm)
    for nm in ("kd", "vd"):
        put(nm, 0, 64)
        put(nm, 0, 64)
        put(nm, 64, 64)
        put(nm, 64, 64)
    return np.asarray(cols, np.int32)


_COLS = _layout()
NP = int(_COLS.shape[0])


def _column_scales():
    cs = np.ones((1, NP), np.float32)
    cs[0, 0:256] = HEAD_DIM ** -0.5 * LOG2E
    cs[0, 768:1024] = HEAD_DIM ** -0.5 * LOG2E
    cs[0, 1792:2048] = DIFF_QK_DIM ** -0.5 * LOG2E
    return cs


COL_QA, COL_KA, COL_VA = 0, 1, 2
COL_QN = 3
COL_KVC = 8
COL_KS, COL_VS, COL_KW, COL_VW, COL_GN = 9, 10, 11, 12, 13
COL_QC, COL_KC, COL_VC = 7, 8, 9
COL_QD, COL_KD, COL_VD = 10, 11, 12


def _alibi_slopes():
    m = 2.0 ** (-8.0 * np.arange(1, 13) / 12.0)
    m = m.astype(np.float32).reshape(4, 3)
    return [float(v) for v in m[:, 0]], [float(v) for v in m[:, 1]], [float(v) for v in m[:, 2]]


SLOPES_NSA, SLOPES_DIFF, SLOPES_SWA = _alibi_slopes()


def _nt(a, b):
    return lax.dot_general(a, b, (((1,), (1,)), ((), ())), preferred_element_type=F32)


def _nn(a, b):
    return jnp.dot(a, b, preferred_element_type=F32)


def _cparams(*sem):
    return pltpu.CompilerParams(dimension_semantics=sem, vmem_limit_bytes=VMEM_LIMIT)


def _rms(x):
    return x * lax.rsqrt(jnp.mean(x * x, axis=-1, keepdims=True) + EPS)


def _inproj_kernel(x_ref, g_ref, w_ref, cs_ref, o_ref):
    h = _rms(x_ref[...]) * g_ref[...]
    o_ref[...] = (_nn(h.astype(BF16), w_ref[...]) * cs_ref[...]).astype(BF16)


def _inproj(x2, gain, w, colscale, tm=512):
    m = x2.shape[0]
    return pl.pallas_call(
        _inproj_kernel,
        grid=(m // tm,),
        in_specs=[pl.BlockSpec((tm, D_MODEL), lambda i: (i, 0)),
                  pl.BlockSpec((1, D_MODEL), lambda i: (0, 0)),
                  pl.BlockSpec((D_MODEL, NP), lambda i: (0, 0)),
                  pl.BlockSpec((1, NP), lambda i: (0, 0))],
        out_specs=pl.BlockSpec((tm, NP), lambda i: (i, 0)),
        out_shape=jax.ShapeDtypeStruct((m, NP), BF16),
        compiler_params=_cparams("parallel"),
        name="inproj",
    )(x2, gain, w, colscale)


def _compress_kernel(c_ref, pea_ref, peb_ref, w1a_ref, w1b_ref, w2_ref, o_ref):
    c = c_ref[0].astype(F32)
    p = _nn((c + pea_ref[...]).astype(BF16), w1a_ref[...])
    r = _nn((c + peb_ref[...]).astype(BF16), w1b_ref[...])
    n = c.shape[0]
    r_next = pltpu.roll(r, n - 1, 0)
    hid = jax.nn.gelu(p + r_next)
    o_ref[0] = _nn(hid.astype(BF16), w2_ref[...]).astype(BF16)


def _compress(chunks, pea, peb, w1a, w1b, w2):
    b, n, w = chunks.shape
    full = lambda a: pl.BlockSpec(a.shape, lambda i: (0,) * a.ndim)
    return pl.pallas_call(
        _compress_kernel,
        grid=(b,),
        in_specs=[pl.BlockSpec((1, n, w), lambda i: (i, 0, 0)), full(pea), full(peb), full(w1a), full(w1b),
                  full(w2)],
        out_specs=pl.BlockSpec((1, n, 2 * LANES), lambda i: (i, 0, 0)),
        out_shape=jax.ShapeDtypeStruct((b, n, 2 * LANES), BF16),
        compiler_params=_cparams("parallel"),
        name="nsa_compress",
    )(chunks, pea, peb, w1a, w1b, w2)


def _half_mask(shape, half):
    lane = lax.broadcasted_iota(jnp.int32, shape, len(shape) - 1)
    return (lane // HEAD_DIM) == half


def _group_norm_store(o_ref, rows, gm_ref, o):
    o_ref[0, rows, :] = (_rms(o) * gm_ref[...]).astype(o_ref.dtype)


def _query_rows(i, blk):
    return pl.ds(pl.multiple_of(i * blk, blk), blk)


def _over_query_blocks(step, blk):
    def kernel(*refs):
        def body(i, c):
            step(i, *refs)
            return c
        lax.fori_loop(0, refs[0].shape[1] // blk, body, 0)
    return kernel


def _sb_step(i, q_ref, k_ref, v_ref, gm_ref, o_ref, acc_ref, carry_ref, lb_ref, lk_ref, tot_ref, *, blk):
    q = q_ref[0, _query_rows(i, blk), :]
    row = lax.broadcasted_iota(jnp.int32, (blk, blk), 0)
    col = lax.broadcasted_iota(jnp.int32, (blk, blk), 1)
    past = col < row
    u = jnp.where(row > col, 1.0, 0.0).astype(BF16)
    qm = []
    for h in range(N_HEADS):
        slab = q[:, (h // 2) * LANES:(h // 2 + 1) * LANES]
        qm.append(jnp.where(_half_mask(slab.shape, h % 2), slab, jnp.zeros_like(slab)))
    qm = [jnp.concatenate(qm[0:2], axis=0), jnp.concatenate(qm[2:4], axis=0)]
    acc_ref[...] = jnp.zeros_like(acc_ref)
    carry_ref[...] = jnp.zeros_like(carry_ref)

    def per_head(x, fn):
        return jnp.concatenate([fn(x[k * blk:(k + 1) * blk]) for k in range(x.shape[0] // blk)], axis=0)

    def a_logits(j):
        rows = pl.ds(pl.multiple_of(j * blk, blk), blk)
        return [_nt(qm[pr], k_ref[0, rows, pr * LANES:(pr + 1) * LANES]) for pr in range(2)]

    def a_finish(zs, slot, diag):
        for pr, z in enumerate(zs):
            r2 = slice(2 * pr * blk, 2 * (pr + 1) * blk)
            sp = jnp.log2(1.0 + jnp.exp2(-jnp.abs(z)))
            lb = jnp.minimum(z, 0.0) - sp
            lk = lb - z
            if diag:
                lk = per_head(lk, lambda t: jnp.where(past, t, 0.0))
                lb = per_head(lb, lambda t: jnp.where(past, t, NEG_INF))
            lb_ref[slot, r2, :] = lb
            lk_ref[slot, r2, :] = lk.astype(BF16)
            tot_ref[slot, r2, :] = jnp.broadcast_to(jnp.sum(lk, axis=-1, keepdims=True), (2 * blk, LANES))

    def b_suffix(slot):
        return _nn(lk_ref[slot], u)

    def b_finish(cs, j, slot):
        rows = pl.ds(pl.multiple_of(j * blk, blk), blk)
        carry = carry_ref[...]
        a = jnp.exp2(lb_ref[slot] + cs + jnp.concatenate([carry] * (blk // LANES), axis=1)).astype(BF16)
        for pr in range(2):
            r2 = slice(2 * pr * blk, 2 * (pr + 1) * blk)
            acc_ref[r2, :] += _nn(a[r2], v_ref[0, rows, pr * LANES:(pr + 1) * LANES])
        carry_ref[...] = carry + tot_ref[slot]

    a_finish(a_logits(i), 0, True)

    def step(t, slot):
        zs = a_logits(i - 1 - t)
        cs = b_suffix(slot)
        a_finish(zs, 1 - slot, False)
        b_finish(cs, i - t, slot)

    def body(tt, c):
        step(2 * tt, 0)
        step(2 * tt + 1, 1)
        return c

    lax.fori_loop(0, i // 2, body, 0)

    @pl.when(i % 2 == 1)
    def _():
        step(i - 1, 0)
        b_finish(b_suffix(1), 0, 1)

    @pl.when(i % 2 == 0)
    def _():
        b_finish(b_suffix(0), 0, 0)

    lo_half = _half_mask((blk, LANES), 0)
    acc = [acc_ref[h * blk:(h + 1) * blk, :] for h in range(N_HEADS)]
    o = jnp.concatenate([jnp.where(lo_half, acc[0], acc[1]), jnp.where(lo_half, acc[2], acc[3])], axis=1)
    _group_norm_store(o_ref, _query_rows(i, blk), gm_ref, o)


def _stick_breaking(proj, gm, blk=256):
    b, s, _ = proj.shape
    stage = lambda w, dt: pltpu.VMEM((2, N_HEADS * blk, w), dt)
    return pl.pallas_call(
        _over_query_blocks(functools.partial(_sb_step, blk=blk), blk),
        grid=(b,),
        in_specs=[pl.BlockSpec((1, s, 256), lambda bi: (bi, 0, COL_QA)),
                  pl.BlockSpec((1, s, 256), lambda bi: (bi, 0, COL_KA)),
                  pl.BlockSpec((1, s, 256), lambda bi: (bi, 0, COL_VA)),
                  pl.BlockSpec((1, 256), lambda bi: (0, 0))],
        out_specs=pl.BlockSpec((1, s, 256), lambda bi: (bi, 0, 0)),
        out_shape=jax.ShapeDtypeStruct((b, s, 256), BF16),
        scratch_shapes=[pltpu.VMEM((N_HEADS * blk, LANES), F32), pltpu.VMEM((N_HEADS * blk, LANES), F32),
                        stage(blk, F32), stage(blk, BF16), stage(LANES, F32)],
        compiler_params=_cparams("parallel"),
        name="stick_breaking",
    )(proj, proj, proj, gm)


N_MAPS = 2 * N_HEADS
POS_LOCAL, POS_BLOCK, POS_ONE = 0, 3, 6
SEL_LANE = 8
MASK_BIAS = -2.0 ** 100


def _bf16_pieces(x, n):
    out = []
    r = np.float32(x)
    for _ in range(n):
        p = np.float32(r.astype(ml_dtypes.bfloat16))
        out.append(float(p))
        r = np.float32(r - p)
    return out


def _position_tables(slopes, s, blk, sel_blocks=False):
    coef = np.zeros((N_HEADS, LANES), np.float32)
    for h in range(N_HEADS):
        pieces = _bf16_pieces(slopes[h] * LOG2E, 3)
        coef[h, POS_LOCAL:POS_LOCAL + 3] = pieces
        coef[h, POS_BLOCK:POS_BLOCK + 3] = [blk * p for p in pieces]
    pos = np.zeros((s, LANES), np.float32)
    idx = np.arange(s)
    pos[:, POS_LOCAL:POS_LOCAL + 3] = (idx % blk)[:, None]
    pos[:, POS_BLOCK:POS_BLOCK + 3] = (idx // blk)[:, None]
    pos[:, POS_ONE:POS_ONE + 2] = 1.0
    if sel_blocks:
        pos[idx, SEL_LANE + idx // SEL_LEN] = 1.0
    return jnp.asarray(coef), jnp.asarray(pos, BF16)


def _diff_step(i, q_ref, k_ref, v_ref, pos_ref, coef_ref, dl_ref, gm_ref, o_ref, qa_ref, mx_ref, acc_ref,
               *, blk, lam_init):
    q = q_ref[0, _query_rows(i, blk), :]
    row = lax.broadcasted_iota(jnp.int32, (blk, blk), 0)
    col = lax.broadcasted_iota(jnp.int32, (blk, blk), 1)
    causal = col <= row
    lane = lax.broadcasted_iota(jnp.int32, (blk, LANES), 1)
    for h in range(N_HEADS):
        slab = q[:, (h // 2) * LANES:(h // 2 + 1) * LANES]
        coef = jnp.broadcast_to(coef_ref[h:h + 1, :], (blk, LANES)).astype(BF16)
        for c in range(2):
            r = slice(((2 * h + c) % 4) * blk, ((2 * h + c) % 4 + 1) * blk)
            qa_ref[h // 2, r, 0:LANES] = jnp.where((lane // DIFF_QK_DIM) == (h % 2) * 2 + c, slab,
                                                   jnp.zeros_like(slab))
            qa_ref[h // 2, r, LANES:2 * LANES] = coef
    ones = jnp.ones((blk, LANES), BF16)

    def per_map(x, fn):
        return jnp.concatenate([fn(x[k * blk:(k + 1) * blk]) for k in range(4)], axis=0)

    def sweep(blocks, second, first):
        chains = [(pr, pl.ds(pl.multiple_of(j * blk, blk), blk), diag) for pr in range(2) for j, diag in blocks]
        zs = []
        for pr, rows, diag in chains:
            ka = jnp.concatenate([k_ref[0, rows, pr * LANES:(pr + 1) * LANES], pos_ref[rows, :]], axis=1)
            zs.append(_nt(qa_ref[pr], ka))
        outs = []
        if second:
            ps = []
            for (pr, rows, diag), z in zip(chains, zs):
                p = jnp.exp2(z)
                if diag:
                    p = per_map(p, lambda t: jnp.where(causal, t, 0.0))
                ps.append(p.astype(BF16))
            for (pr, rows, diag), p in zip(chains, ps):
                vo = jnp.concatenate([v_ref[0, rows, pr * LANES:(pr + 1) * LANES], ones], axis=1)
                outs.append(_nn(p, vo))
        else:
            for (pr, rows, diag), z in zip(chains, zs):
                if diag:
                    z = per_map(z, lambda t: jnp.where(causal, t, NEG_INF))
                zm = z[:, 0:LANES]
                for g in range(1, blk // LANES):
                    zm = jnp.maximum(zm, z[:, g * LANES:(g + 1) * LANES])
                outs.append(zm)
        ref, combine = (acc_ref, jnp.add) if second else (mx_ref, jnp.maximum)
        for pr in range(2):
            mine = [o for (cp, _, _), o in zip(chains, outs) if cp == pr]
            tot = mine[0]
            for o in mine[1:]:
                tot = combine(tot, o)
            ref[pr] = tot if first else combine(ref[pr], tot)

    def run(second):
        @pl.when(i % 2 == 1)
        def _():
            sweep([(i - 1, False), (i, True)], second, True)

        @pl.when(i % 2 == 0)
        def _():
            sweep([(i, True)], second, True)

        def body(tt, c):
            sweep([(2 * tt, False), (2 * tt + 1, False)], second, False)
            return c
        lax.fori_loop(0, i // 2, body, 0)

    run(False)
    lane4 = lax.broadcasted_iota(jnp.int32, (4 * blk, LANES), 1)
    for pr in range(2):
        m = jnp.max(mx_ref[pr], axis=-1, keepdims=True)
        m_hi = m.astype(BF16)
        m_lo = (m - m_hi.astype(F32)).astype(BF16)
        old = qa_ref[pr, :, LANES:2 * LANES]
        qa_ref[pr, :, LANES:2 * LANES] = jnp.where(lane4 == POS_ONE, -m_hi, jnp.where(lane4 == POS_ONE + 1, -m_lo, old))
    run(True)

    dl = dl_ref[...]
    s1 = jnp.sum(dl[0:1] * dl[1:2], axis=-1, keepdims=True)
    s2 = jnp.sum(dl[2:3] * dl[3:4], axis=-1, keepdims=True)
    lam = jnp.exp(s1) - jnp.exp(s2) + lam_init
    outs = []
    for h in range(N_HEADS):
        k1 = (2 * h) % 4
        a1 = acc_ref[h // 2, k1 * blk:(k1 + 1) * blk, :]
        a2 = acc_ref[h // 2, (k1 + 1) * blk:(k1 + 2) * blk, :]
        d = a1[:, :LANES] / a1[:, LANES:] - lam * (a2[:, :LANES] / a2[:, LANES:])
        mine = _half_mask(d.shape, h % 2)
        ms = jnp.sum(jnp.where(mine, d * d, 0.0), axis=-1, keepdims=True) * (1.0 / HEAD_DIM)
        outs.append(d * lax.rsqrt(ms + EPS) * (1.0 - lam_init))
    lo_half = _half_mask((blk, LANES), 0)
    o = jnp.concatenate([jnp.where(lo_half, outs[0], outs[1]), jnp.where(lo_half, outs[2], outs[3])], axis=1)
    o_ref[0, _query_rows(i, blk), :] = (o * gm_ref[...]).astype(o_ref.dtype)


def _differential(proj, dl, gm, lam_init, blk=256):
    b, s, _ = proj.shape
    coef, pos = _position_tables(SLOPES_DIFF, s, blk)
    return pl.pallas_call(
        _over_query_blocks(functools.partial(_diff_step, blk=blk, lam_init=lam_init), blk),
        grid=(b,),
        in_specs=[pl.BlockSpec((1, s, 256), lambda bi: (bi, 0, COL_QC)),
                  pl.BlockSpec((1, s, 256), lambda bi: (bi, 0, COL_KC)),
                  pl.BlockSpec((1, s, 256), lambda bi: (bi, 0, COL_VC)),
                  pl.BlockSpec((s, LANES), lambda bi: (0, 0)),
                  pl.BlockSpec((N_HEADS, LANES), lambda bi: (0, 0)),
                  pl.BlockSpec((4, DIFF_QK_DIM), lambda bi: (0, 0)),
                  pl.BlockSpec((1, 256), lambda bi: (0, 0))],
        out_specs=pl.BlockSpec((1, s, 256), lambda bi: (bi, 0, 0)),
        out_shape=jax.ShapeDtypeStruct((b, s, 256), BF16),
        scratch_shapes=[pltpu.VMEM((2, 4 * blk, 2 * LANES), BF16), pltpu.VMEM((2, 4 * blk, LANES), F32),
                        pltpu.VMEM((2, 4 * blk, 2 * LANES), F32)],
        compiler_params=_cparams("parallel"),
        name="differential",
    )(proj, proj, proj, pos, coef, dl, gm)


def _swa_step(i, q_ref, k_ref, v_ref, sink_ref, gm_ref, o_ref, *, blk, sub):
    row = lax.broadcasted_iota(jnp.int32, (blk, 2 * blk), 0)
    col = lax.broadcasted_iota(jnp.int32, (blk, 2 * blk), 1)
    lo_half = _half_mask((blk, LANES), 0)
    blocks = []
    for s_ in range(sub):
        g = i * sub + s_
        first = jnp.maximum(g - 1, 0)
        start = pl.multiple_of(first * blk, blk)
        dist = (row - col) + (g - first) * blk
        mask = (dist >= 0) & (dist < SWA_WINDOW)
        distf = dist.astype(F32)
        q = q_ref[0, _query_rows(g, blk), :]
        outs = []
        for h in range(N_HEADS):
            lanes = slice((h // 2) * LANES, (h // 2 + 1) * LANES)
            slab = q[:, lanes]
            qm = jnp.where(_half_mask(slab.shape, h % 2), slab, jnp.zeros_like(slab))
            kb = k_ref[0, pl.ds(start, 2 * blk), lanes]
            vb = v_ref[0, pl.ds(start, 2 * blk), lanes]
            z = _nt(qm, kb) * (HEAD_DIM ** -0.5) - SLOPES_SWA[h] * distf
            z = jnp.where(mask, z, NEG_INF)
            sink = sink_ref[h]
            m = jnp.maximum(jnp.max(z, axis=-1, keepdims=True), sink)
            p = jnp.exp(z - m)
            l = jnp.sum(p, axis=-1, keepdims=True) + jnp.exp(sink - m)
            outs.append(_nn(p.astype(BF16), vb) / l)
        blocks.append(jnp.concatenate([jnp.where(lo_half, outs[0], outs[1]), jnp.where(lo_half, outs[2], outs[3])],
                                      axis=1))
    _group_norm_store(o_ref, _query_rows(i, sub * blk), gm_ref, jnp.concatenate(blocks, axis=0))


def _sliding_window(proj, sinks, gm, blk=128, sub=4):
    b, s, _ = proj.shape
    rows = blk * sub
    return pl.pallas_call(
        _over_query_blocks(functools.partial(_swa_step, blk=blk, sub=sub), rows),
        grid=(b,),
        in_specs=[pl.BlockSpec((1, s, 256), lambda bi: (bi, 0, COL_QD)),
                  pl.BlockSpec((1, s, 256), lambda bi: (bi, 0, COL_KD)),
                  pl.BlockSpec((1, s, 256), lambda bi: (bi, 0, COL_VD)),
                  pl.BlockSpec(memory_space=pltpu.SMEM),
                  pl.BlockSpec((1, 256), lambda bi: (0, 0))],
        out_specs=pl.BlockSpec((1, s, 256), lambda bi: (bi, 0, 0)),
        out_shape=jax.ShapeDtypeStruct((b, s, 256), BF16),
        compiler_params=_cparams("parallel"),
        name="sliding_window",
    )(proj, proj, proj, sinks, gm)


def _nsa_step(i, q_ref, g_ref, cmp_ref, ks_ref, vs_ref, kw_ref, vw_ref, pos_ref, coef_ref, gm_ref, o_ref,
              qa_ref, mx_ref, acc_ref, *, blk, n_cmp, n_sel):
    q = q_ref[0, _query_rows(i, blk), :]
    lane = lax.broadcasted_iota(jnp.int32, (blk, LANES), 1)
    coefs = [jnp.broadcast_to(coef_ref[h:h + 1, :], (blk, LANES)) for h in range(N_HEADS)]
    for h in range(N_HEADS):
        slab = q[:, (h // 2) * LANES:(h // 2 + 1) * LANES]
        own = jnp.where(_half_mask(slab.shape, h % 2), slab, jnp.zeros_like(slab))
        qa_ref[0, h * blk:(h + 1) * blk, 0:LANES] = own
        qa_ref[1, h * blk:(h + 1) * blk, 0:LANES] = own
        qa_ref[0, h * blk:(h + 1) * blk, LANES:2 * LANES] = coefs[h].astype(BF16)
    qs = qa_ref[0, :, 0:LANES]

    tq = i * blk + lax.broadcasted_iota(jnp.int32, (blk, LANES), 0)

    row = lax.broadcasted_iota(jnp.int32, (blk, blk), 0)
    col = lax.broadcasted_iota(jnp.int32, (blk, blk), 1)
    causal = col <= row
    in_window = col > row
    ones = jnp.ones((blk, LANES), BF16)

    k_refs, v_refs = (kw_ref, ks_ref), (vw_ref, vs_ref)

    def process(blocks, second):
        groups = [(br, half) for br in sorted({b[0] for b in blocks}) for half in range(2)]
        chains = [(br, pl.ds(pl.multiple_of(j * blk, blk), blk), mask, (br, half))
                  for br, j, mask in blocks for half in range(2)]
        grows = lambda g: slice(2 * g[1] * blk, 2 * (g[1] + 1) * blk)
        zs = []
        for br, rows, mask, g in chains:
            ka = jnp.concatenate([k_refs[br][0, rows, :], pos_ref[rows, :]], axis=1)
            zs.append(_nt(qa_ref[br, grows(g), :], ka))

        def masked(x, mask, fill):
            if mask is None:
                return x
            return jnp.concatenate([jnp.where(mask, x[r * blk:(r + 1) * blk], fill) for r in range(x.shape[0] // blk)],
                                   axis=0)

        if second:
            ps = [masked(jnp.exp2(z), mask, 0.0).astype(BF16) for (br, rows, mask, g), z in zip(chains, zs)]
            outs = []
            for (br, rows, mask, g), p in zip(chains, ps):
                vo = jnp.concatenate([v_refs[br][0, rows, :], ones], axis=1)
                outs.append(_nn(p, vo))
            for g in groups:
                mine = [o for (_, _, _, cg), o in zip(chains, outs) if cg == g]
                acc_ref[g[0], grows(g), :] += functools.reduce(jnp.add, mine)
        else:
            outs = []
            for (br, rows, mask, g), z in zip(chains, zs):
                z = masked(z, mask, NEG_INF)
                zm = z[:, 0:LANES]
                for c in range(1, blk // LANES):
                    zm = jnp.maximum(zm, z[:, c * LANES:(c + 1) * LANES])
                outs.append(zm)
            for g in groups:
                mine = [o for (_, _, _, cg), o in zip(chains, outs) if cg == g]
                mx_ref[g[0], grows(g), :] = jnp.maximum(mx_ref[g[0], grows(g), :], functools.reduce(jnp.maximum, mine))

    def sweep(second):
        def body(tt, c):
            process([(1, 2 * tt, None), (1, 2 * tt + 1, None)], second)
            return c
        lax.fori_loop(0, i // 2, body, 0)

        window = [(0, jnp.maximum(i - 2, 0), in_window & (i >= 2)),
                  (0, jnp.maximum(i - 1, 0), jnp.broadcast_to(i >= 1, (blk, blk))),
                  (0, i, causal)]

        @pl.when(i % 2 == 1)
        def _():
            process(window + [(1, i - 1, None), (1, i, causal)], second)

        @pl.when(i % 2 == 0)
        def _():
            process(window + [(1, i, causal)], second)

    kc = cmp_ref[0, :, 0:LANES]
    vc = cmp_ref[0, :, LANES:2 * LANES]
    dist_c = tq - (CMP_STRIDE * lane + CMP_LEN - 1)
    ok_c = (dist_c >= 0) & (lane < n_cmp)
    dist_cf = dist_c.astype(F32)
    bias_c = jnp.concatenate([jnp.where(ok_c, (-SLOPES_NSA[h] * LOG2E) * dist_cf, NEG_INF)
                              for h in range(N_HEADS)], axis=0)
    z = _nt(qs, kc) + bias_c
    p = jnp.exp2(z - jnp.max(z, axis=-1, keepdims=True))
    p = p / jnp.sum(p, axis=-1, keepdims=True)
    p = jnp.where(bias_c > 0.5 * NEG_INF, p, 0.0).astype(BF16)
    o_cmp = _nn(p, vc)

    sel_rows = -(-(SEL_LANE + n_sel) // 8) * 8
    rj = lax.broadcasted_iota(jnp.int32, (sel_rows, LANES), 0)
    ri = lax.broadcasted_iota(jnp.int32, (sel_rows, LANES), 1)
    to_sel = jnp.where((ri * CMP_STRIDE) // SEL_LEN + SEL_LANE == rj, 1.0, 0.0).astype(BF16)
    imp = _nt(to_sel, p[0:blk])
    for h in range(1, N_HEADS):
        imp = imp + _nt(to_sel, p[h * blk:(h + 1) * blk])
    srow = lax.broadcasted_iota(jnp.int32, (sel_rows, blk), 0)
    blk_id = srow - SEL_LANE
    cur = (i * blk + lax.broadcasted_iota(jnp.int32, (sel_rows, blk), 1)) // SEL_LEN
    forced = (blk_id == 0) | (blk_id == cur) | (blk_id == cur - 1)
    score = jnp.where(blk_id > cur, NEG_INF, jnp.where(forced, FORCE_SCORE, imp))
    score = jnp.where((blk_id >= 0) & (blk_id < n_sel), score, -3.0e38)
    sel = jnp.zeros((sel_rows, blk), F32)
    srowf = srow.astype(F32)
    for _ in range(min(SEL_TOPN, n_sel)):
        best = jnp.max(score, axis=0, keepdims=True)
        idx = jnp.min(jnp.where(score == best, srowf, float(LANES)), axis=0, keepdims=True)
        hit = srowf == idx
        sel = jnp.where(hit, 1.0, sel)
        score = jnp.where(hit, -3.4e38, score)
    sel_bias = jnp.where(sel > 0.5, 0.0, MASK_BIAS)
    sel_bias = jnp.concatenate([sel_bias, jnp.zeros((LANES - sel_rows, blk), F32)], axis=0).T
    in_sel = (lane >= SEL_LANE) & (lane < SEL_LANE + n_sel)

    for h in range(N_HEADS):
        qa_ref[1, h * blk:(h + 1) * blk, LANES:2 * LANES] = jnp.where(in_sel, sel_bias, coefs[h]).astype(BF16)
    mx_ref[...] = jnp.full(mx_ref.shape, NEG_INF, F32)
    acc_ref[...] = jnp.zeros_like(acc_ref)
    sweep(False)
    lane4 = lax.broadcasted_iota(jnp.int32, (N_HEADS * blk, LANES), 1)
    for br in range(2):
        m = jnp.max(mx_ref[br], axis=-1, keepdims=True)
        m_hi = m.astype(BF16)
        m_lo = (m - m_hi.astype(F32)).astype(BF16)
        old = qa_ref[br, :, LANES:2 * LANES]
        qa_ref[br, :, LANES:2 * LANES] = jnp.where(lane4 == POS_ONE, -m_hi, jnp.where(lane4 == POS_ONE + 1, -m_lo, old))
    sweep(True)
    o_win = acc_ref[0, :, 0:LANES] / acc_ref[0, :, LANES:2 * LANES]
    o_sel = acc_ref[1, :, 0:LANES] / acc_ref[1, :, LANES:2 * LANES]

    g = jax.nn.sigmoid(g_ref[0, _query_rows(i, blk), :].astype(F32))
    outs = []
    for h in range(N_HEADS):
        r = slice(h * blk, (h + 1) * blk)
        outs.append(g[:, 3 * h:3 * h + 1] * o_cmp[r] + g[:, 3 * h + 1:3 * h + 2] * o_sel[r]
                    + g[:, 3 * h + 2:3 * h + 3] * o_win[r])
    lo_half = _half_mask((blk, LANES), 0)
    o = jnp.concatenate([jnp.where(lo_half, outs[0], outs[1]), jnp.where(lo_half, outs[2], outs[3])], axis=1)
    _group_norm_store(o_ref, _query_rows(i, blk), gm_ref, o)


def _native_sparse(proj, cmp_kv, gm, blk=256):
    b, s, _ = proj.shape
    n_cmp = (s - CMP_LEN) // CMP_STRIDE + 1
    n_sel = s // SEL_LEN
    assert cmp_kv.shape[1] == LANES and n_cmp <= LANES and SEL_LANE + n_sel <= LANES
    assert NSA_WINDOW == 2 * blk
    coef, pos = _position_tables(SLOPES_NSA, s, blk, sel_blocks=True)
    slab = lambda c: pl.BlockSpec((1, s, LANES), lambda bi: (bi, 0, c))
    rows = N_HEADS * blk
    return pl.pallas_call(
        _over_query_blocks(functools.partial(_nsa_step, blk=blk, n_cmp=n_cmp, n_sel=n_sel), blk),
        grid=(b,),
        in_specs=[pl.BlockSpec((1, s, 256), lambda bi: (bi, 0, COL_QN)),
                  slab(COL_GN),
                  pl.BlockSpec((1, LANES, 2 * LANES), lambda bi: (bi, 0, 0)),
                  slab(COL_KS), slab(COL_VS), slab(COL_KW), slab(COL_VW),
                  pl.BlockSpec((s, LANES), lambda bi: (0, 0)),
                  pl.BlockSpec((N_HEADS, LANES), lambda bi: (0, 0)),
                  pl.BlockSpec((1, 256), lambda bi: (0, 0))],
        out_specs=pl.BlockSpec((1, s, 256), lambda bi: (bi, 0, 0)),
        out_shape=jax.ShapeDtypeStruct((b, s, 256), BF16),
        scratch_shapes=[pltpu.VMEM((2, rows, 2 * LANES), BF16), pltpu.VMEM((2, rows, LANES), F32),
                        pltpu.VMEM((2, rows, 2 * LANES), F32)],
        compiler_params=_cparams("parallel"),
        name="native_sparse",
    )(proj, proj, cmp_kv, proj, proj, proj, proj, pos, coef, gm)


def _mix_mlp_kernel(x_ref, a_ref, b_ref, c_ref, d_ref, wo_ref, g_ref, wu_ref, wd_ref, gf_ref, o_ref, h_ref,
                    *, final_norm):
    f = pl.program_id(1)

    @pl.when(f == 0)
    def _():
        mix = jnp.concatenate([a_ref[...], b_ref[...], c_ref[...], d_ref[...]], axis=1)
        x1 = x_ref[...] + _nn(mix, wo_ref[...])
        o_ref[...] = x1
        h_ref[...] = (_rms(x1) * g_ref[...]).astype(BF16)

    u = jnp.maximum(_nn(h_ref[...], wu_ref[...]), 0.0)
    o_ref[...] += _nn((u * u).astype(BF16), wd_ref[...])

    if final_norm:
        @pl.when(f == pl.num_programs(1) - 1)
        def _():
            o_ref[...] = _rms(o_ref[...]) * gf_ref[...]


def _mix_mlp(x2, mixes, wo, gain, wu, wd, gfinal, final_norm, tm=1024, tf=1024):
    m = x2.shape[0]
    grp = pl.BlockSpec((tm, GROUP_WIDTH), lambda i, f: (i, 0))
    row = pl.BlockSpec((1, D_MODEL), lambda i, f: (0, 0))
    return pl.pallas_call(
        functools.partial(_mix_mlp_kernel, final_norm=final_norm),
        grid=(m // tm, D_FF // tf),
        in_specs=[pl.BlockSpec((tm, D_MODEL), lambda i, f: (i, 0)), grp, grp, grp, grp,
                  pl.BlockSpec((D_MODEL, D_MODEL), lambda i, f: (0, 0)), row,
                  pl.BlockSpec((D_MODEL, tf), lambda i, f: (0, f)),
                  pl.BlockSpec((tf, D_MODEL), lambda i, f: (f, 0)), row],
        out_specs=pl.BlockSpec((tm, D_MODEL), lambda i, f: (i, 0)),
        out_shape=jax.ShapeDtypeStruct((m, D_MODEL), F32),
        scratch_shapes=[pltpu.VMEM((tm, D_MODEL), BF16)],
        compiler_params=_cparams("parallel", "arbitrary"),
        name="mix_mlp",
    )(x2, *mixes, wo, gain, wu, wd, gfinal)


def _compress_weights(pe_k, w1_k, w2_k, pe_v, w1_v, w2_v):
    half = CMP_STRIDE
    d = HEAD_DIM

    def halves(w1):
        w = w1.reshape(CMP_LEN, d, CMP_HIDDEN)
        return w[:half], w[half:]

    ka, kb = halves(w1_k)
    va, vb = halves(w1_v)
    zero = jnp.zeros_like(ka)

    def merge(wk, wv):
        top = jnp.concatenate([wk, zero], axis=-1)
        bot = jnp.concatenate([zero, wv], axis=-1)
        return jnp.concatenate([top, bot], axis=1).reshape(half * 2 * d, 2 * CMP_HIDDEN).astype(BF16)

    w1a, w1b = merge(ka, va), merge(kb, vb)
    pe = jnp.concatenate([pe_k, pe_v], axis=-1)
    pea = pe[:half].reshape(1, half * 2 * d)
    peb = pe[half:].reshape(1, half * 2 * d)
    zk = jnp.zeros_like(w2_k)
    w2 = jnp.concatenate([jnp.concatenate([w2_k, w2_k, zk, zk], axis=1),
                          jnp.concatenate([zk, zk, w2_v, w2_v], axis=1)], axis=0).astype(BF16)
    return pea, peb, w1a, w1b, w2


def kernel(x, norm_attn, w_in, cmp_pe_k, cmp_w1_k, cmp_w2_k, cmp_pe_v, cmp_w1_v, cmp_w2_v, diff_lq1, diff_lk1,
           diff_lq2, diff_lk2, sinks, g_mix, w_out, norm_mlp, w_up, w_down, norm_final):
    b, s, d = x.shape
    m = b * s
    depth = w_in.shape[0]
    cols = jnp.asarray(_COLS)
    w_in_p = jnp.take(jnp.concatenate([w_in, jnp.zeros((depth, d, 1), w_in.dtype)], axis=2), cols,
                      axis=2).astype(BF16)
    w_out_b = w_out.astype(BF16)
    w_up_b = w_up.astype(BF16)
    w_down_b = w_down.astype(BF16)
    gfinal = norm_final.reshape(1, d)

    colscale = jnp.asarray(_column_scales())

    x2 = x.reshape(m, d)
    for l in range(depth):
        proj = _inproj(x2, norm_attn[l].reshape(1, d), w_in_p[l], colscale).reshape(b, s, NP)
        gm = g_mix[l].reshape(N_HEADS, 1, GROUP_WIDTH)

        chunks = proj[:, :, COL_KVC * LANES:(COL_KVC + 1) * LANES].reshape(b, s // CMP_STRIDE, CMP_STRIDE * LANES)
        cmp_kv = _compress(chunks, *_compress_weights(cmp_pe_k[l], cmp_w1_k[l], cmp_w2_k[l],
                                                      cmp_pe_v[l], cmp_w1_v[l], cmp_w2_v[l]))

        o_sb = _stick_breaking(proj, gm[0])
        o_nsa = _native_sparse(proj, cmp_kv, gm[1])
        lam_init = 0.8 - 0.6 * math.exp(-0.3 * l)
        dl = jnp.stack([diff_lq1[l], diff_lk1[l], diff_lq2[l], diff_lk2[l]]).astype(F32)
        o_diff = _differential(proj, dl, gm[2], lam_init)
        o_swa = _sliding_window(proj, sinks[l].astype(F32), gm[3])

        mixes = [o.reshape(m, GROUP_WIDTH) for o in (o_sb, o_nsa, o_diff, o_swa)]
        x2 = _mix_mlp(x2, mixes, w_out_b[l], norm_mlp[l].reshape(1, d), w_up_b[l], w_down_b[l], gfinal,
                      final_norm=(l == depth - 1))
    return x2.reshape(b, s, d)
```

```python
import functools
import math

import ml_dtypes
import numpy as np
import jax
import jax.numpy as jnp
from jax import lax
from jax.experimental import pallas as pl
from jax.experimental.pallas import tpu as pltpu

F32 = jnp.float32
BF16 = jnp.bfloat16

D_MODEL = 1024
DEPTH = 4
HEAD_DIM = 64
GROUP_WIDTH = 256
N_HEADS = 4
D_FF = 4 * D_MODEL
EPS = 1e-6
NEG_INF = -1e30
LOG2E = math.log2(math.e)
LANES = 128

CMP_LEN = 32
CMP_STRIDE = 16
CMP_HIDDEN = 128
SEL_LEN = 64
SEL_TOPN = 8
FORCE_SCORE = 1e9
NSA_WINDOW = 512
SWA_WINDOW = 128
DIFF_QK_DIM = 32

VMEM_LIMIT = 48 * 1024 * 1024

_ORIG = dict(qa=(0, 256), ka=(256, 256), va=(512, 256), qn=(768, 256), kcn=(1024, 64), vcn=(1088, 64),
             ksn=(1152, 64), vsn=(1216, 64), kwn=(1280, 64), vwn=(1344, 64), gn=(1408, 12),
             qc=(1420, 256), kc=(1676, 256), vc=(1932, 256), qd=(2188, 256), kd=(2444, 128), vd=(2572, 128))
IN_COLS = 2700


def _layout():
    cols = []

    def put(name, lo=0, n=None):
        start, size = _ORIG[name]
        n = size - lo if n is None else n
        cols.extend(range(start + lo, start + lo + n))

    def zeros(n):
        cols.extend([IN_COLS] * n)

    for nm in ("qa", "ka", "va", "qn", "kcn", "vcn"):
        put(nm)
    for nm in ("ksn", "vsn", "kwn", "vwn"):
        put(nm)
        put(nm)
    put("gn")
    zeros(LANES - 12)
    for nm in ("qc", "kc", "vc", "qd"):
        put(nm)
    for nm in ("kd", "vd"):
        put(nm, 0, 64)
        put(nm, 0, 64)
        put(nm, 64, 64)
        put(nm, 64, 64)
    return np.asarray(cols, np.int32)


_COLS = _layout()
NP = int(_COLS.shape[0])


def _column_scales():
    cs = np.ones((1, NP), np.float32)
    cs[0, 0:256] = HEAD_DIM ** -0.5 * LOG2E
    cs[0, 768:1024] = HEAD_DIM ** -0.5 * LOG2E
    cs[0, 1792:2048] = DIFF_QK_DIM ** -0.5 * LOG2E
    return cs


COL_QA, COL_KA, COL_VA = 0, 1, 2
COL_QN = 3
COL_KVC = 8
COL_KS, COL_VS, COL_KW, COL_VW, COL_GN = 9, 10, 11, 12, 13
COL_QC, COL_KC, COL_VC = 7, 8, 9
COL_QD, COL_KD, COL_VD = 10, 11, 12


def _alibi_slopes():
    m = 2.0 ** (-8.0 * np.arange(1, 13) / 12.0)
    m = m.astype(np.float32).reshape(4, 3)
    return [float(v) for v in m[:, 0]], [float(v) for v in m[:, 1]], [float(v) for v in m[:, 2]]


SLOPES_NSA, SLOPES_DIFF, SLOPES_SWA = _alibi_slopes()


def _nt(a, b):
    return lax.dot_general(a, b, (((1,), (1,)), ((), ())), preferred_element_type=F32)


def _nn(a, b):
    return jnp.dot(a, b, preferred_element_type=F32)


def _cparams(*sem):
    return pltpu.CompilerParams(dimension_semantics=sem, vmem_limit_bytes=VMEM_LIMIT)


def _rms(x):
    return x * lax.rsqrt(jnp.mean(x * x, axis=-1, keepdims=True) + EPS)


def _inproj_kernel(x_ref, g_ref, w_ref, cs_ref, o_ref):
    h = _rms(x_ref[...]) * g_ref[...]
    o_ref[...] = (_nn(h.astype(BF16), w_ref[...]) * cs_ref[...]).astype(BF16)


def _inproj(x2, gain, w, colscale, tm=512):
    m = x2.shape[0]
    return pl.pallas_call(
        _inproj_kernel,
        grid=(m // tm,),
        in_specs=[pl.BlockSpec((tm, D_MODEL), lambda i: (i, 0)),
                  pl.BlockSpec((1, D_MODEL), lambda i: (0, 0)),
                  pl.BlockSpec((D_MODEL, NP), lambda i: (0, 0)),
                  pl.BlockSpec((1, NP), lambda i: (0, 0))],
        out_specs=pl.BlockSpec((tm, NP), lambda i: (i, 0)),
        out_shape=jax.ShapeDtypeStruct((m, NP), BF16),
        compiler_params=_cparams("parallel"),
        name="inproj",
    )(x2, gain, w, colscale)


def _compress_kernel(c_ref, pea_ref, peb_ref, w1a_ref, w1b_ref, w2_ref, o_ref):
    c = c_ref[0].astype(F32)
    p = _nn((c + pea_ref[...]).astype(BF16), w1a_ref[...])
    r = _nn((c + peb_ref[...]).astype(BF16), w1b_ref[...])
    n = c.shape[0]
    r_next = pltpu.roll(r, n - 1, 0)
    hid = jax.nn.gelu(p + r_next)
    o_ref[0] = _nn(hid.astype(BF16), w2_ref[...]).astype(BF16)


def _compress(chunks, pea, peb, w1a, w1b, w2):
    b, n, w = chunks.shape
    full = lambda a: pl.BlockSpec(a.shape, lambda i: (0,) * a.ndim)
    return pl.pallas_call(
        _compress_kernel,
        grid=(b,),
        in_specs=[pl.BlockSpec((1, n, w), lambda i: (i, 0, 0)), full(pea), full(peb), full(w1a), full(w1b),
                  full(w2)],
        out_specs=pl.BlockSpec((1, n, 2 * LANES), lambda i: (i, 0, 0)),
        out_shape=jax.ShapeDtypeStruct((b, n, 2 * LANES), BF16),
        compiler_params=_cparams("parallel"),
        name="nsa_compress",
    )(chunks, pea, peb, w1a, w1b, w2)


def _half_mask(shape, half):
    lane = lax.broadcasted_iota(jnp.int32, shape, len(shape) - 1)
    return (lane // HEAD_DIM) == half


def _group_norm_store(o_ref, rows, gm_ref, o):
    o_ref[0, rows, :] = (_rms(o) * gm_ref[...]).astype(o_ref.dtype)


def _query_rows(i, blk):
    return pl.ds(pl.multiple_of(i * blk, blk), blk)


def _over_query_blocks(step, blk):
    def kernel(*refs):
        def body(i, c):
            step(i, *refs)
            return c
        lax.fori_loop(0, refs[0].shape[1] // blk, body, 0)
    return kernel


def _sb_step(i, q_ref, k_ref, v_ref, gm_ref, o_ref, acc_ref, carry_ref, lb_ref, lk_ref, tot_ref, *, blk):
    q = q_ref[0, _query_rows(i, blk), :]
    row = lax.broadcasted_iota(jnp.int32, (blk, blk), 0)
    col = lax.broadcasted_iota(jnp.int32, (blk, blk), 1)
    past = col < row
    u = jnp.where(row > col, 1.0, 0.0).astype(BF16)
    qm = []
    for h in range(N_HEADS):
        slab = q[:, (h // 2) * LANES:(h // 2 + 1) * LANES]
        qm.append(jnp.where(_half_mask(slab.shape, h % 2), slab, jnp.zeros_like(slab)))
    qm = [jnp.concatenate(qm[0:2], axis=0), jnp.concatenate(qm[2:4], axis=0)]
    acc_ref[...] = jnp.zeros_like(acc_ref)
    carry_ref[...] = jnp.zeros_like(carry_ref)

    def per_head(x, fn):
        return jnp.concatenate([fn(x[k * blk:(k + 1) * blk]) for k in range(x.shape[0] // blk)], axis=0)

    def a_logits(j):
        rows = pl.ds(pl.multiple_of(j * blk, blk), blk)
        return [_nt(qm[pr], k_ref[0, rows, pr * LANES:(pr + 1) * LANES]) for pr in range(2)]

    def a_finish(zs, slot, diag):
        for pr, z in enumerate(zs):
            r2 = slice(2 * pr * blk, 2 * (pr + 1) * blk)
            sp = jnp.log2(1.0 + jnp.exp2(-jnp.abs(z)))
            lb = jnp.minimum(z, 0.0) - sp
            lk = lb - z
            if diag:
                lk = per_head(lk, lambda t: jnp.where(past, t, 0.0))
                lb = per_head(lb, lambda t: jnp.where(past, t, NEG_INF))
            lb_ref[slot, r2, :] = lb
            lk_ref[slot, r2, :] = lk.astype(BF16)
            tot_ref[slot, r2, :] = jnp.broadcast_to(jnp.sum(lk, axis=-1, keepdims=True), (2 * blk, LANES))

    def b_suffix(slot):
        return _nn(lk_ref[slot], u)

    def b_finish(cs, j, slot):
        rows = pl.ds(pl.multiple_of(j * blk, blk), blk)
        carry = carry_ref[...]
        a = jnp.exp2(lb_ref[slot] + cs + jnp.concatenate([carry] * (blk // LANES), axis=1)).astype(BF16)
        for pr in range(2):
            r2 = slice(2 * pr * blk, 2 * (pr + 1) * blk)
            acc_ref[r2, :] += _nn(a[r2], v_ref[0, rows, pr * LANES:(pr + 1) * LANES])
        carry_ref[...] = carry + tot_ref[slot]

    a_finish(a_logits(i), 0, True)

    def step(t, slot):
        zs = a_logits(i - 1 - t)
        cs = b_suffix(slot)
        a_finish(zs, 1 - slot, False)
        b_finish(cs, i - t, slot)

    def body(tt, c):
        step(2 * tt, 0)
        step(2 * tt + 1, 1)
        return c

    lax.fori_loop(0, i // 2, body, 0)

    @pl.when(i % 2 == 1)
    def _():
        step(i - 1, 0)
        b_finish(b_suffix(1), 0, 1)

    @pl.when(i % 2 == 0)
    def _():
        b_finish(b_suffix(0), 0, 0)

    lo_half = _half_mask((blk, LANES), 0)
    acc = [acc_ref[h * blk:(h + 1) * blk, :] for h in range(N_HEADS)]
    o = jnp.concatenate([jnp.where(lo_half, acc[0], acc[1]), jnp.where(lo_half, acc[2], acc[3])], axis=1)
    _group_norm_store(o_ref, _query_rows(i, blk), gm_ref, o)


def _stick_breaking(proj, gm, blk=256):
    b, s, _ = proj.shape
    stage = lambda w, dt: pltpu.VMEM((2, N_HEADS * blk, w), dt)
    return pl.pallas_call(
        _over_query_blocks(functools.partial(_sb_step, blk=blk), blk),
        grid=(b,),
        in_specs=[pl.BlockSpec((1, s, 256), lambda bi: (bi, 0, COL_QA)),
                  pl.BlockSpec((1, s, 256), lambda bi: (bi, 0, COL_KA)),
                  pl.BlockSpec((1, s, 256), lambda bi: (bi, 0, COL_VA)),
                  pl.BlockSpec((1, 256), lambda bi: (0, 0))],
        out_specs=pl.BlockSpec((1, s, 256), lambda bi: (bi, 0, 0)),
        out_shape=jax.ShapeDtypeStruct((b, s, 256), BF16),
        scratch_shapes=[pltpu.VMEM((N_HEADS * blk, LANES), F32), pltpu.VMEM((N_HEADS * blk, LANES), F32),
                        stage(blk, F32), stage(blk, BF16), stage(LANES, F32)],
        compiler_params=_cparams("parallel"),
        name="stick_breaking",
    )(proj, proj, proj, gm)


N_MAPS = 2 * N_HEADS
POS_LOCAL, POS_BLOCK, POS_ONE = 0, 3, 6
SEL_LANE = 8
MASK_BIAS = -2.0 ** 100


def _bf16_pieces(x, n):
    out = []
    r = np.float32(x)
    for _ in range(n):
        p = np.float32(r.astype(ml_dtypes.bfloat16))
        out.append(float(p))
        r = np.float32(r - p)
    return out


def _position_tables(slopes, s, blk, sel_blocks=False):
    coef = np.zeros((N_HEADS, LANES), np.float32)
    for h in range(N_HEADS):
        pieces = _bf16_pieces(slopes[h] * LOG2E, 3)
        coef[h, POS_LOCAL:POS_LOCAL + 3] = pieces
        coef[h, POS_BLOCK:POS_BLOCK + 3] = [blk * p for p in pieces]
    pos = np.zeros((s, LANES), np.float32)
    idx = np.arange(s)
    pos[:, POS_LOCAL:POS_LOCAL + 3] = (idx % blk)[:, None]
    pos[:, POS_BLOCK:POS_BLOCK + 3] = (idx // blk)[:, None]
    pos[:, POS_ONE:POS_ONE + 2] = 1.0
    if sel_blocks:
        pos[idx, SEL_LANE + idx // SEL_LEN] = 1.0
    return jnp.asarray(coef), jnp.asarray(pos, BF16)


def _diff_step(i, q_ref, k_ref, v_ref, pos_ref, coef_ref, dl_ref, gm_ref, o_ref, qa_ref, mx_ref, acc_ref,
               *, blk, lam_init):
    q = q_ref[0, _query_rows(i, blk), :]
    row = lax.broadcasted_iota(jnp.int32, (blk, blk), 0)
    col = lax.broadcasted_iota(jnp.int32, (blk, blk), 1)
    causal = col <= row
    lane = lax.broadcasted_iota(jnp.int32, (blk, LANES), 1)
    for h in range(N_HEADS):
        slab = q[:, (h // 2) * LANES:(h // 2 + 1) * LANES]
        coef = jnp.broadcast_to(coef_ref[h:h + 1, :], (blk, LANES)).astype(BF16)
        for c in range(2):
            r = slice(((2 * h + c) % 4) * blk, ((2 * h + c) % 4 + 1) * blk)
            qa_ref[h // 2, r, 0:LANES] = jnp.where((lane // DIFF_QK_DIM) == (h % 2) * 2 + c, slab,
                                                   jnp.zeros_like(slab))
            qa_ref[h // 2, r, LANES:2 * LANES] = coef
    ones = jnp.ones((blk, LANES), BF16)

    def per_map(x, fn):
        return jnp.concatenate([fn(x[k * blk:(k + 1) * blk]) for k in range(4)], axis=0)

    def sweep(blocks, second, first):
        chains = [(pr, pl.ds(pl.multiple_of(j * blk, blk), blk), diag) for pr in range(2) for j, diag in blocks]
        zs = []
        for pr, rows, diag in chains:
            ka = jnp.concatenate([k_ref[0, rows, pr * LANES:(pr + 1) * LANES], pos_ref[rows, :]], axis=1)
            zs.append(_nt(qa_ref[pr], ka))
        outs = []
        if second:
            ps = []
            for (pr, rows, diag), z in zip(chains, zs):
                p = jnp.exp2(z)
                if diag:
                    p = per_map(p, lambda t: jnp.where(causal, t, 0.0))
                ps.append(p.astype(BF16))
            for (pr, rows, diag), p in zip(chains, ps):
                vo = jnp.concatenate([v_ref[0, rows, pr * LANES:(pr + 1) * LANES], ones], axis=1)
                outs.append(_nn(p, vo))
        else:
            for (pr, rows, diag), z in zip(chains, zs):
                if diag:
                    z = per_map(z, lambda t: jnp.where(causal, t, NEG_INF))
                zm = z[:, 0:LANES]
                for g in range(1, blk // LANES):
                    zm = jnp.maximum(zm, z[:, g * LANES:(g + 1) * LANES])
                outs.append(zm)
        ref, combine = (acc_ref, jnp.add) if second else (mx_ref, jnp.maximum)
        for pr in range(2):
            mine = [o for (cp, _, _), o in zip(chains, outs) if cp == pr]
            tot = mine[0]
            for o in mine[1:]:
                tot = combine(tot, o)
            ref[pr] = tot if first else combine(ref[pr], tot)

    def run(second):
        @pl.when(i % 2 == 1)
        def _():
            sweep([(i - 1, False), (i, True)], second, True)

        @pl.when(i % 2 == 0)
        def _():
            sweep([(i, True)], second, True)

        def body(tt, c):
            sweep([(2 * tt, False), (2 * tt + 1, False)], second, False)
            return c
        lax.fori_loop(0, i // 2, body, 0)

    run(False)
    lane4 = lax.broadcasted_iota(jnp.int32, (4 * blk, LANES), 1)
    for pr in range(2):
        m = jnp.max(mx_ref[pr], axis=-1, keepdims=True)
        m_hi = m.astype(BF16)
        m_lo = (m - m_hi.astype(F32)).astype(BF16)
        old = qa_ref[pr, :, LANES:2 * LANES]
        qa_ref[pr, :, LANES:2 * LANES] = jnp.where(lane4 == POS_ONE, -m_hi, jnp.where(lane4 == POS_ONE + 1, -m_lo, old))
    run(True)

    dl = dl_ref[...]
    s1 = jnp.sum(dl[0:1] * dl[1:2], axis=-1, keepdims=True)
    s2 = jnp.sum(dl[2:3] * dl[3:4], axis=-1, keepdims=True)
    lam = jnp.exp(s1) - jnp.exp(s2) + lam_init
    outs = []
    for h in range(N_HEADS):
        k1 = (2 * h) % 4
        a1 = acc_ref[h // 2, k1 * blk:(k1 + 1) * blk, :]
        a2 = acc_ref[h // 2, (k1 + 1) * blk:(k1 + 2) * blk, :]
        d = a1[:, :LANES] / a1[:, LANES:] - lam * (a2[:, :LANES] / a2[:, LANES:])
        mine = _half_mask(d.shape, h % 2)
        ms = jnp.sum(jnp.where(mine, d * d, 0.0), axis=-1, keepdims=True) * (1.0 / HEAD_DIM)
        outs.append(d * lax.rsqrt(ms + EPS) * (1.0 - lam_init))
    lo_half = _half_mask((blk, LANES), 0)
    o = jnp.concatenate([jnp.where(lo_half, outs[0], outs[1]), jnp.where(lo_half, outs[2], outs[3])], axis=1)
    o_ref[0, _query_rows(i, blk), :] = (o * gm_ref[...]).astype(o_ref.dtype)


def _differential(proj, dl, gm, lam_init, blk=256):
    b, s, _ = proj.shape
    coef, pos = _position_tables(SLOPES_DIFF, s, blk)
    return pl.pallas_call(
        _over_query_blocks(functools.partial(_diff_step, blk=blk, lam_init=lam_init), blk),
        grid=(b,),
        in_specs=[pl.BlockSpec((1, s, 256), lambda bi: (bi, 0, COL_QC)),
                  pl.BlockSpec((1, s, 256), lambda bi: (bi, 0, COL_KC)),
                  pl.BlockSpec((1, s, 256), lambda bi: (bi, 0, COL_VC)),
                  pl.BlockSpec((s, LANES), lambda bi: (0, 0)),
                  pl.BlockSpec((N_HEADS, LANES), lambda bi: (0, 0)),
                  pl.BlockSpec((4, DIFF_QK_DIM), lambda bi: (0, 0)),
                  pl.BlockSpec((1, 256), lambda bi: (0, 0))],
        out_specs=pl.BlockSpec((1, s, 256), lambda bi: (bi, 0, 0)),
        out_shape=jax.ShapeDtypeStruct((b, s, 256), BF16),
        scratch_shapes=[pltpu.VMEM((2, 4 * blk, 2 * LANES), BF16), pltpu.VMEM((2, 4 * blk, LANES), F32),
                        pltpu.VMEM((2, 4 * blk, 2 * LANES), F32)],
        compiler_params=_cparams("parallel"),
        name="differential",
    )(proj, proj, proj, pos, coef, dl, gm)


def _swa_step(i, q_ref, k_ref, v_ref, sink_ref, gm_ref, o_ref, *, blk, sub):
    row = lax.broadcasted_iota(jnp.int32, (blk, 2 * blk), 0)
    col = lax.broadcasted_iota(jnp.int32, (blk, 2 * blk), 1)
    lo_half = _half_mask((blk, LANES), 0)
    blocks = []
    for s_ in range(sub):
        g = i * sub + s_
        first = jnp.maximum(g - 1, 0)
        start = pl.multiple_of(first * blk, blk)
        dist = (row - col) + (g - first) * blk
        mask = (dist >= 0) & (dist < SWA_WINDOW)
        distf = dist.astype(F32)
        q = q_ref[0, _query_rows(g, blk), :]
        outs = []
        for h in range(N_HEADS):
            lanes = slice((h // 2) * LANES, (h // 2 + 1) * LANES)
            slab = q[:, lanes]
            qm = jnp.where(_half_mask(slab.shape, h % 2), slab, jnp.zeros_like(slab))
            kb = k_ref[0, pl.ds(start, 2 * blk), lanes]
            vb = v_ref[0, pl.ds(start, 2 * blk), lanes]
            z = _nt(qm, kb) * (HEAD_DIM ** -0.5) - SLOPES_SWA[h] * distf
            z = jnp.where(mask, z, NEG_INF)
            sink = sink_ref[h]
            m = jnp.maximum(jnp.max(z, axis=-1, keepdims=True), sink)
            p = jnp.exp(z - m)
            l = jnp.sum(p, axis=-1, keepdims=True) + jnp.exp(sink - m)
            outs.append(_nn(p.astype(BF16), vb) / l)
        blocks.append(jnp.concatenate([jnp.where(lo_half, outs[0], outs[1]), jnp.where(lo_half, outs[2], outs[3])],
                                      axis=1))
    _group_norm_store(o_ref, _query_rows(i, sub * blk), gm_ref, jnp.concatenate(blocks, axis=0))


def _sliding_window(proj, sinks, gm, blk=128, sub=4):
    b, s, _ = proj.shape
    rows = blk * sub
    return pl.pallas_call(
        _over_query_blocks(functools.partial(_swa_step, blk=blk, sub=sub), rows),
        grid=(b,),
        in_specs=[pl.BlockSpec((1, s, 256), lambda bi: (bi, 0, COL_QD)),
                  pl.BlockSpec((1, s, 256), lambda bi: (bi, 0, COL_KD)),
                  pl.BlockSpec((1, s, 256), lambda bi: (bi, 0, COL_VD)),
                  pl.BlockSpec(memory_space=pltpu.SMEM),
                  pl.BlockSpec((1, 256), lambda bi: (0, 0))],
        out_specs=pl.BlockSpec((1, s, 256), lambda bi: (bi, 0, 0)),
        out_shape=jax.ShapeDtypeStruct((b, s, 256), BF16),
        compiler_params=_cparams("parallel"),
        name="sliding_window",
    )(proj, proj, proj, sinks, gm)


def _nsa_step(i, q_ref, g_ref, cmp_ref, ks_ref, vs_ref, kw_ref, vw_ref, pos_ref, coef_ref, gm_ref, o_ref,
              qa_ref, mx_ref, acc_ref, *, blk, n_cmp, n_sel):
    q = q_ref[0, _query_rows(i, blk), :]
    lane = lax.broadcasted_iota(jnp.int32, (blk, LANES), 1)
    coefs = [jnp.broadcast_to(coef_ref[h:h + 1, :], (blk, LANES)) for h in range(N_HEADS)]
    for h in range(N_HEADS):
        slab = q[:, (h // 2) * LANES:(h // 2 + 1) * LANES]
        own = jnp.where(_half_mask(slab.shape, h % 2), slab, jnp.zeros_like(slab))
        qa_ref[0, h * blk:(h + 1) * blk, 0:LANES] = own
        qa_ref[1, h * blk:(h + 1) * blk, 0:LANES] = own
        qa_ref[0, h * blk:(h + 1) * blk, LANES:2 * LANES] = coefs[h].astype(BF16)
    qs = qa_ref[1, :, 0:LANES]

    tq = i * blk + lax.broadcasted_iota(jnp.int32, (blk, LANES), 0)

    row = lax.broadcasted_iota(jnp.int32, (blk, blk), 0)
    col = lax.broadcasted_iota(jnp.int32, (blk, blk), 1)
    causal = col <= row
    in_window = col > row
    ones = jnp.ones((blk, LANES), BF16)

    k_refs, v_refs = (kw_ref, ks_ref), (vw_ref, vs_ref)

    def process(blocks, second, first=False):
        groups = [(br, half) for br in sorted({b[0] for b in blocks}) for half in range(2)]
        chains = [(br, pl.ds(pl.multiple_of(j * blk, blk), blk), mask, (br, half))
                  for br, j, mask in blocks for half in range(2)]
        grows = lambda g: slice(2 * g[1] * blk, 2 * (g[1] + 1) * blk)
        zs = []
        for br, rows, mask, g in chains:
            ka = jnp.concatenate([k_refs[br][0, rows, :], pos_ref[rows, :]], axis=1)
            zs.append(_nt(qa_ref[br, grows(g), :], ka))

        def masked(x, mask, fill):
            if mask is None:
                return x
            return jnp.concatenate([jnp.where(mask, x[r * blk:(r + 1) * blk], fill) for r in range(x.shape[0] // blk)],
                                   axis=0)

        if second:
            ps = [masked(jnp.exp2(z), mask, 0.0).astype(BF16) for (br, rows, mask, g), z in zip(chains, zs)]
            outs = []
            for (br, rows, mask, g), p in zip(chains, ps):
                vo = jnp.concatenate([v_refs[br][0, rows, :], ones], axis=1)
                outs.append(_nn(p, vo))
            for g in groups:
                mine = [o for (_, _, _, cg), o in zip(chains, outs) if cg == g]
                tot = functools.reduce(jnp.add, mine)
                acc_ref[g[0], grows(g), :] = tot if first else acc_ref[g[0], grows(g), :] + tot
        else:
            outs = []
            for (br, rows, mask, g), z in zip(chains, zs):
                z = masked(z, mask, NEG_INF)
                zm = z[:, 0:LANES]
                for c in range(1, blk // LANES):
                    zm = jnp.maximum(zm, z[:, c * LANES:(c + 1) * LANES])
                outs.append(zm)
            for g in groups:
                mine = [o for (_, _, _, cg), o in zip(chains, outs) if cg == g]
                tot = functools.reduce(jnp.maximum, mine)
                mx_ref[g[0], grows(g), :] = tot if first else jnp.maximum(mx_ref[g[0], grows(g), :], tot)

    window = [(0, jnp.maximum(i - 2, 0), in_window & (i >= 2)),
              (0, jnp.maximum(i - 1, 0), jnp.broadcast_to(i >= 1, (blk, blk))),
              (0, i, causal)]
    lane4 = lax.broadcasted_iota(jnp.int32, (N_HEADS * blk, LANES), 1)

    def fold_max(br):
        m = jnp.max(mx_ref[br], axis=-1, keepdims=True)
        m_hi = m.astype(BF16)
        m_lo = (m - m_hi.astype(F32)).astype(BF16)
        old = qa_ref[br, :, LANES:2 * LANES]
        qa_ref[br, :, LANES:2 * LANES] = jnp.where(lane4 == POS_ONE, -m_hi, jnp.where(lane4 == POS_ONE + 1, -m_lo, old))

    def sweep(second):
        @pl.when(i % 2 == 1)
        def _():
            process([(1, i - 1, None), (1, i, causal)], second, first=True)

        @pl.when(i % 2 == 0)
        def _():
            process([(1, i, causal)], second, first=True)

        def body(tt, c):
            process([(1, 2 * tt, None), (1, 2 * tt + 1, None)], second)
            return c
        lax.fori_loop(0, i // 2, body, 0)

    process(window, False, first=True)
    fold_max(0)
    process(window, True, first=True)
    o_win = acc_ref[0, :, 0:LANES] / acc_ref[0, :, LANES:2 * LANES]

    kc = cmp_ref[0, :, 0:LANES]
    vc = cmp_ref[0, :, LANES:2 * LANES]
    dist_c = tq - (CMP_STRIDE * lane + CMP_LEN - 1)
    ok_c = (dist_c >= 0) & (lane < n_cmp)
    dist_cf = dist_c.astype(F32)
    bias_c = jnp.concatenate([jnp.where(ok_c, (-SLOPES_NSA[h] * LOG2E) * dist_cf, NEG_INF)
                              for h in range(N_HEADS)], axis=0)
    z = _nt(qs, kc) + bias_c
    p = jnp.exp2(z - jnp.max(z, axis=-1, keepdims=True))
    p = p / jnp.sum(p, axis=-1, keepdims=True)
    p = jnp.where(bias_c > 0.5 * NEG_INF, p, 0.0).astype(BF16)
    o_cmp = _nn(p, vc)

    sel_rows = -(-(SEL_LANE + n_sel) // 8) * 8
    rj = lax.broadcasted_iota(jnp.int32, (sel_rows, LANES), 0)
    ri = lax.broadcasted_iota(jnp.int32, (sel_rows, LANES), 1)
    to_sel = jnp.where((ri * CMP_STRIDE) // SEL_LEN + SEL_LANE == rj, 1.0, 0.0).astype(BF16)
    imp = _nt(to_sel, p[0:blk])
    for h in range(1, N_HEADS):
        imp = imp + _nt(to_sel, p[h * blk:(h + 1) * blk])
    srow = lax.broadcasted_iota(jnp.int32, (sel_rows, blk), 0)
    blk_id = srow - SEL_LANE
    cur = (i * blk + lax.broadcasted_iota(jnp.int32, (sel_rows, blk), 1)) // SEL_LEN
    forced = (blk_id == 0) | (blk_id == cur) | (blk_id == cur - 1)
    score = jnp.where(blk_id > cur, NEG_INF, jnp.where(forced, FORCE_SCORE, imp))
    score = jnp.where((blk_id >= 0) & (blk_id < n_sel), score, -3.0e38)
    sel = jnp.zeros((sel_rows, blk), F32)
    srowf = srow.astype(F32)
    for _ in range(min(SEL_TOPN, n_sel)):
        best = jnp.max(score, axis=0, keepdims=True)
        idx = jnp.min(jnp.where(score == best, srowf, float(LANES)), axis=0, keepdims=True)
        hit = srowf == idx
        sel = jnp.where(hit, 1.0, sel)
        score = jnp.where(hit, -3.4e38, score)
    sel_bias = jnp.where(sel > 0.5, 0.0, MASK_BIAS)
    sel_bias = jnp.concatenate([sel_bias, jnp.zeros((LANES - sel_rows, blk), F32)], axis=0).T
    in_sel = (lane >= SEL_LANE) & (lane < SEL_LANE + n_sel)

    for h in range(N_HEADS):
        qa_ref[1, h * blk:(h + 1) * blk, LANES:2 * LANES] = jnp.where(in_sel, sel_bias, coefs[h]).astype(BF16)
    sweep(False)
    fold_max(1)
    sweep(True)
    o_sel = acc_ref[1, :, 0:LANES] / acc_ref[1, :, LANES:2 * LANES]

    g = jax.nn.sigmoid(g_ref[0, _query_rows(i, blk), :].astype(F32))
    outs = []
    for h in range(N_HEADS):
        r = slice(h * blk, (h + 1) * blk)
        outs.append(g[:, 3 * h:3 * h + 1] * o_cmp[r] + g[:, 3 * h + 1:3 * h + 2] * o_sel[r]
                    + g[:, 3 * h + 2:3 * h + 3] * o_win[r])
    lo_half = _half_mask((blk, LANES), 0)
    o = jnp.concatenate([jnp.where(lo_half, outs[0], outs[1]), jnp.where(lo_half, outs[2], outs[3])], axis=1)
    _group_norm_store(o_ref, _query_rows(i, blk), gm_ref, o)


def _native_sparse(proj, cmp_kv, gm, blk=256):
    b, s, _ = proj.shape
    n_cmp = (s - CMP_LEN) // CMP_STRIDE + 1
    n_sel = s // SEL_LEN
    assert cmp_kv.shape[1] == LANES and n_cmp <= LANES and SEL_LANE + n_sel <= LANES
    assert NSA_WINDOW == 2 * blk
    coef, pos = _position_tables(SLOPES_NSA, s, blk, sel_blocks=True)
    slab = lambda c: pl.BlockSpec((1, s, LANES), lambda bi: (bi, 0, c))
    rows = N_HEADS * blk
    return pl.pallas_call(
        _over_query_blocks(functools.partial(_nsa_step, blk=blk, n_cmp=n_cmp, n_sel=n_sel), blk),
        grid=(b,),
        in_specs=[pl.BlockSpec((1, s, 256), lambda bi: (bi, 0, COL_QN)),
                  slab(COL_GN),
                  pl.BlockSpec((1, LANES, 2 * LANES), lambda bi: (bi, 0, 0)),
                  slab(COL_KS), slab(COL_VS), slab(COL_KW), slab(COL_VW),
                  pl.BlockSpec((s, LANES), lambda bi: (0, 0)),
                  pl.BlockSpec((N_HEADS, LANES), lambda bi: (0, 0)),
                  pl.BlockSpec((1, 256), lambda bi: (0, 0))],
        out_specs=pl.BlockSpec((1, s, 256), lambda bi: (bi, 0, 0)),
        out_shape=jax.ShapeDtypeStruct((b, s, 256), BF16),
        scratch_shapes=[pltpu.VMEM((2, rows, 2 * LANES), BF16), pltpu.VMEM((2, rows, LANES), F32),
                        pltpu.VMEM((2, rows, 2 * LANES), F32)],
        compiler_params=_cparams("parallel"),
        name="native_sparse",
    )(proj, proj, cmp_kv, proj, proj, proj, proj, pos, coef, gm)


def _mix_mlp_kernel(x_ref, a_ref, b_ref, c_ref, d_ref, wo_ref, g_ref, wu_ref, wd_ref, gf_ref, o_ref, h_ref,
                    *, final_norm):
    f = pl.program_id(1)

    @pl.when(f == 0)
    def _():
        mix = jnp.concatenate([a_ref[...], b_ref[...], c_ref[...], d_ref[...]], axis=1)
        x1 = x_ref[...] + _nn(mix, wo_ref[...])
        o_ref[...] = x1
        h_ref[...] = (_rms(x1) * g_ref[...]).astype(BF16)

    u = jnp.maximum(_nn(h_ref[...], wu_ref[...]), 0.0)
    o_ref[...] += _nn((u * u).astype(BF16), wd_ref[...])

    if final_norm:
        @pl.when(f == pl.num_programs(1) - 1)
        def _():
            o_ref[...] = _rms(o_ref[...]) * gf_ref[...]


def _mix_mlp(x2, mixes, wo, gain, wu, wd, gfinal, final_norm, tm=1024, tf=1024):
    m = x2.shape[0]
    grp = pl.BlockSpec((tm, GROUP_WIDTH), lambda i, f: (i, 0))
    row = pl.BlockSpec((1, D_MODEL), lambda i, f: (0, 0))
    return pl.pallas_call(
        functools.partial(_mix_mlp_kernel, final_norm=final_norm),
        grid=(m // tm, D_FF // tf),
        in_specs=[pl.BlockSpec((tm, D_MODEL), lambda i, f: (i, 0)), grp, grp, grp, grp,
                  pl.BlockSpec((D_MODEL, D_MODEL), lambda i, f: (0, 0)), row,
                  pl.BlockSpec((D_MODEL, tf), lambda i, f: (0, f)),
                  pl.BlockSpec((tf, D_MODEL), lambda i, f: (f, 0)), row],
        out_specs=pl.BlockSpec((tm, D_MODEL), lambda i, f: (i, 0)),
        out_shape=jax.ShapeDtypeStruct((m, D_MODEL), F32),
        scratch_shapes=[pltpu.VMEM((tm, D_MODEL), BF16)],
        compiler_params=_cparams("parallel", "arbitrary"),
        name="mix_mlp",
    )(x2, *mixes, wo, gain, wu, wd, gfinal)


def _compress_weights(pe_k, w1_k, w2_k, pe_v, w1_v, w2_v):
    half = CMP_STRIDE
    d = HEAD_DIM

    def halves(w1):
        w = w1.reshape(CMP_LEN, d, CMP_HIDDEN)
        return w[:half], w[half:]

    ka, kb = halves(w1_k)
    va, vb = halves(w1_v)
    zero = jnp.zeros_like(ka)

    def merge(wk, wv):
        top = jnp.concatenate([wk, zero], axis=-1)
        bot = jnp.concatenate([zero, wv], axis=-1)
        return jnp.concatenate([top, bot], axis=1).reshape(half * 2 * d, 2 * CMP_HIDDEN).astype(BF16)

    w1a, w1b = merge(ka, va), merge(kb, vb)
    pe = jnp.concatenate([pe_k, pe_v], axis=-1)
    pea = pe[:half].reshape(1, half * 2 * d)
    peb = pe[half:].reshape(1, half * 2 * d)
    zk = jnp.zeros_like(w2_k)
    w2 = jnp.concatenate([jnp.concatenate([w2_k, w2_k, zk, zk], axis=1),
                          jnp.concatenate([zk, zk, w2_v, w2_v], axis=1)], axis=0).astype(BF16)
    return pea, peb, w1a, w1b, w2


def kernel(x, norm_attn, w_in, cmp_pe_k, cmp_w1_k, cmp_w2_k, cmp_pe_v, cmp_w1_v, cmp_w2_v, diff_lq1, diff_lk1,
           diff_lq2, diff_lk2, sinks, g_mix, w_out, norm_mlp, w_up, w_down, norm_final):
    b, s, d = x.shape
    m = b * s
    depth = w_in.shape[0]
    cols = jnp.asarray(_COLS)
    w_in_p = jnp.take(jnp.concatenate([w_in, jnp.zeros((depth, d, 1), w_in.dtype)], axis=2), cols,
                      axis=2).astype(BF16)
    w_out_b = w_out.astype(BF16)
    w_up_b = w_up.astype(BF16)
    w_down_b = w_down.astype(BF16)
    gfinal = norm_final.reshape(1, d)

    colscale = jnp.asarray(_column_scales())

    x2 = x.reshape(m, d)
    for l in range(depth):
        proj = _inproj(x2, norm_attn[l].reshape(1, d), w_in_p[l], colscale).reshape(b, s, NP)
        gm = g_mix[l].reshape(N_HEADS, 1, GROUP_WIDTH)

        chunks = proj[:, :, COL_KVC * LANES:(COL_KVC + 1) * LANES].reshape(b, s // CMP_STRIDE, CMP_STRIDE * LANES)
        cmp_kv = _compress(chunks, *_compress_weights(cmp_pe_k[l], cmp_w1_k[l], cmp_w2_k[l],
                                                      cmp_pe_v[l], cmp_w1_v[l], cmp_w2_v[l]))

        o_sb = _stick_breaking(proj, gm[0])
        o_nsa = _native_sparse(proj, cmp_kv, gm[1])
        lam_init = 0.8 - 0.6 * math.exp(-0.3 * l)
        dl = jnp.stack([diff_lq1[l], diff_lk1[l], diff_lq2[l], diff_lk2[l]]).astype(F32)
        o_diff = _differential(proj, dl, gm[2], lam_init)
        o_swa = _sliding_window(proj, sinks[l].astype(F32), gm[3])

        mixes = [o.reshape(m, GROUP_WIDTH) for o in (o_sb, o_nsa, o_diff, o_swa)]
        x2 = _mix_mlp(x2, mixes, w_out_b[l], norm_mlp[l].reshape(1, d), w_up_b[l], w_down_b[l], gfinal,
                      final_norm=(l == depth - 1))
    return x2.reshape(b, s, d)
```

```python
import functools
import math

import ml_dtypes
import numpy as np
import jax
import jax.numpy as jnp
from jax import lax
from jax.experimental import pallas as pl
from jax.experimental.pallas import tpu as pltpu

F32 = jnp.float32
BF16 = jnp.bfloat16

D_MODEL = 1024
DEPTH = 4
HEAD_DIM = 64
GROUP_WIDTH = 256
N_HEADS = 4
D_FF = 4 * D_MODEL
EPS = 1e-6
NEG_INF = -1e30
LOG2E = math.log2(math.e)
LANES = 128

CMP_LEN = 32
CMP_STRIDE = 16
CMP_HIDDEN = 128
SEL_LEN = 64
SEL_TOPN = 8
FORCE_SCORE = 1e9
NSA_WINDOW = 512
SWA_WINDOW = 128
DIFF_QK_DIM = 32

VMEM_LIMIT = 48 * 1024 * 1024

_ORIG = dict(qa=(0, 256), ka=(256, 256), va=(512, 256), qn=(768, 256), kcn=(1024, 64), vcn=(1088, 64),
             ksn=(1152, 64), vsn=(1216, 64), kwn=(1280, 64), vwn=(1344, 64), gn=(1408, 12),
             qc=(1420, 256), kc=(1676, 256), vc=(1932, 256), qd=(2188, 256), kd=(2444, 128), vd=(2572, 128))
IN_COLS = 2700


def _layout():
    cols = []

    def put(name, lo=0, n=None):
        start, size = _ORIG[name]
        n = size - lo if n is None else n
        cols.extend(range(start + lo, start + lo + n))

    def zeros(n):
        cols.extend([IN_COLS] * n)

    for nm in ("qa", "ka", "va", "qn", "kcn", "vcn"):
        put(nm)
    for nm in ("ksn", "vsn", "kwn", "vwn"):
        put(nm)
        put(nm)
    put("gn")
    zeros(LANES - 12)
    for nm in ("qc", "kc", "vc", "qd"):
        put(nm)
    for nm in ("kd", "vd"):
        put(nm, 0, 64)
        put(nm, 0, 64)
        put(nm, 64, 64)
        put(nm, 64, 64)
    return np.asarray(cols, np.int32)


_COLS = _layout()
NP = int(_COLS.shape[0])


def _column_scales():
    cs = np.ones((1, NP), np.float32)
    cs[0, 0:256] = HEAD_DIM ** -0.5 * LOG2E
    cs[0, 768:1024] = HEAD_DIM ** -0.5 * LOG2E
    cs[0, 1792:2048] = DIFF_QK_DIM ** -0.5 * LOG2E
    return cs


COL_QA, COL_KA, COL_VA = 0, 1, 2
COL_QN = 3
COL_KVC = 8
COL_KS, COL_VS, COL_KW, COL_VW, COL_GN = 9, 10, 11, 12, 13
COL_QC, COL_KC, COL_VC = 7, 8, 9
COL_QD, COL_KD, COL_VD = 10, 11, 12


def _alibi_slopes():
    m = 2.0 ** (-8.0 * np.arange(1, 13) / 12.0)
    m = m.astype(np.float32).reshape(4, 3)
    return [float(v) for v in m[:, 0]], [float(v) for v in m[:, 1]], [float(v) for v in m[:, 2]]


SLOPES_NSA, SLOPES_DIFF, SLOPES_SWA = _alibi_slopes()


def _nt(a, b):
    return lax.dot_general(a, b, (((1,), (1,)), ((), ())), preferred_element_type=F32)


def _nn(a, b):
    return jnp.dot(a, b, preferred_element_type=F32)


def _cparams(*sem):
    return pltpu.CompilerParams(dimension_semantics=sem, vmem_limit_bytes=VMEM_LIMIT)


def _rms(x):
    return x * lax.rsqrt(jnp.mean(x * x, axis=-1, keepdims=True) + EPS)


def _inproj_kernel(x_ref, g_ref, w_ref, cs_ref, o_ref):
    h = _rms(x_ref[...]) * g_ref[...]
    o_ref[...] = (_nn(h.astype(BF16), w_ref[...]) * cs_ref[...]).astype(BF16)


def _inproj(x2, gain, w, colscale, tm=512):
    m = x2.shape[0]
    return pl.pallas_call(
        _inproj_kernel,
        grid=(m // tm,),
        in_specs=[pl.BlockSpec((tm, D_MODEL), lambda i: (i, 0)),
                  pl.BlockSpec((1, D_MODEL), lambda i: (0, 0)),
                  pl.BlockSpec((D_MODEL, NP), lambda i: (0, 0)),
                  pl.BlockSpec((1, NP), lambda i: (0, 0))],
        out_specs=pl.BlockSpec((tm, NP), lambda i: (i, 0)),
        out_shape=jax.ShapeDtypeStruct((m, NP), BF16),
        compiler_params=_cparams("parallel"),
        name="inproj",
    )(x2, gain, w, colscale)


def _compress_kernel(c_ref, pea_ref, peb_ref, w1a_ref, w1b_ref, w2_ref, o_ref):
    c = c_ref[0].astype(F32)
    p = _nn((c + pea_ref[...]).astype(BF16), w1a_ref[...])
    r = _nn((c + peb_ref[...]).astype(BF16), w1b_ref[...])
    n = c.shape[0]
    r_next = pltpu.roll(r, n - 1, 0)
    hid = jax.nn.gelu(p + r_next)
    o_ref[0] = _nn(hid.astype(BF16), w2_ref[...]).astype(BF16)


def _compress(chunks, pea, peb, w1a, w1b, w2):
    b, n, w = chunks.shape
    full = lambda a: pl.BlockSpec(a.shape, lambda i: (0,) * a.ndim)
    return pl.pallas_call(
        _compress_kernel,
        grid=(b,),
        in_specs=[pl.BlockSpec((1, n, w), lambda i: (i, 0, 0)), full(pea), full(peb), full(w1a), full(w1b),
                  full(w2)],
        out_specs=pl.BlockSpec((1, n, 2 * LANES), lambda i: (i, 0, 0)),
        out_shape=jax.ShapeDtypeStruct((b, n, 2 * LANES), BF16),
        compiler_params=_cparams("parallel"),
        name="nsa_compress",
    )(chunks, pea, peb, w1a, w1b, w2)


def _half_mask(shape, half):
    lane = lax.broadcasted_iota(jnp.int32, shape, len(shape) - 1)
    return (lane // HEAD_DIM) == half


def _group_norm_store(o_ref, rows, gm_ref, o):
    o_ref[0, rows, :] = (_rms(o) * gm_ref[...]).astype(o_ref.dtype)


def _query_rows(i, blk):
    return pl.ds(pl.multiple_of(i * blk, blk), blk)


def _over_query_blocks(step, blk):
    def kernel(*refs):
        def body(i, c):
            step(i, *refs)
            return c
        lax.fori_loop(0, refs[0].shape[1] // blk, body, 0)
    return kernel


def _sb_step(i, q_ref, k_ref, v_ref, gm_ref, o_ref, acc_ref, carry_ref, lb_ref, lk_ref, tot_ref, *, blk):
    q = q_ref[0, _query_rows(i, blk), :]
    row = lax.broadcasted_iota(jnp.int32, (blk, blk), 0)
    col = lax.broadcasted_iota(jnp.int32, (blk, blk), 1)
    past = col < row
    u = jnp.where(row > col, 1.0, 0.0).astype(BF16)
    qm = []
    for h in range(N_HEADS):
        slab = q[:, (h // 2) * LANES:(h // 2 + 1) * LANES]
        qm.append(jnp.where(_half_mask(slab.shape, h % 2), slab, jnp.zeros_like(slab)))
    qm = [jnp.concatenate(qm[0:2], axis=0), jnp.concatenate(qm[2:4], axis=0)]
    acc_ref[...] = jnp.zeros_like(acc_ref)
    carry_ref[...] = jnp.zeros_like(carry_ref)

    def per_head(x, fn):
        return jnp.concatenate([fn(x[k * blk:(k + 1) * blk]) for k in range(x.shape[0] // blk)], axis=0)

    def a_logits(j):
        rows = pl.ds(pl.multiple_of(j * blk, blk), blk)
        return [_nt(qm[pr], k_ref[0, rows, pr * LANES:(pr + 1) * LANES]) for pr in range(2)]

    def a_finish(zs, slot, diag):
        for pr, z in enumerate(zs):
            r2 = slice(2 * pr * blk, 2 * (pr + 1) * blk)
            sp = jnp.log2(1.0 + jnp.exp2(-jnp.abs(z)))
            lb = jnp.minimum(z, 0.0) - sp
            lk = lb - z
            if diag:
                lk = per_head(lk, lambda t: jnp.where(past, t, 0.0))
                lb = per_head(lb, lambda t: jnp.where(past, t, NEG_INF))
            lb_ref[slot, r2, :] = lb
            lk_ref[slot, r2, :] = lk.astype(BF16)
            tot_ref[slot, r2, :] = jnp.broadcast_to(jnp.sum(lk, axis=-1, keepdims=True), (2 * blk, LANES))

    def b_suffix(slot):
        return _nn(lk_ref[slot], u)

    def b_finish(cs, j, slot):
        rows = pl.ds(pl.multiple_of(j * blk, blk), blk)
        carry = carry_ref[...]
        a = jnp.exp2(lb_ref[slot] + cs + jnp.concatenate([carry] * (blk // LANES), axis=1)).astype(BF16)
        for pr in range(2):
            r2 = slice(2 * pr * blk, 2 * (pr + 1) * blk)
            acc_ref[r2, :] += _nn(a[r2], v_ref[0, rows, pr * LANES:(pr + 1) * LANES])
        carry_ref[...] = carry + tot_ref[slot]

    a_finish(a_logits(i), 0, True)

    def step(t, slot):
        zs = a_logits(i - 1 - t)
        cs = b_suffix(slot)
        a_finish(zs, 1 - slot, False)
        b_finish(cs, i - t, slot)

    def body(tt, c):
        step(2 * tt, 0)
        step(2 * tt + 1, 1)
        return c

    lax.fori_loop(0, i // 2, body, 0)

    @pl.when(i % 2 == 1)
    def _():
        step(i - 1, 0)
        b_finish(b_suffix(1), 0, 1)

    @pl.when(i % 2 == 0)
    def _():
        b_finish(b_suffix(0), 0, 0)

    lo_half = _half_mask((blk, LANES), 0)
    acc = [acc_ref[h * blk:(h + 1) * blk, :] for h in range(N_HEADS)]
    o = jnp.concatenate([jnp.where(lo_half, acc[0], acc[1]), jnp.where(lo_half, acc[2], acc[3])], axis=1)
    _group_norm_store(o_ref, _query_rows(i, blk), gm_ref, o)


def _stick_breaking(proj, gm, blk=256):
    b, s, _ = proj.shape
    stage = lambda w, dt: pltpu.VMEM((2, N_HEADS * blk, w), dt)
    return pl.pallas_call(
        _over_query_blocks(functools.partial(_sb_step, blk=blk), blk),
        grid=(b,),
        in_specs=[pl.BlockSpec((1, s, 256), lambda bi: (bi, 0, COL_QA)),
                  pl.BlockSpec((1, s, 256), lambda bi: (bi, 0, COL_KA)),
                  pl.BlockSpec((1, s, 256), lambda bi: (bi, 0, COL_VA)),
                  pl.BlockSpec((1, 256), lambda bi: (0, 0))],
        out_specs=pl.BlockSpec((1, s, 256), lambda bi: (bi, 0, 0)),
        out_shape=jax.ShapeDtypeStruct((b, s, 256), BF16),
        scratch_shapes=[pltpu.VMEM((N_HEADS * blk, LANES), F32), pltpu.VMEM((N_HEADS * blk, LANES), F32),
                        stage(blk, F32), stage(blk, BF16), stage(LANES, F32)],
        compiler_params=_cparams("parallel"),
        name="stick_breaking",
    )(proj, proj, proj, gm)


N_MAPS = 2 * N_HEADS
POS_LOCAL, POS_BLOCK, POS_ONE = 0, 3, 6
SEL_LANE = 8
MASK_BIAS = -2.0 ** 100


def _bf16_pieces(x, n):
    out = []
    r = np.float32(x)
    for _ in range(n):
        p = np.float32(r.astype(ml_dtypes.bfloat16))
        out.append(float(p))
        r = np.float32(r - p)
    return out


def _position_tables(slopes, s, blk, sel_blocks=False):
    coef = np.zeros((N_HEADS, LANES), np.float32)
    for h in range(N_HEADS):
        pieces = _bf16_pieces(slopes[h] * LOG2E, 3)
        coef[h, POS_LOCAL:POS_LOCAL + 3] = pieces
        coef[h, POS_BLOCK:POS_BLOCK + 3] = [blk * p for p in pieces]
    pos = np.zeros((s, LANES), np.float32)
    idx = np.arange(s)
    pos[:, POS_LOCAL:POS_LOCAL + 3] = (idx % blk)[:, None]
    pos[:, POS_BLOCK:POS_BLOCK + 3] = (idx // blk)[:, None]
    pos[:, POS_ONE:POS_ONE + 2] = 1.0
    if sel_blocks:
        pos[idx, SEL_LANE + idx // SEL_LEN] = 1.0
    return jnp.asarray(coef), jnp.asarray(pos, BF16)


def _diff_step(i, q_ref, k_ref, v_ref, pos_ref, coef_ref, dl_ref, gm_ref, o_ref, qa_ref, mx_ref, acc_ref,
               *, blk, lam_init):
    q = q_ref[0, _query_rows(i, blk), :]
    row = lax.broadcasted_iota(jnp.int32, (blk, blk), 0)
    col = lax.broadcasted_iota(jnp.int32, (blk, blk), 1)
    causal = col <= row
    lane = lax.broadcasted_iota(jnp.int32, (blk, LANES), 1)
    for h in range(N_HEADS):
        slab = q[:, (h // 2) * LANES:(h // 2 + 1) * LANES]
        coef = jnp.broadcast_to(coef_ref[h:h + 1, :], (blk, LANES)).astype(BF16)
        for c in range(2):
            r = slice(((2 * h + c) % 4) * blk, ((2 * h + c) % 4 + 1) * blk)
            qa_ref[h // 2, r, 0:LANES] = jnp.where((lane // DIFF_QK_DIM) == (h % 2) * 2 + c, slab,
                                                   jnp.zeros_like(slab))
            qa_ref[h // 2, r, LANES:2 * LANES] = coef
    ones = jnp.ones((blk, LANES), BF16)

    def per_map(x, fn):
        return jnp.concatenate([fn(x[k * blk:(k + 1) * blk]) for k in range(4)], axis=0)

    def sweep(blocks, first):
        chains = [(pr, pl.ds(pl.multiple_of(j * blk, blk), blk), diag) for pr in range(2) for j, diag in blocks]
        zs = []
        for pr, rows, diag in chains:
            ka = jnp.concatenate([k_ref[0, rows, pr * LANES:(pr + 1) * LANES], pos_ref[rows, :]], axis=1)
            z = _nt(qa_ref[pr], ka)
            if diag:
                z = per_map(z, lambda t: jnp.where(causal, t, NEG_INF))
            zs.append(z)
        m_new, alpha = [None, None], [None, None]
        for pr in range(2):
            zm = None
            for (cp, _, _), z in zip(chains, zs):
                if cp == pr:
                    for g in range(blk // LANES):
                        part = z[:, g * LANES:(g + 1) * LANES]
                        zm = part if zm is None else jnp.maximum(zm, part)
            cm = jnp.broadcast_to(jnp.max(zm, axis=-1, keepdims=True), (4 * blk, LANES))
            if first:
                m_new[pr] = cm
            else:
                m_old = mx_ref[pr]
                m_new[pr] = jnp.maximum(m_old, cm)
                alpha[pr] = jnp.exp2(m_old - m_new[pr])
            mx_ref[pr] = m_new[pr]
        ps = [jnp.exp2(z - jnp.concatenate([m_new[pr]] * (blk // LANES), axis=1)).astype(BF16)
              for (pr, _, _), z in zip(chains, zs)]
        outs = []
        for (pr, rows, diag), p in zip(chains, ps):
            vo = jnp.concatenate([v_ref[0, rows, pr * LANES:(pr + 1) * LANES], ones], axis=1)
            outs.append(_nn(p, vo))
        for pr in range(2):
            tot = functools.reduce(jnp.add, [o for (cp, _, _), o in zip(chains, outs) if cp == pr])
            acc_ref[pr] = tot if first else acc_ref[pr] * jnp.concatenate([alpha[pr]] * 2, axis=1) + tot

    @pl.when(i % 2 == 1)
    def _():
        sweep([(i - 1, False), (i, True)], True)

    @pl.when(i % 2 == 0)
    def _():
        sweep([(i, True)], True)

    def body(tt, c):
        sweep([(2 * tt, False), (2 * tt + 1, False)], False)
        return c
    lax.fori_loop(0, i // 2, body, 0)

    dl = dl_ref[...]
    s1 = jnp.sum(dl[0:1] * dl[1:2], axis=-1, keepdims=True)
    s2 = jnp.sum(dl[2:3] * dl[3:4], axis=-1, keepdims=True)
    lam = jnp.exp(s1) - jnp.exp(s2) + lam_init
    outs = []
    for h in range(N_HEADS):
        k1 = (2 * h) % 4
        a1 = acc_ref[h // 2, k1 * blk:(k1 + 1) * blk, :]
        a2 = acc_ref[h // 2, (k1 + 1) * blk:(k1 + 2) * blk, :]
        d = a1[:, :LANES] / a1[:, LANES:] - lam * (a2[:, :LANES] / a2[:, LANES:])
        mine = _half_mask(d.shape, h % 2)
        ms = jnp.sum(jnp.where(mine, d * d, 0.0), axis=-1, keepdims=True) * (1.0 / HEAD_DIM)
        outs.append(d * lax.rsqrt(ms + EPS) * (1.0 - lam_init))
    lo_half = _half_mask((blk, LANES), 0)
    o = jnp.concatenate([jnp.where(lo_half, outs[0], outs[1]), jnp.where(lo_half, outs[2], outs[3])], axis=1)
    o_ref[0, _query_rows(i, blk), :] = (o * gm_ref[...]).astype(o_ref.dtype)


def _differential(proj, dl, gm, lam_init, blk=256):
    b, s, _ = proj.shape
    coef, pos = _position_tables(SLOPES_DIFF, s, blk)
    return pl.pallas_call(
        _over_query_blocks(functools.partial(_diff_step, blk=blk, lam_init=lam_init), blk),
        grid=(b,),
        in_specs=[pl.BlockSpec((1, s, 256), lambda bi: (bi, 0, COL_QC)),
                  pl.BlockSpec((1, s, 256), lambda bi: (bi, 0, COL_KC)),
                  pl.BlockSpec((1, s, 256), lambda bi: (bi, 0, COL_VC)),
                  pl.BlockSpec((s, LANES), lambda bi: (0, 0)),
                  pl.BlockSpec((N_HEADS, LANES), lambda bi: (0, 0)),
                  pl.BlockSpec((4, DIFF_QK_DIM), lambda bi: (0, 0)),
                  pl.BlockSpec((1, 256), lambda bi: (0, 0))],
        out_specs=pl.BlockSpec((1, s, 256), lambda bi: (bi, 0, 0)),
        out_shape=jax.ShapeDtypeStruct((b, s, 256), BF16),
        scratch_shapes=[pltpu.VMEM((2, 4 * blk, 2 * LANES), BF16), pltpu.VMEM((2, 4 * blk, LANES), F32),
                        pltpu.VMEM((2, 4 * blk, 2 * LANES), F32)],
        compiler_params=_cparams("parallel"),
        name="differential",
    )(proj, proj, proj, pos, coef, dl, gm)


def _swa_step(i, q_ref, k_ref, v_ref, sink_ref, gm_ref, o_ref, *, blk, sub):
    row = lax.broadcasted_iota(jnp.int32, (blk, 2 * blk), 0)
    col = lax.broadcasted_iota(jnp.int32, (blk, 2 * blk), 1)
    lo_half = _half_mask((blk, LANES), 0)
    blocks = []
    for s_ in range(sub):
        g = i * sub + s_
        first = jnp.maximum(g - 1, 0)
        start = pl.multiple_of(first * blk, blk)
        dist = (row - col) + (g - first) * blk
        mask = (dist >= 0) & (dist < SWA_WINDOW)
        distf = dist.astype(F32)
        q = q_ref[0, _query_rows(g, blk), :]
        outs = []
        for h in range(N_HEADS):
            lanes = slice((h // 2) * LANES, (h // 2 + 1) * LANES)
            slab = q[:, lanes]
            qm = jnp.where(_half_mask(slab.shape, h % 2), slab, jnp.zeros_like(slab))
            kb = k_ref[0, pl.ds(start, 2 * blk), lanes]
            vb = v_ref[0, pl.ds(start, 2 * blk), lanes]
            z = _nt(qm, kb) * (HEAD_DIM ** -0.5) - SLOPES_SWA[h] * distf
            z = jnp.where(mask, z, NEG_INF)
            sink = sink_ref[h]
            m = jnp.maximum(jnp.max(z, axis=-1, keepdims=True), sink)
            p = jnp.exp(z - m)
            l = jnp.sum(p, axis=-1, keepdims=True) + jnp.exp(sink - m)
            outs.append(_nn(p.astype(BF16), vb) / l)
        blocks.append(jnp.concatenate([jnp.where(lo_half, outs[0], outs[1]), jnp.where(lo_half, outs[2], outs[3])],
                                      axis=1))
    _group_norm_store(o_ref, _query_rows(i, sub * blk), gm_ref, jnp.concatenate(blocks, axis=0))


def _sliding_window(proj, sinks, gm, blk=128, sub=4):
    b, s, _ = proj.shape
    rows = blk * sub
    return pl.pallas_call(
        _over_query_blocks(functools.partial(_swa_step, blk=blk, sub=sub), rows),
        grid=(b,),
        in_specs=[pl.BlockSpec((1, s, 256), lambda bi: (bi, 0, COL_QD)),
                  pl.BlockSpec((1, s, 256), lambda bi: (bi, 0, COL_KD)),
                  pl.BlockSpec((1, s, 256), lambda bi: (bi, 0, COL_VD)),
                  pl.BlockSpec(memory_space=pltpu.SMEM),
                  pl.BlockSpec((1, 256), lambda bi: (0, 0))],
        out_specs=pl.BlockSpec((1, s, 256), lambda bi: (bi, 0, 0)),
        out_shape=jax.ShapeDtypeStruct((b, s, 256), BF16),
        compiler_params=_cparams("parallel"),
        name="sliding_window",
    )(proj, proj, proj, sinks, gm)


def _nsa_step(i, q_ref, g_ref, cmp_ref, ks_ref, vs_ref, kw_ref, vw_ref, pos_ref, coef_ref, gm_ref, o_ref,
              qa_ref, mx_ref, acc_ref, *, blk, n_cmp, n_sel):
    q = q_ref[0, _query_rows(i, blk), :]
    lane = lax.broadcasted_iota(jnp.int32, (blk, LANES), 1)
    coefs = [jnp.broadcast_to(coef_ref[h:h + 1, :], (blk, LANES)) for h in range(N_HEADS)]
    for h in range(N_HEADS):
        slab = q[:, (h // 2) * LANES:(h // 2 + 1) * LANES]
        own = jnp.where(_half_mask(slab.shape, h % 2), slab, jnp.zeros_like(slab))
        qa_ref[0, h * blk:(h + 1) * blk, 0:LANES] = own
        qa_ref[1, h * blk:(h + 1) * blk, 0:LANES] = own
        qa_ref[0, h * blk:(h + 1) * blk, LANES:2 * LANES] = coefs[h].astype(BF16)
    qs = qa_ref[1, :, 0:LANES]

    tq = i * blk + lax.broadcasted_iota(jnp.int32, (blk, LANES), 0)

    row = lax.broadcasted_iota(jnp.int32, (blk, blk), 0)
    col = lax.broadcasted_iota(jnp.int32, (blk, blk), 1)
    causal = col <= row
    in_window = col > row
    ones = jnp.ones((blk, LANES), BF16)

    k_refs, v_refs = (kw_ref, ks_ref), (vw_ref, vs_ref)

    def process(blocks, first=False):
        groups = [(br, half) for br in sorted({b[0] for b in blocks}) for half in range(2)]
        chains = [(br, pl.ds(pl.multiple_of(j * blk, blk), blk), mask, (br, half))
                  for br, j, mask in blocks for half in range(2)]
        grows = lambda g: slice(2 * g[1] * blk, 2 * (g[1] + 1) * blk)
        zs = []
        for br, rows, mask, g in chains:
            ka = jnp.concatenate([k_refs[br][0, rows, :], pos_ref[rows, :]], axis=1)
            z = _nt(qa_ref[br, grows(g), :], ka)
            if mask is not None:
                z = jnp.concatenate([jnp.where(mask, z[r * blk:(r + 1) * blk], NEG_INF) for r in range(2)], axis=0)
            zs.append(z)
        m_new, alpha = {}, {}
        for g in groups:
            zm = None
            for (_, _, _, cg), z in zip(chains, zs):
                if cg == g:
                    for c in range(blk // LANES):
                        part = z[:, c * LANES:(c + 1) * LANES]
                        zm = part if zm is None else jnp.maximum(zm, part)
            cm = jnp.broadcast_to(jnp.max(zm, axis=-1, keepdims=True), (2 * blk, LANES))
            if first:
                m_new[g] = cm
            else:
                m_old = mx_ref[g[0], grows(g), :]
                m_new[g] = jnp.maximum(m_old, cm)
                alpha[g] = jnp.exp2(m_old - m_new[g])
            mx_ref[g[0], grows(g), :] = m_new[g]
        ps = [jnp.exp2(z - jnp.concatenate([m_new[g]] * (blk // LANES), axis=1)).astype(BF16)
              for (_, _, _, g), z in zip(chains, zs)]
        outs = []
        for (br, rows, mask, g), p in zip(chains, ps):
            vo = jnp.concatenate([v_refs[br][0, rows, :], ones], axis=1)
            outs.append(_nn(p, vo))
        for g in groups:
            tot = functools.reduce(jnp.add, [o for (_, _, _, cg), o in zip(chains, outs) if cg == g])
            if first:
                acc_ref[g[0], grows(g), :] = tot
            else:
                acc_ref[g[0], grows(g), :] = acc_ref[g[0], grows(g), :] * jnp.concatenate([alpha[g]] * 2, axis=1) + tot

    window = [(0, jnp.maximum(i - 2, 0), in_window & (i >= 2)),
              (0, jnp.maximum(i - 1, 0), jnp.broadcast_to(i >= 1, (blk, blk))),
              (0, i, causal)]

    process(window, first=True)
    o_win = acc_ref[0, :, 0:LANES] / acc_ref[0, :, LANES:2 * LANES]

    kc = cmp_ref[0, :, 0:LANES]
    vc = cmp_ref[0, :, LANES:2 * LANES]
    dist_c = tq - (CMP_STRIDE * lane + CMP_LEN - 1)
    ok_c = (dist_c >= 0) & (lane < n_cmp)
    dist_cf = dist_c.astype(F32)
    bias_c = jnp.concatenate([jnp.where(ok_c, (-SLOPES_NSA[h] * LOG2E) * dist_cf, NEG_INF)
                              for h in range(N_HEADS)], axis=0)
    z = _nt(qs, kc) + bias_c
    p = jnp.exp2(z - jnp.max(z, axis=-1, keepdims=True))
    p = p / jnp.sum(p, axis=-1, keepdims=True)
    p = jnp.where(bias_c > 0.5 * NEG_INF, p, 0.0).astype(BF16)
    o_cmp = _nn(p, vc)

    sel_rows = -(-(SEL_LANE + n_sel) // 8) * 8
    rj = lax.broadcasted_iota(jnp.int32, (sel_rows, LANES), 0)
    ri = lax.broadcasted_iota(jnp.int32, (sel_rows, LANES), 1)
    to_sel = jnp.where((ri * CMP_STRIDE) // SEL_LEN + SEL_LANE == rj, 1.0, 0.0).astype(BF16)
    imp = _nt(to_sel, p[0:blk])
    for h in range(1, N_HEADS):
        imp = imp + _nt(to_sel, p[h * blk:(h + 1) * blk])
    srow = lax.broadcasted_iota(jnp.int32, (sel_rows, blk), 0)
    blk_id = srow - SEL_LANE
    cur = (i * blk + lax.broadcasted_iota(jnp.int32, (sel_rows, blk), 1)) // SEL_LEN
    forced = (blk_id == 0) | (blk_id == cur) | (blk_id == cur - 1)
    score = jnp.where(blk_id > cur, NEG_INF, jnp.where(forced, FORCE_SCORE, imp))
    score = jnp.where((blk_id >= 0) & (blk_id < n_sel), score, -3.0e38)
    sel = jnp.zeros((sel_rows, blk), F32)
    srowf = srow.astype(F32)
    for _ in range(min(SEL_TOPN, n_sel)):
        best = jnp.max(score, axis=0, keepdims=True)
        idx = jnp.min(jnp.where(score == best, srowf, float(LANES)), axis=0, keepdims=True)
        hit = srowf == idx
        sel = jnp.where(hit, 1.0, sel)
        score = jnp.where(hit, -3.4e38, score)
    sel_bias = jnp.where(sel > 0.5, 0.0, MASK_BIAS)
    sel_bias = jnp.concatenate([sel_bias, jnp.zeros((LANES - sel_rows, blk), F32)], axis=0).T
    in_sel = (lane >= SEL_LANE) & (lane < SEL_LANE + n_sel)

    for h in range(N_HEADS):
        qa_ref[1, h * blk:(h + 1) * blk, LANES:2 * LANES] = jnp.where(in_sel, sel_bias, coefs[h]).astype(BF16)

    @pl.when(i % 2 == 1)
    def _():
        process([(1, i - 1, None), (1, i, causal)], first=True)

    @pl.when(i % 2 == 0)
    def _():
        process([(1, i, causal)], first=True)

    def body(tt, c):
        process([(1, 2 * tt, None), (1, 2 * tt + 1, None)])
        return c
    lax.fori_loop(0, i // 2, body, 0)
    o_sel = acc_ref[1, :, 0:LANES] / acc_ref[1, :, LANES:2 * LANES]

    g = jax.nn.sigmoid(g_ref[0, _query_rows(i, blk), :].astype(F32))
    outs = []
    for h in range(N_HEADS):
        r = slice(h * blk, (h + 1) * blk)
        outs.append(g[:, 3 * h:3 * h + 1] * o_cmp[r] + g[:, 3 * h + 1:3 * h + 2] * o_sel[r]
                    + g[:, 3 * h + 2:3 * h + 3] * o_win[r])
    lo_half = _half_mask((blk, LANES), 0)
    o = jnp.concatenate([jnp.where(lo_half, outs[0], outs[1]), jnp.where(lo_half, outs[2], outs[3])], axis=1)
    _group_norm_store(o_ref, _query_rows(i, blk), gm_ref, o)


def _native_sparse(proj, cmp_kv, gm, blk=256):
    b, s, _ = proj.shape
    n_cmp = (s - CMP_LEN) // CMP_STRIDE + 1
    n_sel = s // SEL_LEN
    assert cmp_kv.shape[1] == LANES and n_cmp <= LANES and SEL_LANE + n_sel <= LANES
    assert NSA_WINDOW == 2 * blk
    coef, pos = _position_tables(SLOPES_NSA, s, blk, sel_blocks=True)
    slab = lambda c: pl.BlockSpec((1, s, LANES), lambda bi: (bi, 0, c))
    rows = N_HEADS * blk
    return pl.pallas_call(
        _over_query_blocks(functools.partial(_nsa_step, blk=blk, n_cmp=n_cmp, n_sel=n_sel), blk),
        grid=(b,),
        in_specs=[pl.BlockSpec((1, s, 256), lambda bi: (bi, 0, COL_QN)),
                  slab(COL_GN),
                  pl.BlockSpec((1, LANES, 2 * LANES), lambda bi: (bi, 0, 0)),
                  slab(COL_KS), slab(COL_VS), slab(COL_KW), slab(COL_VW),
                  pl.BlockSpec((s, LANES), lambda bi: (0, 0)),
                  pl.BlockSpec((N_HEADS, LANES), lambda bi: (0, 0)),
                  pl.BlockSpec((1, 256), lambda bi: (0, 0))],
        out_specs=pl.BlockSpec((1, s, 256), lambda bi: (bi, 0, 0)),
        out_shape=jax.ShapeDtypeStruct((b, s, 256), BF16),
        scratch_shapes=[pltpu.VMEM((2, rows, 2 * LANES), BF16), pltpu.VMEM((2, rows, LANES), F32),
                        pltpu.VMEM((2, rows, 2 * LANES), F32)],
        compiler_params=_cparams("parallel"),
        name="native_sparse",
    )(proj, proj, cmp_kv, proj, proj, proj, proj, pos, coef, gm)


def _mix_mlp_kernel(x_ref, a_ref, b_ref, c_ref, d_ref, wo_ref, g_ref, wu_ref, wd_ref, gf_ref, o_ref, h_ref,
                    *, final_norm):
    f = pl.program_id(1)

    @pl.when(f == 0)
    def _():
        mix = jnp.concatenate([a_ref[...], b_ref[...], c_ref[...], d_ref[...]], axis=1)
        x1 = x_ref[...] + _nn(mix, wo_ref[...])
        o_ref[...] = x1
        h_ref[...] = (_rms(x1) * g_ref[...]).astype(BF16)

    u = jnp.maximum(_nn(h_ref[...], wu_ref[...]), 0.0)
    o_ref[...] += _nn((u * u).astype(BF16), wd_ref[...])

    if final_norm:
        @pl.when(f == pl.num_programs(1) - 1)
        def _():
            o_ref[...] = _rms(o_ref[...]) * gf_ref[...]


def _mix_mlp(x2, mixes, wo, gain, wu, wd, gfinal, final_norm, tm=1024, tf=1024):
    m = x2.shape[0]
    grp = pl.BlockSpec((tm, GROUP_WIDTH), lambda i, f: (i, 0))
    row = pl.BlockSpec((1, D_MODEL), lambda i, f: (0, 0))
    return pl.pallas_call(
        functools.partial(_mix_mlp_kernel, final_norm=final_norm),
        grid=(m // tm, D_FF // tf),
        in_specs=[pl.BlockSpec((tm, D_MODEL), lambda i, f: (i, 0)), grp, grp, grp, grp,
                  pl.BlockSpec((D_MODEL, D_MODEL), lambda i, f: (0, 0)), row,
                  pl.BlockSpec((D_MODEL, tf), lambda i, f: (0, f)),
                  pl.BlockSpec((tf, D_MODEL), lambda i, f: (f, 0)), row],
        out_specs=pl.BlockSpec((tm, D_MODEL), lambda i, f: (i, 0)),
        out_shape=jax.ShapeDtypeStruct((m, D_MODEL), F32),
        scratch_shapes=[pltpu.VMEM((tm, D_MODEL), BF16)],
        compiler_params=_cparams("parallel", "arbitrary"),
        name="mix_mlp",
    )(x2, *mixes, wo, gain, wu, wd, gfinal)


def _compress_weights(pe_k, w1_k, w2_k, pe_v, w1_v, w2_v):
    half = CMP_STRIDE
    d = HEAD_DIM

    def halves(w1):
        w = w1.reshape(CMP_LEN, d, CMP_HIDDEN)
        return w[:half], w[half:]

    ka, kb = halves(w1_k)
    va, vb = halves(w1_v)
    zero = jnp.zeros_like(ka)

    def merge(wk, wv):
        top = jnp.concatenate([wk, zero], axis=-1)
        bot = jnp.concatenate([zero, wv], axis=-1)
        return jnp.concatenate([top, bot], axis=1).reshape(half * 2 * d, 2 * CMP_HIDDEN).astype(BF16)

    w1a, w1b = merge(ka, va), merge(kb, vb)
    pe = jnp.concatenate([pe_k, pe_v], axis=-1)
    pea = pe[:half].reshape(1, half * 2 * d)
    peb = pe[half:].reshape(1, half * 2 * d)
    zk = jnp.zeros_like(w2_k)
    w2 = jnp.concatenate([jnp.concatenate([w2_k, w2_k, zk, zk], axis=1),
                          jnp.concatenate([zk, zk, w2_v, w2_v], axis=1)], axis=0).astype(BF16)
    return pea, peb, w1a, w1b, w2


def kernel(x, norm_attn, w_in, cmp_pe_k, cmp_w1_k, cmp_w2_k, cmp_pe_v, cmp_w1_v, cmp_w2_v, diff_lq1, diff_lk1,
           diff_lq2, diff_lk2, sinks, g_mix, w_out, norm_mlp, w_up, w_down, norm_final):
    b, s, d = x.shape
    m = b * s
    depth = w_in.shape[0]
    cols = jnp.asarray(_COLS)
    w_in_p = jnp.take(jnp.concatenate([w_in, jnp.zeros((depth, d, 1), w_in.dtype)], axis=2), cols,
                      axis=2).astype(BF16)
    w_out_b = w_out.astype(BF16)
    w_up_b = w_up.astype(BF16)
    w_down_b = w_down.astype(BF16)
    gfinal = norm_final.reshape(1, d)

    colscale = jnp.asarray(_column_scales())

    x2 = x.reshape(m, d)
    for l in range(depth):
        proj = _inproj(x2, norm_attn[l].reshape(1, d), w_in_p[l], colscale).reshape(b, s, NP)
        gm = g_mix[l].reshape(N_HEADS, 1, GROUP_WIDTH)

        chunks = proj[:, :, COL_KVC * LANES:(COL_KVC + 1) * LANES].reshape(b, s // CMP_STRIDE, CMP_STRIDE * LANES)
        cmp_kv = _compress(chunks, *_compress_weights(cmp_pe_k[l], cmp_w1_k[l], cmp_w2_k[l],
                                                      cmp_pe_v[l], cmp_w1_v[l], cmp_w2_v[l]))

        o_sb = _stick_breaking(proj, gm[0])
        o_nsa = _native_sparse(proj, cmp_kv, gm[1])
        lam_init = 0.8 - 0.6 * math.exp(-0.3 * l)
        dl = jnp.stack([diff_lq1[l], diff_lk1[l], diff_lq2[l], diff_lk2[l]]).astype(F32)
        o_diff = _differential(proj, dl, gm[2], lam_init)
        o_swa = _sliding_window(proj, sinks[l].astype(F32), gm[3])

        mixes = [o.reshape(m, GROUP_WIDTH) for o in (o_sb, o_nsa, o_diff, o_swa)]
        x2 = _mix_mlp(x2, mixes, w_out_b[l], norm_mlp[l].reshape(1, d), w_up_b[l], w_down_b[l], gfinal,
                      final_norm=(l == depth - 1))
    return x2.reshape(b, s, d)
```

```python
import functools
import math

import ml_dtypes
import numpy as np
import jax
import jax.numpy as jnp
from jax import lax
from jax.experimental import pallas as pl
from jax.experimental.pallas import tpu as pltpu

F32 = jnp.float32
BF16 = jnp.bfloat16

D_MODEL = 1024
DEPTH = 4
HEAD_DIM = 64
GROUP_WIDTH = 256
N_HEADS = 4
D_FF = 4 * D_MODEL
EPS = 1e-6
NEG_INF = -1e30
LOG2E = math.log2(math.e)
LANES = 128

CMP_LEN = 32
CMP_STRIDE = 16
CMP_HIDDEN = 128
SEL_LEN = 64
SEL_TOPN = 8
FORCE_SCORE = 1e9
NSA_WINDOW = 512
SWA_WINDOW = 128
DIFF_QK_DIM = 32

VMEM_LIMIT = 48 * 1024 * 1024

_ORIG = dict(qa=(0, 256), ka=(256, 256), va=(512, 256), qn=(768, 256), kcn=(1024, 64), vcn=(1088, 64),
             ksn=(1152, 64), vsn=(1216, 64), kwn=(1280, 64), vwn=(1344, 64), gn=(1408, 12),
             qc=(1420, 256), kc=(1676, 256), vc=(1932, 256), qd=(2188, 256), kd=(2444, 128), vd=(2572, 128))
IN_COLS = 2700


def _layout():
    cols = []

    def put(name, lo=0, n=None):
        start, size = _ORIG[name]
        n = size - lo if n is None else n
        cols.extend(range(start + lo, start + lo + n))

    def zeros(n):
        cols.extend([IN_COLS] * n)

    for nm in ("qa", "ka", "va", "qn", "kcn", "vcn"):
        put(nm)
    for nm in ("ksn", "vsn", "kwn", "vwn"):
        put(nm)
        put(nm)
    put("gn")
    zeros(LANES - 12)
    for nm in ("qc", "kc", "vc", "qd"):
        put(nm)
    for nm in ("kd", "vd"):
        put(nm, 0, 64)
        put(nm, 0, 64)
        put(nm, 64, 64)
        put(nm, 64, 64)
    return np.asarray(cols, np.int32)


_COLS = _layout()
NP = int(_COLS.shape[0])


def _column_scales():
    cs = np.ones((1, NP), np.float32)
    cs[0, 0:256] = HEAD_DIM ** -0.5 * LOG2E
    cs[0, 768:1024] = HEAD_DIM ** -0.5 * LOG2E
    cs[0, 1792:2048] = DIFF_QK_DIM ** -0.5 * LOG2E
    return cs


COL_QA, COL_KA, COL_VA = 0, 1, 2
COL_QN = 3
COL_KVC = 8
COL_KS, COL_VS, COL_KW, COL_VW, COL_GN = 9, 10, 11, 12, 13
COL_QC, COL_KC, COL_VC = 7, 8, 9
COL_QD, COL_KD, COL_VD = 10, 11, 12


def _alibi_slopes():
    m = 2.0 ** (-8.0 * np.arange(1, 13) / 12.0)
    m = m.astype(np.float32).reshape(4, 3)
    return [float(v) for v in m[:, 0]], [float(v) for v in m[:, 1]], [float(v) for v in m[:, 2]]


SLOPES_NSA, SLOPES_DIFF, SLOPES_SWA = _alibi_slopes()


def _nt(a, b):
    return lax.dot_general(a, b, (((1,), (1,)), ((), ())), preferred_element_type=F32)


def _nn(a, b):
    return jnp.dot(a, b, preferred_element_type=F32)


def _cparams(*sem):
    return pltpu.CompilerParams(dimension_semantics=sem, vmem_limit_bytes=VMEM_LIMIT)


def _rms(x):
    return x * lax.rsqrt(jnp.mean(x * x, axis=-1, keepdims=True) + EPS)


def _inproj_kernel(x_ref, g_ref, w_ref, cs_ref, o_ref):
    h = _rms(x_ref[...]) * g_ref[...]
    o_ref[...] = (_nn(h.astype(BF16), w_ref[...]) * cs_ref[...]).astype(BF16)


def _inproj(x2, gain, w, colscale, tm=512):
    m = x2.shape[0]
    return pl.pallas_call(
        _inproj_kernel,
        grid=(m // tm,),
        in_specs=[pl.BlockSpec((tm, D_MODEL), lambda i: (i, 0)),
                  pl.BlockSpec((1, D_MODEL), lambda i: (0, 0)),
                  pl.BlockSpec((D_MODEL, NP), lambda i: (0, 0)),
                  pl.BlockSpec((1, NP), lambda i: (0, 0))],
        out_specs=pl.BlockSpec((tm, NP), lambda i: (i, 0)),
        out_shape=jax.ShapeDtypeStruct((m, NP), BF16),
        compiler_params=_cparams("parallel"),
        name="inproj",
    )(x2, gain, w, colscale)


def _compress_kernel(c_ref, pea_ref, peb_ref, w1a_ref, w1b_ref, w2_ref, o_ref):
    c = c_ref[0].astype(F32)
    p = _nn((c + pea_ref[...]).astype(BF16), w1a_ref[...])
    r = _nn((c + peb_ref[...]).astype(BF16), w1b_ref[...])
    n = c.shape[0]
    r_next = pltpu.roll(r, n - 1, 0)
    hid = jax.nn.gelu(p + r_next)
    o_ref[0] = _nn(hid.astype(BF16), w2_ref[...]).astype(BF16)


def _compress(chunks, pea, peb, w1a, w1b, w2):
    b, n, w = chunks.shape
    full = lambda a: pl.BlockSpec(a.shape, lambda i: (0,) * a.ndim)
    return pl.pallas_call(
        _compress_kernel,
        grid=(b,),
        in_specs=[pl.BlockSpec((1, n, w), lambda i: (i, 0, 0)), full(pea), full(peb), full(w1a), full(w1b),
                  full(w2)],
        out_specs=pl.BlockSpec((1, n, 2 * LANES), lambda i: (i, 0, 0)),
        out_shape=jax.ShapeDtypeStruct((b, n, 2 * LANES), BF16),
        compiler_params=_cparams("parallel"),
        name="nsa_compress",
    )(chunks, pea, peb, w1a, w1b, w2)


def _half_mask(shape, half):
    lane = lax.broadcasted_iota(jnp.int32, shape, len(shape) - 1)
    return (lane // HEAD_DIM) == half


def _group_norm_store(o_ref, rows, gm_ref, o):
    o_ref[0, rows, :] = (_rms(o) * gm_ref[...]).astype(o_ref.dtype)


def _query_rows(i, blk):
    return pl.ds(pl.multiple_of(i * blk, blk), blk)


def _over_query_blocks(step, blk):
    def kernel(*refs):
        def body(i, c):
            step(i, *refs)
            return c
        lax.fori_loop(0, refs[0].shape[1] // blk, body, 0)
    return kernel


def _sb_step(i, q_ref, k_ref, v_ref, gm_ref, o_ref, acc_ref, carry_ref, lb_ref, lk_ref, tot_ref, *, blk):
    q = q_ref[0, _query_rows(i, blk), :]
    row = lax.broadcasted_iota(jnp.int32, (blk, blk), 0)
    col = lax.broadcasted_iota(jnp.int32, (blk, blk), 1)
    past = col < row
    u = jnp.where(row > col, 1.0, 0.0).astype(BF16)
    qm = []
    for h in range(N_HEADS):
        slab = q[:, (h // 2) * LANES:(h // 2 + 1) * LANES]
        qm.append(jnp.where(_half_mask(slab.shape, h % 2), slab, jnp.zeros_like(slab)))
    qm = [jnp.concatenate(qm[0:2], axis=0), jnp.concatenate(qm[2:4], axis=0)]
    acc_ref[...] = jnp.zeros_like(acc_ref)
    carry_ref[...] = jnp.zeros_like(carry_ref)

    def per_head(x, fn):
        return jnp.concatenate([fn(x[k * blk:(k + 1) * blk]) for k in range(x.shape[0] // blk)], axis=0)

    def a_logits(j):
        rows = pl.ds(pl.multiple_of(j * blk, blk), blk)
        return [_nt(qm[pr], k_ref[0, rows, pr * LANES:(pr + 1) * LANES]) for pr in range(2)]

    def a_finish(zs, slot, diag):
        for pr, z in enumerate(zs):
            r2 = slice(2 * pr * blk, 2 * (pr + 1) * blk)
            sp = jnp.log2(1.0 + jnp.exp2(-jnp.abs(z)))
            lb = jnp.minimum(z, 0.0) - sp
            lk = lb - z
            if diag:
                lk = per_head(lk, lambda t: jnp.where(past, t, 0.0))
                lb = per_head(lb, lambda t: jnp.where(past, t, NEG_INF))
            lb_ref[slot, r2, :] = lb
            lk_ref[slot, r2, :] = lk.astype(BF16)
            tot_ref[slot, r2, :] = jnp.broadcast_to(jnp.sum(lk, axis=-1, keepdims=True), (2 * blk, LANES))

    def b_suffix(slot):
        return _nn(lk_ref[slot], u)

    def b_finish(cs, j, slot):
        rows = pl.ds(pl.multiple_of(j * blk, blk), blk)
        carry = carry_ref[...]
        a = jnp.exp2(lb_ref[slot] + cs + jnp.concatenate([carry] * (blk // LANES), axis=1)).astype(BF16)
        for pr in range(2):
            r2 = slice(2 * pr * blk, 2 * (pr + 1) * blk)
            acc_ref[r2, :] += _nn(a[r2], v_ref[0, rows, pr * LANES:(pr + 1) * LANES])
        carry_ref[...] = carry + tot_ref[slot]

    a_finish(a_logits(i), 0, True)

    def step(t, slot):
        zs = a_logits(i - 1 - t)
        cs = b_suffix(slot)
        a_finish(zs, 1 - slot, False)
        b_finish(cs, i - t, slot)

    def body(tt, c):
        step(2 * tt, 0)
        step(2 * tt + 1, 1)
        return c

    lax.fori_loop(0, i // 2, body, 0)

    @pl.when(i % 2 == 1)
    def _():
        step(i - 1, 0)
        b_finish(b_suffix(1), 0, 1)

    @pl.when(i % 2 == 0)
    def _():
        b_finish(b_suffix(0), 0, 0)

    lo_half = _half_mask((blk, LANES), 0)
    acc = [acc_ref[h * blk:(h + 1) * blk, :] for h in range(N_HEADS)]
    o = jnp.concatenate([jnp.where(lo_half, acc[0], acc[1]), jnp.where(lo_half, acc[2], acc[3])], axis=1)
    _group_norm_store(o_ref, _query_rows(i, blk), gm_ref, o)


def _stick_breaking(proj, gm, blk=256):
    b, s, _ = proj.shape
    stage = lambda w, dt: pltpu.VMEM((2, N_HEADS * blk, w), dt)
    return pl.pallas_call(
        _over_query_blocks(functools.partial(_sb_step, blk=blk), blk),
        grid=(b,),
        in_specs=[pl.BlockSpec((1, s, 256), lambda bi: (bi, 0, COL_QA)),
                  pl.BlockSpec((1, s, 256), lambda bi: (bi, 0, COL_KA)),
                  pl.BlockSpec((1, s, 256), lambda bi: (bi, 0, COL_VA)),
                  pl.BlockSpec((1, 256), lambda bi: (0, 0))],
        out_specs=pl.BlockSpec((1, s, 256), lambda bi: (bi, 0, 0)),
        out_shape=jax.ShapeDtypeStruct((b, s, 256), BF16),
        scratch_shapes=[pltpu.VMEM((N_HEADS * blk, LANES), F32), pltpu.VMEM((N_HEADS * blk, LANES), F32),
                        stage(blk, F32), stage(blk, BF16), stage(LANES, F32)],
        compiler_params=_cparams("parallel"),
        name="stick_breaking",
    )(proj, proj, proj, gm)


N_MAPS = 2 * N_HEADS
POS_LOCAL, POS_BLOCK, POS_ONE = 0, 3, 6
SEL_LANE = 8
MASK_BIAS = -2.0 ** 100


def _bf16_pieces(x, n):
    out = []
    r = np.float32(x)
    for _ in range(n):
        p = np.float32(r.astype(ml_dtypes.bfloat16))
        out.append(float(p))
        r = np.float32(r - p)
    return out


def _position_tables(slopes, s, blk, sel_blocks=False):
    coef = np.zeros((N_HEADS, LANES), np.float32)
    for h in range(N_HEADS):
        pieces = _bf16_pieces(slopes[h] * LOG2E, 3)
        coef[h, POS_LOCAL:POS_LOCAL + 3] = pieces
        coef[h, POS_BLOCK:POS_BLOCK + 3] = [blk * p for p in pieces]
    pos = np.zeros((s, LANES), np.float32)
    idx = np.arange(s)
    pos[:, POS_LOCAL:POS_LOCAL + 3] = (idx % blk)[:, None]
    pos[:, POS_BLOCK:POS_BLOCK + 3] = (idx // blk)[:, None]
    pos[:, POS_ONE:POS_ONE + 2] = 1.0
    if sel_blocks:
        pos[idx, SEL_LANE + idx // SEL_LEN] = 1.0
    return jnp.asarray(coef), jnp.asarray(pos, BF16)


def _diff_step(i, q_ref, k_ref, v_ref, pos_ref, coef_ref, dl_ref, gm_ref, o_ref, qa_ref, mx_ref, acc_ref,
               *, blk, lam_init):
    q = q_ref[0, _query_rows(i, blk), :]
    row = lax.broadcasted_iota(jnp.int32, (blk, blk), 0)
    col = lax.broadcasted_iota(jnp.int32, (blk, blk), 1)
    causal = col <= row
    lane = lax.broadcasted_iota(jnp.int32, (blk, LANES), 1)
    for h in range(N_HEADS):
        slab = q[:, (h // 2) * LANES:(h // 2 + 1) * LANES]
        coef = jnp.broadcast_to(coef_ref[h:h + 1, :], (blk, LANES)).astype(BF16)
        for c in range(2):
            r = slice(((2 * h + c) % 4) * blk, ((2 * h + c) % 4 + 1) * blk)
            qa_ref[h // 2, r, 0:LANES] = jnp.where((lane // DIFF_QK_DIM) == (h % 2) * 2 + c, slab,
                                                   jnp.zeros_like(slab))
            qa_ref[h // 2, r, LANES:2 * LANES] = coef
    ones = jnp.ones((blk, LANES), BF16)

    def per_map(x, fn):
        return jnp.concatenate([fn(x[k * blk:(k + 1) * blk]) for k in range(4)], axis=0)

    def sweep(blocks, first):
        chains = [(pr, pl.ds(pl.multiple_of(j * blk, blk), blk), diag) for pr in range(2) for j, diag in blocks]
        zs = []
        for pr, rows, diag in chains:
            ka = jnp.concatenate([k_ref[0, rows, pr * LANES:(pr + 1) * LANES], pos_ref[rows, :]], axis=1)
            z = _nt(qa_ref[pr], ka)
            if diag:
                z = per_map(z, lambda t: jnp.where(causal, t, NEG_INF))
            zs.append(z)
        m_new, alpha = [None, None], [None, None]
        for pr in range(2):
            zm = None
            for (cp, _, _), z in zip(chains, zs):
                if cp == pr:
                    for g in range(blk // LANES):
                        part = z[:, g * LANES:(g + 1) * LANES]
                        zm = part if zm is None else jnp.maximum(zm, part)
            cm = jnp.broadcast_to(jnp.max(zm, axis=-1, keepdims=True), (4 * blk, LANES))
            if first:
                m_new[pr] = cm
            else:
                m_old = mx_ref[pr]
                m_new[pr] = jnp.maximum(m_old, cm)
                alpha[pr] = jnp.exp2(m_old - m_new[pr])
            mx_ref[pr] = m_new[pr]
        ps = [jnp.exp2(z - jnp.concatenate([m_new[pr]] * (blk // LANES), axis=1)).astype(BF16)
              for (pr, _, _), z in zip(chains, zs)]
        outs = []
        for (pr, rows, diag), p in zip(chains, ps):
            vo = jnp.concatenate([v_ref[0, rows, pr * LANES:(pr + 1) * LANES], ones], axis=1)
            outs.append(_nn(p, vo))
        for pr in range(2):
            tot = functools.reduce(jnp.add, [o for (cp, _, _), o in zip(chains, outs) if cp == pr])
            acc_ref[pr] = tot if first else acc_ref[pr] * jnp.concatenate([alpha[pr]] * 2, axis=1) + tot

    base = (i // 4) * 4

    @pl.when(i % 2 == 1)
    def _():
        sweep([(i - 1, False), (i, True)], True)

    @pl.when(i % 2 == 0)
    def _():
        sweep([(i, True)], True)

    @pl.when(i % 4 >= 2)
    def _():
        sweep([(base, False), (base + 1, False)], False)

    def body(tt, c):
        sweep([(4 * tt + k, False) for k in range(4)], False)
        return c
    lax.fori_loop(0, i // 4, body, 0)

    dl = dl_ref[...]
    s1 = jnp.sum(dl[0:1] * dl[1:2], axis=-1, keepdims=True)
    s2 = jnp.sum(dl[2:3] * dl[3:4], axis=-1, keepdims=True)
    lam = jnp.exp(s1) - jnp.exp(s2) + lam_init
    outs = []
    for h in range(N_HEADS):
        k1 = (2 * h) % 4
        a1 = acc_ref[h // 2, k1 * blk:(k1 + 1) * blk, :]
        a2 = acc_ref[h // 2, (k1 + 1) * blk:(k1 + 2) * blk, :]
        d = a1[:, :LANES] / a1[:, LANES:] - lam * (a2[:, :LANES] / a2[:, LANES:])
        mine = _half_mask(d.shape, h % 2)
        ms = jnp.sum(jnp.where(mine, d * d, 0.0), axis=-1, keepdims=True) * (1.0 / HEAD_DIM)
        outs.append(d * lax.rsqrt(ms + EPS) * (1.0 - lam_init))
    lo_half = _half_mask((blk, LANES), 0)
    o = jnp.concatenate([jnp.where(lo_half, outs[0], outs[1]), jnp.where(lo_half, outs[2], outs[3])], axis=1)
    o_ref[0, _query_rows(i, blk), :] = (o * gm_ref[...]).astype(o_ref.dtype)


def _differential(proj, dl, gm, lam_init, blk=256):
    b, s, _ = proj.shape
    coef, pos = _position_tables(SLOPES_DIFF, s, blk)
    return pl.pallas_call(
        _over_query_blocks(functools.partial(_diff_step, blk=blk, lam_init=lam_init), blk),
        grid=(b,),
        in_specs=[pl.BlockSpec((1, s, 256), lambda bi: (bi, 0, COL_QC)),
                  pl.BlockSpec((1, s, 256), lambda bi: (bi, 0, COL_KC)),
                  pl.BlockSpec((1, s, 256), lambda bi: (bi, 0, COL_VC)),
                  pl.BlockSpec((s, LANES), lambda bi: (0, 0)),
                  pl.BlockSpec((N_HEADS, LANES), lambda bi: (0, 0)),
                  pl.BlockSpec((4, DIFF_QK_DIM), lambda bi: (0, 0)),
                  pl.BlockSpec((1, 256), lambda bi: (0, 0))],
        out_specs=pl.BlockSpec((1, s, 256), lambda bi: (bi, 0, 0)),
        out_shape=jax.ShapeDtypeStruct((b, s, 256), BF16),
        scratch_shapes=[pltpu.VMEM((2, 4 * blk, 2 * LANES), BF16), pltpu.VMEM((2, 4 * blk, LANES), F32),
                        pltpu.VMEM((2, 4 * blk, 2 * LANES), F32)],
        compiler_params=_cparams("parallel"),
        name="differential",
    )(proj, proj, proj, pos, coef, dl, gm)


def _swa_step(i, q_ref, k_ref, v_ref, sink_ref, gm_ref, o_ref, *, blk, sub):
    row = lax.broadcasted_iota(jnp.int32, (blk, 2 * blk), 0)
    col = lax.broadcasted_iota(jnp.int32, (blk, 2 * blk), 1)
    lo_half = _half_mask((blk, LANES), 0)
    blocks = []
    for s_ in range(sub):
        g = i * sub + s_
        first = jnp.maximum(g - 1, 0)
        start = pl.multiple_of(first * blk, blk)
        dist = (row - col) + (g - first) * blk
        mask = (dist >= 0) & (dist < SWA_WINDOW)
        distf = dist.astype(F32)
        q = q_ref[0, _query_rows(g, blk), :]
        outs = []
        for h in range(N_HEADS):
            lanes = slice((h // 2) * LANES, (h // 2 + 1) * LANES)
            slab = q[:, lanes]
            qm = jnp.where(_half_mask(slab.shape, h % 2), slab, jnp.zeros_like(slab))
            kb = k_ref[0, pl.ds(start, 2 * blk), lanes]
            vb = v_ref[0, pl.ds(start, 2 * blk), lanes]
            z = _nt(qm, kb) * (HEAD_DIM ** -0.5) - SLOPES_SWA[h] * distf
            z = jnp.where(mask, z, NEG_INF)
            sink = sink_ref[h]
            m = jnp.maximum(jnp.max(z, axis=-1, keepdims=True), sink)
            p = jnp.exp(z - m)
            l = jnp.sum(p, axis=-1, keepdims=True) + jnp.exp(sink - m)
            outs.append(_nn(p.astype(BF16), vb) / l)
        blocks.append(jnp.concatenate([jnp.where(lo_half, outs[0], outs[1]), jnp.where(lo_half, outs[2], outs[3])],
                                      axis=1))
    _group_norm_store(o_ref, _query_rows(i, sub * blk), gm_ref, jnp.concatenate(blocks, axis=0))


def _sliding_window(proj, sinks, gm, blk=128, sub=4):
    b, s, _ = proj.shape
    rows = blk * sub
    return pl.pallas_call(
        _over_query_blocks(functools.partial(_swa_step, blk=blk, sub=sub), rows),
        grid=(b,),
        in_specs=[pl.BlockSpec((1, s, 256), lambda bi: (bi, 0, COL_QD)),
                  pl.BlockSpec((1, s, 256), lambda bi: (bi, 0, COL_KD)),
                  pl.BlockSpec((1, s, 256), lambda bi: (bi, 0, COL_VD)),
                  pl.BlockSpec(memory_space=pltpu.SMEM),
                  pl.BlockSpec((1, 256), lambda bi: (0, 0))],
        out_specs=pl.BlockSpec((1, s, 256), lambda bi: (bi, 0, 0)),
        out_shape=jax.ShapeDtypeStruct((b, s, 256), BF16),
        compiler_params=_cparams("parallel"),
        name="sliding_window",
    )(proj, proj, proj, sinks, gm)


def _nsa_step(i, q_ref, g_ref, cmp_ref, ks_ref, vs_ref, kw_ref, vw_ref, pos_ref, coef_ref, gm_ref, o_ref,
              qa_ref, mx_ref, acc_ref, *, blk, n_cmp, n_sel):
    q = q_ref[0, _query_rows(i, blk), :]
    lane = lax.broadcasted_iota(jnp.int32, (blk, LANES), 1)
    coefs = [jnp.broadcast_to(coef_ref[h:h + 1, :], (blk, LANES)) for h in range(N_HEADS)]
    for h in range(N_HEADS):
        slab = q[:, (h // 2) * LANES:(h // 2 + 1) * LANES]
        own = jnp.where(_half_mask(slab.shape, h % 2), slab, jnp.zeros_like(slab))
        qa_ref[0, h * blk:(h + 1) * blk, 0:LANES] = own
        qa_ref[1, h * blk:(h + 1) * blk, 0:LANES] = own
        qa_ref[0, h * blk:(h + 1) * blk, LANES:2 * LANES] = coefs[h].astype(BF16)
    qs = qa_ref[1, :, 0:LANES]

    tq = i * blk + lax.broadcasted_iota(jnp.int32, (blk, LANES), 0)

    row = lax.broadcasted_iota(jnp.int32, (blk, blk), 0)
    col = lax.broadcasted_iota(jnp.int32, (blk, blk), 1)
    causal = col <= row
    in_window = col > row
    ones = jnp.ones((blk, LANES), BF16)

    k_refs, v_refs = (kw_ref, ks_ref), (vw_ref, vs_ref)

    def process(blocks, first=False):
        groups = [(br, half) for br in sorted({b[0] for b in blocks}) for half in range(2)]
        chains = [(br, pl.ds(pl.multiple_of(j * blk, blk), blk), mask, (br, half))
                  for br, j, mask in blocks for half in range(2)]
        grows = lambda g: slice(2 * g[1] * blk, 2 * (g[1] + 1) * blk)
        zs = []
        for br, rows, mask, g in chains:
            ka = jnp.concatenate([k_refs[br][0, rows, :], pos_ref[rows, :]], axis=1)
            z = _nt(qa_ref[br, grows(g), :], ka)
            if mask is not None:
                z = jnp.concatenate([jnp.where(mask, z[r * blk:(r + 1) * blk], NEG_INF) for r in range(2)], axis=0)
            zs.append(z)
        m_new, alpha = {}, {}
        for g in groups:
            zm = None
            for (_, _, _, cg), z in zip(chains, zs):
                if cg == g:
                    for c in range(blk // LANES):
                        part = z[:, c * LANES:(c + 1) * LANES]
                        zm = part if zm is None else jnp.maximum(zm, part)
            cm = jnp.broadcast_to(jnp.max(zm, axis=-1, keepdims=True), (2 * blk, LANES))
            if first:
                m_new[g] = cm
            else:
                m_old = mx_ref[g[0], grows(g), :]
                m_new[g] = jnp.maximum(m_old, cm)
                alpha[g] = jnp.exp2(m_old - m_new[g])
            mx_ref[g[0], grows(g), :] = m_new[g]
        ps = [jnp.exp2(z - jnp.concatenate([m_new[g]] * (blk // LANES), axis=1)).astype(BF16)
              for (_, _, _, g), z in zip(chains, zs)]
        outs = []
        for (br, rows, mask, g), p in zip(chains, ps):
            vo = jnp.concatenate([v_refs[br][0, rows, :], ones], axis=1)
            outs.append(_nn(p, vo))
        for g in groups:
            tot = functools.reduce(jnp.add, [o for (_, _, _, cg), o in zip(chains, outs) if cg == g])
            if first:
                acc_ref[g[0], grows(g), :] = tot
            else:
                acc_ref[g[0], grows(g), :] = acc_ref[g[0], grows(g), :] * jnp.concatenate([alpha[g]] * 2, axis=1) + tot

    window = [(0, jnp.maximum(i - 2, 0), in_window & (i >= 2)),
              (0, jnp.maximum(i - 1, 0), jnp.broadcast_to(i >= 1, (blk, blk))),
              (0, i, causal)]

    process(window, first=True)
    o_win = acc_ref[0, :, 0:LANES] / acc_ref[0, :, LANES:2 * LANES]

    kc = cmp_ref[0, :, 0:LANES]
    vc = cmp_ref[0, :, LANES:2 * LANES]
    dist_c = tq - (CMP_STRIDE * lane + CMP_LEN - 1)
    ok_c = (dist_c >= 0) & (lane < n_cmp)
    dist_cf = dist_c.astype(F32)
    bias_c = jnp.concatenate([jnp.where(ok_c, (-SLOPES_NSA[h] * LOG2E) * dist_cf, NEG_INF)
                              for h in range(N_HEADS)], axis=0)
    z = _nt(qs, kc) + bias_c
    p = jnp.exp2(z - jnp.max(z, axis=-1, keepdims=True))
    p = p / jnp.sum(p, axis=-1, keepdims=True)
    p = jnp.where(bias_c > 0.5 * NEG_INF, p, 0.0).astype(BF16)
    o_cmp = _nn(p, vc)

    sel_rows = -(-(SEL_LANE + n_sel) // 8) * 8
    rj = lax.broadcasted_iota(jnp.int32, (sel_rows, LANES), 0)
    ri = lax.broadcasted_iota(jnp.int32, (sel_rows, LANES), 1)
    to_sel = jnp.where((ri * CMP_STRIDE) // SEL_LEN + SEL_LANE == rj, 1.0, 0.0).astype(BF16)
    imp = _nt(to_sel, p[0:blk])
    for h in range(1, N_HEADS):
        imp = imp + _nt(to_sel, p[h * blk:(h + 1) * blk])
    srow = lax.broadcasted_iota(jnp.int32, (sel_rows, blk), 0)
    blk_id = srow - SEL_LANE
    cur = (i * blk + lax.broadcasted_iota(jnp.int32, (sel_rows, blk), 1)) // SEL_LEN
    forced = (blk_id == 0) | (blk_id == cur) | (blk_id == cur - 1)
    score = jnp.where(blk_id > cur, NEG_INF, jnp.where(forced, FORCE_SCORE, imp))
    score = jnp.where((blk_id >= 0) & (blk_id < n_sel), score, -3.0e38)
    sel = jnp.zeros((sel_rows, blk), F32)
    srowf = srow.astype(F32)
    for _ in range(min(SEL_TOPN, n_sel)):
        best = jnp.max(score, axis=0, keepdims=True)
        idx = jnp.min(jnp.where(score == best, srowf, float(LANES)), axis=0, keepdims=True)
        hit = srowf == idx
        sel = jnp.where(hit, 1.0, sel)
        score = jnp.where(hit, -3.4e38, score)
    sel_bias = jnp.where(sel > 0.5, 0.0, MASK_BIAS)
    sel_bias = jnp.concatenate([sel_bias, jnp.zeros((LANES - sel_rows, blk), F32)], axis=0).T
    in_sel = (lane >= SEL_LANE) & (lane < SEL_LANE + n_sel)

    for h in range(N_HEADS):
        qa_ref[1, h * blk:(h + 1) * blk, LANES:2 * LANES] = jnp.where(in_sel, sel_bias, coefs[h]).astype(BF16)
    base = (i // 4) * 4

    @pl.when(i % 2 == 1)
    def _():
        process([(1, i - 1, None), (1, i, causal)], first=True)

    @pl.when(i % 2 == 0)
    def _():
        process([(1, i, causal)], first=True)

    @pl.when(i % 4 >= 2)
    def _():
        process([(1, base, None), (1, base + 1, None)])

    def body(tt, c):
        process([(1, 4 * tt + k, None) for k in range(4)])
        return c
    lax.fori_loop(0, i // 4, body, 0)
    o_sel = acc_ref[1, :, 0:LANES] / acc_ref[1, :, LANES:2 * LANES]

    g = jax.nn.sigmoid(g_ref[0, _query_rows(i, blk), :].astype(F32))
    outs = []
    for h in range(N_HEADS):
        r = slice(h * blk, (h + 1) * blk)
        outs.append(g[:, 3 * h:3 * h + 1] * o_cmp[r] + g[:, 3 * h + 1:3 * h + 2] * o_sel[r]
                    + g[:, 3 * h + 2:3 * h + 3] * o_win[r])
    lo_half = _half_mask((blk, LANES), 0)
    o = jnp.concatenate([jnp.where(lo_half, outs[0], outs[1]), jnp.where(lo_half, outs[2], outs[3])], axis=1)
    _group_norm_store(o_ref, _query_rows(i, blk), gm_ref, o)


def _native_sparse(proj, cmp_kv, gm, blk=256):
    b, s, _ = proj.shape
    n_cmp = (s - CMP_LEN) // CMP_STRIDE + 1
    n_sel = s // SEL_LEN
    assert cmp_kv.shape[1] == LANES and n_cmp <= LANES and SEL_LANE + n_sel <= LANES
    assert NSA_WINDOW == 2 * blk
    coef, pos = _position_tables(SLOPES_NSA, s, blk, sel_blocks=True)
    slab = lambda c: pl.BlockSpec((1, s, LANES), lambda bi: (bi, 0, c))
    rows = N_HEADS * blk
    return pl.pallas_call(
        _over_query_blocks(functools.partial(_nsa_step, blk=blk, n_cmp=n_cmp, n_sel=n_sel), blk),
        grid=(b,),
        in_specs=[pl.BlockSpec((1, s, 256), lambda bi: (bi, 0, COL_QN)),
                  slab(COL_GN),
                  pl.BlockSpec((1, LANES, 2 * LANES), lambda bi: (bi, 0, 0)),
                  slab(COL_KS), slab(COL_VS), slab(COL_KW), slab(COL_VW),
                  pl.BlockSpec((s, LANES), lambda bi: (0, 0)),
                  pl.BlockSpec((N_HEADS, LANES), lambda bi: (0, 0)),
                  pl.BlockSpec((1, 256), lambda bi: (0, 0))],
        out_specs=pl.BlockSpec((1, s, 256), lambda bi: (bi, 0, 0)),
        out_shape=jax.ShapeDtypeStruct((b, s, 256), BF16),
        scratch_shapes=[pltpu.VMEM((2, rows, 2 * LANES), BF16), pltpu.VMEM((2, rows, LANES), F32),
                        pltpu.VMEM((2, rows, 2 * LANES), F32)],
        compiler_params=_cparams("parallel"),
        name="native_sparse",
    )(proj, proj, cmp_kv, proj, proj, proj, proj, pos, coef, gm)


def _mix_mlp_kernel(x_ref, a_ref, b_ref, c_ref, d_ref, wo_ref, g_ref, wu_ref, wd_ref, gf_ref, o_ref, h_ref,
                    *, final_norm):
    f = pl.program_id(1)

    @pl.when(f == 0)
    def _():
        mix = jnp.concatenate([a_ref[...], b_ref[...], c_ref[...], d_ref[...]], axis=1)
        x1 = x_ref[...] + _nn(mix, wo_ref[...])
        o_ref[...] = x1
        h_ref[...] = (_rms(x1) * g_ref[...]).astype(BF16)

    u = jnp.maximum(_nn(h_ref[...], wu_ref[...]), 0.0)
    o_ref[...] += _nn((u * u).astype(BF16), wd_ref[...])

    if final_norm:
        @pl.when(f == pl.num_programs(1) - 1)
        def _():
            o_ref[...] = _rms(o_ref[...]) * gf_ref[...]


def _mix_mlp(x2, mixes, wo, gain, wu, wd, gfinal, final_norm, tm=1024, tf=1024):
    m = x2.shape[0]
    grp = pl.BlockSpec((tm, GROUP_WIDTH), lambda i, f: (i, 0))
    row = pl.BlockSpec((1, D_MODEL), lambda i, f: (0, 0))
    return pl.pallas_call(
        functools.partial(_mix_mlp_kernel, final_norm=final_norm),
        grid=(m // tm, D_FF // tf),
        in_specs=[pl.BlockSpec((tm, D_MODEL), lambda i, f: (i, 0)), grp, grp, grp, grp,
                  pl.BlockSpec((D_MODEL, D_MODEL), lambda i, f: (0, 0)), row,
                  pl.BlockSpec((D_MODEL, tf), lambda i, f: (0, f)),
                  pl.BlockSpec((tf, D_MODEL), lambda i, f: (f, 0)), row],
        out_specs=pl.BlockSpec((tm, D_MODEL), lambda i, f: (i, 0)),
        out_shape=jax.ShapeDtypeStruct((m, D_MODEL), F32),
        scratch_shapes=[pltpu.VMEM((tm, D_MODEL), BF16)],
        compiler_params=_cparams("parallel", "arbitrary"),
        name="mix_mlp",
    )(x2, *mixes, wo, gain, wu, wd, gfinal)


def _compress_weights(pe_k, w1_k, w2_k, pe_v, w1_v, w2_v):
    half = CMP_STRIDE
    d = HEAD_DIM

    def halves(w1):
        w = w1.reshape(CMP_LEN, d, CMP_HIDDEN)
        return w[:half], w[half:]

    ka, kb = halves(w1_k)
    va, vb = halves(w1_v)
    zero = jnp.zeros_like(ka)

    def merge(wk, wv):
        top = jnp.concatenate([wk, zero], axis=-1)
        bot = jnp.concatenate([zero, wv], axis=-1)
        return jnp.concatenate([top, bot], axis=1).reshape(half * 2 * d, 2 * CMP_HIDDEN).astype(BF16)

    w1a, w1b = merge(ka, va), merge(kb, vb)
    pe = jnp.concatenate([pe_k, pe_v], axis=-1)
    pea = pe[:half].reshape(1, half * 2 * d)
    peb = pe[half:].reshape(1, half * 2 * d)
    zk = jnp.zeros_like(w2_k)
    w2 = jnp.concatenate([jnp.concatenate([w2_k, w2_k, zk, zk], axis=1),
                          jnp.concatenate([zk, zk, w2_v, w2_v], axis=1)], axis=0).astype(BF16)
    return pea, peb, w1a, w1b, w2


def kernel(x, norm_attn, w_in, cmp_pe_k, cmp_w1_k, cmp_w2_k, cmp_pe_v, cmp_w1_v, cmp_w2_v, diff_lq1, diff_lk1,
           diff_lq2, diff_lk2, sinks, g_mix, w_out, norm_mlp, w_up, w_down, norm_final):
    b, s, d = x.shape
    m = b * s
    depth = w_in.shape[0]
    cols = jnp.asarray(_COLS)
    w_in_p = jnp.take(jnp.concatenate([w_in, jnp.zeros((depth, d, 1), w_in.dtype)], axis=2), cols,
                      axis=2).astype(BF16)
    w_out_b = w_out.astype(BF16)
    w_up_b = w_up.astype(BF16)
    w_down_b = w_down.astype(BF16)
    gfinal = norm_final.reshape(1, d)

    colscale = jnp.asarray(_column_scales())

    x2 = x.reshape(m, d)
    for l in range(depth):
        proj = _inproj(x2, norm_attn[l].reshape(1, d), w_in_p[l], colscale).reshape(b, s, NP)
        gm = g_mix[l].reshape(N_HEADS, 1, GROUP_WIDTH)

        chunks = proj[:, :, COL_KVC * LANES:(COL_KVC + 1) * LANES].reshape(b, s // CMP_STRIDE, CMP_STRIDE * LANES)
        cmp_kv = _compress(chunks, *_compress_weights(cmp_pe_k[l], cmp_w1_k[l], cmp_w2_k[l],
                                                      cmp_pe_v[l], cmp_w1_v[l], cmp_w2_v[l]))

        o_sb = _stick_breaking(proj, gm[0])
        o_nsa = _native_sparse(proj, cmp_kv, gm[1])
        lam_init = 0.8 - 0.6 * math.exp(-0.3 * l)
        dl = jnp.stack([diff_lq1[l], diff_lk1[l], diff_lq2[l], diff_lk2[l]]).astype(F32)
        o_diff = _differential(proj, dl, gm[2], lam_init)
        o_swa = _sliding_window(proj, sinks[l].astype(F32), gm[3])

        mixes = [o.reshape(m, GROUP_WIDTH) for o in (o_sb, o_nsa, o_diff, o_swa)]
        x2 = _mix_mlp(x2, mixes, w_out_b[l], norm_mlp[l].reshape(1, d), w_up_b[l], w_down_b[l], gfinal,
                      final_norm=(l == depth - 1))
    return x2.reshape(b, s, d)
```

```python
import functools
import math

import ml_dtypes
import numpy as np
import jax
import jax.numpy as jnp
from jax import lax
from jax.experimental import pallas as pl
from jax.experimental.pallas import tpu as pltpu

F32 = jnp.float32
BF16 = jnp.bfloat16

D_MODEL = 1024
DEPTH = 4
HEAD_DIM = 64
GROUP_WIDTH = 256
N_HEADS = 4
D_FF = 4 * D_MODEL
EPS = 1e-6
NEG_INF = -1e30
LOG2E = math.log2(math.e)
LANES = 128

CMP_LEN = 32
CMP_STRIDE = 16
CMP_HIDDEN = 128
SEL_LEN = 64
SEL_TOPN = 8
FORCE_SCORE = 1e9
NSA_WINDOW = 512
SWA_WINDOW = 128
DIFF_QK_DIM = 32

VMEM_LIMIT = 48 * 1024 * 1024

_ORIG = dict(qa=(0, 256), ka=(256, 256), va=(512, 256), qn=(768, 256), kcn=(1024, 64), vcn=(1088, 64),
             ksn=(1152, 64), vsn=(1216, 64), kwn=(1280, 64), vwn=(1344, 64), gn=(1408, 12),
             qc=(1420, 256), kc=(1676, 256), vc=(1932, 256), qd=(2188, 256), kd=(2444, 128), vd=(2572, 128))
IN_COLS = 2700


def _layout():
    cols = []

    def put(name, lo=0, n=None):
        start, size = _ORIG[name]
        n = size - lo if n is None else n
        cols.extend(range(start + lo, start + lo + n))

    def zeros(n):
        cols.extend([IN_COLS] * n)

    for nm in ("qa", "ka", "va", "qn", "kcn", "vcn"):
        put(nm)
    for nm in ("ksn", "vsn", "kwn", "vwn"):
        put(nm)
        put(nm)
    put("gn")
    zeros(LANES - 12)
    for nm in ("qc", "kc", "vc"):
        put(nm)
    for h in SWA_HEAD_ORDER:
        put("qd", h * HEAD_DIM, HEAD_DIM)
    put("kd")
    put("vd")
    return np.asarray(cols, np.int32)


SWA_HEAD_ORDER = (0, 2, 1, 3)
_COLS = _layout()
NP = int(_COLS.shape[0])


def _column_scales():
    cs = np.ones((1, NP), np.float32)
    cs[0, 0:256] = HEAD_DIM ** -0.5 * LOG2E
    cs[0, 768:1024] = HEAD_DIM ** -0.5 * LOG2E
    cs[0, 1792:2048] = DIFF_QK_DIM ** -0.5 * LOG2E
    cs[0, 2560:2816] = HEAD_DIM ** -0.5 * LOG2E
    return cs


COL_QA, COL_KA, COL_VA = 0, 1, 2
COL_QN = 3
COL_KVC = 8
COL_KS, COL_VS, COL_KW, COL_VW, COL_GN = 9, 10, 11, 12, 13
COL_QC, COL_KC, COL_VC = 7, 8, 9
COL_QD = 10
COL_KD, COL_VD = 22, 23


def _alibi_slopes():
    m = 2.0 ** (-8.0 * np.arange(1, 13) / 12.0)
    m = m.astype(np.float32).reshape(4, 3)
    return [float(v) for v in m[:, 0]], [float(v) for v in m[:, 1]], [float(v) for v in m[:, 2]]


SLOPES_NSA, SLOPES_DIFF, SLOPES_SWA = _alibi_slopes()


def _nt(a, b):
    return lax.dot_general(a, b, (((1,), (1,)), ((), ())), preferred_element_type=F32)


def _nn(a, b):
    return jnp.dot(a, b, preferred_element_type=F32)


def _cparams(*sem):
    return pltpu.CompilerParams(dimension_semantics=sem, vmem_limit_bytes=VMEM_LIMIT)


def _rms(x):
    return x * lax.rsqrt(jnp.mean(x * x, axis=-1, keepdims=True) + EPS)


def _inproj_kernel(x_ref, g_ref, w_ref, cs_ref, o_ref):
    h = _rms(x_ref[...]) * g_ref[...]
    o_ref[...] = (_nn(h.astype(BF16), w_ref[...]) * cs_ref[...]).astype(BF16)


def _inproj(x2, gain, w, colscale, tm=512):
    m = x2.shape[0]
    return pl.pallas_call(
        _inproj_kernel,
        grid=(m // tm,),
        in_specs=[pl.BlockSpec((tm, D_MODEL), lambda i: (i, 0)),
                  pl.BlockSpec((1, D_MODEL), lambda i: (0, 0)),
                  pl.BlockSpec((D_MODEL, NP), lambda i: (0, 0)),
                  pl.BlockSpec((1, NP), lambda i: (0, 0))],
        out_specs=pl.BlockSpec((tm, NP), lambda i: (i, 0)),
        out_shape=jax.ShapeDtypeStruct((m, NP), BF16),
        compiler_params=_cparams("parallel"),
        name="inproj",
    )(x2, gain, w, colscale)


def _compress_kernel(c_ref, pea_ref, peb_ref, w1a_ref, w1b_ref, w2_ref, o_ref):
    c = c_ref[0].astype(F32)
    p = _nn((c + pea_ref[...]).astype(BF16), w1a_ref[...])
    r = _nn((c + peb_ref[...]).astype(BF16), w1b_ref[...])
    n = c.shape[0]
    r_next = pltpu.roll(r, n - 1, 0)
    hid = jax.nn.gelu(p + r_next)
    o_ref[0] = _nn(hid.astype(BF16), w2_ref[...]).astype(BF16)


def _compress(chunks, pea, peb, w1a, w1b, w2):
    b, n, w = chunks.shape
    full = lambda a: pl.BlockSpec(a.shape, lambda i: (0,) * a.ndim)
    return pl.pallas_call(
        _compress_kernel,
        grid=(b,),
        in_specs=[pl.BlockSpec((1, n, w), lambda i: (i, 0, 0)), full(pea), full(peb), full(w1a), full(w1b),
                  full(w2)],
        out_specs=pl.BlockSpec((1, n, 2 * LANES), lambda i: (i, 0, 0)),
        out_shape=jax.ShapeDtypeStruct((b, n, 2 * LANES), BF16),
        compiler_params=_cparams("parallel"),
        name="nsa_compress",
    )(chunks, pea, peb, w1a, w1b, w2)


def _half_mask(shape, half):
    lane = lax.broadcasted_iota(jnp.int32, shape, len(shape) - 1)
    return (lane // HEAD_DIM) == half


def _group_norm_store(o_ref, rows, gm_ref, o):
    o_ref[0, rows, :] = (_rms(o) * gm_ref[...]).astype(o_ref.dtype)


def _query_rows(i, blk):
    return pl.ds(pl.multiple_of(i * blk, blk), blk)


def _over_query_blocks(step, blk):
    def kernel(*refs):
        def body(i, c):
            step(i, *refs)
            return c
        lax.fori_loop(0, refs[0].shape[1] // blk, body, 0)
    return kernel


def _sb_step(i, q_ref, k_ref, v_ref, gm_ref, o_ref, acc_ref, carry_ref, lb_ref, lk_ref, tot_ref, *, blk):
    q = q_ref[0, _query_rows(i, blk), :]
    row = lax.broadcasted_iota(jnp.int32, (blk, blk), 0)
    col = lax.broadcasted_iota(jnp.int32, (blk, blk), 1)
    past = col < row
    u = jnp.where(row > col, 1.0, 0.0).astype(BF16)
    qm = []
    for h in range(N_HEADS):
        slab = q[:, (h // 2) * LANES:(h // 2 + 1) * LANES]
        qm.append(jnp.where(_half_mask(slab.shape, h % 2), slab, jnp.zeros_like(slab)))
    qm = [jnp.concatenate(qm[0:2], axis=0), jnp.concatenate(qm[2:4], axis=0)]
    acc_ref[...] = jnp.zeros_like(acc_ref)
    carry_ref[...] = jnp.zeros_like(carry_ref)

    def per_head(x, fn):
        return jnp.concatenate([fn(x[k * blk:(k + 1) * blk]) for k in range(x.shape[0] // blk)], axis=0)

    def a_logits(j):
        rows = pl.ds(pl.multiple_of(j * blk, blk), blk)
        return [_nt(qm[pr], k_ref[0, rows, pr * LANES:(pr + 1) * LANES]) for pr in range(2)]

    def a_finish(zs, slot, diag):
        for pr, z in enumerate(zs):
            r2 = slice(2 * pr * blk, 2 * (pr + 1) * blk)
            sp = jnp.log2(1.0 + jnp.exp2(-jnp.abs(z)))
            lb = jnp.minimum(z, 0.0) - sp
            lk = lb - z
            if diag:
                lk = per_head(lk, lambda t: jnp.where(past, t, 0.0))
                lb = per_head(lb, lambda t: jnp.where(past, t, NEG_INF))
            lb_ref[slot, r2, :] = lb
            lk_ref[slot, r2, :] = lk.astype(BF16)
            tot_ref[slot, r2, :] = jnp.broadcast_to(jnp.sum(lk, axis=-1, keepdims=True), (2 * blk, LANES))

    def b_suffix(slot):
        return _nn(lk_ref[slot], u)

    def b_finish(cs, j, slot):
        rows = pl.ds(pl.multiple_of(j * blk, blk), blk)
        carry = carry_ref[...]
        a = jnp.exp2(lb_ref[slot] + cs + jnp.concatenate([carry] * (blk // LANES), axis=1)).astype(BF16)
        for pr in range(2):
            r2 = slice(2 * pr * blk, 2 * (pr + 1) * blk)
            acc_ref[r2, :] += _nn(a[r2], v_ref[0, rows, pr * LANES:(pr + 1) * LANES])
        carry_ref[...] = carry + tot_ref[slot]

    a_finish(a_logits(i), 0, True)

    def step(t, slot):
        zs = a_logits(i - 1 - t)
        cs = b_suffix(slot)
        a_finish(zs, 1 - slot, False)
        b_finish(cs, i - t, slot)

    def body(tt, c):
        step(2 * tt, 0)
        step(2 * tt + 1, 1)
        return c

    lax.fori_loop(0, i // 2, body, 0)

    @pl.when(i % 2 == 1)
    def _():
        step(i - 1, 0)
        b_finish(b_suffix(1), 0, 1)

    @pl.when(i % 2 == 0)
    def _():
        b_finish(b_suffix(0), 0, 0)

    lo_half = _half_mask((blk, LANES), 0)
    acc = [acc_ref[h * blk:(h + 1) * blk, :] for h in range(N_HEADS)]
    o = jnp.concatenate([jnp.where(lo_half, acc[0], acc[1]), jnp.where(lo_half, acc[2], acc[3])], axis=1)
    _group_norm_store(o_ref, _query_rows(i, blk), gm_ref, o)


def _stick_breaking(proj, gm, blk=256):
    b, s, _ = proj.shape
    stage = lambda w, dt: pltpu.VMEM((2, N_HEADS * blk, w), dt)
    return pl.pallas_call(
        _over_query_blocks(functools.partial(_sb_step, blk=blk), blk),
        grid=(b,),
        in_specs=[pl.BlockSpec((1, s, 256), lambda bi: (bi, 0, COL_QA)),
                  pl.BlockSpec((1, s, 256), lambda bi: (bi, 0, COL_KA)),
                  pl.BlockSpec((1, s, 256), lambda bi: (bi, 0, COL_VA)),
                  pl.BlockSpec((1, 256), lambda bi: (0, 0))],
        out_specs=pl.BlockSpec((1, s, 256), lambda bi: (bi, 0, 0)),
        out_shape=jax.ShapeDtypeStruct((b, s, 256), BF16),
        scratch_shapes=[pltpu.VMEM((N_HEADS * blk, LANES), F32), pltpu.VMEM((N_HEADS * blk, LANES), F32),
                        stage(blk, F32), stage(blk, BF16), stage(LANES, F32)],
        compiler_params=_cparams("parallel"),
        name="stick_breaking",
    )(proj, proj, proj, gm)


POS_LOCAL, POS_BLOCK = 0, 3
SEL_LANE = 8
MASK_BIAS = -2.0 ** 100


def _bf16_pieces(x, n):
    out = []
    r = np.float32(x)
    for _ in range(n):
        p = np.float32(r.astype(ml_dtypes.bfloat16))
        out.append(float(p))
        r = np.float32(r - p)
    return out


def _position_tables(slopes, s, blk, sel_blocks=False):
    coef = np.zeros((N_HEADS, LANES), np.float32)
    for h in range(N_HEADS):
        pieces = _bf16_pieces(slopes[h] * LOG2E, 3)
        coef[h, POS_LOCAL:POS_LOCAL + 3] = pieces
        coef[h, POS_BLOCK:POS_BLOCK + 3] = [blk * p for p in pieces]
    pos = np.zeros((s, LANES), np.float32)
    idx = np.arange(s)
    pos[:, POS_LOCAL:POS_LOCAL + 3] = (idx % blk)[:, None]
    pos[:, POS_BLOCK:POS_BLOCK + 3] = (idx // blk)[:, None]
    if sel_blocks:
        pos[idx, SEL_LANE + idx // SEL_LEN] = 1.0
    return jnp.asarray(coef), jnp.asarray(pos, BF16)


def _diff_step(i, q_ref, k_ref, v_ref, pos_ref, coef_ref, dl_ref, gm_ref, o_ref, qa_ref, mx_ref, acc_ref,
               *, blk, lam_init):
    q = q_ref[0, _query_rows(i, blk), :]
    row = lax.broadcasted_iota(jnp.int32, (blk, blk), 0)
    col = lax.broadcasted_iota(jnp.int32, (blk, blk), 1)
    causal = col <= row
    lane = lax.broadcasted_iota(jnp.int32, (blk, LANES), 1)
    for h in range(N_HEADS):
        slab = q[:, (h // 2) * LANES:(h // 2 + 1) * LANES]
        coef = jnp.broadcast_to(coef_ref[h:h + 1, :], (blk, LANES)).astype(BF16)
        for c in range(2):
            r = slice(((2 * h + c) % 4) * blk, ((2 * h + c) % 4 + 1) * blk)
            qa_ref[h // 2, r, 0:LANES] = jnp.where((lane // DIFF_QK_DIM) == (h % 2) * 2 + c, slab,
                                                   jnp.zeros_like(slab))
            qa_ref[h // 2, r, LANES:2 * LANES] = coef
    ones = jnp.ones((blk, LANES), BF16)

    def per_map(x, fn):
        return jnp.concatenate([fn(x[k * blk:(k + 1) * blk]) for k in range(4)], axis=0)

    def sweep(blocks, first):
        chains = [(pr, pl.ds(pl.multiple_of(j * blk, blk), blk), diag) for pr in range(2) for j, diag in blocks]
        zs = []
        for pr, rows, diag in chains:
            ka = jnp.concatenate([k_ref[0, rows, pr * LANES:(pr + 1) * LANES], pos_ref[rows, :]], axis=1)
            z = _nt(qa_ref[pr], ka)
            if diag:
                z = per_map(z, lambda t: jnp.where(causal, t, NEG_INF))
            zs.append(z)
        m_new, alpha = [None, None], [None, None]
        for pr in range(2):
            zm = None
            for (cp, _, _), z in zip(chains, zs):
                if cp == pr:
                    for g in range(blk // LANES):
                        part = z[:, g * LANES:(g + 1) * LANES]
                        zm = part if zm is None else jnp.maximum(zm, part)
            cm = jnp.broadcast_to(jnp.max(zm, axis=-1, keepdims=True), (4 * blk, LANES))
            if first:
                m_new[pr] = cm
            else:
                m_old = mx_ref[pr]
                m_new[pr] = jnp.maximum(m_old, cm)
                alpha[pr] = jnp.exp2(m_old - m_new[pr])
            mx_ref[pr] = m_new[pr]
        ps = [jnp.exp2(z - jnp.concatenate([m_new[pr]] * (blk // LANES), axis=1)).astype(BF16)
              for (pr, _, _), z in zip(chains, zs)]
        outs = []
        for (pr, rows, diag), p in zip(chains, ps):
            vo = jnp.concatenate([v_ref[0, rows, pr * LANES:(pr + 1) * LANES], ones], axis=1)
            outs.append(_nn(p, vo))
        for pr in range(2):
            tot = functools.reduce(jnp.add, [o for (cp, _, _), o in zip(chains, outs) if cp == pr])
            acc_ref[pr] = tot if first else acc_ref[pr] * jnp.concatenate([alpha[pr]] * 2, axis=1) + tot

    base = (i // 4) * 4

    @pl.when(i % 2 == 1)
    def _():
        sweep([(i - 1, False), (i, True)], True)

    @pl.when(i % 2 == 0)
    def _():
        sweep([(i, True)], True)

    @pl.when(i % 4 >= 2)
    def _():
        sweep([(base, False), (base + 1, False)], False)

    def body(tt, c):
        sweep([(4 * tt + k, False) for k in range(4)], False)
        return c
    lax.fori_loop(0, i // 4, body, 0)

    dl = dl_ref[...]
    s1 = jnp.sum(dl[0:1] * dl[1:2], axis=-1, keepdims=True)
    s2 = jnp.sum(dl[2:3] * dl[3:4], axis=-1, keepdims=True)
    lam = jnp.exp(s1) - jnp.exp(s2) + lam_init
    outs = []
    for h in range(N_HEADS):
        k1 = (2 * h) % 4
        a1 = acc_ref[h // 2, k1 * blk:(k1 + 1) * blk, :]
        a2 = acc_ref[h // 2, (k1 + 1) * blk:(k1 + 2) * blk, :]
        d = a1[:, :LANES] / a1[:, LANES:] - lam * (a2[:, :LANES] / a2[:, LANES:])
        mine = _half_mask(d.shape, h % 2)
        ms = jnp.sum(jnp.where(mine, d * d, 0.0), axis=-1, keepdims=True) * (1.0 / HEAD_DIM)
        outs.append(d * lax.rsqrt(ms + EPS) * (1.0 - lam_init))
    lo_half = _half_mask((blk, LANES), 0)
    o = jnp.concatenate([jnp.where(lo_half, outs[0], outs[1]), jnp.where(lo_half, outs[2], outs[3])], axis=1)
    o_ref[0, _query_rows(i, blk), :] = (o * gm_ref[...]).astype(o_ref.dtype)


def _differential(proj, dl, gm, lam_init, blk=256):
    b, s, _ = proj.shape
    coef, pos = _position_tables(SLOPES_DIFF, s, blk)
    return pl.pallas_call(
        _over_query_blocks(functools.partial(_diff_step, blk=blk, lam_init=lam_init), blk),
        grid=(b,),
        in_specs=[pl.BlockSpec((1, s, 256), lambda bi: (bi, 0, COL_QC)),
                  pl.BlockSpec((1, s, 256), lambda bi: (bi, 0, COL_KC)),
                  pl.BlockSpec((1, s, 256), lambda bi: (bi, 0, COL_VC)),
                  pl.BlockSpec((s, LANES), lambda bi: (0, 0)),
                  pl.BlockSpec((N_HEADS, LANES), lambda bi: (0, 0)),
                  pl.BlockSpec((4, DIFF_QK_DIM), lambda bi: (0, 0)),
                  pl.BlockSpec((1, 256), lambda bi: (0, 0))],
        out_specs=pl.BlockSpec((1, s, 256), lambda bi: (bi, 0, 0)),
        out_shape=jax.ShapeDtypeStruct((b, s, 256), BF16),
        scratch_shapes=[pltpu.VMEM((2, 4 * blk, 2 * LANES), BF16), pltpu.VMEM((2, 4 * blk, LANES), F32),
                        pltpu.VMEM((2, 4 * blk, 2 * LANES), F32)],
        compiler_params=_cparams("parallel"),
        name="differential",
    )(proj, proj, proj, pos, coef, dl, gm)


def _swa_step(i, q_ref, k_ref, v_ref, sink_ref, gm_ref, o_ref, *, blk, sub):
    row = lax.broadcasted_iota(jnp.int32, (blk, 2 * blk), 0)
    col = lax.broadcasted_iota(jnp.int32, (blk, 2 * blk), 1)
    lo_half = _half_mask((blk, LANES), 0)
    blocks = []
    for s_ in range(sub):
        g = i * sub + s_
        first = jnp.maximum(g - 1, 0)
        start = pl.multiple_of(first * blk, blk)
        dist = (row - col) + (g - first) * blk
        mask = (dist >= 0) & (dist < SWA_WINDOW)
        distf = dist.astype(F32)
        q = q_ref[0, _query_rows(g, blk), :]
        kb = k_ref[0, pl.ds(start, 2 * blk), :]
        vb = v_ref[0, pl.ds(start, 2 * blk), :]
        outs = []
        for n, h in enumerate(SWA_HEAD_ORDER):
            slab = q[:, (n // 2) * LANES:(n // 2 + 1) * LANES]
            qm = jnp.where(_half_mask(slab.shape, n % 2), slab, jnp.zeros_like(slab))
            z = _nt(qm, kb) - (SLOPES_SWA[h] * LOG2E) * distf
            z = jnp.where(mask, z, NEG_INF)
            sink = sink_ref[h] * LOG2E
            m = jnp.maximum(jnp.max(z, axis=-1, keepdims=True), sink)
            p = jnp.exp2(z - m)
            l = jnp.sum(p, axis=-1, keepdims=True) + jnp.exp2(sink - m)
            outs.append(_nn(p.astype(BF16), vb) / l)
        blocks.append(jnp.concatenate([jnp.where(lo_half, outs[0], outs[1]), jnp.where(lo_half, outs[2], outs[3])],
                                      axis=1))
    _group_norm_store(o_ref, _query_rows(i, sub * blk), gm_ref, jnp.concatenate(blocks, axis=0))


def _sliding_window(proj, sinks, gm, blk=128, sub=4):
    b, s, _ = proj.shape
    rows = blk * sub
    return pl.pallas_call(
        _over_query_blocks(functools.partial(_swa_step, blk=blk, sub=sub), rows),
        grid=(b,),
        in_specs=[pl.BlockSpec((1, s, 256), lambda bi: (bi, 0, COL_QD)),
                  pl.BlockSpec((1, s, LANES), lambda bi: (bi, 0, COL_KD)),
                  pl.BlockSpec((1, s, LANES), lambda bi: (bi, 0, COL_VD)),
                  pl.BlockSpec(memory_space=pltpu.SMEM),
                  pl.BlockSpec((1, 256), lambda bi: (0, 0))],
        out_specs=pl.BlockSpec((1, s, 256), lambda bi: (bi, 0, 0)),
        out_shape=jax.ShapeDtypeStruct((b, s, 256), BF16),
        compiler_params=_cparams("parallel"),
        name="sliding_window",
    )(proj, proj, proj, sinks, gm)


def _nsa_step(i, q_ref, g_ref, cmp_ref, ks_ref, vs_ref, kw_ref, vw_ref, pos_ref, coef_ref, gm_ref, o_ref,
              qa_ref, mx_ref, acc_ref, *, blk, n_cmp, n_sel):
    q = q_ref[0, _query_rows(i, blk), :]
    lane = lax.broadcasted_iota(jnp.int32, (blk, LANES), 1)
    coefs = [jnp.broadcast_to(coef_ref[h:h + 1, :], (blk, LANES)) for h in range(N_HEADS)]
    for h in range(N_HEADS):
        slab = q[:, (h // 2) * LANES:(h // 2 + 1) * LANES]
        own = jnp.where(_half_mask(slab.shape, h % 2), slab, jnp.zeros_like(slab))
        qa_ref[0, h * blk:(h + 1) * blk, 0:LANES] = own
        qa_ref[1, h * blk:(h + 1) * blk, 0:LANES] = own
        qa_ref[0, h * blk:(h + 1) * blk, LANES:2 * LANES] = coefs[h].astype(BF16)
    qs = qa_ref[1, :, 0:LANES]

    tq = i * blk + lax.broadcasted_iota(jnp.int32, (blk, LANES), 0)

    row = lax.broadcasted_iota(jnp.int32, (blk, blk), 0)
    col = lax.broadcasted_iota(jnp.int32, (blk, blk), 1)
    causal = col <= row
    in_window = col > row
    ones = jnp.ones((blk, LANES), BF16)

    k_refs, v_refs = (kw_ref, ks_ref), (vw_ref, vs_ref)

    def process(blocks, first=False):
        groups = [(br, half) for br in sorted({b[0] for b in blocks}) for half in range(2)]
        chains = [(br, pl.ds(pl.multiple_of(j * blk, blk), blk), mask, (br, half))
                  for br, j, mask in blocks for half in range(2)]
        grows = lambda g: slice(2 * g[1] * blk, 2 * (g[1] + 1) * blk)
        zs = []
        for br, rows, mask, g in chains:
            ka = jnp.concatenate([k_refs[br][0, rows, :], pos_ref[rows, :]], axis=1)
            z = _nt(qa_ref[br, grows(g), :], ka)
            if mask is not None:
                z = jnp.concatenate([jnp.where(mask, z[r * blk:(r + 1) * blk], NEG_INF) for r in range(2)], axis=0)
            zs.append(z)
        m_new, alpha = {}, {}
        for g in groups:
            zm = None
            for (_, _, _, cg), z in zip(chains, zs):
                if cg == g:
                    for c in range(blk // LANES):
                        part = z[:, c * LANES:(c + 1) * LANES]
                        zm = part if zm is None else jnp.maximum(zm, part)
            cm = jnp.broadcast_to(jnp.max(zm, axis=-1, keepdims=True), (2 * blk, LANES))
            if first:
                m_new[g] = cm
            else:
                m_old = mx_ref[g[0], grows(g), :]
                m_new[g] = jnp.maximum(m_old, cm)
                alpha[g] = jnp.exp2(m_old - m_new[g])
            mx_ref[g[0], grows(g), :] = m_new[g]
        ps = [jnp.exp2(z - jnp.concatenate([m_new[g]] * (blk // LANES), axis=1)).astype(BF16)
              for (_, _, _, g), z in zip(chains, zs)]
        outs = []
        for (br, rows, mask, g), p in zip(chains, ps):
            vo = jnp.concatenate([v_refs[br][0, rows, :], ones], axis=1)
            outs.append(_nn(p, vo))
        for g in groups:
            tot = functools.reduce(jnp.add, [o for (_, _, _, cg), o in zip(chains, outs) if cg == g])
            if first:
                acc_ref[g[0], grows(g), :] = tot
            else:
                acc_ref[g[0], grows(g), :] = acc_ref[g[0], grows(g), :] * jnp.concatenate([alpha[g]] * 2, axis=1) + tot

    window = [(0, jnp.maximum(i - 2, 0), in_window & (i >= 2)),
              (0, jnp.maximum(i - 1, 0), jnp.broadcast_to(i >= 1, (blk, blk))),
              (0, i, causal)]

    process(window, first=True)
    o_win = acc_ref[0, :, 0:LANES] / acc_ref[0, :, LANES:2 * LANES]

    kc = cmp_ref[0, :, 0:LANES]
    vc = cmp_ref[0, :, LANES:2 * LANES]
    dist_c = tq - (CMP_STRIDE * lane + CMP_LEN - 1)
    ok_c = (dist_c >= 0) & (lane < n_cmp)
    dist_cf = dist_c.astype(F32)
    bias_c = jnp.concatenate([jnp.where(ok_c, (-SLOPES_NSA[h] * LOG2E) * dist_cf, NEG_INF)
                              for h in range(N_HEADS)], axis=0)
    z = _nt(qs, kc) + bias_c
    p = jnp.exp2(z - jnp.max(z, axis=-1, keepdims=True))
    p = p / jnp.sum(p, axis=-1, keepdims=True)
    p = jnp.where(bias_c > 0.5 * NEG_INF, p, 0.0).astype(BF16)
    o_cmp = _nn(p, vc)

    sel_rows = -(-(SEL_LANE + n_sel) // 8) * 8
    rj = lax.broadcasted_iota(jnp.int32, (sel_rows, LANES), 0)
    ri = lax.broadcasted_iota(jnp.int32, (sel_rows, LANES), 1)
    to_sel = jnp.where((ri * CMP_STRIDE) // SEL_LEN + SEL_LANE == rj, 1.0, 0.0).astype(BF16)
    imp = _nt(to_sel, p[0:blk])
    for h in range(1, N_HEADS):
        imp = imp + _nt(to_sel, p[h * blk:(h + 1) * blk])
    srow = lax.broadcasted_iota(jnp.int32, (sel_rows, blk), 0)
    blk_id = srow - SEL_LANE
    cur = (i * blk + lax.broadcasted_iota(jnp.int32, (sel_rows, blk), 1)) // SEL_LEN
    forced = (blk_id == 0) | (blk_id == cur) | (blk_id == cur - 1)
    score = jnp.where(blk_id > cur, NEG_INF, jnp.where(forced, FORCE_SCORE, imp))
    score = jnp.where((blk_id >= 0) & (blk_id < n_sel), score, -3.0e38)
    sel = jnp.zeros((sel_rows, blk), F32)
    srowf = srow.astype(F32)
    for _ in range(min(SEL_TOPN, n_sel)):
        best = jnp.max(score, axis=0, keepdims=True)
        idx = jnp.min(jnp.where(score == best, srowf, float(LANES)), axis=0, keepdims=True)
        hit = srowf == idx
        sel = jnp.where(hit, 1.0, sel)
        score = jnp.where(hit, -3.4e38, score)
    sel_bias = jnp.where(sel > 0.5, 0.0, MASK_BIAS)
    sel_bias = jnp.concatenate([sel_bias, jnp.zeros((LANES - sel_rows, blk), F32)], axis=0).T
    in_sel = (lane >= SEL_LANE) & (lane < SEL_LANE + n_sel)

    for h in range(N_HEADS):
        qa_ref[1, h * blk:(h + 1) * blk, LANES:2 * LANES] = jnp.where(in_sel, sel_bias, coefs[h]).astype(BF16)
    base = (i // 4) * 4

    @pl.when(i % 2 == 1)
    def _():
        process([(1, i - 1, None), (1, i, causal)], first=True)

    @pl.when(i % 2 == 0)
    def _():
        process([(1, i, causal)], first=True)

    @pl.when(i % 4 >= 2)
    def _():
        process([(1, base, None), (1, base + 1, None)])

    def body(tt, c):
        process([(1, 4 * tt + k, None) for k in range(4)])
        return c
    lax.fori_loop(0, i // 4, body, 0)
    o_sel = acc_ref[1, :, 0:LANES] / acc_ref[1, :, LANES:2 * LANES]

    g = jax.nn.sigmoid(g_ref[0, _query_rows(i, blk), :].astype(F32))
    outs = []
    for h in range(N_HEADS):
        r = slice(h * blk, (h + 1) * blk)
        outs.append(g[:, 3 * h:3 * h + 1] * o_cmp[r] + g[:, 3 * h + 1:3 * h + 2] * o_sel[r]
                    + g[:, 3 * h + 2:3 * h + 3] * o_win[r])
    lo_half = _half_mask((blk, LANES), 0)
    o = jnp.concatenate([jnp.where(lo_half, outs[0], outs[1]), jnp.where(lo_half, outs[2], outs[3])], axis=1)
    _group_norm_store(o_ref, _query_rows(i, blk), gm_ref, o)


def _native_sparse(proj, cmp_kv, gm, blk=256):
    b, s, _ = proj.shape
    n_cmp = (s - CMP_LEN) // CMP_STRIDE + 1
    n_sel = s // SEL_LEN
    assert cmp_kv.shape[1] == LANES and n_cmp <= LANES and SEL_LANE + n_sel <= LANES
    assert NSA_WINDOW == 2 * blk
    coef, pos = _position_tables(SLOPES_NSA, s, blk, sel_blocks=True)
    slab = lambda c: pl.BlockSpec((1, s, LANES), lambda bi: (bi, 0, c))
    rows = N_HEADS * blk
    return pl.pallas_call(
        _over_query_blocks(functools.partial(_nsa_step, blk=blk, n_cmp=n_cmp, n_sel=n_sel), blk),
        grid=(b,),
        in_specs=[pl.BlockSpec((1, s, 256), lambda bi: (bi, 0, COL_QN)),
                  slab(COL_GN),
                  pl.BlockSpec((1, LANES, 2 * LANES), lambda bi: (bi, 0, 0)),
                  slab(COL_KS), slab(COL_VS), slab(COL_KW), slab(COL_VW),
                  pl.BlockSpec((s, LANES), lambda bi: (0, 0)),
                  pl.BlockSpec((N_HEADS, LANES), lambda bi: (0, 0)),
                  pl.BlockSpec((1, 256), lambda bi: (0, 0))],
        out_specs=pl.BlockSpec((1, s, 256), lambda bi: (bi, 0, 0)),
        out_shape=jax.ShapeDtypeStruct((b, s, 256), BF16),
        scratch_shapes=[pltpu.VMEM((2, rows, 2 * LANES), BF16), pltpu.VMEM((2, rows, LANES), F32),
                        pltpu.VMEM((2, rows, 2 * LANES), F32)],
        compiler_params=_cparams("parallel"),
        name="native_sparse",
    )(proj, proj, cmp_kv, proj, proj, proj, proj, pos, coef, gm)


def _mix_mlp_kernel(x_ref, a_ref, b_ref, c_ref, d_ref, wo_ref, g_ref, wu_ref, wd_ref, gf_ref, o_ref, h_ref,
                    *, final_norm):
    f = pl.program_id(1)

    @pl.when(f == 0)
    def _():
        mix = jnp.concatenate([a_ref[...], b_ref[...], c_ref[...], d_ref[...]], axis=1)
        x1 = x_ref[...] + _nn(mix, wo_ref[...])
        o_ref[...] = x1
        h_ref[...] = (_rms(x1) * g_ref[...]).astype(BF16)

    u = jnp.maximum(_nn(h_ref[...], wu_ref[...]), 0.0)
    o_ref[...] += _nn((u * u).astype(BF16), wd_ref[...])

    if final_norm:
        @pl.when(f == pl.num_programs(1) - 1)
        def _():
            o_ref[...] = _rms(o_ref[...]) * gf_ref[...]


def _mix_mlp(x2, mixes, wo, gain, wu, wd, gfinal, final_norm, tm=1024, tf=1024):
    m = x2.shape[0]
    grp = pl.BlockSpec((tm, GROUP_WIDTH), lambda i, f: (i, 0))
    row = pl.BlockSpec((1, D_MODEL), lambda i, f: (0, 0))
    return pl.pallas_call(
        functools.partial(_mix_mlp_kernel, final_norm=final_norm),
        grid=(m // tm, D_FF // tf),
        in_specs=[pl.BlockSpec((tm, D_MODEL), lambda i, f: (i, 0)), grp, grp, grp, grp,
                  pl.BlockSpec((D_MODEL, D_MODEL), lambda i, f: (0, 0)), row,
                  pl.BlockSpec((D_MODEL, tf), lambda i, f: (0, f)),
                  pl.BlockSpec((tf, D_MODEL), lambda i, f: (f, 0)), row],
        out_specs=pl.BlockSpec((tm, D_MODEL), lambda i, f: (i, 0)),
        out_shape=jax.ShapeDtypeStruct((m, D_MODEL), F32),
        scratch_shapes=[pltpu.VMEM((tm, D_MODEL), BF16)],
        compiler_params=_cparams("parallel", "arbitrary"),
        name="mix_mlp",
    )(x2, *mixes, wo, gain, wu, wd, gfinal)


def _compress_weights(pe_k, w1_k, w2_k, pe_v, w1_v, w2_v):
    half = CMP_STRIDE
    d = HEAD_DIM

    def halves(w1):
        w = w1.reshape(CMP_LEN, d, CMP_HIDDEN)
        return w[:half], w[half:]

    ka, kb = halves(w1_k)
    va, vb = halves(w1_v)
    zero = jnp.zeros_like(ka)

    def merge(wk, wv):
        top = jnp.concatenate([wk, zero], axis=-1)
        bot = jnp.concatenate([zero, wv], axis=-1)
        return jnp.concatenate([top, bot], axis=1).reshape(half * 2 * d, 2 * CMP_HIDDEN).astype(BF16)

    w1a, w1b = merge(ka, va), merge(kb, vb)
    pe = jnp.concatenate([pe_k, pe_v], axis=-1)
    pea = pe[:half].reshape(1, half * 2 * d)
    peb = pe[half:].reshape(1, half * 2 * d)
    zk = jnp.zeros_like(w2_k)
    w2 = jnp.concatenate([jnp.concatenate([w2_k, w2_k, zk, zk], axis=1),
                          jnp.concatenate([zk, zk, w2_v, w2_v], axis=1)], axis=0).astype(BF16)
    return pea, peb, w1a, w1b, w2


def kernel(x, norm_attn, w_in, cmp_pe_k, cmp_w1_k, cmp_w2_k, cmp_pe_v, cmp_w1_v, cmp_w2_v, diff_lq1, diff_lk1,
           diff_lq2, diff_lk2, sinks, g_mix, w_out, norm_mlp, w_up, w_down, norm_final):
    b, s, d = x.shape
    m = b * s
    depth = w_in.shape[0]
    cols = jnp.asarray(_COLS)
    w_in_p = jnp.take(jnp.concatenate([w_in, jnp.zeros((depth, d, 1), w_in.dtype)], axis=2), cols,
                      axis=2).astype(BF16)
    swa_perm = np.concatenate([np.arange(h * HEAD_DIM, (h + 1) * HEAD_DIM) for h in SWA_HEAD_ORDER])
    mix_perm = jnp.asarray(np.concatenate([np.arange(3 * GROUP_WIDTH), 3 * GROUP_WIDTH + swa_perm]))
    g_mix = jnp.take(g_mix, mix_perm, axis=1)
    w_out_b = jnp.take(w_out, mix_perm, axis=1).astype(BF16)
    w_up_b = w_up.astype(BF16)
    w_down_b = w_down.astype(BF16)
    gfinal = norm_final.reshape(1, d)

    colscale = jnp.asarray(_column_scales())

    x2 = x.reshape(m, d)
    for l in range(depth):
        proj = _inproj(x2, norm_attn[l].reshape(1, d), w_in_p[l], colscale).reshape(b, s, NP)
        gm = g_mix[l].reshape(N_HEADS, 1, GROUP_WIDTH)

        chunks = proj[:, :, COL_KVC * LANES:(COL_KVC + 1) * LANES].reshape(b, s // CMP_STRIDE, CMP_STRIDE * LANES)
        cmp_kv = _compress(chunks, *_compress_weights(cmp_pe_k[l], cmp_w1_k[l], cmp_w2_k[l],
                                                      cmp_pe_v[l], cmp_w1_v[l], cmp_w2_v[l]))

        o_sb = _stick_breaking(proj, gm[0])
        o_nsa = _native_sparse(proj, cmp_kv, gm[1])
        lam_init = 0.8 - 0.6 * math.exp(-0.3 * l)
        dl = jnp.stack([diff_lq1[l], diff_lk1[l], diff_lq2[l], diff_lk2[l]]).astype(F32)
        o_diff = _differential(proj, dl, gm[2], lam_init)
        o_swa = _sliding_window(proj, sinks[l].astype(F32), gm[3])

        mixes = [o.reshape(m, GROUP_WIDTH) for o in (o_sb, o_nsa, o_diff, o_swa)]
        x2 = _mix_mlp(x2, mixes, w_out_b[l], norm_mlp[l].reshape(1, d), w_up_b[l], w_down_b[l], gfinal,
                      final_norm=(l == depth - 1))
    return x2.reshape(b, s, d)
```

```python
import functools
import math

import ml_dtypes
import numpy as np
import jax
import jax.numpy as jnp
from jax import lax
from jax.experimental import pallas as pl
from jax.experimental.pallas import tpu as pltpu

F32 = jnp.float32
BF16 = jnp.bfloat16

D_MODEL = 1024
DEPTH = 4
HEAD_DIM = 64
GROUP_WIDTH = 256
N_HEADS = 4
D_FF = 4 * D_MODEL
EPS = 1e-6
NEG_INF = -1e30
LOG2E = math.log2(math.e)
LANES = 128

CMP_LEN = 32
CMP_STRIDE = 16
CMP_HIDDEN = 128
SEL_LEN = 64
SEL_TOPN = 8
FORCE_SCORE = 1e9
NSA_WINDOW = 512
SWA_WINDOW = 128
DIFF_QK_DIM = 32

VMEM_LIMIT = 48 * 1024 * 1024

_ORIG = dict(qa=(0, 256), ka=(256, 256), va=(512, 256), qn=(768, 256), kcn=(1024, 64), vcn=(1088, 64),
             ksn=(1152, 64), vsn=(1216, 64), kwn=(1280, 64), vwn=(1344, 64), gn=(1408, 12),
             qc=(1420, 256), kc=(1676, 256), vc=(1932, 256), qd=(2188, 256), kd=(2444, 128), vd=(2572, 128))


def _layout():
    segs = []

    def put(name, lo=0, n=None):
        start, size = _ORIG[name]
        segs.append((start + lo, size - lo if n is None else n))

    def zeros(n):
        segs.append((None, n))

    for nm in ("qa", "ka", "va", "qn", "kcn", "vcn"):
        put(nm)
    for nm in ("ksn", "vsn", "kwn", "vwn"):
        put(nm)
        put(nm)
    put("gn")
    zeros(LANES - 12)
    for nm in ("qc", "kc", "vc"):
        put(nm)
    for h in SWA_HEAD_ORDER:
        put("qd", h * HEAD_DIM, HEAD_DIM)
    put("kd")
    put("vd")
    return segs


SWA_HEAD_ORDER = (0, 2, 1, 3)
_SEGMENTS = _layout()
NP = sum(n for _, n in _SEGMENTS)


def _column_scales():
    cs = np.ones((1, NP), np.float32)
    cs[0, 0:256] = HEAD_DIM ** -0.5 * LOG2E
    cs[0, 768:1024] = HEAD_DIM ** -0.5 * LOG2E
    cs[0, 1792:2048] = DIFF_QK_DIM ** -0.5 * LOG2E
    cs[0, 2560:2816] = HEAD_DIM ** -0.5 * LOG2E
    return cs


COL_QA, COL_KA, COL_VA = 0, 1, 2
COL_QN = 3
COL_KVC = 8
COL_KS, COL_VS, COL_KW, COL_VW, COL_GN = 9, 10, 11, 12, 13
COL_QC, COL_KC, COL_VC = 7, 8, 9
COL_QD = 10
COL_KD, COL_VD = 22, 23


def _alibi_slopes():
    m = 2.0 ** (-8.0 * np.arange(1, 13) / 12.0)
    m = m.astype(np.float32).reshape(4, 3)
    return [float(v) for v in m[:, 0]], [float(v) for v in m[:, 1]], [float(v) for v in m[:, 2]]


SLOPES_NSA, SLOPES_DIFF, SLOPES_SWA = _alibi_slopes()


def _nt(a, b):
    return lax.dot_general(a, b, (((1,), (1,)), ((), ())), preferred_element_type=F32)


def _nn(a, b):
    return jnp.dot(a, b, preferred_element_type=F32)


def _cparams(*sem):
    return pltpu.CompilerParams(dimension_semantics=sem, vmem_limit_bytes=VMEM_LIMIT)


def _rms(x):
    return x * lax.rsqrt(jnp.mean(x * x, axis=-1, keepdims=True) + EPS)


def _inproj_kernel(x_ref, g_ref, w_ref, cs_ref, o_ref):
    h = _rms(x_ref[...]) * g_ref[...]
    o_ref[...] = (_nn(h.astype(BF16), w_ref[...]) * cs_ref[...]).astype(BF16)


def _inproj(x2, gain, w, colscale, tm=512):
    m = x2.shape[0]
    return pl.pallas_call(
        _inproj_kernel,
        grid=(m // tm,),
        in_specs=[pl.BlockSpec((tm, D_MODEL), lambda i: (i, 0)),
                  pl.BlockSpec((1, D_MODEL), lambda i: (0, 0)),
                  pl.BlockSpec((D_MODEL, NP), lambda i: (0, 0)),
                  pl.BlockSpec((1, NP), lambda i: (0, 0))],
        out_specs=pl.BlockSpec((tm, NP), lambda i: (i, 0)),
        out_shape=jax.ShapeDtypeStruct((m, NP), BF16),
        compiler_params=_cparams("parallel"),
        name="inproj",
    )(x2, gain, w, colscale)


def _compress_kernel(c_ref, pea_ref, peb_ref, w1a_ref, w1b_ref, w2_ref, o_ref):
    c = c_ref[0].astype(F32)
    p = _nn((c + pea_ref[...]).astype(BF16), w1a_ref[...])
    r = _nn((c + peb_ref[...]).astype(BF16), w1b_ref[...])
    n = c.shape[0]
    r_next = pltpu.roll(r, n - 1, 0)
    hid = jax.nn.gelu(p + r_next)
    o_ref[0] = _nn(hid.astype(BF16), w2_ref[...]).astype(BF16)


def _compress(chunks, pea, peb, w1a, w1b, w2):
    b, n, w = chunks.shape
    full = lambda a: pl.BlockSpec(a.shape, lambda i: (0,) * a.ndim)
    return pl.pallas_call(
        _compress_kernel,
        grid=(b,),
        in_specs=[pl.BlockSpec((1, n, w), lambda i: (i, 0, 0)), full(pea), full(peb), full(w1a), full(w1b),
                  full(w2)],
        out_specs=pl.BlockSpec((1, n, 2 * LANES), lambda i: (i, 0, 0)),
        out_shape=jax.ShapeDtypeStruct((b, n, 2 * LANES), BF16),
        compiler_params=_cparams("parallel"),
        name="nsa_compress",
    )(chunks, pea, peb, w1a, w1b, w2)


def _half_mask(shape, half):
    lane = lax.broadcasted_iota(jnp.int32, shape, len(shape) - 1)
    return (lane // HEAD_DIM) == half


def _group_norm_store(o_ref, rows, gm_ref, o):
    o_ref[0, rows, :] = (_rms(o) * gm_ref[...]).astype(o_ref.dtype)


def _query_rows(i, blk):
    return pl.ds(pl.multiple_of(i * blk, blk), blk)


def _over_query_blocks(step, blk):
    def kernel(*refs):
        def body(i, c):
            step(i, *refs)
            return c
        lax.fori_loop(0, refs[0].shape[1] // blk, body, 0)
    return kernel


def _sb_step(i, q_ref, k_ref, v_ref, gm_ref, o_ref, acc_ref, carry_ref, lb_ref, lk_ref, tot_ref, *, blk):
    q = q_ref[0, _query_rows(i, blk), :]
    row = lax.broadcasted_iota(jnp.int32, (blk, blk), 0)
    col = lax.broadcasted_iota(jnp.int32, (blk, blk), 1)
    past = col < row
    u = jnp.where(row > col, 1.0, 0.0).astype(BF16)
    qm = []
    for h in range(N_HEADS):
        slab = q[:, (h // 2) * LANES:(h // 2 + 1) * LANES]
        qm.append(jnp.where(_half_mask(slab.shape, h % 2), slab, jnp.zeros_like(slab)))
    qm = [jnp.concatenate(qm[0:2], axis=0), jnp.concatenate(qm[2:4], axis=0)]
    acc_ref[...] = jnp.zeros_like(acc_ref)
    carry_ref[...] = jnp.zeros_like(carry_ref)

    def per_head(x, fn):
        return jnp.concatenate([fn(x[k * blk:(k + 1) * blk]) for k in range(x.shape[0] // blk)], axis=0)

    def a_logits(j):
        rows = pl.ds(pl.multiple_of(j * blk, blk), blk)
        return [_nt(qm[pr], k_ref[0, rows, pr * LANES:(pr + 1) * LANES]) for pr in range(2)]

    def a_finish(zs, slot, diag):
        for pr, z in enumerate(zs):
            r2 = slice(2 * pr * blk, 2 * (pr + 1) * blk)
            sp = jnp.log2(1.0 + jnp.exp2(-jnp.abs(z)))
            lb = jnp.minimum(z, 0.0) - sp
            lk = lb - z
            if diag:
                lk = per_head(lk, lambda t: jnp.where(past, t, 0.0))
                lb = per_head(lb, lambda t: jnp.where(past, t, NEG_INF))
            lb_ref[slot, r2, :] = lb
            lk_ref[slot, r2, :] = lk.astype(BF16)
            tot_ref[slot, r2, :] = jnp.broadcast_to(jnp.sum(lk, axis=-1, keepdims=True), (2 * blk, LANES))

    def b_suffix(slot):
        return _nn(lk_ref[slot], u)

    def b_finish(cs, j, slot):
        rows = pl.ds(pl.multiple_of(j * blk, blk), blk)
        carry = carry_ref[...]
        a = jnp.exp2(lb_ref[slot] + cs + jnp.concatenate([carry] * (blk // LANES), axis=1)).astype(BF16)
        for pr in range(2):
            r2 = slice(2 * pr * blk, 2 * (pr + 1) * blk)
            acc_ref[r2, :] += _nn(a[r2], v_ref[0, rows, pr * LANES:(pr + 1) * LANES])
        carry_ref[...] = carry + tot_ref[slot]

    a_finish(a_logits(i), 0, True)

    def step(t, slot):
        zs = a_logits(i - 1 - t)
        cs = b_suffix(slot)
        a_finish(zs, 1 - slot, False)
        b_finish(cs, i - t, slot)

    def body(tt, c):
        step(2 * tt, 0)
        step(2 * tt + 1, 1)
        return c

    lax.fori_loop(0, i // 2, body, 0)

    @pl.when(i % 2 == 1)
    def _():
        step(i - 1, 0)
        b_finish(b_suffix(1), 0, 1)

    @pl.when(i % 2 == 0)
    def _():
        b_finish(b_suffix(0), 0, 0)

    lo_half = _half_mask((blk, LANES), 0)
    acc = [acc_ref[h * blk:(h + 1) * blk, :] for h in range(N_HEADS)]
    o = jnp.concatenate([jnp.where(lo_half, acc[0], acc[1]), jnp.where(lo_half, acc[2], acc[3])], axis=1)
    _group_norm_store(o_ref, _query_rows(i, blk), gm_ref, o)


def _stick_breaking(proj, gm, blk=256):
    b, s, _ = proj.shape
    stage = lambda w, dt: pltpu.VMEM((2, N_HEADS * blk, w), dt)
    return pl.pallas_call(
        _over_query_blocks(functools.partial(_sb_step, blk=blk), blk),
        grid=(b,),
        in_specs=[pl.BlockSpec((1, s, 256), lambda bi: (bi, 0, COL_QA)),
                  pl.BlockSpec((1, s, 256), lambda bi: (bi, 0, COL_KA)),
                  pl.BlockSpec((1, s, 256), lambda bi: (bi, 0, COL_VA)),
                  pl.BlockSpec((1, 256), lambda bi: (0, 0))],
        out_specs=pl.BlockSpec((1, s, 256), lambda bi: (bi, 0, 0)),
        out_shape=jax.ShapeDtypeStruct((b, s, 256), BF16),
        scratch_shapes=[pltpu.VMEM((N_HEADS * blk, LANES), F32), pltpu.VMEM((N_HEADS * blk, LANES), F32),
                        stage(blk, F32), stage(blk, BF16), stage(LANES, F32)],
        compiler_params=_cparams("parallel"),
        name="stick_breaking",
    )(proj, proj, proj, gm)


POS_LOCAL, POS_BLOCK = 0, 3
SEL_LANE = 8
MASK_BIAS = -2.0 ** 100


def _bf16_pieces(x, n):
    out = []
    r = np.float32(x)
    for _ in range(n):
        p = np.float32(r.astype(ml_dtypes.bfloat16))
        out.append(float(p))
        r = np.float32(r - p)
    return out


def _position_tables(slopes, s, blk, sel_blocks=False):
    coef = np.zeros((N_HEADS, LANES), np.float32)
    for h in range(N_HEADS):
        pieces = _bf16_pieces(slopes[h] * LOG2E, 3)
        coef[h, POS_LOCAL:POS_LOCAL + 3] = pieces
        coef[h, POS_BLOCK:POS_BLOCK + 3] = [blk * p for p in pieces]
    pos = np.zeros((s, LANES), np.float32)
    idx = np.arange(s)
    pos[:, POS_LOCAL:POS_LOCAL + 3] = (idx % blk)[:, None]
    pos[:, POS_BLOCK:POS_BLOCK + 3] = (idx // blk)[:, None]
    if sel_blocks:
        pos[idx, SEL_LANE + idx // SEL_LEN] = 1.0
    return jnp.asarray(coef), jnp.asarray(pos, BF16)


def _diff_step(i, q_ref, k_ref, v_ref, pos_ref, coef_ref, dl_ref, gm_ref, o_ref, qa_ref, mx_ref, acc_ref,
               *, blk, lam_init):
    q = q_ref[0, _query_rows(i, blk), :]
    row = lax.broadcasted_iota(jnp.int32, (blk, blk), 0)
    col = lax.broadcasted_iota(jnp.int32, (blk, blk), 1)
    causal = col <= row
    lane = lax.broadcasted_iota(jnp.int32, (blk, LANES), 1)
    for h in range(N_HEADS):
        slab = q[:, (h // 2) * LANES:(h // 2 + 1) * LANES]
        coef = jnp.broadcast_to(coef_ref[h:h + 1, :], (blk, LANES)).astype(BF16)
        for c in range(2):
            r = slice(((2 * h + c) % 4) * blk, ((2 * h + c) % 4 + 1) * blk)
            qa_ref[h // 2, r, 0:LANES] = jnp.where((lane // DIFF_QK_DIM) == (h % 2) * 2 + c, slab,
                                                   jnp.zeros_like(slab))
            qa_ref[h // 2, r, LANES:2 * LANES] = coef
    ones = jnp.ones((blk, LANES), BF16)

    def per_map(x, fn):
        return jnp.concatenate([fn(x[k * blk:(k + 1) * blk]) for k in range(4)], axis=0)

    def sweep(blocks, first):
        chains = [(pr, pl.ds(pl.multiple_of(j * blk, blk), blk), diag) for pr in range(2) for j, diag in blocks]
        zs = []
        for pr, rows, diag in chains:
            ka = jnp.concatenate([k_ref[0, rows, pr * LANES:(pr + 1) * LANES], pos_ref[rows, :]], axis=1)
            z = _nt(qa_ref[pr], ka)
            if diag:
                z = per_map(z, lambda t: jnp.where(causal, t, NEG_INF))
            zs.append(z)
        m_new, alpha = [None, None], [None, None]
        for pr in range(2):
            zm = None
            for (cp, _, _), z in zip(chains, zs):
                if cp == pr:
                    for g in range(blk // LANES):
                        part = z[:, g * LANES:(g + 1) * LANES]
                        zm = part if zm is None else jnp.maximum(zm, part)
            cm = jnp.broadcast_to(jnp.max(zm, axis=-1, keepdims=True), (4 * blk, LANES))
            if first:
                m_new[pr] = cm
            else:
                m_old = mx_ref[pr]
                m_new[pr] = jnp.maximum(m_old, cm)
                alpha[pr] = jnp.exp2(m_old - m_new[pr])
            mx_ref[pr] = m_new[pr]
        ps = [jnp.exp2(z - jnp.concatenate([m_new[pr]] * (blk // LANES), axis=1)).astype(BF16)
              for (pr, _, _), z in zip(chains, zs)]
        outs = []
        for (pr, rows, diag), p in zip(chains, ps):
            vo = jnp.concatenate([v_ref[0, rows, pr * LANES:(pr + 1) * LANES], ones], axis=1)
            outs.append(_nn(p, vo))
        for pr in range(2):
            tot = functools.reduce(jnp.add, [o for (cp, _, _), o in zip(chains, outs) if cp == pr])
            acc_ref[pr] = tot if first else acc_ref[pr] * jnp.concatenate([alpha[pr]] * 2, axis=1) + tot

    base = (i // 4) * 4

    @pl.when(i % 2 == 1)
    def _():
        sweep([(i - 1, False), (i, True)], True)

    @pl.when(i % 2 == 0)
    def _():
        sweep([(i, True)], True)

    @pl.when(i % 4 >= 2)
    def _():
        sweep([(base, False), (base + 1, False)], False)

    def body(tt, c):
        sweep([(4 * tt + k, False) for k in range(4)], False)
        return c
    lax.fori_loop(0, i // 4, body, 0)

    dl = dl_ref[...]
    s1 = jnp.sum(dl[0:1] * dl[1:2], axis=-1, keepdims=True)
    s2 = jnp.sum(dl[2:3] * dl[3:4], axis=-1, keepdims=True)
    lam = jnp.exp(s1) - jnp.exp(s2) + lam_init
    outs = []
    for h in range(N_HEADS):
        k1 = (2 * h) % 4
        a1 = acc_ref[h // 2, k1 * blk:(k1 + 1) * blk, :]
        a2 = acc_ref[h // 2, (k1 + 1) * blk:(k1 + 2) * blk, :]
        d = a1[:, :LANES] / a1[:, LANES:] - lam * (a2[:, :LANES] / a2[:, LANES:])
        mine = _half_mask(d.shape, h % 2)
        ms = jnp.sum(jnp.where(mine, d * d, 0.0), axis=-1, keepdims=True) * (1.0 / HEAD_DIM)
        outs.append(d * lax.rsqrt(ms + EPS) * (1.0 - lam_init))
    lo_half = _half_mask((blk, LANES), 0)
    o = jnp.concatenate([jnp.where(lo_half, outs[0], outs[1]), jnp.where(lo_half, outs[2], outs[3])], axis=1)
    o_ref[0, _query_rows(i, blk), :] = (o * gm_ref[...]).astype(o_ref.dtype)


def _differential(proj, dl, gm, lam_init, blk=256):
    b, s, _ = proj.shape
    coef, pos = _position_tables(SLOPES_DIFF, s, blk)
    return pl.pallas_call(
        _over_query_blocks(functools.partial(_diff_step, blk=blk, lam_init=lam_init), blk),
        grid=(b,),
        in_specs=[pl.BlockSpec((1, s, 256), lambda bi: (bi, 0, COL_QC)),
                  pl.BlockSpec((1, s, 256), lambda bi: (bi, 0, COL_KC)),
                  pl.BlockSpec((1, s, 256), lambda bi: (bi, 0, COL_VC)),
                  pl.BlockSpec((s, LANES), lambda bi: (0, 0)),
                  pl.BlockSpec((N_HEADS, LANES), lambda bi: (0, 0)),
                  pl.BlockSpec((4, DIFF_QK_DIM), lambda bi: (0, 0)),
                  pl.BlockSpec((1, 256), lambda bi: (0, 0))],
        out_specs=pl.BlockSpec((1, s, 256), lambda bi: (bi, 0, 0)),
        out_shape=jax.ShapeDtypeStruct((b, s, 256), BF16),
        scratch_shapes=[pltpu.VMEM((2, 4 * blk, 2 * LANES), BF16), pltpu.VMEM((2, 4 * blk, LANES), F32),
                        pltpu.VMEM((2, 4 * blk, 2 * LANES), F32)],
        compiler_params=_cparams("parallel"),
        name="differential",
    )(proj, proj, proj, pos, coef, dl, gm)


def _swa_step(i, q_ref, k_ref, v_ref, sink_ref, gm_ref, o_ref, *, blk, sub):
    row = lax.broadcasted_iota(jnp.int32, (blk, 2 * blk), 0)
    col = lax.broadcasted_iota(jnp.int32, (blk, 2 * blk), 1)
    lo_half = _half_mask((blk, LANES), 0)
    blocks = []
    for s_ in range(sub):
        g = i * sub + s_
        first = jnp.maximum(g - 1, 0)
        start = pl.multiple_of(first * blk, blk)
        dist = (row - col) + (g - first) * blk
        mask = (dist >= 0) & (dist < SWA_WINDOW)
        distf = dist.astype(F32)
        q = q_ref[0, _query_rows(g, blk), :]
        kb = k_ref[0, pl.ds(start, 2 * blk), :]
        vb = v_ref[0, pl.ds(start, 2 * blk), :]
        outs = []
        for n, h in enumerate(SWA_HEAD_ORDER):
            slab = q[:, (n // 2) * LANES:(n // 2 + 1) * LANES]
            qm = jnp.where(_half_mask(slab.shape, n % 2), slab, jnp.zeros_like(slab))
            z = _nt(qm, kb) - (SLOPES_SWA[h] * LOG2E) * distf
            z = jnp.where(mask, z, NEG_INF)
            sink = sink_ref[h] * LOG2E
            m = jnp.maximum(jnp.max(z, axis=-1, keepdims=True), sink)
            p = jnp.exp2(z - m)
            l = jnp.sum(p, axis=-1, keepdims=True) + jnp.exp2(sink - m)
            outs.append(_nn(p.astype(BF16), vb) / l)
        blocks.append(jnp.concatenate([jnp.where(lo_half, outs[0], outs[1]), jnp.where(lo_half, outs[2], outs[3])],
                                      axis=1))
    _group_norm_store(o_ref, _query_rows(i, sub * blk), gm_ref, jnp.concatenate(blocks, axis=0))


def _sliding_window(proj, sinks, gm, blk=128, sub=4):
    b, s, _ = proj.shape
    rows = blk * sub
    return pl.pallas_call(
        _over_query_blocks(functools.partial(_swa_step, blk=blk, sub=sub), rows),
        grid=(b,),
        in_specs=[pl.BlockSpec((1, s, 256), lambda bi: (bi, 0, COL_QD)),
                  pl.BlockSpec((1, s, LANES), lambda bi: (bi, 0, COL_KD)),
                  pl.BlockSpec((1, s, LANES), lambda bi: (bi, 0, COL_VD)),
                  pl.BlockSpec(memory_space=pltpu.SMEM),
                  pl.BlockSpec((1, 256), lambda bi: (0, 0))],
        out_specs=pl.BlockSpec((1, s, 256), lambda bi: (bi, 0, 0)),
        out_shape=jax.ShapeDtypeStruct((b, s, 256), BF16),
        compiler_params=_cparams("parallel"),
        name="sliding_window",
    )(proj, proj, proj, sinks, gm)


def _nsa_step(i, q_ref, g_ref, cmp_ref, ks_ref, vs_ref, kw_ref, vw_ref, pos_ref, coef_ref, gm_ref, o_ref,
              qa_ref, mx_ref, acc_ref, *, blk, n_cmp, n_sel):
    q = q_ref[0, _query_rows(i, blk), :]
    lane = lax.broadcasted_iota(jnp.int32, (blk, LANES), 1)
    coefs = [jnp.broadcast_to(coef_ref[h:h + 1, :], (blk, LANES)) for h in range(N_HEADS)]
    for h in range(N_HEADS):
        slab = q[:, (h // 2) * LANES:(h // 2 + 1) * LANES]
        own = jnp.where(_half_mask(slab.shape, h % 2), slab, jnp.zeros_like(slab))
        qa_ref[0, h * blk:(h + 1) * blk, 0:LANES] = own
        qa_ref[1, h * blk:(h + 1) * blk, 0:LANES] = own
        qa_ref[0, h * blk:(h + 1) * blk, LANES:2 * LANES] = coefs[h].astype(BF16)
    qs = qa_ref[1, :, 0:LANES]

    tq = i * blk + lax.broadcasted_iota(jnp.int32, (blk, LANES), 0)

    row = lax.broadcasted_iota(jnp.int32, (blk, blk), 0)
    col = lax.broadcasted_iota(jnp.int32, (blk, blk), 1)
    causal = col <= row
    in_window = col > row
    ones = jnp.ones((blk, LANES), BF16)

    k_refs, v_refs = (kw_ref, ks_ref), (vw_ref, vs_ref)

    def process(blocks, first=False):
        groups = [(br, half) for br in sorted({b[0] for b in blocks}) for half in range(2)]
        chains = [(br, pl.ds(pl.multiple_of(j * blk, blk), blk), mask, (br, half))
                  for br, j, mask in blocks for half in range(2)]
        grows = lambda g: slice(2 * g[1] * blk, 2 * (g[1] + 1) * blk)
        zs = []
        for br, rows, mask, g in chains:
            ka = jnp.concatenate([k_refs[br][0, rows, :], pos_ref[rows, :]], axis=1)
            z = _nt(qa_ref[br, grows(g), :], ka)
            if mask is not None:
                z = jnp.concatenate([jnp.where(mask, z[r * blk:(r + 1) * blk], NEG_INF) for r in range(2)], axis=0)
            zs.append(z)
        m_new, alpha = {}, {}
        for g in groups:
            zm = None
            for (_, _, _, cg), z in zip(chains, zs):
                if cg == g:
                    for c in range(blk // LANES):
                        part = z[:, c * LANES:(c + 1) * LANES]
                        zm = part if zm is None else jnp.maximum(zm, part)
            cm = jnp.broadcast_to(jnp.max(zm, axis=-1, keepdims=True), (2 * blk, LANES))
            if first:
                m_new[g] = cm
            else:
                m_old = mx_ref[g[0], grows(g), :]
                m_new[g] = jnp.maximum(m_old, cm)
                alpha[g] = jnp.exp2(m_old - m_new[g])
            mx_ref[g[0], grows(g), :] = m_new[g]
        ps = [jnp.exp2(z - jnp.concatenate([m_new[g]] * (blk // LANES), axis=1)).astype(BF16)
              for (_, _, _, g), z in zip(chains, zs)]
        outs = []
        for (br, rows, mask, g), p in zip(chains, ps):
            vo = jnp.concatenate([v_refs[br][0, rows, :], ones], axis=1)
            outs.append(_nn(p, vo))
        for g in groups:
            tot = functools.reduce(jnp.add, [o for (_, _, _, cg), o in zip(chains, outs) if cg == g])
            if first:
                acc_ref[g[0], grows(g), :] = tot
            else:
                acc_ref[g[0], grows(g), :] = acc_ref[g[0], grows(g), :] * jnp.concatenate([alpha[g]] * 2, axis=1) + tot

    window = [(0, jnp.maximum(i - 2, 0), in_window & (i >= 2)),
              (0, jnp.maximum(i - 1, 0), jnp.broadcast_to(i >= 1, (blk, blk))),
              (0, i, causal)]

    process(window, first=True)
    o_win = acc_ref[0, :, 0:LANES] / acc_ref[0, :, LANES:2 * LANES]

    kc = cmp_ref[0, :, 0:LANES]
    vc = cmp_ref[0, :, LANES:2 * LANES]
    dist_c = tq - (CMP_STRIDE * lane + CMP_LEN - 1)
    ok_c = (dist_c >= 0) & (lane < n_cmp)
    dist_cf = dist_c.astype(F32)
    bias_c = jnp.concatenate([jnp.where(ok_c, (-SLOPES_NSA[h] * LOG2E) * dist_cf, NEG_INF)
                              for h in range(N_HEADS)], axis=0)
    z = _nt(qs, kc) + bias_c
    p = jnp.exp2(z - jnp.max(z, axis=-1, keepdims=True))
    p = p / jnp.sum(p, axis=-1, keepdims=True)
    p = jnp.where(bias_c > 0.5 * NEG_INF, p, 0.0).astype(BF16)
    o_cmp = _nn(p, vc)

    sel_rows = -(-(SEL_LANE + n_sel) // 8) * 8
    rj = lax.broadcasted_iota(jnp.int32, (sel_rows, LANES), 0)
    ri = lax.broadcasted_iota(jnp.int32, (sel_rows, LANES), 1)
    to_sel = jnp.where((ri * CMP_STRIDE) // SEL_LEN + SEL_LANE == rj, 1.0, 0.0).astype(BF16)
    imp = _nt(to_sel, p[0:blk])
    for h in range(1, N_HEADS):
        imp = imp + _nt(to_sel, p[h * blk:(h + 1) * blk])
    srow = lax.broadcasted_iota(jnp.int32, (sel_rows, blk), 0)
    blk_id = srow - SEL_LANE
    cur = (i * blk + lax.broadcasted_iota(jnp.int32, (sel_rows, blk), 1)) // SEL_LEN
    forced = (blk_id == 0) | (blk_id == cur) | (blk_id == cur - 1)
    score = jnp.where(blk_id > cur, NEG_INF, jnp.where(forced, FORCE_SCORE, imp))
    score = jnp.where((blk_id >= 0) & (blk_id < n_sel), score, -3.0e38)
    sel = jnp.zeros((sel_rows, blk), F32)
    srowf = srow.astype(F32)
    for _ in range(min(SEL_TOPN, n_sel)):
        best = jnp.max(score, axis=0, keepdims=True)
        idx = jnp.min(jnp.where(score == best, srowf, float(LANES)), axis=0, keepdims=True)
        hit = srowf == idx
        sel = jnp.where(hit, 1.0, sel)
        score = jnp.where(hit, -3.4e38, score)
    sel_bias = jnp.where(sel > 0.5, 0.0, MASK_BIAS)
    sel_bias = jnp.concatenate([sel_bias, jnp.zeros((LANES - sel_rows, blk), F32)], axis=0).T
    in_sel = (lane >= SEL_LANE) & (lane < SEL_LANE + n_sel)

    for h in range(N_HEADS):
        qa_ref[1, h * blk:(h + 1) * blk, LANES:2 * LANES] = jnp.where(in_sel, sel_bias, coefs[h]).astype(BF16)
    base = (i // 4) * 4

    @pl.when(i % 2 == 1)
    def _():
        process([(1, i - 1, None), (1, i, causal)], first=True)

    @pl.when(i % 2 == 0)
    def _():
        process([(1, i, causal)], first=True)

    @pl.when(i % 4 >= 2)
    def _():
        process([(1, base, None), (1, base + 1, None)])

    def body(tt, c):
        process([(1, 4 * tt + k, None) for k in range(4)])
        return c
    lax.fori_loop(0, i // 4, body, 0)
    o_sel = acc_ref[1, :, 0:LANES] / acc_ref[1, :, LANES:2 * LANES]

    g = jax.nn.sigmoid(g_ref[0, _query_rows(i, blk), :].astype(F32))
    outs = []
    for h in range(N_HEADS):
        r = slice(h * blk, (h + 1) * blk)
        outs.append(g[:, 3 * h:3 * h + 1] * o_cmp[r] + g[:, 3 * h + 1:3 * h + 2] * o_sel[r]
                    + g[:, 3 * h + 2:3 * h + 3] * o_win[r])
    lo_half = _half_mask((blk, LANES), 0)
    o = jnp.concatenate([jnp.where(lo_half, outs[0], outs[1]), jnp.where(lo_half, outs[2], outs[3])], axis=1)
    _group_norm_store(o_ref, _query_rows(i, blk), gm_ref, o)


def _native_sparse(proj, cmp_kv, gm, blk=256):
    b, s, _ = proj.shape
    n_cmp = (s - CMP_LEN) // CMP_STRIDE + 1
    n_sel = s // SEL_LEN
    assert cmp_kv.shape[1] == LANES and n_cmp <= LANES and SEL_LANE + n_sel <= LANES
    assert NSA_WINDOW == 2 * blk
    coef, pos = _position_tables(SLOPES_NSA, s, blk, sel_blocks=True)
    slab = lambda c: pl.BlockSpec((1, s, LANES), lambda bi: (bi, 0, c))
    rows = N_HEADS * blk
    return pl.pallas_call(
        _over_query_blocks(functools.partial(_nsa_step, blk=blk, n_cmp=n_cmp, n_sel=n_sel), blk),
        grid=(b,),
        in_specs=[pl.BlockSpec((1, s, 256), lambda bi: (bi, 0, COL_QN)),
                  slab(COL_GN),
                  pl.BlockSpec((1, LANES, 2 * LANES), lambda bi: (bi, 0, 0)),
                  slab(COL_KS), slab(COL_VS), slab(COL_KW), slab(COL_VW),
                  pl.BlockSpec((s, LANES), lambda bi: (0, 0)),
                  pl.BlockSpec((N_HEADS, LANES), lambda bi: (0, 0)),
                  pl.BlockSpec((1, 256), lambda bi: (0, 0))],
        out_specs=pl.BlockSpec((1, s, 256), lambda bi: (bi, 0, 0)),
        out_shape=jax.ShapeDtypeStruct((b, s, 256), BF16),
        scratch_shapes=[pltpu.VMEM((2, rows, 2 * LANES), BF16), pltpu.VMEM((2, rows, LANES), F32),
                        pltpu.VMEM((2, rows, 2 * LANES), F32)],
        compiler_params=_cparams("parallel"),
        name="native_sparse",
    )(proj, proj, cmp_kv, proj, proj, proj, proj, pos, coef, gm)


def _mix_mlp_kernel(x_ref, a_ref, b_ref, c_ref, d_ref, wo_ref, g_ref, wu_ref, wd_ref, gf_ref, o_ref, h_ref,
                    *, final_norm):
    f = pl.program_id(1)

    @pl.when(f == 0)
    def _():
        mix = jnp.concatenate([a_ref[...], b_ref[...], c_ref[...], d_ref[...]], axis=1)
        x1 = x_ref[...] + _nn(mix, wo_ref[...])
        o_ref[...] = x1
        h_ref[...] = (_rms(x1) * g_ref[...]).astype(BF16)

    u = jnp.maximum(_nn(h_ref[...], wu_ref[...]), 0.0)
    o_ref[...] += _nn((u * u).astype(BF16), wd_ref[...])

    if final_norm:
        @pl.when(f == pl.num_programs(1) - 1)
        def _():
            o_ref[...] = _rms(o_ref[...]) * gf_ref[...]


def _mix_mlp(x2, mixes, wo, gain, wu, wd, gfinal, final_norm, tm=1024, tf=1024):
    m = x2.shape[0]
    grp = pl.BlockSpec((tm, GROUP_WIDTH), lambda i, f: (i, 0))
    row = pl.BlockSpec((1, D_MODEL), lambda i, f: (0, 0))
    return pl.pallas_call(
        functools.partial(_mix_mlp_kernel, final_norm=final_norm),
        grid=(m // tm, D_FF // tf),
        in_specs=[pl.BlockSpec((tm, D_MODEL), lambda i, f: (i, 0)), grp, grp, grp, grp,
                  pl.BlockSpec((D_MODEL, D_MODEL), lambda i, f: (0, 0)), row,
                  pl.BlockSpec((D_MODEL, tf), lambda i, f: (0, f)),
                  pl.BlockSpec((tf, D_MODEL), lambda i, f: (f, 0)), row],
        out_specs=pl.BlockSpec((tm, D_MODEL), lambda i, f: (i, 0)),
        out_shape=jax.ShapeDtypeStruct((m, D_MODEL), F32),
        scratch_shapes=[pltpu.VMEM((tm, D_MODEL), BF16)],
        compiler_params=_cparams("parallel", "arbitrary"),
        name="mix_mlp",
    )(x2, *mixes, wo, gain, wu, wd, gfinal)


def _compress_weights(pe_k, w1_k, w2_k, pe_v, w1_v, w2_v):
    half = CMP_STRIDE
    d = HEAD_DIM

    def halves(w1):
        w = w1.reshape(CMP_LEN, d, CMP_HIDDEN)
        return w[:half], w[half:]

    ka, kb = halves(w1_k)
    va, vb = halves(w1_v)
    zero = jnp.zeros_like(ka)

    def merge(wk, wv):
        top = jnp.concatenate([wk, zero], axis=-1)
        bot = jnp.concatenate([zero, wv], axis=-1)
        return jnp.concatenate([top, bot], axis=1).reshape(half * 2 * d, 2 * CMP_HIDDEN).astype(BF16)

    w1a, w1b = merge(ka, va), merge(kb, vb)
    pe = jnp.concatenate([pe_k, pe_v], axis=-1)
    pea = pe[:half].reshape(1, half * 2 * d)
    peb = pe[half:].reshape(1, half * 2 * d)
    zk = jnp.zeros_like(w2_k)
    w2 = jnp.concatenate([jnp.concatenate([w2_k, w2_k, zk, zk], axis=1),
                          jnp.concatenate([zk, zk, w2_v, w2_v], axis=1)], axis=0).astype(BF16)
    return pea, peb, w1a, w1b, w2


def kernel(x, norm_attn, w_in, cmp_pe_k, cmp_w1_k, cmp_w2_k, cmp_pe_v, cmp_w1_v, cmp_w2_v, diff_lq1, diff_lk1,
           diff_lq2, diff_lk2, sinks, g_mix, w_out, norm_mlp, w_up, w_down, norm_final):
    b, s, d = x.shape
    m = b * s
    depth = w_in.shape[0]
    w_in_p = jnp.concatenate([jnp.zeros((depth, d, n), w_in.dtype) if a is None else w_in[:, :, a:a + n]
                              for a, n in _SEGMENTS], axis=2).astype(BF16)
    swa0 = 3 * GROUP_WIDTH
    mix_segs = [(0, swa0)] + [(swa0 + h * HEAD_DIM, HEAD_DIM) for h in SWA_HEAD_ORDER]
    g_mix = jnp.concatenate([g_mix[:, a:a + n] for a, n in mix_segs], axis=1)
    w_out_b = jnp.concatenate([w_out[:, a:a + n] for a, n in mix_segs], axis=1).astype(BF16)
    w_up_b = w_up.astype(BF16)
    w_down_b = w_down.astype(BF16)
    gfinal = norm_final.reshape(1, d)

    colscale = jnp.asarray(_column_scales())

    x2 = x.reshape(m, d)
    for l in range(depth):
        proj = _inproj(x2, norm_attn[l].reshape(1, d), w_in_p[l], colscale).reshape(b, s, NP)
        gm = g_mix[l].reshape(N_HEADS, 1, GROUP_WIDTH)

        chunks = proj[:, :, COL_KVC * LANES:(COL_KVC + 1) * LANES].reshape(b, s // CMP_STRIDE, CMP_STRIDE * LANES)
        cmp_kv = _compress(chunks, *_compress_weights(cmp_pe_k[l], cmp_w1_k[l], cmp_w2_k[l],
                                                      cmp_pe_v[l], cmp_w1_v[l], cmp_w2_v[l]))

        o_sb = _stick_breaking(proj, gm[0])
        o_nsa = _native_sparse(proj, cmp_kv, gm[1])
        lam_init = 0.8 - 0.6 * math.exp(-0.3 * l)
        dl = jnp.stack([diff_lq1[l], diff_lk1[l], diff_lq2[l], diff_lk2[l]]).astype(F32)
        o_diff = _differential(proj, dl, gm[2], lam_init)
        o_swa = _sliding_window(proj, sinks[l].astype(F32), gm[3])

        mixes = [o.reshape(m, GROUP_WIDTH) for o in (o_sb, o_nsa, o_diff, o_swa)]
        x2 = _mix_mlp(x2, mixes, w_out_b[l], norm_mlp[l].reshape(1, d), w_up_b[l], w_down_b[l], gfinal,
                      final_norm=(l == depth - 1))
    return x2.reshape(b, s, d)
```

```python
import functools
import math

import ml_dtypes
import numpy as np
import jax
import jax.numpy as jnp
from jax import lax
from jax.experimental import pallas as pl
from jax.experimental.pallas import tpu as pltpu

F32 = jnp.float32
BF16 = jnp.bfloat16

D_MODEL = 1024
DEPTH = 4
HEAD_DIM = 64
GROUP_WIDTH = 256
N_HEADS = 4
D_FF = 4 * D_MODEL
EPS = 1e-6
NEG_INF = -1e30
LOG2E = math.log2(math.e)
LANES = 128

CMP_LEN = 32
CMP_STRIDE = 16
CMP_HIDDEN = 128
SEL_LEN = 64
SEL_TOPN = 8
FORCE_SCORE = 1e9
NSA_WINDOW = 512
SWA_WINDOW = 128
DIFF_QK_DIM = 32

VMEM_LIMIT = 48 * 1024 * 1024

_ORIG = dict(qa=(0, 256), ka=(256, 256), va=(512, 256), qn=(768, 256), kcn=(1024, 64), vcn=(1088, 64),
             ksn=(1152, 64), vsn=(1216, 64), kwn=(1280, 64), vwn=(1344, 64), gn=(1408, 12),
             qc=(1420, 256), kc=(1676, 256), vc=(1932, 256), qd=(2188, 256), kd=(2444, 128), vd=(2572, 128))


def _layout():
    segs, first = [], {}

    def put(name, lo=0, n=None):
        start, size = _ORIG[name]
        first.setdefault(name, sum(w for _, w in segs))
        segs.append((start + lo, size - lo if n is None else n))

    def zeros(n):
        segs.append((None, n))

    for nm in ("qa", "ka", "va", "qn", "kcn", "vcn"):
        put(nm)
    for nm in ("ksn", "vsn", "kwn", "vwn"):
        put(nm)
        put(nm)
    put("gn")
    zeros(LANES - 12)
    for nm in ("qc", "kc", "vc"):
        put(nm)
    for h in SWA_HEAD_ORDER:
        put("qd", h * HEAD_DIM, HEAD_DIM)
    put("kd")
    put("vd")
    return segs, first


SWA_HEAD_ORDER = (0, 2, 1, 3)
_SEGMENTS, _FIRST = _layout()
NP = sum(n for _, n in _SEGMENTS)


def _column_scales():
    cs = np.ones((1, NP), np.float32)
    for name, dim in (("qa", HEAD_DIM), ("qn", HEAD_DIM), ("qc", DIFF_QK_DIM), ("qd", HEAD_DIM)):
        cs[0, _FIRST[name]:_FIRST[name] + GROUP_WIDTH] = dim ** -0.5 * LOG2E
    return cs


def _col_block(name, width):
    block, rem = divmod(_FIRST[name], width)
    assert rem == 0, (name, width)
    return block


COL_QA, COL_KA, COL_VA, COL_QN = (_col_block(n, GROUP_WIDTH) for n in ("qa", "ka", "va", "qn"))
COL_QC, COL_KC, COL_VC, COL_QD = (_col_block(n, GROUP_WIDTH) for n in ("qc", "kc", "vc", "qd"))
COL_KVC, COL_KS, COL_VS, COL_KW, COL_VW, COL_GN, COL_KD, COL_VD = (
    _col_block(n, LANES) for n in ("kcn", "ksn", "vsn", "kwn", "vwn", "gn", "kd", "vd"))


def _alibi_slopes():
    m = 2.0 ** (-8.0 * np.arange(1, 13) / 12.0)
    m = m.astype(np.float32).reshape(4, 3)
    return [float(v) for v in m[:, 0]], [float(v) for v in m[:, 1]], [float(v) for v in m[:, 2]]


SLOPES_NSA, SLOPES_DIFF, SLOPES_SWA = _alibi_slopes()


def _nt(a, b):
    return lax.dot_general(a, b, (((1,), (1,)), ((), ())), preferred_element_type=F32)


def _nn(a, b):
    return jnp.dot(a, b, preferred_element_type=F32)


def _cparams(*sem):
    return pltpu.CompilerParams(dimension_semantics=sem, vmem_limit_bytes=VMEM_LIMIT)


def _rms(x):
    return x * lax.rsqrt(jnp.mean(x * x, axis=-1, keepdims=True) + EPS)


def _inproj_kernel(x_ref, g_ref, w_ref, cs_ref, o_ref):
    h = _rms(x_ref[...]) * g_ref[...]
    o_ref[...] = (_nn(h.astype(BF16), w_ref[...]) * cs_ref[...]).astype(BF16)


def _inproj(x2, gain, w, colscale, tm=512):
    m = x2.shape[0]
    return pl.pallas_call(
        _inproj_kernel,
        grid=(m // tm,),
        in_specs=[pl.BlockSpec((tm, D_MODEL), lambda i: (i, 0)),
                  pl.BlockSpec((1, D_MODEL), lambda i: (0, 0)),
                  pl.BlockSpec((D_MODEL, NP), lambda i: (0, 0)),
                  pl.BlockSpec((1, NP), lambda i: (0, 0))],
        out_specs=pl.BlockSpec((tm, NP), lambda i: (i, 0)),
        out_shape=jax.ShapeDtypeStruct((m, NP), BF16),
        compiler_params=_cparams("parallel"),
        name="inproj",
    )(x2, gain, w, colscale)


def _compress_kernel(c_ref, pea_ref, peb_ref, w1a_ref, w1b_ref, w2_ref, o_ref):
    c = c_ref[0].astype(F32)
    p = _nn((c + pea_ref[...]).astype(BF16), w1a_ref[...])
    r = _nn((c + peb_ref[...]).astype(BF16), w1b_ref[...])
    n = c.shape[0]
    r_next = pltpu.roll(r, n - 1, 0)
    hid = jax.nn.gelu(p + r_next)
    o_ref[0] = _nn(hid.astype(BF16), w2_ref[...]).astype(BF16)


def _compress(chunks, pea, peb, w1a, w1b, w2):
    b, n, w = chunks.shape
    full = lambda a: pl.BlockSpec(a.shape, lambda i: (0,) * a.ndim)
    return pl.pallas_call(
        _compress_kernel,
        grid=(b,),
        in_specs=[pl.BlockSpec((1, n, w), lambda i: (i, 0, 0)), full(pea), full(peb), full(w1a), full(w1b),
                  full(w2)],
        out_specs=pl.BlockSpec((1, n, 2 * LANES), lambda i: (i, 0, 0)),
        out_shape=jax.ShapeDtypeStruct((b, n, 2 * LANES), BF16),
        compiler_params=_cparams("parallel"),
        name="nsa_compress",
    )(chunks, pea, peb, w1a, w1b, w2)


def _half_mask(shape, half):
    lane = lax.broadcasted_iota(jnp.int32, shape, len(shape) - 1)
    return (lane // HEAD_DIM) == half


def _group_norm_store(o_ref, rows, gm_ref, o):
    o_ref[0, rows, :] = (_rms(o) * gm_ref[...]).astype(o_ref.dtype)


def _query_rows(i, blk):
    return pl.ds(pl.multiple_of(i * blk, blk), blk)


def _over_query_blocks(step, blk):
    def kernel(*refs):
        def body(i, c):
            step(i, *refs)
            return c
        lax.fori_loop(0, refs[0].shape[1] // blk, body, 0)
    return kernel


def _sb_step(i, q_ref, k_ref, v_ref, gm_ref, o_ref, acc_ref, carry_ref, lb_ref, lk_ref, tot_ref, *, blk):
    q = q_ref[0, _query_rows(i, blk), :]
    row = lax.broadcasted_iota(jnp.int32, (blk, blk), 0)
    col = lax.broadcasted_iota(jnp.int32, (blk, blk), 1)
    past = col < row
    u = jnp.where(row > col, 1.0, 0.0).astype(BF16)
    qm = []
    for h in range(N_HEADS):
        slab = q[:, (h // 2) * LANES:(h // 2 + 1) * LANES]
        qm.append(jnp.where(_half_mask(slab.shape, h % 2), slab, jnp.zeros_like(slab)))
    qm = [jnp.concatenate(qm[0:2], axis=0), jnp.concatenate(qm[2:4], axis=0)]
    acc_ref[...] = jnp.zeros_like(acc_ref)
    carry_ref[...] = jnp.zeros_like(carry_ref)

    def per_head(x, fn):
        return jnp.concatenate([fn(x[k * blk:(k + 1) * blk]) for k in range(x.shape[0] // blk)], axis=0)

    def a_logits(j):
        rows = pl.ds(pl.multiple_of(j * blk, blk), blk)
        return [_nt(qm[pr], k_ref[0, rows, pr * LANES:(pr + 1) * LANES]) for pr in range(2)]

    def a_finish(zs, slot, diag):
        for pr, z in enumerate(zs):
            r2 = slice(2 * pr * blk, 2 * (pr + 1) * blk)
            sp = jnp.log2(1.0 + jnp.exp2(-jnp.abs(z)))
            lb = jnp.minimum(z, 0.0) - sp
            lk = lb - z
            if diag:
                lk = per_head(lk, lambda t: jnp.where(past, t, 0.0))
                lb = per_head(lb, lambda t: jnp.where(past, t, NEG_INF))
            lb_ref[slot, r2, :] = lb
            lk_ref[slot, r2, :] = lk.astype(BF16)
            tot_ref[slot, r2, :] = jnp.broadcast_to(jnp.sum(lk, axis=-1, keepdims=True), (2 * blk, LANES))

    def b_suffix(slot):
        return _nn(lk_ref[slot], u)

    def b_finish(cs, j, slot):
        rows = pl.ds(pl.multiple_of(j * blk, blk), blk)
        carry = carry_ref[...]
        a = jnp.exp2(lb_ref[slot] + cs + jnp.concatenate([carry] * (blk // LANES), axis=1)).astype(BF16)
        for pr in range(2):
            r2 = slice(2 * pr * blk, 2 * (pr + 1) * blk)
            acc_ref[r2, :] += _nn(a[r2], v_ref[0, rows, pr * LANES:(pr + 1) * LANES])
        carry_ref[...] = carry + tot_ref[slot]

    a_finish(a_logits(i), 0, True)

    def step(t, slot):
        zs = a_logits(i - 1 - t)
        cs = b_suffix(slot)
        a_finish(zs, 1 - slot, False)
        b_finish(cs, i - t, slot)

    def body(tt, c):
        step(2 * tt, 0)
        step(2 * tt + 1, 1)
        return c

    lax.fori_loop(0, i // 2, body, 0)

    @pl.when(i % 2 == 1)
    def _():
        step(i - 1, 0)
        b_finish(b_suffix(1), 0, 1)

    @pl.when(i % 2 == 0)
    def _():
        b_finish(b_suffix(0), 0, 0)

    lo_half = _half_mask((blk, LANES), 0)
    acc = [acc_ref[h * blk:(h + 1) * blk, :] for h in range(N_HEADS)]
    o = jnp.concatenate([jnp.where(lo_half, acc[0], acc[1]), jnp.where(lo_half, acc[2], acc[3])], axis=1)
    _group_norm_store(o_ref, _query_rows(i, blk), gm_ref, o)


def _stick_breaking(proj, gm, blk=256):
    b, s, _ = proj.shape
    stage = lambda w, dt: pltpu.VMEM((2, N_HEADS * blk, w), dt)
    return pl.pallas_call(
        _over_query_blocks(functools.partial(_sb_step, blk=blk), blk),
        grid=(b,),
        in_specs=[pl.BlockSpec((1, s, GROUP_WIDTH), lambda bi: (bi, 0, COL_QA)),
                  pl.BlockSpec((1, s, GROUP_WIDTH), lambda bi: (bi, 0, COL_KA)),
                  pl.BlockSpec((1, s, GROUP_WIDTH), lambda bi: (bi, 0, COL_VA)),
                  pl.BlockSpec((1, GROUP_WIDTH), lambda bi: (0, 0))],
        out_specs=pl.BlockSpec((1, s, GROUP_WIDTH), lambda bi: (bi, 0, 0)),
        out_shape=jax.ShapeDtypeStruct((b, s, GROUP_WIDTH), BF16),
        scratch_shapes=[pltpu.VMEM((N_HEADS * blk, LANES), F32), pltpu.VMEM((N_HEADS * blk, LANES), F32),
                        stage(blk, F32), stage(blk, BF16), stage(LANES, F32)],
        compiler_params=_cparams("parallel"),
        name="stick_breaking",
    )(proj, proj, proj, gm)


POS_LOCAL, POS_BLOCK = 0, 3
SEL_LANE = 8
MASK_BIAS = -2.0 ** 100


def _bf16_pieces(x, n):
    out = []
    r = np.float32(x)
    for _ in range(n):
        p = np.float32(r.astype(ml_dtypes.bfloat16))
        out.append(float(p))
        r = np.float32(r - p)
    return out


def _position_tables(slopes, s, blk, sel_blocks=False):
    coef = np.zeros((N_HEADS, LANES), np.float32)
    for h in range(N_HEADS):
        pieces = _bf16_pieces(slopes[h] * LOG2E, 3)
        coef[h, POS_LOCAL:POS_LOCAL + 3] = pieces
        coef[h, POS_BLOCK:POS_BLOCK + 3] = [blk * p for p in pieces]
    pos = np.zeros((s, LANES), np.float32)
    idx = np.arange(s)
    pos[:, POS_LOCAL:POS_LOCAL + 3] = (idx % blk)[:, None]
    pos[:, POS_BLOCK:POS_BLOCK + 3] = (idx // blk)[:, None]
    if sel_blocks:
        pos[idx, SEL_LANE + idx // SEL_LEN] = 1.0
    return jnp.asarray(coef), jnp.asarray(pos, BF16)


def _diff_step(i, q_ref, k_ref, v_ref, pos_ref, coef_ref, dl_ref, gm_ref, o_ref, qa_ref, mx_ref, acc_ref,
               *, blk, lam_init):
    q = q_ref[0, _query_rows(i, blk), :]
    row = lax.broadcasted_iota(jnp.int32, (blk, blk), 0)
    col = lax.broadcasted_iota(jnp.int32, (blk, blk), 1)
    causal = col <= row
    lane = lax.broadcasted_iota(jnp.int32, (blk, LANES), 1)
    for h in range(N_HEADS):
        slab = q[:, (h // 2) * LANES:(h // 2 + 1) * LANES]
        coef = jnp.broadcast_to(coef_ref[h:h + 1, :], (blk, LANES)).astype(BF16)
        for c in range(2):
            r = slice(((2 * h + c) % 4) * blk, ((2 * h + c) % 4 + 1) * blk)
            qa_ref[h // 2, r, 0:LANES] = jnp.where((lane // DIFF_QK_DIM) == (h % 2) * 2 + c, slab,
                                                   jnp.zeros_like(slab))
            qa_ref[h // 2, r, LANES:2 * LANES] = coef
    ones = jnp.ones((blk, LANES), BF16)

    def per_map(x, fn):
        return jnp.concatenate([fn(x[k * blk:(k + 1) * blk]) for k in range(4)], axis=0)

    def sweep(blocks, first):
        chains = [(pr, pl.ds(pl.multiple_of(j * blk, blk), blk), diag) for pr in range(2) for j, diag in blocks]
        zs = []
        for pr, rows, diag in chains:
            ka = jnp.concatenate([k_ref[0, rows, pr * LANES:(pr + 1) * LANES], pos_ref[rows, :]], axis=1)
            z = _nt(qa_ref[pr], ka)
            if diag:
                z = per_map(z, lambda t: jnp.where(causal, t, NEG_INF))
            zs.append(z)
        m_new, alpha = [None, None], [None, None]
        for pr in range(2):
            zm = None
            for (cp, _, _), z in zip(chains, zs):
                if cp == pr:
                    for g in range(blk // LANES):
                        part = z[:, g * LANES:(g + 1) * LANES]
                        zm = part if zm is None else jnp.maximum(zm, part)
            cm = jnp.broadcast_to(jnp.max(zm, axis=-1, keepdims=True), (4 * blk, LANES))
            if first:
                m_new[pr] = cm
            else:
                m_old = mx_ref[pr]
                m_new[pr] = jnp.maximum(m_old, cm)
                alpha[pr] = jnp.exp2(m_old - m_new[pr])
            mx_ref[pr] = m_new[pr]
        ps = [jnp.exp2(z - jnp.concatenate([m_new[pr]] * (blk // LANES), axis=1)).astype(BF16)
              for (pr, _, _), z in zip(chains, zs)]
        outs = []
        for (pr, rows, diag), p in zip(chains, ps):
            vo = jnp.concatenate([v_ref[0, rows, pr * LANES:(pr + 1) * LANES], ones], axis=1)
            outs.append(_nn(p, vo))
        for pr in range(2):
            tot = functools.reduce(jnp.add, [o for (cp, _, _), o in zip(chains, outs) if cp == pr])
            acc_ref[pr] = tot if first else acc_ref[pr] * jnp.concatenate([alpha[pr]] * 2, axis=1) + tot

    base = (i // 4) * 4

    @pl.when(i % 2 == 1)
    def _():
        sweep([(i - 1, False), (i, True)], True)

    @pl.when(i % 2 == 0)
    def _():
        sweep([(i, True)], True)

    @pl.when(i % 4 >= 2)
    def _():
        sweep([(base, False), (base + 1, False)], False)

    def body(tt, c):
        sweep([(4 * tt + k, False) for k in range(4)], False)
        return c
    lax.fori_loop(0, i // 4, body, 0)

    dl = dl_ref[...]
    s1 = jnp.sum(dl[0:1] * dl[1:2], axis=-1, keepdims=True)
    s2 = jnp.sum(dl[2:3] * dl[3:4], axis=-1, keepdims=True)
    lam = jnp.exp(s1) - jnp.exp(s2) + lam_init
    outs = []
    for h in range(N_HEADS):
        k1 = (2 * h) % 4
        a1 = acc_ref[h // 2, k1 * blk:(k1 + 1) * blk, :]
        a2 = acc_ref[h // 2, (k1 + 1) * blk:(k1 + 2) * blk, :]
        d = a1[:, :LANES] / a1[:, LANES:] - lam * (a2[:, :LANES] / a2[:, LANES:])
        mine = _half_mask(d.shape, h % 2)
        ms = jnp.sum(jnp.where(mine, d * d, 0.0), axis=-1, keepdims=True) * (1.0 / HEAD_DIM)
        outs.append(d * lax.rsqrt(ms + EPS) * (1.0 - lam_init))
    lo_half = _half_mask((blk, LANES), 0)
    o = jnp.concatenate([jnp.where(lo_half, outs[0], outs[1]), jnp.where(lo_half, outs[2], outs[3])], axis=1)
    o_ref[0, _query_rows(i, blk), :] = (o * gm_ref[...]).astype(o_ref.dtype)


def _differential(proj, dl, gm, lam_init, blk=256):
    b, s, _ = proj.shape
    coef, pos = _position_tables(SLOPES_DIFF, s, blk)
    return pl.pallas_call(
        _over_query_blocks(functools.partial(_diff_step, blk=blk, lam_init=lam_init), blk),
        grid=(b,),
        in_specs=[pl.BlockSpec((1, s, GROUP_WIDTH), lambda bi: (bi, 0, COL_QC)),
                  pl.BlockSpec((1, s, GROUP_WIDTH), lambda bi: (bi, 0, COL_KC)),
                  pl.BlockSpec((1, s, GROUP_WIDTH), lambda bi: (bi, 0, COL_VC)),
                  pl.BlockSpec((s, LANES), lambda bi: (0, 0)),
                  pl.BlockSpec((N_HEADS, LANES), lambda bi: (0, 0)),
                  pl.BlockSpec((4, DIFF_QK_DIM), lambda bi: (0, 0)),
                  pl.BlockSpec((1, GROUP_WIDTH), lambda bi: (0, 0))],
        out_specs=pl.BlockSpec((1, s, GROUP_WIDTH), lambda bi: (bi, 0, 0)),
        out_shape=jax.ShapeDtypeStruct((b, s, GROUP_WIDTH), BF16),
        scratch_shapes=[pltpu.VMEM((2, 4 * blk, 2 * LANES), BF16), pltpu.VMEM((2, 4 * blk, LANES), F32),
                        pltpu.VMEM((2, 4 * blk, 2 * LANES), F32)],
        compiler_params=_cparams("parallel"),
        name="differential",
    )(proj, proj, proj, pos, coef, dl, gm)


def _swa_step(i, q_ref, k_ref, v_ref, sink_ref, gm_ref, o_ref, *, blk, sub):
    row = lax.broadcasted_iota(jnp.int32, (blk, 2 * blk), 0)
    col = lax.broadcasted_iota(jnp.int32, (blk, 2 * blk), 1)
    lo_half = _half_mask((blk, LANES), 0)
    blocks = []
    for s_ in range(sub):
        g = i * sub + s_
        first = jnp.maximum(g - 1, 0)
        start = pl.multiple_of(first * blk, blk)
        dist = (row - col) + (g - first) * blk
        mask = (dist >= 0) & (dist < SWA_WINDOW)
        distf = dist.astype(F32)
        q = q_ref[0, _query_rows(g, blk), :]
        kb = k_ref[0, pl.ds(start, 2 * blk), :]
        vb = v_ref[0, pl.ds(start, 2 * blk), :]
        outs = []
        for n, h in enumerate(SWA_HEAD_ORDER):
            slab = q[:, (n // 2) * LANES:(n // 2 + 1) * LANES]
            qm = jnp.where(_half_mask(slab.shape, n % 2), slab, jnp.zeros_like(slab))
            z = _nt(qm, kb) - (SLOPES_SWA[h] * LOG2E) * distf
            z = jnp.where(mask, z, NEG_INF)
            sink = sink_ref[h] * LOG2E
            m = jnp.maximum(jnp.max(z, axis=-1, keepdims=True), sink)
            p = jnp.exp2(z - m)
            l = jnp.sum(p, axis=-1, keepdims=True) + jnp.exp2(sink - m)
            outs.append(_nn(p.astype(BF16), vb) / l)
        blocks.append(jnp.concatenate([jnp.where(lo_half, outs[0], outs[1]), jnp.where(lo_half, outs[2], outs[3])],
                                      axis=1))
    _group_norm_store(o_ref, _query_rows(i, sub * blk), gm_ref, jnp.concatenate(blocks, axis=0))


def _sliding_window(proj, sinks, gm, blk=128, sub=4):
    b, s, _ = proj.shape
    rows = blk * sub
    return pl.pallas_call(
        _over_query_blocks(functools.partial(_swa_step, blk=blk, sub=sub), rows),
        grid=(b,),
        in_specs=[pl.BlockSpec((1, s, GROUP_WIDTH), lambda bi: (bi, 0, COL_QD)),
                  pl.BlockSpec((1, s, LANES), lambda bi: (bi, 0, COL_KD)),
                  pl.BlockSpec((1, s, LANES), lambda bi: (bi, 0, COL_VD)),
                  pl.BlockSpec(memory_space=pltpu.SMEM),
                  pl.BlockSpec((1, GROUP_WIDTH), lambda bi: (0, 0))],
        out_specs=pl.BlockSpec((1, s, GROUP_WIDTH), lambda bi: (bi, 0, 0)),
        out_shape=jax.ShapeDtypeStruct((b, s, GROUP_WIDTH), BF16),
        compiler_params=_cparams("parallel"),
        name="sliding_window",
    )(proj, proj, proj, sinks, gm)


def _nsa_step(i, q_ref, g_ref, cmp_ref, ks_ref, vs_ref, kw_ref, vw_ref, pos_ref, coef_ref, gm_ref, o_ref,
              qa_ref, mx_ref, acc_ref, *, blk, n_cmp, n_sel):
    q = q_ref[0, _query_rows(i, blk), :]
    lane = lax.broadcasted_iota(jnp.int32, (blk, LANES), 1)
    coefs = [jnp.broadcast_to(coef_ref[h:h + 1, :], (blk, LANES)) for h in range(N_HEADS)]
    for h in range(N_HEADS):
        slab = q[:, (h // 2) * LANES:(h // 2 + 1) * LANES]
        own = jnp.where(_half_mask(slab.shape, h % 2), slab, jnp.zeros_like(slab))
        qa_ref[0, h * blk:(h + 1) * blk, 0:LANES] = own
        qa_ref[1, h * blk:(h + 1) * blk, 0:LANES] = own
        qa_ref[0, h * blk:(h + 1) * blk, LANES:2 * LANES] = coefs[h].astype(BF16)
    qs = qa_ref[1, :, 0:LANES]

    tq = i * blk + lax.broadcasted_iota(jnp.int32, (blk, LANES), 0)

    row = lax.broadcasted_iota(jnp.int32, (blk, blk), 0)
    col = lax.broadcasted_iota(jnp.int32, (blk, blk), 1)
    causal = col <= row
    in_window = col > row
    ones = jnp.ones((blk, LANES), BF16)

    k_refs, v_refs = (kw_ref, ks_ref), (vw_ref, vs_ref)

    def process(blocks, first=False):
        groups = [(br, half) for br in sorted({b[0] for b in blocks}) for half in range(2)]
        chains = [(br, pl.ds(pl.multiple_of(j * blk, blk), blk), mask, (br, half))
                  for br, j, mask in blocks for half in range(2)]
        grows = lambda g: slice(2 * g[1] * blk, 2 * (g[1] + 1) * blk)
        zs = []
        for br, rows, mask, g in chains:
            ka = jnp.concatenate([k_refs[br][0, rows, :], pos_ref[rows, :]], axis=1)
            z = _nt(qa_ref[br, grows(g), :], ka)
            if mask is not None:
                z = jnp.concatenate([jnp.where(mask, z[r * blk:(r + 1) * blk], NEG_INF) for r in range(2)], axis=0)
            zs.append(z)
        m_new, alpha = {}, {}
        for g in groups:
            zm = None
            for (_, _, _, cg), z in zip(chains, zs):
                if cg == g:
                    for c in range(blk // LANES):
                        part = z[:, c * LANES:(c + 1) * LANES]
                        zm = part if zm is None else jnp.maximum(zm, part)
            cm = jnp.broadcast_to(jnp.max(zm, axis=-1, keepdims=True), (2 * blk, LANES))
            if first:
                m_new[g] = cm
            else:
                m_old = mx_ref[g[0], grows(g), :]
                m_new[g] = jnp.maximum(m_old, cm)
                alpha[g] = jnp.exp2(m_old - m_new[g])
            mx_ref[g[0], grows(g), :] = m_new[g]
        ps = [jnp.exp2(z - jnp.concatenate([m_new[g]] * (blk // LANES), axis=1)).astype(BF16)
              for (_, _, _, g), z in zip(chains, zs)]
        outs = []
        for (br, rows, mask, g), p in zip(chains, ps):
            vo = jnp.concatenate([v_refs[br][0, rows, :], ones], axis=1)
            outs.append(_nn(p, vo))
        for g in groups:
            tot = functools.reduce(jnp.add, [o for (_, _, _, cg), o in zip(chains, outs) if cg == g])
            if first:
                acc_ref[g[0], grows(g), :] = tot
            else:
                acc_ref[g[0], grows(g), :] = acc_ref[g[0], grows(g), :] * jnp.concatenate([alpha[g]] * 2, axis=1) + tot

    window = [(0, jnp.maximum(i - 2, 0), in_window & (i >= 2)),
              (0, jnp.maximum(i - 1, 0), jnp.broadcast_to(i >= 1, (blk, blk))),
              (0, i, causal)]

    process(window, first=True)
    o_win = acc_ref[0, :, 0:LANES] / acc_ref[0, :, LANES:2 * LANES]

    kc = cmp_ref[0, :, 0:LANES]
    vc = cmp_ref[0, :, LANES:2 * LANES]
    dist_c = tq - (CMP_STRIDE * lane + CMP_LEN - 1)
    ok_c = (dist_c >= 0) & (lane < n_cmp)
    dist_cf = dist_c.astype(F32)
    bias_c = jnp.concatenate([jnp.where(ok_c, (-SLOPES_NSA[h] * LOG2E) * dist_cf, NEG_INF)
                              for h in range(N_HEADS)], axis=0)
    z = _nt(qs, kc) + bias_c
    p = jnp.exp2(z - jnp.max(z, axis=-1, keepdims=True))
    p = p / jnp.sum(p, axis=-1, keepdims=True)
    p = jnp.where(bias_c > 0.5 * NEG_INF, p, 0.0).astype(BF16)
    o_cmp = _nn(p, vc)

    sel_rows = -(-(SEL_LANE + n_sel) // 8) * 8
    rj = lax.broadcasted_iota(jnp.int32, (sel_rows, LANES), 0)
    ri = lax.broadcasted_iota(jnp.int32, (sel_rows, LANES), 1)
    to_sel = jnp.where((ri * CMP_STRIDE) // SEL_LEN + SEL_LANE == rj, 1.0, 0.0).astype(BF16)
    imp = _nt(to_sel, p[0:blk])
    for h in range(1, N_HEADS):
        imp = imp + _nt(to_sel, p[h * blk:(h + 1) * blk])
    srow = lax.broadcasted_iota(jnp.int32, (sel_rows, blk), 0)
    blk_id = srow - SEL_LANE
    cur = (i * blk + lax.broadcasted_iota(jnp.int32, (sel_rows, blk), 1)) // SEL_LEN
    forced = (blk_id == 0) | (blk_id == cur) | (blk_id == cur - 1)
    score = jnp.where(blk_id > cur, NEG_INF, jnp.where(forced, FORCE_SCORE, imp))
    score = jnp.where((blk_id >= 0) & (blk_id < n_sel), score, -3.0e38)
    sel = jnp.zeros((sel_rows, blk), F32)
    srowf = srow.astype(F32)
    for _ in range(min(SEL_TOPN, n_sel)):
        best = jnp.max(score, axis=0, keepdims=True)
        idx = jnp.min(jnp.where(score == best, srowf, float(LANES)), axis=0, keepdims=True)
        hit = srowf == idx
        sel = jnp.where(hit, 1.0, sel)
        score = jnp.where(hit, -3.4e38, score)
    sel_bias = jnp.where(sel > 0.5, 0.0, MASK_BIAS)
    sel_bias = jnp.concatenate([sel_bias, jnp.zeros((LANES - sel_rows, blk), F32)], axis=0).T
    in_sel = (lane >= SEL_LANE) & (lane < SEL_LANE + n_sel)

    for h in range(N_HEADS):
        qa_ref[1, h * blk:(h + 1) * blk, LANES:2 * LANES] = jnp.where(in_sel, sel_bias, coefs[h]).astype(BF16)
    base = (i // 4) * 4

    @pl.when(i % 2 == 1)
    def _():
        process([(1, i - 1, None), (1, i, causal)], first=True)

    @pl.when(i % 2 == 0)
    def _():
        process([(1, i, causal)], first=True)

    @pl.when(i % 4 >= 2)
    def _():
        process([(1, base, None), (1, base + 1, None)])

    def body(tt, c):
        process([(1, 4 * tt + k, None) for k in range(4)])
        return c
    lax.fori_loop(0, i // 4, body, 0)
    o_sel = acc_ref[1, :, 0:LANES] / acc_ref[1, :, LANES:2 * LANES]

    g = jax.nn.sigmoid(g_ref[0, _query_rows(i, blk), :].astype(F32))
    outs = []
    for h in range(N_HEADS):
        r = slice(h * blk, (h + 1) * blk)
        outs.append(g[:, 3 * h:3 * h + 1] * o_cmp[r] + g[:, 3 * h + 1:3 * h + 2] * o_sel[r]
                    + g[:, 3 * h + 2:3 * h + 3] * o_win[r])
    lo_half = _half_mask((blk, LANES), 0)
    o = jnp.concatenate([jnp.where(lo_half, outs[0], outs[1]), jnp.where(lo_half, outs[2], outs[3])], axis=1)
    _group_norm_store(o_ref, _query_rows(i, blk), gm_ref, o)


def _native_sparse(proj, cmp_kv, gm, blk=256):
    b, s, _ = proj.shape
    n_cmp = (s - CMP_LEN) // CMP_STRIDE + 1
    n_sel = s // SEL_LEN
    assert cmp_kv.shape[1] == LANES and n_cmp <= LANES and SEL_LANE + n_sel <= LANES
    assert NSA_WINDOW == 2 * blk
    coef, pos = _position_tables(SLOPES_NSA, s, blk, sel_blocks=True)
    slab = lambda c: pl.BlockSpec((1, s, LANES), lambda bi: (bi, 0, c))
    rows = N_HEADS * blk
    return pl.pallas_call(
        _over_query_blocks(functools.partial(_nsa_step, blk=blk, n_cmp=n_cmp, n_sel=n_sel), blk),
        grid=(b,),
        in_specs=[pl.BlockSpec((1, s, GROUP_WIDTH), lambda bi: (bi, 0, COL_QN)),
                  slab(COL_GN),
                  pl.BlockSpec((1, LANES, 2 * LANES), lambda bi: (bi, 0, 0)),
                  slab(COL_KS), slab(COL_VS), slab(COL_KW), slab(COL_VW),
                  pl.BlockSpec((s, LANES), lambda bi: (0, 0)),
                  pl.BlockSpec((N_HEADS, LANES), lambda bi: (0, 0)),
                  pl.BlockSpec((1, GROUP_WIDTH), lambda bi: (0, 0))],
        out_specs=pl.BlockSpec((1, s, GROUP_WIDTH), lambda bi: (bi, 0, 0)),
        out_shape=jax.ShapeDtypeStruct((b, s, GROUP_WIDTH), BF16),
        scratch_shapes=[pltpu.VMEM((2, rows, 2 * LANES), BF16), pltpu.VMEM((2, rows, LANES), F32),
                        pltpu.VMEM((2, rows, 2 * LANES), F32)],
        compiler_params=_cparams("parallel"),
        name="native_sparse",
    )(proj, proj, cmp_kv, proj, proj, proj, proj, pos, coef, gm)


def _mix_mlp_kernel(x_ref, a_ref, b_ref, c_ref, d_ref, wo_ref, g_ref, wu_ref, wd_ref, gf_ref, o_ref, h_ref,
                    *, final_norm):
    f = pl.program_id(1)

    @pl.when(f == 0)
    def _():
        mix = jnp.concatenate([a_ref[...], b_ref[...], c_ref[...], d_ref[...]], axis=1)
        x1 = x_ref[...] + _nn(mix, wo_ref[...])
        o_ref[...] = x1
        h_ref[...] = (_rms(x1) * g_ref[...]).astype(BF16)

    u = jnp.maximum(_nn(h_ref[...], wu_ref[...]), 0.0)
    o_ref[...] += _nn((u * u).astype(BF16), wd_ref[...])

    if final_norm:
        @pl.when(f == pl.num_programs(1) - 1)
        def _():
            o_ref[...] = _rms(o_ref[...]) * gf_ref[...]


def _mix_mlp(x2, mixes, wo, gain, wu, wd, gfinal, final_norm, tm=1024, tf=1024):
    m = x2.shape[0]
    grp = pl.BlockSpec((tm, GROUP_WIDTH), lambda i, f: (i, 0))
    row = pl.BlockSpec((1, D_MODEL), lambda i, f: (0, 0))
    return pl.pallas_call(
        functools.partial(_mix_mlp_kernel, final_norm=final_norm),
        grid=(m // tm, D_FF // tf),
        in_specs=[pl.BlockSpec((tm, D_MODEL), lambda i, f: (i, 0)), grp, grp, grp, grp,
                  pl.BlockSpec((D_MODEL, D_MODEL), lambda i, f: (0, 0)), row,
                  pl.BlockSpec((D_MODEL, tf), lambda i, f: (0, f)),
                  pl.BlockSpec((tf, D_MODEL), lambda i, f: (f, 0)), row],
        out_specs=pl.BlockSpec((tm, D_MODEL), lambda i, f: (i, 0)),
        out_shape=jax.ShapeDtypeStruct((m, D_MODEL), F32),
        scratch_shapes=[pltpu.VMEM((tm, D_MODEL), BF16)],
        compiler_params=_cparams("parallel", "arbitrary"),
        name="mix_mlp",
    )(x2, *mixes, wo, gain, wu, wd, gfinal)


def _compress_weights(pe_k, w1_k, w2_k, pe_v, w1_v, w2_v):
    half = CMP_STRIDE
    d = HEAD_DIM

    def halves(w1):
        w = w1.reshape(CMP_LEN, d, CMP_HIDDEN)
        return w[:half], w[half:]

    ka, kb = halves(w1_k)
    va, vb = halves(w1_v)
    zero = jnp.zeros_like(ka)

    def merge(wk, wv):
        top = jnp.concatenate([wk, zero], axis=-1)
        bot = jnp.concatenate([zero, wv], axis=-1)
        return jnp.concatenate([top, bot], axis=1).reshape(half * 2 * d, 2 * CMP_HIDDEN).astype(BF16)

    w1a, w1b = merge(ka, va), merge(kb, vb)
    pe = jnp.concatenate([pe_k, pe_v], axis=-1)
    pea = pe[:half].reshape(1, half * 2 * d)
    peb = pe[half:].reshape(1, half * 2 * d)
    zk = jnp.zeros_like(w2_k)
    w2 = jnp.concatenate([jnp.concatenate([w2_k, w2_k, zk, zk], axis=1),
                          jnp.concatenate([zk, zk, w2_v, w2_v], axis=1)], axis=0).astype(BF16)
    return pea, peb, w1a, w1b, w2


def kernel(x, norm_attn, w_in, cmp_pe_k, cmp_w1_k, cmp_w2_k, cmp_pe_v, cmp_w1_v, cmp_w2_v, diff_lq1, diff_lk1,
           diff_lq2, diff_lk2, sinks, g_mix, w_out, norm_mlp, w_up, w_down, norm_final):
    b, s, d = x.shape
    m = b * s
    depth = w_in.shape[0]
    w_in_p = jnp.concatenate([jnp.zeros((depth, d, n), w_in.dtype) if a is None else w_in[:, :, a:a + n]
                              for a, n in _SEGMENTS], axis=2).astype(BF16)
    swa0 = 3 * GROUP_WIDTH
    mix_segs = [(0, swa0)] + [(swa0 + h * HEAD_DIM, HEAD_DIM) for h in SWA_HEAD_ORDER]
    g_mix = jnp.concatenate([g_mix[:, a:a + n] for a, n in mix_segs], axis=1)
    w_out_b = jnp.concatenate([w_out[:, a:a + n] for a, n in mix_segs], axis=1).astype(BF16)
    w_up_b = w_up.astype(BF16)
    w_down_b = w_down.astype(BF16)
    gfinal = norm_final.reshape(1, d)

    colscale = jnp.asarray(_column_scales())

    x2 = x.reshape(m, d)
    for l in range(depth):
        proj = _inproj(x2, norm_attn[l].reshape(1, d), w_in_p[l], colscale).reshape(b, s, NP)
        gm = g_mix[l].reshape(N_HEADS, 1, GROUP_WIDTH)

        chunks = proj[:, :, COL_KVC * LANES:(COL_KVC + 1) * LANES].reshape(b, s // CMP_STRIDE, CMP_STRIDE * LANES)
        cmp_kv = _compress(chunks, *_compress_weights(cmp_pe_k[l], cmp_w1_k[l], cmp_w2_k[l],
                                                      cmp_pe_v[l], cmp_w1_v[l], cmp_w2_v[l]))

        o_sb = _stick_breaking(proj, gm[0])
        o_nsa = _native_sparse(proj, cmp_kv, gm[1])
        lam_init = 0.8 - 0.6 * math.exp(-0.3 * l)
        dl = jnp.stack([diff_lq1[l], diff_lk1[l], diff_lq2[l], diff_lk2[l]]).astype(F32)
        o_diff = _differential(proj, dl, gm[2], lam_init)
        o_swa = _sliding_window(proj, sinks[l].astype(F32), gm[3])

        mixes = [o.reshape(m, GROUP_WIDTH) for o in (o_sb, o_nsa, o_diff, o_swa)]
        x2 = _mix_mlp(x2, mixes, w_out_b[l], norm_mlp[l].reshape(1, d), w_up_b[l], w_down_b[l], gfinal,
                      final_norm=(l == depth - 1))
    return x2.reshape(b, s, d)
```

```python
import functools
import math

import ml_dtypes
import numpy as np
import jax
import jax.numpy as jnp
from jax import lax
from jax.experimental import pallas as pl
from jax.experimental.pallas import tpu as pltpu

F32 = jnp.float32
BF16 = jnp.bfloat16

D_MODEL = 1024
DEPTH = 4
HEAD_DIM = 64
GROUP_WIDTH = 256
N_HEADS = 4
D_FF = 4 * D_MODEL
EPS = 1e-6
NEG_INF = -1e30
LOG2E = math.log2(math.e)
LANES = 128

CMP_LEN = 32
CMP_STRIDE = 16
CMP_HIDDEN = 128
SEL_LEN = 64
SEL_TOPN = 8
FORCE_SCORE = 1e9
NSA_WINDOW = 512
SWA_WINDOW = 128
DIFF_QK_DIM = 32

VMEM_LIMIT = 48 * 1024 * 1024

_ORIG = dict(qa=(0, 256), ka=(256, 256), va=(512, 256), qn=(768, 256), kcn=(1024, 64), vcn=(1088, 64),
             ksn=(1152, 64), vsn=(1216, 64), kwn=(1280, 64), vwn=(1344, 64), gn=(1408, 12),
             qc=(1420, 256), kc=(1676, 256), vc=(1932, 256), qd=(2188, 256), kd=(2444, 128), vd=(2572, 128))


def _layout():
    segs, first = [], {}

    def put(name, lo=0, n=None):
        start, size = _ORIG[name]
        first.setdefault(name, sum(w for _, w in segs))
        segs.append((start + lo, size - lo if n is None else n))

    def zeros(n):
        segs.append((None, n))

    for nm in ("qa", "ka", "va", "qn", "kcn", "vcn"):
        put(nm)
    for nm in ("ksn", "vsn", "kwn", "vwn"):
        put(nm)
        put(nm)
    put("gn")
    zeros(LANES - 12)
    for nm in ("qc", "kc", "vc"):
        put(nm)
    for h in SWA_HEAD_ORDER:
        put("qd", h * HEAD_DIM, HEAD_DIM)
    put("kd")
    put("vd")
    return segs, first


SWA_HEAD_ORDER = (0, 2, 1, 3)
_SEGMENTS, _FIRST = _layout()
NP = sum(n for _, n in _SEGMENTS)


def _column_scales():
    cs = np.ones((1, NP), np.float32)
    for name, dim in (("qa", HEAD_DIM), ("qn", HEAD_DIM), ("qc", DIFF_QK_DIM), ("qd", HEAD_DIM)):
        cs[0, _FIRST[name]:_FIRST[name] + GROUP_WIDTH] = dim ** -0.5 * LOG2E
    return cs


def _col_block(name, width):
    block, rem = divmod(_FIRST[name], width)
    assert rem == 0, (name, width)
    return block


COL_QA, COL_KA, COL_VA, COL_QN = (_col_block(n, GROUP_WIDTH) for n in ("qa", "ka", "va", "qn"))
COL_QC, COL_KC, COL_VC, COL_QD = (_col_block(n, GROUP_WIDTH) for n in ("qc", "kc", "vc", "qd"))
COL_KVC, COL_KS, COL_VS, COL_KW, COL_VW, COL_GN, COL_KD, COL_VD = (
    _col_block(n, LANES) for n in ("kcn", "ksn", "vsn", "kwn", "vwn", "gn", "kd", "vd"))


def _alibi_slopes():
    m = 2.0 ** (-8.0 * np.arange(1, 13) / 12.0)
    m = m.astype(np.float32).reshape(4, 3)
    return [float(v) for v in m[:, 0]], [float(v) for v in m[:, 1]], [float(v) for v in m[:, 2]]


SLOPES_NSA, SLOPES_DIFF, SLOPES_SWA = _alibi_slopes()


def _nt(a, b):
    return lax.dot_general(a, b, (((1,), (1,)), ((), ())), preferred_element_type=F32)


def _nn(a, b):
    return jnp.dot(a, b, preferred_element_type=F32)


def _cparams(*sem):
    return pltpu.CompilerParams(dimension_semantics=sem, vmem_limit_bytes=VMEM_LIMIT)


def _rms(x):
    return x * lax.rsqrt(jnp.mean(x * x, axis=-1, keepdims=True) + EPS)


def _inproj_kernel(x_ref, g_ref, w_ref, cs_ref, o_ref):
    h = _rms(x_ref[...]) * g_ref[...]
    o_ref[...] = (_nn(h.astype(BF16), w_ref[...]) * cs_ref[...]).astype(BF16)


def _inproj(x2, gain, w, colscale, tm=512):
    m = x2.shape[0]
    return pl.pallas_call(
        _inproj_kernel,
        grid=(m // tm,),
        in_specs=[pl.BlockSpec((tm, D_MODEL), lambda i: (i, 0)),
                  pl.BlockSpec((1, D_MODEL), lambda i: (0, 0)),
                  pl.BlockSpec((D_MODEL, NP), lambda i: (0, 0)),
                  pl.BlockSpec((1, NP), lambda i: (0, 0))],
        out_specs=pl.BlockSpec((tm, NP), lambda i: (i, 0)),
        out_shape=jax.ShapeDtypeStruct((m, NP), BF16),
        compiler_params=_cparams("parallel"),
        name="inproj",
    )(x2, gain, w, colscale)


def _compress_kernel(c_ref, pea_ref, peb_ref, w1a_ref, w1b_ref, w2_ref, o_ref):
    c = c_ref[0].astype(F32)
    p = _nn((c + pea_ref[...]).astype(BF16), w1a_ref[...])
    r = _nn((c + peb_ref[...]).astype(BF16), w1b_ref[...])
    n = c.shape[0]
    r_next = pltpu.roll(r, n - 1, 0)
    hid = jax.nn.gelu(p + r_next)
    o_ref[0] = _nn(hid.astype(BF16), w2_ref[...]).astype(BF16)


def _compress(chunks, pea, peb, w1a, w1b, w2):
    b, n, w = chunks.shape
    full = lambda a: pl.BlockSpec(a.shape, lambda i: (0,) * a.ndim)
    return pl.pallas_call(
        _compress_kernel,
        grid=(b,),
        in_specs=[pl.BlockSpec((1, n, w), lambda i: (i, 0, 0)), full(pea), full(peb), full(w1a), full(w1b),
                  full(w2)],
        out_specs=pl.BlockSpec((1, n, 2 * LANES), lambda i: (i, 0, 0)),
        out_shape=jax.ShapeDtypeStruct((b, n, 2 * LANES), BF16),
        compiler_params=_cparams("parallel"),
        name="nsa_compress",
    )(chunks, pea, peb, w1a, w1b, w2)


def _half_mask(shape, half):
    lane = lax.broadcasted_iota(jnp.int32, shape, len(shape) - 1)
    return (lane // HEAD_DIM) == half


def _group_norm_store(o_ref, rows, gm_ref, o):
    o_ref[0, rows, :] = (_rms(o) * gm_ref[...]).astype(o_ref.dtype)


def _query_rows(i, blk):
    return pl.ds(pl.multiple_of(i * blk, blk), blk)


def _over_query_blocks(step, blk):
    def kernel(*refs):
        def body(i, c):
            step(i, *refs)
            return c
        lax.fori_loop(0, refs[0].shape[1] // blk, body, 0)
    return kernel


def _sb_step(i, q_ref, k_ref, v_ref, gm_ref, o_ref, acc_ref, carry_ref, lb_ref, lk_ref, tot_ref, *, blk):
    q = q_ref[0, _query_rows(i, blk), :]
    row = lax.broadcasted_iota(jnp.int32, (blk, blk), 0)
    col = lax.broadcasted_iota(jnp.int32, (blk, blk), 1)
    past = col < row
    u = jnp.where(row > col, 1.0, 0.0).astype(BF16)
    qm = []
    for h in range(N_HEADS):
        slab = q[:, (h // 2) * LANES:(h // 2 + 1) * LANES]
        qm.append(jnp.where(_half_mask(slab.shape, h % 2), slab, jnp.zeros_like(slab)))
    qm = [jnp.concatenate(qm[0:2], axis=0), jnp.concatenate(qm[2:4], axis=0)]
    acc_ref[...] = jnp.zeros_like(acc_ref)
    carry_ref[...] = jnp.zeros_like(carry_ref)

    def per_head(x, fn):
        return jnp.concatenate([fn(x[k * blk:(k + 1) * blk]) for k in range(x.shape[0] // blk)], axis=0)

    def a_logits(j):
        rows = pl.ds(pl.multiple_of(j * blk, blk), blk)
        return [_nt(qm[pr], k_ref[0, rows, pr * LANES:(pr + 1) * LANES]) for pr in range(2)]

    def a_finish(zs, slot, diag):
        for pr, z in enumerate(zs):
            r2 = slice(2 * pr * blk, 2 * (pr + 1) * blk)
            sp = jnp.log2(1.0 + jnp.exp2(-jnp.abs(z)))
            lb = jnp.minimum(z, 0.0) - sp
            lk = lb - z
            if diag:
                lk = per_head(lk, lambda t: jnp.where(past, t, 0.0))
                lb = per_head(lb, lambda t: jnp.where(past, t, NEG_INF))
            lb_ref[slot, r2, :] = lb
            lk_ref[slot, r2, :] = lk.astype(BF16)
            tot_ref[slot, r2, :] = jnp.broadcast_to(jnp.sum(lk, axis=-1, keepdims=True), (2 * blk, LANES))

    def b_suffix(slot):
        return _nn(lk_ref[slot], u)

    def b_finish(cs, j, slot):
        rows = pl.ds(pl.multiple_of(j * blk, blk), blk)
        carry = carry_ref[...]
        a = jnp.exp2(lb_ref[slot] + cs + jnp.concatenate([carry] * (blk // LANES), axis=1)).astype(BF16)
        for pr in range(2):
            r2 = slice(2 * pr * blk, 2 * (pr + 1) * blk)
            acc_ref[r2, :] += _nn(a[r2], v_ref[0, rows, pr * LANES:(pr + 1) * LANES])
        carry_ref[...] = carry + tot_ref[slot]

    a_finish(a_logits(i), 0, True)

    def step(t, slot):
        zs = a_logits(i - 1 - t)
        cs = b_suffix(slot)
        b_finish(cs, i - t, slot)
        a_finish(zs, 1 - slot, False)

    def body(tt, c):
        step(2 * tt, 0)
        step(2 * tt + 1, 1)
        return c

    lax.fori_loop(0, i // 2, body, 0)

    @pl.when(i % 2 == 1)
    def _():
        step(i - 1, 0)
        b_finish(b_suffix(1), 0, 1)

    @pl.when(i % 2 == 0)
    def _():
        b_finish(b_suffix(0), 0, 0)

    lo_half = _half_mask((blk, LANES), 0)
    acc = [acc_ref[h * blk:(h + 1) * blk, :] for h in range(N_HEADS)]
    o = jnp.concatenate([jnp.where(lo_half, acc[0], acc[1]), jnp.where(lo_half, acc[2], acc[3])], axis=1)
    _group_norm_store(o_ref, _query_rows(i, blk), gm_ref, o)


def _stick_breaking(proj, gm, blk=256):
    b, s, _ = proj.shape
    stage = lambda w, dt: pltpu.VMEM((2, N_HEADS * blk, w), dt)
    return pl.pallas_call(
        _over_query_blocks(functools.partial(_sb_step, blk=blk), blk),
        grid=(b,),
        in_specs=[pl.BlockSpec((1, s, GROUP_WIDTH), lambda bi: (bi, 0, COL_QA)),
                  pl.BlockSpec((1, s, GROUP_WIDTH), lambda bi: (bi, 0, COL_KA)),
                  pl.BlockSpec((1, s, GROUP_WIDTH), lambda bi: (bi, 0, COL_VA)),
                  pl.BlockSpec((1, GROUP_WIDTH), lambda bi: (0, 0))],
        out_specs=pl.BlockSpec((1, s, GROUP_WIDTH), lambda bi: (bi, 0, 0)),
        out_shape=jax.ShapeDtypeStruct((b, s, GROUP_WIDTH), BF16),
        scratch_shapes=[pltpu.VMEM((N_HEADS * blk, LANES), F32), pltpu.VMEM((N_HEADS * blk, LANES), F32),
                        stage(blk, F32), stage(blk, BF16), stage(LANES, F32)],
        compiler_params=_cparams("parallel"),
        name="stick_breaking",
    )(proj, proj, proj, gm)


POS_LOCAL, POS_BLOCK = 0, 3
SEL_LANE = 8
MASK_BIAS = -2.0 ** 100


def _bf16_pieces(x, n):
    out = []
    r = np.float32(x)
    for _ in range(n):
        p = np.float32(r.astype(ml_dtypes.bfloat16))
        out.append(float(p))
        r = np.float32(r - p)
    return out


def _position_tables(slopes, s, blk, sel_blocks=False):
    coef = np.zeros((N_HEADS, LANES), np.float32)
    for h in range(N_HEADS):
        pieces = _bf16_pieces(slopes[h] * LOG2E, 3)
        coef[h, POS_LOCAL:POS_LOCAL + 3] = pieces
        coef[h, POS_BLOCK:POS_BLOCK + 3] = [blk * p for p in pieces]
    pos = np.zeros((s, LANES), np.float32)
    idx = np.arange(s)
    pos[:, POS_LOCAL:POS_LOCAL + 3] = (idx % blk)[:, None]
    pos[:, POS_BLOCK:POS_BLOCK + 3] = (idx // blk)[:, None]
    if sel_blocks:
        pos[idx, SEL_LANE + idx // SEL_LEN] = 1.0
    return jnp.asarray(coef), jnp.asarray(pos, BF16)


def _diff_step(i, q_ref, k_ref, v_ref, pos_ref, coef_ref, dl_ref, gm_ref, o_ref, qa_ref, mx_ref, acc_ref,
               *, blk, lam_init):
    q = q_ref[0, _query_rows(i, blk), :]
    row = lax.broadcasted_iota(jnp.int32, (blk, blk), 0)
    col = lax.broadcasted_iota(jnp.int32, (blk, blk), 1)
    causal = col <= row
    lane = lax.broadcasted_iota(jnp.int32, (blk, LANES), 1)
    for h in range(N_HEADS):
        slab = q[:, (h // 2) * LANES:(h // 2 + 1) * LANES]
        coef = jnp.broadcast_to(coef_ref[h:h + 1, :], (blk, LANES)).astype(BF16)
        for c in range(2):
            r = slice(((2 * h + c) % 4) * blk, ((2 * h + c) % 4 + 1) * blk)
            qa_ref[h // 2, r, 0:LANES] = jnp.where((lane // DIFF_QK_DIM) == (h % 2) * 2 + c, slab,
                                                   jnp.zeros_like(slab))
            qa_ref[h // 2, r, LANES:2 * LANES] = coef
    ones = jnp.ones((blk, LANES), BF16)

    def per_map(x, fn):
        return jnp.concatenate([fn(x[k * blk:(k + 1) * blk]) for k in range(4)], axis=0)

    def sweep(blocks, first):
        chains = [(pr, pl.ds(pl.multiple_of(j * blk, blk), blk), diag) for pr in range(2) for j, diag in blocks]
        zs = []
        for pr, rows, diag in chains:
            ka = jnp.concatenate([k_ref[0, rows, pr * LANES:(pr + 1) * LANES], pos_ref[rows, :]], axis=1)
            z = _nt(qa_ref[pr], ka)
            if diag:
                z = per_map(z, lambda t: jnp.where(causal, t, NEG_INF))
            zs.append(z)
        m_new, alpha = [None, None], [None, None]
        for pr in range(2):
            zm = None
            for (cp, _, _), z in zip(chains, zs):
                if cp == pr:
                    for g in range(blk // LANES):
                        part = z[:, g * LANES:(g + 1) * LANES]
                        zm = part if zm is None else jnp.maximum(zm, part)
            cm = jnp.broadcast_to(jnp.max(zm, axis=-1, keepdims=True), (4 * blk, LANES))
            if first:
                m_new[pr] = cm
            else:
                m_old = mx_ref[pr]
                m_new[pr] = jnp.maximum(m_old, cm)
                alpha[pr] = jnp.exp2(m_old - m_new[pr])
            mx_ref[pr] = m_new[pr]
        ps = [jnp.exp2(z - jnp.concatenate([m_new[pr]] * (blk // LANES), axis=1)).astype(BF16)
              for (pr, _, _), z in zip(chains, zs)]
        outs = []
        for (pr, rows, diag), p in zip(chains, ps):
            vo = jnp.concatenate([v_ref[0, rows, pr * LANES:(pr + 1) * LANES], ones], axis=1)
            outs.append(_nn(p, vo))
        for pr in range(2):
            tot = functools.reduce(jnp.add, [o for (cp, _, _), o in zip(chains, outs) if cp == pr])
            acc_ref[pr] = tot if first else acc_ref[pr] * jnp.concatenate([alpha[pr]] * 2, axis=1) + tot

    base = (i // 4) * 4

    @pl.when(i % 2 == 1)
    def _():
        sweep([(i - 1, False), (i, True)], True)

    @pl.when(i % 2 == 0)
    def _():
        sweep([(i, True)], True)

    @pl.when(i % 4 >= 2)
    def _():
        sweep([(base, False), (base + 1, False)], False)

    def body(tt, c):
        sweep([(4 * tt + k, False) for k in range(4)], False)
        return c
    lax.fori_loop(0, i // 4, body, 0)

    dl = dl_ref[...]
    s1 = jnp.sum(dl[0:1] * dl[1:2], axis=-1, keepdims=True)
    s2 = jnp.sum(dl[2:3] * dl[3:4], axis=-1, keepdims=True)
    lam = jnp.exp(s1) - jnp.exp(s2) + lam_init
    outs = []
    for h in range(N_HEADS):
        k1 = (2 * h) % 4
        a1 = acc_ref[h // 2, k1 * blk:(k1 + 1) * blk, :]
        a2 = acc_ref[h // 2, (k1 + 1) * blk:(k1 + 2) * blk, :]
        d = a1[:, :LANES] / a1[:, LANES:] - lam * (a2[:, :LANES] / a2[:, LANES:])
        mine = _half_mask(d.shape, h % 2)
        ms = jnp.sum(jnp.where(mine, d * d, 0.0), axis=-1, keepdims=True) * (1.0 / HEAD_DIM)
        outs.append(d * lax.rsqrt(ms + EPS) * (1.0 - lam_init))
    lo_half = _half_mask((blk, LANES), 0)
    o = jnp.concatenate([jnp.where(lo_half, outs[0], outs[1]), jnp.where(lo_half, outs[2], outs[3])], axis=1)
    o_ref[0, _query_rows(i, blk), :] = (o * gm_ref[...]).astype(o_ref.dtype)


def _differential(proj, dl, gm, lam_init, blk=256):
    b, s, _ = proj.shape
    coef, pos = _position_tables(SLOPES_DIFF, s, blk)
    return pl.pallas_call(
        _over_query_blocks(functools.partial(_diff_step, blk=blk, lam_init=lam_init), blk),
        grid=(b,),
        in_specs=[pl.BlockSpec((1, s, GROUP_WIDTH), lambda bi: (bi, 0, COL_QC)),
                  pl.BlockSpec((1, s, GROUP_WIDTH), lambda bi: (bi, 0, COL_KC)),
                  pl.BlockSpec((1, s, GROUP_WIDTH), lambda bi: (bi, 0, COL_VC)),
                  pl.BlockSpec((s, LANES), lambda bi: (0, 0)),
                  pl.BlockSpec((N_HEADS, LANES), lambda bi: (0, 0)),
                  pl.BlockSpec((4, DIFF_QK_DIM), lambda bi: (0, 0)),
                  pl.BlockSpec((1, GROUP_WIDTH), lambda bi: (0, 0))],
        out_specs=pl.BlockSpec((1, s, GROUP_WIDTH), lambda bi: (bi, 0, 0)),
        out_shape=jax.ShapeDtypeStruct((b, s, GROUP_WIDTH), BF16),
        scratch_shapes=[pltpu.VMEM((2, 4 * blk, 2 * LANES), BF16), pltpu.VMEM((2, 4 * blk, LANES), F32),
                        pltpu.VMEM((2, 4 * blk, 2 * LANES), F32)],
        compiler_params=_cparams("parallel"),
        name="differential",
    )(proj, proj, proj, pos, coef, dl, gm)


def _swa_step(i, q_ref, k_ref, v_ref, sink_ref, gm_ref, o_ref, *, blk, sub):
    row = lax.broadcasted_iota(jnp.int32, (blk, 2 * blk), 0)
    col = lax.broadcasted_iota(jnp.int32, (blk, 2 * blk), 1)
    lo_half = _half_mask((blk, LANES), 0)
    blocks = []
    for s_ in range(sub):
        g = i * sub + s_
        first = jnp.maximum(g - 1, 0)
        start = pl.multiple_of(first * blk, blk)
        dist = (row - col) + (g - first) * blk
        mask = (dist >= 0) & (dist < SWA_WINDOW)
        distf = dist.astype(F32)
        q = q_ref[0, _query_rows(g, blk), :]
        kb = k_ref[0, pl.ds(start, 2 * blk), :]
        vb = v_ref[0, pl.ds(start, 2 * blk), :]
        outs = []
        for n, h in enumerate(SWA_HEAD_ORDER):
            slab = q[:, (n // 2) * LANES:(n // 2 + 1) * LANES]
            qm = jnp.where(_half_mask(slab.shape, n % 2), slab, jnp.zeros_like(slab))
            z = _nt(qm, kb) - (SLOPES_SWA[h] * LOG2E) * distf
            z = jnp.where(mask, z, NEG_INF)
            sink = sink_ref[h] * LOG2E
            m = jnp.maximum(jnp.max(z, axis=-1, keepdims=True), sink)
            p = jnp.exp2(z - m)
            l = jnp.sum(p, axis=-1, keepdims=True) + jnp.exp2(sink - m)
            outs.append(_nn(p.astype(BF16), vb) / l)
        blocks.append(jnp.concatenate([jnp.where(lo_half, outs[0], outs[1]), jnp.where(lo_half, outs[2], outs[3])],
                                      axis=1))
    _group_norm_store(o_ref, _query_rows(i, sub * blk), gm_ref, jnp.concatenate(blocks, axis=0))


def _sliding_window(proj, sinks, gm, blk=128, sub=4):
    b, s, _ = proj.shape
    rows = blk * sub
    return pl.pallas_call(
        _over_query_blocks(functools.partial(_swa_step, blk=blk, sub=sub), rows),
        grid=(b,),
        in_specs=[pl.BlockSpec((1, s, GROUP_WIDTH), lambda bi: (bi, 0, COL_QD)),
                  pl.BlockSpec((1, s, LANES), lambda bi: (bi, 0, COL_KD)),
                  pl.BlockSpec((1, s, LANES), lambda bi: (bi, 0, COL_VD)),
                  pl.BlockSpec(memory_space=pltpu.SMEM),
                  pl.BlockSpec((1, GROUP_WIDTH), lambda bi: (0, 0))],
        out_specs=pl.BlockSpec((1, s, GROUP_WIDTH), lambda bi: (bi, 0, 0)),
        out_shape=jax.ShapeDtypeStruct((b, s, GROUP_WIDTH), BF16),
        compiler_params=_cparams("parallel"),
        name="sliding_window",
    )(proj, proj, proj, sinks, gm)


def _nsa_step(i, q_ref, g_ref, cmp_ref, ks_ref, vs_ref, kw_ref, vw_ref, pos_ref, coef_ref, gm_ref, o_ref,
              qa_ref, mx_ref, acc_ref, *, blk, n_cmp, n_sel):
    q = q_ref[0, _query_rows(i, blk), :]
    lane = lax.broadcasted_iota(jnp.int32, (blk, LANES), 1)
    coefs = [jnp.broadcast_to(coef_ref[h:h + 1, :], (blk, LANES)) for h in range(N_HEADS)]
    for h in range(N_HEADS):
        slab = q[:, (h // 2) * LANES:(h // 2 + 1) * LANES]
        own = jnp.where(_half_mask(slab.shape, h % 2), slab, jnp.zeros_like(slab))
        qa_ref[0, h * blk:(h + 1) * blk, 0:LANES] = own
        qa_ref[1, h * blk:(h + 1) * blk, 0:LANES] = own
        qa_ref[0, h * blk:(h + 1) * blk, LANES:2 * LANES] = coefs[h].astype(BF16)
    qs = qa_ref[1, :, 0:LANES]

    tq = i * blk + lax.broadcasted_iota(jnp.int32, (blk, LANES), 0)

    row = lax.broadcasted_iota(jnp.int32, (blk, blk), 0)
    col = lax.broadcasted_iota(jnp.int32, (blk, blk), 1)
    causal = col <= row
    in_window = col > row
    ones = jnp.ones((blk, LANES), BF16)

    k_refs, v_refs = (kw_ref, ks_ref), (vw_ref, vs_ref)

    def process(blocks, first=False):
        groups = [(br, half) for br in sorted({b[0] for b in blocks}) for half in range(2)]
        chains = [(br, pl.ds(pl.multiple_of(j * blk, blk), blk), mask, (br, half))
                  for br, j, mask in blocks for half in range(2)]
        grows = lambda g: slice(2 * g[1] * blk, 2 * (g[1] + 1) * blk)
        zs = []
        for br, rows, mask, g in chains:
            ka = jnp.concatenate([k_refs[br][0, rows, :], pos_ref[rows, :]], axis=1)
            z = _nt(qa_ref[br, grows(g), :], ka)
            if mask is not None:
                z = jnp.concatenate([jnp.where(mask, z[r * blk:(r + 1) * blk], NEG_INF) for r in range(2)], axis=0)
            zs.append(z)
        m_new, alpha = {}, {}
        for g in groups:
            zm = None
            for (_, _, _, cg), z in zip(chains, zs):
                if cg == g:
                    for c in range(blk // LANES):
                        part = z[:, c * LANES:(c + 1) * LANES]
                        zm = part if zm is None else jnp.maximum(zm, part)
            cm = jnp.broadcast_to(jnp.max(zm, axis=-1, keepdims=True), (2 * blk, LANES))
            if first:
                m_new[g] = cm
            else:
                m_old = mx_ref[g[0], grows(g), :]
                m_new[g] = jnp.maximum(m_old, cm)
                alpha[g] = jnp.exp2(m_old - m_new[g])
            mx_ref[g[0], grows(g), :] = m_new[g]
        ps = [jnp.exp2(z - jnp.concatenate([m_new[g]] * (blk // LANES), axis=1)).astype(BF16)
              for (_, _, _, g), z in zip(chains, zs)]
        outs = []
        for (br, rows, mask, g), p in zip(chains, ps):
            vo = jnp.concatenate([v_refs[br][0, rows, :], ones], axis=1)
            outs.append(_nn(p, vo))
        for g in groups:
            tot = functools.reduce(jnp.add, [o for (_, _, _, cg), o in zip(chains, outs) if cg == g])
            if first:
                acc_ref[g[0], grows(g), :] = tot
            else:
                acc_ref[g[0], grows(g), :] = acc_ref[g[0], grows(g), :] * jnp.concatenate([alpha[g]] * 2, axis=1) + tot

    window = [(0, jnp.maximum(i - 2, 0), in_window & (i >= 2)),
              (0, jnp.maximum(i - 1, 0), jnp.broadcast_to(i >= 1, (blk, blk))),
              (0, i, causal)]

    kc = cmp_ref[0, :, 0:LANES]
    vc = cmp_ref[0, :, LANES:2 * LANES]
    dist_c = tq - (CMP_STRIDE * lane + CMP_LEN - 1)
    ok_c = (dist_c >= 0) & (lane < n_cmp)
    dist_cf = dist_c.astype(F32)
    bias_c = jnp.concatenate([jnp.where(ok_c, (-SLOPES_NSA[h] * LOG2E) * dist_cf, NEG_INF)
                              for h in range(N_HEADS)], axis=0)
    z = _nt(qs, kc) + bias_c
    p = jnp.exp2(z - jnp.max(z, axis=-1, keepdims=True))
    p = p / jnp.sum(p, axis=-1, keepdims=True)
    p = jnp.where(bias_c > 0.5 * NEG_INF, p, 0.0).astype(BF16)
    o_cmp = _nn(p, vc)

    sel_rows = -(-(SEL_LANE + n_sel) // 8) * 8
    rj = lax.broadcasted_iota(jnp.int32, (sel_rows, LANES), 0)
    ri = lax.broadcasted_iota(jnp.int32, (sel_rows, LANES), 1)
    to_sel = jnp.where((ri * CMP_STRIDE) // SEL_LEN + SEL_LANE == rj, 1.0, 0.0).astype(BF16)
    imp = _nt(to_sel, p[0:blk])
    for h in range(1, N_HEADS):
        imp = imp + _nt(to_sel, p[h * blk:(h + 1) * blk])
    srow = lax.broadcasted_iota(jnp.int32, (sel_rows, blk), 0)
    blk_id = srow - SEL_LANE
    cur = (i * blk + lax.broadcasted_iota(jnp.int32, (sel_rows, blk), 1)) // SEL_LEN
    forced = (blk_id == 0) | (blk_id == cur) | (blk_id == cur - 1)
    score = jnp.where(blk_id > cur, NEG_INF, jnp.where(forced, FORCE_SCORE, imp))
    score = jnp.where((blk_id >= 0) & (blk_id < n_sel), score, -3.0e38)
    sel = jnp.zeros((sel_rows, blk), F32)
    srowf = srow.astype(F32)
    for _ in range(min(SEL_TOPN, n_sel)):
        best = jnp.max(score, axis=0, keepdims=True)
        idx = jnp.min(jnp.where(score == best, srowf, float(LANES)), axis=0, keepdims=True)
        hit = srowf == idx
        sel = jnp.where(hit, 1.0, sel)
        score = jnp.where(hit, -3.4e38, score)
    sel_bias = jnp.where(sel > 0.5, 0.0, MASK_BIAS)
    sel_bias = jnp.concatenate([sel_bias, jnp.zeros((LANES - sel_rows, blk), F32)], axis=0).T
    in_sel = (lane >= SEL_LANE) & (lane < SEL_LANE + n_sel)

    process(window, first=True)
    o_win = acc_ref[0, :, 0:LANES] / acc_ref[0, :, LANES:2 * LANES]

    for h in range(N_HEADS):
        qa_ref[1, h * blk:(h + 1) * blk, LANES:2 * LANES] = jnp.where(in_sel, sel_bias, coefs[h]).astype(BF16)
    base = (i // 4) * 4

    @pl.when(i % 2 == 1)
    def _():
        process([(1, i - 1, None), (1, i, causal)], first=True)

    @pl.when(i % 2 == 0)
    def _():
        process([(1, i, causal)], first=True)

    @pl.when(i % 4 >= 2)
    def _():
        process([(1, base, None), (1, base + 1, None)])

    def body(tt, c):
        process([(1, 4 * tt + k, None) for k in range(4)])
        return c
    lax.fori_loop(0, i // 4, body, 0)
    o_sel = acc_ref[1, :, 0:LANES] / acc_ref[1, :, LANES:2 * LANES]

    g = jax.nn.sigmoid(g_ref[0, _query_rows(i, blk), :].astype(F32))
    outs = []
    for h in range(N_HEADS):
        r = slice(h * blk, (h + 1) * blk)
        outs.append(g[:, 3 * h:3 * h + 1] * o_cmp[r] + g[:, 3 * h + 1:3 * h + 2] * o_sel[r]
                    + g[:, 3 * h + 2:3 * h + 3] * o_win[r])
    lo_half = _half_mask((blk, LANES), 0)
    o = jnp.concatenate([jnp.where(lo_half, outs[0], outs[1]), jnp.where(lo_half, outs[2], outs[3])], axis=1)
    _group_norm_store(o_ref, _query_rows(i, blk), gm_ref, o)


def _native_sparse(proj, cmp_kv, gm, blk=256):
    b, s, _ = proj.shape
    n_cmp = (s - CMP_LEN) // CMP_STRIDE + 1
    n_sel = s // SEL_LEN
    assert cmp_kv.shape[1] == LANES and n_cmp <= LANES and SEL_LANE + n_sel <= LANES
    assert NSA_WINDOW == 2 * blk
    coef, pos = _position_tables(SLOPES_NSA, s, blk, sel_blocks=True)
    slab = lambda c: pl.BlockSpec((1, s, LANES), lambda bi: (bi, 0, c))
    rows = N_HEADS * blk
    return pl.pallas_call(
        _over_query_blocks(functools.partial(_nsa_step, blk=blk, n_cmp=n_cmp, n_sel=n_sel), blk),
        grid=(b,),
        in_specs=[pl.BlockSpec((1, s, GROUP_WIDTH), lambda bi: (bi, 0, COL_QN)),
                  slab(COL_GN),
                  pl.BlockSpec((1, LANES, 2 * LANES), lambda bi: (bi, 0, 0)),
                  slab(COL_KS), slab(COL_VS), slab(COL_KW), slab(COL_VW),
                  pl.BlockSpec((s, LANES), lambda bi: (0, 0)),
                  pl.BlockSpec((N_HEADS, LANES), lambda bi: (0, 0)),
                  pl.BlockSpec((1, GROUP_WIDTH), lambda bi: (0, 0))],
        out_specs=pl.BlockSpec((1, s, GROUP_WIDTH), lambda bi: (bi, 0, 0)),
        out_shape=jax.ShapeDtypeStruct((b, s, GROUP_WIDTH), BF16),
        scratch_shapes=[pltpu.VMEM((2, rows, 2 * LANES), BF16), pltpu.VMEM((2, rows, LANES), F32),
                        pltpu.VMEM((2, rows, 2 * LANES), F32)],
        compiler_params=_cparams("parallel"),
        name="native_sparse",
    )(proj, proj, cmp_kv, proj, proj, proj, proj, pos, coef, gm)


def _mix_mlp_kernel(x_ref, a_ref, b_ref, c_ref, d_ref, wo_ref, g_ref, wu_ref, wd_ref, gf_ref, o_ref, h_ref,
                    *, final_norm):
    f = pl.program_id(1)

    @pl.when(f == 0)
    def _():
        mix = jnp.concatenate([a_ref[...], b_ref[...], c_ref[...], d_ref[...]], axis=1)
        x1 = x_ref[...] + _nn(mix, wo_ref[...])
        o_ref[...] = x1
        h_ref[...] = (_rms(x1) * g_ref[...]).astype(BF16)

    u = jnp.maximum(_nn(h_ref[...], wu_ref[...]), 0.0)
    o_ref[...] += _nn((u * u).astype(BF16), wd_ref[...])

    if final_norm:
        @pl.when(f == pl.num_programs(1) - 1)
        def _():
            o_ref[...] = _rms(o_ref[...]) * gf_ref[...]


def _mix_mlp(x2, mixes, wo, gain, wu, wd, gfinal, final_norm, tm=1024, tf=1024):
    m = x2.shape[0]
    grp = pl.BlockSpec((tm, GROUP_WIDTH), lambda i, f: (i, 0))
    row = pl.BlockSpec((1, D_MODEL), lambda i, f: (0, 0))
    return pl.pallas_call(
        functools.partial(_mix_mlp_kernel, final_norm=final_norm),
        grid=(m // tm, D_FF // tf),
        in_specs=[pl.BlockSpec((tm, D_MODEL), lambda i, f: (i, 0)), grp, grp, grp, grp,
                  pl.BlockSpec((D_MODEL, D_MODEL), lambda i, f: (0, 0)), row,
                  pl.BlockSpec((D_MODEL, tf), lambda i, f: (0, f)),
                  pl.BlockSpec((tf, D_MODEL), lambda i, f: (f, 0)), row],
        out_specs=pl.BlockSpec((tm, D_MODEL), lambda i, f: (i, 0)),
        out_shape=jax.ShapeDtypeStruct((m, D_MODEL), F32),
        scratch_shapes=[pltpu.VMEM((tm, D_MODEL), BF16)],
        compiler_params=_cparams("parallel", "arbitrary"),
        name="mix_mlp",
    )(x2, *mixes, wo, gain, wu, wd, gfinal)


def _compress_weights(pe_k, w1_k, w2_k, pe_v, w1_v, w2_v):
    half = CMP_STRIDE
    d = HEAD_DIM

    def halves(w1):
        w = w1.reshape(CMP_LEN, d, CMP_HIDDEN)
        return w[:half], w[half:]

    ka, kb = halves(w1_k)
    va, vb = halves(w1_v)
    zero = jnp.zeros_like(ka)

    def merge(wk, wv):
        top = jnp.concatenate([wk, zero], axis=-1)
        bot = jnp.concatenate([zero, wv], axis=-1)
        return jnp.concatenate([top, bot], axis=1).reshape(half * 2 * d, 2 * CMP_HIDDEN).astype(BF16)

    w1a, w1b = merge(ka, va), merge(kb, vb)
    pe = jnp.concatenate([pe_k, pe_v], axis=-1)
    pea = pe[:half].reshape(1, half * 2 * d)
    peb = pe[half:].reshape(1, half * 2 * d)
    zk = jnp.zeros_like(w2_k)
    w2 = jnp.concatenate([jnp.concatenate([w2_k, w2_k, zk, zk], axis=1),
                          jnp.concatenate([zk, zk, w2_v, w2_v], axis=1)], axis=0).astype(BF16)
    return pea, peb, w1a, w1b, w2


def kernel(x, norm_attn, w_in, cmp_pe_k, cmp_w1_k, cmp_w2_k, cmp_pe_v, cmp_w1_v, cmp_w2_v, diff_lq1, diff_lk1,
           diff_lq2, diff_lk2, sinks, g_mix, w_out, norm_mlp, w_up, w_down, norm_final):
    b, s, d = x.shape
    m = b * s
    depth = w_in.shape[0]
    w_in_p = jnp.concatenate([jnp.zeros((depth, d, n), w_in.dtype) if a is None else w_in[:, :, a:a + n]
                              for a, n in _SEGMENTS], axis=2).astype(BF16)
    swa0 = 3 * GROUP_WIDTH
    mix_segs = [(0, swa0)] + [(swa0 + h * HEAD_DIM, HEAD_DIM) for h in SWA_HEAD_ORDER]
    g_mix = jnp.concatenate([g_mix[:, a:a + n] for a, n in mix_segs], axis=1)
    w_out_b = jnp.concatenate([w_out[:, a:a + n] for a, n in mix_segs], axis=1).astype(BF16)
    w_up_b = w_up.astype(BF16)
    w_down_b = w_down.astype(BF16)
    gfinal = norm_final.reshape(1, d)

    colscale = jnp.asarray(_column_scales())

    x2 = x.reshape(m, d)
    for l in range(depth):
        proj = _inproj(x2, norm_attn[l].reshape(1, d), w_in_p[l], colscale).reshape(b, s, NP)
        gm = g_mix[l].reshape(N_HEADS, 1, GROUP_WIDTH)

        chunks = proj[:, :, COL_KVC * LANES:(COL_KVC + 1) * LANES].reshape(b, s // CMP_STRIDE, CMP_STRIDE * LANES)
        cmp_kv = _compress(chunks, *_compress_weights(cmp_pe_k[l], cmp_w1_k[l], cmp_w2_k[l],
                                                      cmp_pe_v[l], cmp_w1_v[l], cmp_w2_v[l]))

        o_sb = _stick_breaking(proj, gm[0])
        o_nsa = _native_sparse(proj, cmp_kv, gm[1])
        lam_init = 0.8 - 0.6 * math.exp(-0.3 * l)
        dl = jnp.stack([diff_lq1[l], diff_lk1[l], diff_lq2[l], diff_lk2[l]]).astype(F32)
        o_diff = _differential(proj, dl, gm[2], lam_init)
        o_swa = _sliding_window(proj, sinks[l].astype(F32), gm[3])

        mixes = [o.reshape(m, GROUP_WIDTH) for o in (o_sb, o_nsa, o_diff, o_swa)]
        x2 = _mix_mlp(x2, mixes, w_out_b[l], norm_mlp[l].reshape(1, d), w_up_b[l], w_down_b[l], gfinal,
                      final_norm=(l == depth - 1))
    return x2.reshape(b, s, d)
```

```python
import functools
import math

import ml_dtypes
import numpy as np
import jax
import jax.numpy as jnp
from jax import lax
from jax.experimental import pallas as pl
from jax.experimental.pallas import tpu as pltpu

F32 = jnp.float32
BF16 = jnp.bfloat16

D_MODEL = 1024
DEPTH = 4
HEAD_DIM = 64
GROUP_WIDTH = 256
N_HEADS = 4
D_FF = 4 * D_MODEL
EPS = 1e-6
NEG_INF = -1e30
LOG2E = math.log2(math.e)
LANES = 128

CMP_LEN = 32
CMP_STRIDE = 16
CMP_HIDDEN = 128
SEL_LEN = 64
SEL_TOPN = 8
FORCE_SCORE = 1e9
NSA_WINDOW = 512
SWA_WINDOW = 128
DIFF_QK_DIM = 32

VMEM_LIMIT = 48 * 1024 * 1024

_ORIG = dict(qa=(0, 256), ka=(256, 256), va=(512, 256), qn=(768, 256), kcn=(1024, 64), vcn=(1088, 64),
             ksn=(1152, 64), vsn=(1216, 64), kwn=(1280, 64), vwn=(1344, 64), gn=(1408, 12),
             qc=(1420, 256), kc=(1676, 256), vc=(1932, 256), qd=(2188, 256), kd=(2444, 128), vd=(2572, 128))


def _layout():
    segs, first = [], {}

    def put(name, lo=0, n=None):
        start, size = _ORIG[name]
        first.setdefault(name, sum(w for _, w in segs))
        segs.append((start + lo, size - lo if n is None else n))

    def zeros(n):
        segs.append((None, n))

    for nm in ("qa", "ka", "va", "qn", "kcn", "vcn"):
        put(nm)
    for nm in ("ksn", "vsn", "kwn", "vwn"):
        put(nm)
        put(nm)
    put("gn")
    zeros(LANES - 12)
    for nm in ("qc", "kc", "vc"):
        put(nm)
    for h in SWA_HEAD_ORDER:
        put("qd", h * HEAD_DIM, HEAD_DIM)
    put("kd")
    put("vd")
    return segs, first


SWA_HEAD_ORDER = (0, 2, 1, 3)
_SEGMENTS, _FIRST = _layout()
NP = sum(n for _, n in _SEGMENTS)


def _column_scales():
    cs = np.ones((1, NP), np.float32)
    for name, dim in (("qa", HEAD_DIM), ("qn", HEAD_DIM), ("qc", DIFF_QK_DIM), ("qd", HEAD_DIM)):
        cs[0, _FIRST[name]:_FIRST[name] + GROUP_WIDTH] = dim ** -0.5 * LOG2E
    return cs


def _col_block(name, width):
    block, rem = divmod(_FIRST[name], width)
    assert rem == 0, (name, width)
    return block


COL_QA, COL_KA, COL_VA, COL_QN = (_col_block(n, GROUP_WIDTH) for n in ("qa", "ka", "va", "qn"))
COL_QC, COL_KC, COL_VC, COL_QD = (_col_block(n, GROUP_WIDTH) for n in ("qc", "kc", "vc", "qd"))
COL_KVC, COL_KS, COL_VS, COL_KW, COL_VW, COL_GN, COL_KD, COL_VD = (
    _col_block(n, LANES) for n in ("kcn", "ksn", "vsn", "kwn", "vwn", "gn", "kd", "vd"))


def _alibi_slopes():
    m = 2.0 ** (-8.0 * np.arange(1, 13) / 12.0)
    m = m.astype(np.float32).reshape(4, 3)
    return [float(v) for v in m[:, 0]], [float(v) for v in m[:, 1]], [float(v) for v in m[:, 2]]


SLOPES_NSA, SLOPES_DIFF, SLOPES_SWA = _alibi_slopes()


def _nt(a, b):
    return lax.dot_general(a, b, (((1,), (1,)), ((), ())), preferred_element_type=F32)


def _nn(a, b):
    return jnp.dot(a, b, preferred_element_type=F32)


def _cparams(*sem):
    return pltpu.CompilerParams(dimension_semantics=sem, vmem_limit_bytes=VMEM_LIMIT)


def _rms(x):
    return x * lax.rsqrt(jnp.mean(x * x, axis=-1, keepdims=True) + EPS)


def _inproj_kernel(x_ref, g_ref, w_ref, cs_ref, o_ref):
    h = _rms(x_ref[...]) * g_ref[...]
    o_ref[...] = (_nn(h.astype(BF16), w_ref[...]) * cs_ref[...]).astype(BF16)


def _inproj(x2, gain, w, colscale, tm=512):
    m = x2.shape[0]
    return pl.pallas_call(
        _inproj_kernel,
        grid=(m // tm,),
        in_specs=[pl.BlockSpec((tm, D_MODEL), lambda i: (i, 0)),
                  pl.BlockSpec((1, D_MODEL), lambda i: (0, 0)),
                  pl.BlockSpec((D_MODEL, NP), lambda i: (0, 0)),
                  pl.BlockSpec((1, NP), lambda i: (0, 0))],
        out_specs=pl.BlockSpec((tm, NP), lambda i: (i, 0)),
        out_shape=jax.ShapeDtypeStruct((m, NP), BF16),
        compiler_params=_cparams("parallel"),
        name="inproj",
    )(x2, gain, w, colscale)


def _compress_kernel(c_ref, pea_ref, peb_ref, w1a_ref, w1b_ref, w2_ref, o_ref):
    c = c_ref[0].astype(F32)
    p = _nn((c + pea_ref[...]).astype(BF16), w1a_ref[...])
    r = _nn((c + peb_ref[...]).astype(BF16), w1b_ref[...])
    n = c.shape[0]
    r_next = pltpu.roll(r, n - 1, 0)
    hid = jax.nn.gelu(p + r_next)
    o_ref[0] = _nn(hid.astype(BF16), w2_ref[...]).astype(BF16)


def _compress(chunks, pea, peb, w1a, w1b, w2):
    b, n, w = chunks.shape
    full = lambda a: pl.BlockSpec(a.shape, lambda i: (0,) * a.ndim)
    return pl.pallas_call(
        _compress_kernel,
        grid=(b,),
        in_specs=[pl.BlockSpec((1, n, w), lambda i: (i, 0, 0)), full(pea), full(peb), full(w1a), full(w1b),
                  full(w2)],
        out_specs=pl.BlockSpec((1, n, 2 * LANES), lambda i: (i, 0, 0)),
        out_shape=jax.ShapeDtypeStruct((b, n, 2 * LANES), BF16),
        compiler_params=_cparams("parallel"),
        name="nsa_compress",
    )(chunks, pea, peb, w1a, w1b, w2)


def _half_mask(shape, half):
    lane = lax.broadcasted_iota(jnp.int32, shape, len(shape) - 1)
    return (lane // HEAD_DIM) == half


def _group_norm_store(o_ref, rows, gm_ref, o):
    o_ref[0, rows, :] = (_rms(o) * gm_ref[...]).astype(o_ref.dtype)


def _query_rows(i, blk):
    return pl.ds(pl.multiple_of(i * blk, blk), blk)


def _over_query_blocks(step, blk):
    def kernel(*refs):
        def body(i, c):
            step(i, *refs)
            return c
        lax.fori_loop(0, refs[0].shape[1] // blk, body, 0)
    return kernel


def _sb_step(i, q_ref, k_ref, v_ref, gm_ref, o_ref, acc_ref, carry_ref, lb_ref, lk_ref, tot_ref, *, blk):
    q = q_ref[0, _query_rows(i, blk), :]
    row = lax.broadcasted_iota(jnp.int32, (blk, blk), 0)
    col = lax.broadcasted_iota(jnp.int32, (blk, blk), 1)
    past = col < row
    u = jnp.where(row > col, 1.0, 0.0).astype(BF16)
    qm = []
    for h in range(N_HEADS):
        slab = q[:, (h // 2) * LANES:(h // 2 + 1) * LANES]
        qm.append(jnp.where(_half_mask(slab.shape, h % 2), slab, jnp.zeros_like(slab)))
    qm = [jnp.concatenate(qm[0:2], axis=0), jnp.concatenate(qm[2:4], axis=0)]
    acc_ref[...] = jnp.zeros_like(acc_ref)
    carry_ref[...] = jnp.zeros_like(carry_ref)

    def per_head(x, fn):
        return jnp.concatenate([fn(x[k * blk:(k + 1) * blk]) for k in range(x.shape[0] // blk)], axis=0)

    def a_logits(j):
        rows = pl.ds(pl.multiple_of(j * blk, blk), blk)
        return [_nt(qm[pr], k_ref[0, rows, pr * LANES:(pr + 1) * LANES]) for pr in range(2)]

    def a_finish(zs, slot, diag):
        for pr, z in enumerate(zs):
            r2 = slice(2 * pr * blk, 2 * (pr + 1) * blk)
            sp = jnp.log2(1.0 + jnp.exp2(-jnp.abs(z)))
            lb = jnp.minimum(z, 0.0) - sp
            lk = lb - z
            if diag:
                lk = per_head(lk, lambda t: jnp.where(past, t, 0.0))
                lb = per_head(lb, lambda t: jnp.where(past, t, NEG_INF))
            lb_ref[slot, r2, :] = lb
            lk_ref[slot, r2, :] = lk.astype(BF16)
            tot_ref[slot, r2, :] = jnp.broadcast_to(jnp.sum(lk, axis=-1, keepdims=True), (2 * blk, LANES))

    def b_suffix(slot):
        return _nn(lk_ref[slot], u)

    def b_finish(cs, j, slot):
        rows = pl.ds(pl.multiple_of(j * blk, blk), blk)
        carry = carry_ref[...]
        a = jnp.exp2(lb_ref[slot] + cs + jnp.concatenate([carry] * (blk // LANES), axis=1)).astype(BF16)
        for pr in range(2):
            r2 = slice(2 * pr * blk, 2 * (pr + 1) * blk)
            acc_ref[r2, :] += _nn(a[r2], v_ref[0, rows, pr * LANES:(pr + 1) * LANES])
        carry_ref[...] = carry + tot_ref[slot]

    a_finish(a_logits(i), 0, True)

    def step(t, slot):
        zs = a_logits(i - 1 - t)
        cs = b_suffix(slot)
        b_finish(cs, i - t, slot)
        a_finish(zs, 1 - slot, False)

    def body(tt, c):
        step(2 * tt, 0)
        step(2 * tt + 1, 1)
        return c

    lax.fori_loop(0, i // 2, body, 0)

    @pl.when(i % 2 == 1)
    def _():
        step(i - 1, 0)
        b_finish(b_suffix(1), 0, 1)

    @pl.when(i % 2 == 0)
    def _():
        b_finish(b_suffix(0), 0, 0)

    lo_half = _half_mask((blk, LANES), 0)
    acc = [acc_ref[h * blk:(h + 1) * blk, :] for h in range(N_HEADS)]
    o = jnp.concatenate([jnp.where(lo_half, acc[0], acc[1]), jnp.where(lo_half, acc[2], acc[3])], axis=1)
    _group_norm_store(o_ref, _query_rows(i, blk), gm_ref, o)


def _stick_breaking(proj, gm, blk=256):
    b, s, _ = proj.shape
    stage = lambda w, dt: pltpu.VMEM((2, N_HEADS * blk, w), dt)
    return pl.pallas_call(
        _over_query_blocks(functools.partial(_sb_step, blk=blk), blk),
        grid=(b,),
        in_specs=[pl.BlockSpec((1, s, GROUP_WIDTH), lambda bi: (bi, 0, COL_QA)),
                  pl.BlockSpec((1, s, GROUP_WIDTH), lambda bi: (bi, 0, COL_KA)),
                  pl.BlockSpec((1, s, GROUP_WIDTH), lambda bi: (bi, 0, COL_VA)),
                  pl.BlockSpec((1, GROUP_WIDTH), lambda bi: (0, 0))],
        out_specs=pl.BlockSpec((1, s, GROUP_WIDTH), lambda bi: (bi, 0, 0)),
        out_shape=jax.ShapeDtypeStruct((b, s, GROUP_WIDTH), BF16),
        scratch_shapes=[pltpu.VMEM((N_HEADS * blk, LANES), F32), pltpu.VMEM((N_HEADS * blk, LANES), F32),
                        stage(blk, F32), stage(blk, BF16), stage(LANES, F32)],
        compiler_params=_cparams("parallel"),
        name="stick_breaking",
    )(proj, proj, proj, gm)


POS_LOCAL, POS_BLOCK = 0, 3
SEL_LANE = 8
MASK_BIAS = -2.0 ** 100


def _bf16_pieces(x, n):
    out = []
    r = np.float32(x)
    for _ in range(n):
        p = np.float32(r.astype(ml_dtypes.bfloat16))
        out.append(float(p))
        r = np.float32(r - p)
    return out


def _position_tables(slopes, s, blk, sel_blocks=False):
    coef = np.zeros((N_HEADS, LANES), np.float32)
    for h in range(N_HEADS):
        pieces = _bf16_pieces(slopes[h] * LOG2E, 3)
        coef[h, POS_LOCAL:POS_LOCAL + 3] = pieces
        coef[h, POS_BLOCK:POS_BLOCK + 3] = [blk * p for p in pieces]
    pos = np.zeros((s, LANES), np.float32)
    idx = np.arange(s)
    pos[:, POS_LOCAL:POS_LOCAL + 3] = (idx % blk)[:, None]
    pos[:, POS_BLOCK:POS_BLOCK + 3] = (idx // blk)[:, None]
    if sel_blocks:
        pos[idx, SEL_LANE + idx // SEL_LEN] = 1.0
    return jnp.asarray(coef), jnp.asarray(pos, BF16)


def _diff_step(i, q_ref, k_ref, v_ref, pos_ref, coef_ref, dl_ref, gm_ref, o_ref, qa_ref, mx_ref, acc_ref,
               *, blk, lam_init):
    q = q_ref[0, _query_rows(i, blk), :]
    row = lax.broadcasted_iota(jnp.int32, (blk, blk), 0)
    col = lax.broadcasted_iota(jnp.int32, (blk, blk), 1)
    causal = col <= row
    lane = lax.broadcasted_iota(jnp.int32, (blk, LANES), 1)
    for h in range(N_HEADS):
        slab = q[:, (h // 2) * LANES:(h // 2 + 1) * LANES]
        coef = jnp.broadcast_to(coef_ref[h:h + 1, :], (blk, LANES)).astype(BF16)
        for c in range(2):
            r = slice(((2 * h + c) % 4) * blk, ((2 * h + c) % 4 + 1) * blk)
            qa_ref[h // 2, r, 0:LANES] = jnp.where((lane // DIFF_QK_DIM) == (h % 2) * 2 + c, slab,
                                                   jnp.zeros_like(slab))
            qa_ref[h // 2, r, LANES:2 * LANES] = coef
    ones = jnp.ones((blk, LANES), BF16)

    def per_map(x, fn):
        return jnp.concatenate([fn(x[k * blk:(k + 1) * blk]) for k in range(4)], axis=0)

    def sweep(blocks, first):
        chains = [(pr, pl.ds(pl.multiple_of(j * blk, blk), blk), diag) for pr in range(2) for j, diag in blocks]
        zs = []
        for pr, rows, diag in chains:
            ka = jnp.concatenate([k_ref[0, rows, pr * LANES:(pr + 1) * LANES], pos_ref[rows, :]], axis=1)
            z = _nt(qa_ref[pr], ka)
            if diag:
                z = per_map(z, lambda t: jnp.where(causal, t, NEG_INF))
            zs.append(z)
        m_new, alpha = [None, None], [None, None]
        for pr in range(2):
            zm = None
            for (cp, _, _), z in zip(chains, zs):
                if cp == pr:
                    for g in range(blk // LANES):
                        part = z[:, g * LANES:(g + 1) * LANES]
                        zm = part if zm is None else jnp.maximum(zm, part)
            cm = jnp.broadcast_to(jnp.max(zm, axis=-1, keepdims=True), (4 * blk, LANES))
            if first:
                m_new[pr] = cm
            else:
                m_old = mx_ref[pr]
                m_new[pr] = jnp.maximum(m_old, cm)
                alpha[pr] = jnp.exp2(m_old - m_new[pr])
            mx_ref[pr] = m_new[pr]
        ps = [jnp.exp2(z - jnp.concatenate([m_new[pr]] * (blk // LANES), axis=1)).astype(BF16)
              for (pr, _, _), z in zip(chains, zs)]
        outs = []
        for (pr, rows, diag), p in zip(chains, ps):
            vo = jnp.concatenate([v_ref[0, rows, pr * LANES:(pr + 1) * LANES], ones], axis=1)
            outs.append(_nn(p, vo))
        for pr in range(2):
            tot = functools.reduce(jnp.add, [o for (cp, _, _), o in zip(chains, outs) if cp == pr])
            acc_ref[pr] = tot if first else acc_ref[pr] * jnp.concatenate([alpha[pr]] * 2, axis=1) + tot

    base = (i // 4) * 4

    @pl.when(i % 2 == 1)
    def _():
        sweep([(i - 1, False), (i, True)], True)

    @pl.when(i % 2 == 0)
    def _():
        sweep([(i, True)], True)

    @pl.when(i % 4 >= 2)
    def _():
        sweep([(base, False), (base + 1, False)], False)

    def body(tt, c):
        sweep([(4 * tt + k, False) for k in range(4)], False)
        return c
    lax.fori_loop(0, i // 4, body, 0)

    dl = dl_ref[...]
    s1 = jnp.sum(dl[0:1] * dl[1:2], axis=-1, keepdims=True)
    s2 = jnp.sum(dl[2:3] * dl[3:4], axis=-1, keepdims=True)
    lam = jnp.exp(s1) - jnp.exp(s2) + lam_init
    outs = []
    for h in range(N_HEADS):
        k1 = (2 * h) % 4
        a1 = acc_ref[h // 2, k1 * blk:(k1 + 1) * blk, :]
        a2 = acc_ref[h // 2, (k1 + 1) * blk:(k1 + 2) * blk, :]
        d = a1[:, :LANES] / a1[:, LANES:] - lam * (a2[:, :LANES] / a2[:, LANES:])
        mine = _half_mask(d.shape, h % 2)
        ms = jnp.sum(jnp.where(mine, d * d, 0.0), axis=-1, keepdims=True) * (1.0 / HEAD_DIM)
        outs.append(d * lax.rsqrt(ms + EPS) * (1.0 - lam_init))
    lo_half = _half_mask((blk, LANES), 0)
    o = jnp.concatenate([jnp.where(lo_half, outs[0], outs[1]), jnp.where(lo_half, outs[2], outs[3])], axis=1)
    o_ref[0, _query_rows(i, blk), :] = (o * gm_ref[...]).astype(o_ref.dtype)


def _differential(proj, dl, gm, lam_init, blk=256):
    b, s, _ = proj.shape
    coef, pos = _position_tables(SLOPES_DIFF, s, blk)
    return pl.pallas_call(
        _over_query_blocks(functools.partial(_diff_step, blk=blk, lam_init=lam_init), blk),
        grid=(b,),
        in_specs=[pl.BlockSpec((1, s, GROUP_WIDTH), lambda bi: (bi, 0, COL_QC)),
                  pl.BlockSpec((1, s, GROUP_WIDTH), lambda bi: (bi, 0, COL_KC)),
                  pl.BlockSpec((1, s, GROUP_WIDTH), lambda bi: (bi, 0, COL_VC)),
                  pl.BlockSpec((s, LANES), lambda bi: (0, 0)),
                  pl.BlockSpec((N_HEADS, LANES), lambda bi: (0, 0)),
                  pl.BlockSpec((4, DIFF_QK_DIM), lambda bi: (0, 0)),
                  pl.BlockSpec((1, GROUP_WIDTH), lambda bi: (0, 0))],
        out_specs=pl.BlockSpec((1, s, GROUP_WIDTH), lambda bi: (bi, 0, 0)),
        out_shape=jax.ShapeDtypeStruct((b, s, GROUP_WIDTH), BF16),
        scratch_shapes=[pltpu.VMEM((2, 4 * blk, 2 * LANES), BF16), pltpu.VMEM((2, 4 * blk, LANES), F32),
                        pltpu.VMEM((2, 4 * blk, 2 * LANES), F32)],
        compiler_params=_cparams("parallel"),
        name="differential",
    )(proj, proj, proj, pos, coef, dl, gm)


def _swa_step(i, q_ref, k_ref, v_ref, sink_ref, gm_ref, o_ref, *, blk, sub):
    row = lax.broadcasted_iota(jnp.int32, (blk, 2 * blk), 0)
    col = lax.broadcasted_iota(jnp.int32, (blk, 2 * blk), 1)
    lo_half = _half_mask((blk, LANES), 0)
    blocks = []
    for s_ in range(sub):
        g = i * sub + s_
        first = jnp.maximum(g - 1, 0)
        start = pl.multiple_of(first * blk, blk)
        dist = (row - col) + (g - first) * blk
        mask = (dist >= 0) & (dist < SWA_WINDOW)
        distf = dist.astype(F32)
        q = q_ref[0, _query_rows(g, blk), :]
        kb = k_ref[0, pl.ds(start, 2 * blk), :]
        vb = v_ref[0, pl.ds(start, 2 * blk), :]
        outs = []
        for n, h in enumerate(SWA_HEAD_ORDER):
            slab = q[:, (n // 2) * LANES:(n // 2 + 1) * LANES]
            qm = jnp.where(_half_mask(slab.shape, n % 2), slab, jnp.zeros_like(slab))
            z = _nt(qm, kb) - (SLOPES_SWA[h] * LOG2E) * distf
            z = jnp.where(mask, z, NEG_INF)
            sink = sink_ref[h] * LOG2E
            m = jnp.maximum(jnp.max(z, axis=-1, keepdims=True), sink)
            p = jnp.exp2(z - m)
            l = jnp.sum(p, axis=-1, keepdims=True) + jnp.exp2(sink - m)
            outs.append(_nn(p.astype(BF16), vb) / l)
        blocks.append(jnp.concatenate([jnp.where(lo_half, outs[0], outs[1]), jnp.where(lo_half, outs[2], outs[3])],
                                      axis=1))
    _group_norm_store(o_ref, _query_rows(i, sub * blk), gm_ref, jnp.concatenate(blocks, axis=0))


def _sliding_window(proj, sinks, gm, blk=128, sub=4):
    b, s, _ = proj.shape
    rows = blk * sub
    return pl.pallas_call(
        _over_query_blocks(functools.partial(_swa_step, blk=blk, sub=sub), rows),
        grid=(b,),
        in_specs=[pl.BlockSpec((1, s, GROUP_WIDTH), lambda bi: (bi, 0, COL_QD)),
                  pl.BlockSpec((1, s, LANES), lambda bi: (bi, 0, COL_KD)),
                  pl.BlockSpec((1, s, LANES), lambda bi: (bi, 0, COL_VD)),
                  pl.BlockSpec(memory_space=pltpu.SMEM),
                  pl.BlockSpec((1, GROUP_WIDTH), lambda bi: (0, 0))],
        out_specs=pl.BlockSpec((1, s, GROUP_WIDTH), lambda bi: (bi, 0, 0)),
        out_shape=jax.ShapeDtypeStruct((b, s, GROUP_WIDTH), BF16),
        compiler_params=_cparams("parallel"),
        name="sliding_window",
    )(proj, proj, proj, sinks, gm)


def _nsa_step(i, q_ref, g_ref, cmp_ref, ks_ref, vs_ref, kw_ref, vw_ref, pos_ref, coef_ref, gm_ref, o_ref,
              qa_ref, mx_ref, acc_ref, *, blk, n_cmp, n_sel):
    q = q_ref[0, _query_rows(i, blk), :]
    lane = lax.broadcasted_iota(jnp.int32, (blk, LANES), 1)
    coefs = [jnp.broadcast_to(coef_ref[h:h + 1, :], (blk, LANES)) for h in range(N_HEADS)]
    for h in range(N_HEADS):
        slab = q[:, (h // 2) * LANES:(h // 2 + 1) * LANES]
        own = jnp.where(_half_mask(slab.shape, h % 2), slab, jnp.zeros_like(slab))
        qa_ref[0, h * blk:(h + 1) * blk, 0:LANES] = own
        qa_ref[1, h * blk:(h + 1) * blk, 0:LANES] = own
        qa_ref[0, h * blk:(h + 1) * blk, LANES:2 * LANES] = coefs[h].astype(BF16)
    qs = qa_ref[1, :, 0:LANES]

    tq = i * blk + lax.broadcasted_iota(jnp.int32, (blk, LANES), 0)

    row = lax.broadcasted_iota(jnp.int32, (blk, blk), 0)
    col = lax.broadcasted_iota(jnp.int32, (blk, blk), 1)
    causal = col <= row
    in_window = col > row
    ones = jnp.ones((blk, LANES), BF16)

    k_refs, v_refs = (kw_ref, ks_ref), (vw_ref, vs_ref)

    def process(blocks, first=False):
        groups = [(br, half) for br in sorted({b[0] for b in blocks}) for half in range(2)]
        chains = [(br, pl.ds(pl.multiple_of(j * blk, blk), blk), mask, (br, half))
                  for br, j, mask in blocks for half in range(2)]
        grows = lambda g: slice(2 * g[1] * blk, 2 * (g[1] + 1) * blk)
        zs = []
        for br, rows, mask, g in chains:
            ka = jnp.concatenate([k_refs[br][0, rows, :], pos_ref[rows, :]], axis=1)
            z = _nt(qa_ref[br, grows(g), :], ka)
            if mask is not None:
                z = jnp.concatenate([jnp.where(mask, z[r * blk:(r + 1) * blk], NEG_INF) for r in range(2)], axis=0)
            zs.append(z)
        m_new, alpha = {}, {}
        for g in groups:
            zm = None
            for (_, _, _, cg), z in zip(chains, zs):
                if cg == g:
                    for c in range(blk // LANES):
                        part = z[:, c * LANES:(c + 1) * LANES]
                        zm = part if zm is None else jnp.maximum(zm, part)
            cm = jnp.broadcast_to(jnp.max(zm, axis=-1, keepdims=True), (2 * blk, LANES))
            if first:
                m_new[g] = cm
            else:
                m_old = mx_ref[g[0], grows(g), :]
                m_new[g] = jnp.maximum(m_old, cm)
                alpha[g] = jnp.exp2(m_old - m_new[g])
            mx_ref[g[0], grows(g), :] = m_new[g]
        ps = [jnp.exp2(z - jnp.concatenate([m_new[g]] * (blk // LANES), axis=1)).astype(BF16)
              for (_, _, _, g), z in zip(chains, zs)]
        outs = []
        for (br, rows, mask, g), p in zip(chains, ps):
            vo = jnp.concatenate([v_refs[br][0, rows, :], ones], axis=1)
            outs.append(_nn(p, vo))
        for g in groups:
            tot = functools.reduce(jnp.add, [o for (_, _, _, cg), o in zip(chains, outs) if cg == g])
            if first:
                acc_ref[g[0], grows(g), :] = tot
            else:
                acc_ref[g[0], grows(g), :] = acc_ref[g[0], grows(g), :] * jnp.concatenate([alpha[g]] * 2, axis=1) + tot

    window = [(0, jnp.maximum(i - 2, 0), in_window & (i >= 2)),
              (0, jnp.maximum(i - 1, 0), jnp.broadcast_to(i >= 1, (blk, blk))),
              (0, i, causal)]

    kc = cmp_ref[0, :, 0:LANES]
    vc = cmp_ref[0, :, LANES:2 * LANES]
    dist_c = tq - (CMP_STRIDE * lane + CMP_LEN - 1)
    ok_c = (dist_c >= 0) & (lane < n_cmp)
    dist_cf = dist_c.astype(F32)
    bias_c = jnp.concatenate([jnp.where(ok_c, (-SLOPES_NSA[h] * LOG2E) * dist_cf, NEG_INF)
                              for h in range(N_HEADS)], axis=0)
    z = _nt(qs, kc) + bias_c
    p = jnp.exp2(z - jnp.max(z, axis=-1, keepdims=True))
    p = p / jnp.sum(p, axis=-1, keepdims=True)
    p = jnp.where(bias_c > 0.5 * NEG_INF, p, 0.0).astype(BF16)
    o_cmp = _nn(p, vc)

    sel_rows = -(-(SEL_LANE + n_sel) // 8) * 8
    rj = lax.broadcasted_iota(jnp.int32, (sel_rows, LANES), 0)
    ri = lax.broadcasted_iota(jnp.int32, (sel_rows, LANES), 1)
    to_sel = jnp.where((ri * CMP_STRIDE) // SEL_LEN + SEL_LANE == rj, 1.0, 0.0).astype(BF16)
    imp = _nt(to_sel, p[0:blk])
    for h in range(1, N_HEADS):
        imp = imp + _nt(to_sel, p[h * blk:(h + 1) * blk])
    srow = lax.broadcasted_iota(jnp.int32, (sel_rows, blk), 0)
    blk_id = srow - SEL_LANE
    cur = (i * blk + lax.broadcasted_iota(jnp.int32, (sel_rows, blk), 1)) // SEL_LEN
    forced = (blk_id == 0) | (blk_id == cur) | (blk_id == cur - 1)
    score = jnp.where(blk_id > cur, NEG_INF, jnp.where(forced, FORCE_SCORE, imp))
    score = jnp.where((blk_id >= 0) & (blk_id < n_sel), score, -3.0e38)
    sel = jnp.zeros((sel_rows, blk), F32)
    srowf = srow.astype(F32)
    for _ in range(min(SEL_TOPN, n_sel)):
        best = jnp.max(score, axis=0, keepdims=True)
        idx = jnp.min(jnp.where(score == best, srowf, float(LANES)), axis=0, keepdims=True)
        hit = srowf == idx
        sel = jnp.where(hit, 1.0, sel)
        score = jnp.where(hit, -3.4e38, score)
    sel_bias = jnp.where(sel > 0.5, 0.0, MASK_BIAS)
    sel_bias = jnp.concatenate([sel_bias, jnp.zeros((LANES - sel_rows, blk), F32)], axis=0).T
    in_sel = (lane >= SEL_LANE) & (lane < SEL_LANE + n_sel)

    process(window, first=True)
    o_win = acc_ref[0, :, 0:LANES] / acc_ref[0, :, LANES:2 * LANES]

    g = jax.nn.sigmoid(g_ref[0, _query_rows(i, blk), :].astype(F32))
    part = [g[:, 3 * h:3 * h + 1] * o_cmp[h * blk:(h + 1) * blk] + g[:, 3 * h + 2:3 * h + 3] * o_win[h * blk:(h + 1) * blk]
            for h in range(N_HEADS)]

    for h in range(N_HEADS):
        qa_ref[1, h * blk:(h + 1) * blk, LANES:2 * LANES] = jnp.where(in_sel, sel_bias, coefs[h]).astype(BF16)
    base = (i // 4) * 4

    @pl.when(i % 2 == 1)
    def _():
        process([(1, i - 1, None), (1, i, causal)], first=True)

    @pl.when(i % 2 == 0)
    def _():
        process([(1, i, causal)], first=True)

    @pl.when(i % 4 >= 2)
    def _():
        process([(1, base, None), (1, base + 1, None)])

    def body(tt, c):
        process([(1, 4 * tt + k, None) for k in range(4)])
        return c
    lax.fori_loop(0, i // 4, body, 0)
    o_sel = acc_ref[1, :, 0:LANES] / acc_ref[1, :, LANES:2 * LANES]

    outs = [part[h] + g[:, 3 * h + 1:3 * h + 2] * o_sel[h * blk:(h + 1) * blk] for h in range(N_HEADS)]
    lo_half = _half_mask((blk, LANES), 0)
    o = jnp.concatenate([jnp.where(lo_half, outs[0], outs[1]), jnp.where(lo_half, outs[2], outs[3])], axis=1)
    _group_norm_store(o_ref, _query_rows(i, blk), gm_ref, o)


def _native_sparse(proj, cmp_kv, gm, blk=256):
    b, s, _ = proj.shape
    n_cmp = (s - CMP_LEN) // CMP_STRIDE + 1
    n_sel = s // SEL_LEN
    assert cmp_kv.shape[1] == LANES and n_cmp <= LANES and SEL_LANE + n_sel <= LANES
    assert NSA_WINDOW == 2 * blk
    coef, pos = _position_tables(SLOPES_NSA, s, blk, sel_blocks=True)
    slab = lambda c: pl.BlockSpec((1, s, LANES), lambda bi: (bi, 0, c))
    rows = N_HEADS * blk
    return pl.pallas_call(
        _over_query_blocks(functools.partial(_nsa_step, blk=blk, n_cmp=n_cmp, n_sel=n_sel), blk),
        grid=(b,),
        in_specs=[pl.BlockSpec((1, s, GROUP_WIDTH), lambda bi: (bi, 0, COL_QN)),
                  slab(COL_GN),
                  pl.BlockSpec((1, LANES, 2 * LANES), lambda bi: (bi, 0, 0)),
                  slab(COL_KS), slab(COL_VS), slab(COL_KW), slab(COL_VW),
                  pl.BlockSpec((s, LANES), lambda bi: (0, 0)),
                  pl.BlockSpec((N_HEADS, LANES), lambda bi: (0, 0)),
                  pl.BlockSpec((1, GROUP_WIDTH), lambda bi: (0, 0))],
        out_specs=pl.BlockSpec((1, s, GROUP_WIDTH), lambda bi: (bi, 0, 0)),
        out_shape=jax.ShapeDtypeStruct((b, s, GROUP_WIDTH), BF16),
        scratch_shapes=[pltpu.VMEM((2, rows, 2 * LANES), BF16), pltpu.VMEM((2, rows, LANES), F32),
                        pltpu.VMEM((2, rows, 2 * LANES), F32)],
        compiler_params=_cparams("parallel"),
        name="native_sparse",
    )(proj, proj, cmp_kv, proj, proj, proj, proj, pos, coef, gm)


def _mix_mlp_kernel(x_ref, a_ref, b_ref, c_ref, d_ref, wo_ref, g_ref, wu_ref, wd_ref, gf_ref, o_ref, h_ref,
                    *, final_norm):
    f = pl.program_id(1)

    @pl.when(f == 0)
    def _():
        mix = jnp.concatenate([a_ref[...], b_ref[...], c_ref[...], d_ref[...]], axis=1)
        x1 = x_ref[...] + _nn(mix, wo_ref[...])
        o_ref[...] = x1
        h_ref[...] = (_rms(x1) * g_ref[...]).astype(BF16)

    u = jnp.maximum(_nn(h_ref[...], wu_ref[...]), 0.0)
    o_ref[...] += _nn((u * u).astype(BF16), wd_ref[...])

    if final_norm:
        @pl.when(f == pl.num_programs(1) - 1)
        def _():
            o_ref[...] = _rms(o_ref[...]) * gf_ref[...]


def _mix_mlp(x2, mixes, wo, gain, wu, wd, gfinal, final_norm, tm=1024, tf=1024):
    m = x2.shape[0]
    grp = pl.BlockSpec((tm, GROUP_WIDTH), lambda i, f: (i, 0))
    row = pl.BlockSpec((1, D_MODEL), lambda i, f: (0, 0))
    return pl.pallas_call(
        functools.partial(_mix_mlp_kernel, final_norm=final_norm),
        grid=(m // tm, D_FF // tf),
        in_specs=[pl.BlockSpec((tm, D_MODEL), lambda i, f: (i, 0)), grp, grp, grp, grp,
                  pl.BlockSpec((D_MODEL, D_MODEL), lambda i, f: (0, 0)), row,
                  pl.BlockSpec((D_MODEL, tf), lambda i, f: (0, f)),
                  pl.BlockSpec((tf, D_MODEL), lambda i, f: (f, 0)), row],
        out_specs=pl.BlockSpec((tm, D_MODEL), lambda i, f: (i, 0)),
        out_shape=jax.ShapeDtypeStruct((m, D_MODEL), F32),
        scratch_shapes=[pltpu.VMEM((tm, D_MODEL), BF16)],
        compiler_params=_cparams("parallel", "arbitrary"),
        name="mix_mlp",
    )(x2, *mixes, wo, gain, wu, wd, gfinal)


def _compress_weights(pe_k, w1_k, w2_k, pe_v, w1_v, w2_v):
    half = CMP_STRIDE
    d = HEAD_DIM

    def halves(w1):
        w = w1.reshape(CMP_LEN, d, CMP_HIDDEN)
        return w[:half], w[half:]

    ka, kb = halves(w1_k)
    va, vb = halves(w1_v)
    zero = jnp.zeros_like(ka)

    def merge(wk, wv):
        top = jnp.concatenate([wk, zero], axis=-1)
        bot = jnp.concatenate([zero, wv], axis=-1)
        return jnp.concatenate([top, bot], axis=1).reshape(half * 2 * d, 2 * CMP_HIDDEN).astype(BF16)

    w1a, w1b = merge(ka, va), merge(kb, vb)
    pe = jnp.concatenate([pe_k, pe_v], axis=-1)
    pea = pe[:half].reshape(1, half * 2 * d)
    peb = pe[half:].reshape(1, half * 2 * d)
    zk = jnp.zeros_like(w2_k)
    w2 = jnp.concatenate([jnp.concatenate([w2_k, w2_k, zk, zk], axis=1),
                          jnp.concatenate([zk, zk, w2_v, w2_v], axis=1)], axis=0).astype(BF16)
    return pea, peb, w1a, w1b, w2


def kernel(x, norm_attn, w_in, cmp_pe_k, cmp_w1_k, cmp_w2_k, cmp_pe_v, cmp_w1_v, cmp_w2_v, diff_lq1, diff_lk1,
           diff_lq2, diff_lk2, sinks, g_mix, w_out, norm_mlp, w_up, w_down, norm_final):
    b, s, d = x.shape
    m = b * s
    depth = w_in.shape[0]
    w_in_p = jnp.concatenate([jnp.zeros((depth, d, n), w_in.dtype) if a is None else w_in[:, :, a:a + n]
                              for a, n in _SEGMENTS], axis=2).astype(BF16)
    swa0 = 3 * GROUP_WIDTH
    mix_segs = [(0, swa0)] + [(swa0 + h * HEAD_DIM, HEAD_DIM) for h in SWA_HEAD_ORDER]
    g_mix = jnp.concatenate([g_mix[:, a:a + n] for a, n in mix_segs], axis=1)
    w_out_b = jnp.concatenate([w_out[:, a:a + n] for a, n in mix_segs], axis=1).astype(BF16)
    w_up_b = w_up.astype(BF16)
    w_down_b = w_down.astype(BF16)
    gfinal = norm_final.reshape(1, d)

    colscale = jnp.asarray(_column_scales())

    x2 = x.reshape(m, d)
    for l in range(depth):
        proj = _inproj(x2, norm_attn[l].reshape(1, d), w_in_p[l], colscale).reshape(b, s, NP)
        gm = g_mix[l].reshape(N_HEADS, 1, GROUP_WIDTH)

        chunks = proj[:, :, COL_KVC * LANES:(COL_KVC + 1) * LANES].reshape(b, s // CMP_STRIDE, CMP_STRIDE * LANES)
        cmp_kv = _compress(chunks, *_compress_weights(cmp_pe_k[l], cmp_w1_k[l], cmp_w2_k[l],
                                                      cmp_pe_v[l], cmp_w1_v[l], cmp_w2_v[l]))

        o_sb = _stick_breaking(proj, gm[0])
        o_nsa = _native_sparse(proj, cmp_kv, gm[1])
        lam_init = 0.8 - 0.6 * math.exp(-0.3 * l)
        dl = jnp.stack([diff_lq1[l], diff_lk1[l], diff_lq2[l], diff_lk2[l]]).astype(F32)
        o_diff = _differential(proj, dl, gm[2], lam_init)
        o_swa = _sliding_window(proj, sinks[l].astype(F32), gm[3])

        mixes = [o.reshape(m, GROUP_WIDTH) for o in (o_sb, o_nsa, o_diff, o_swa)]
        x2 = _mix_mlp(x2, mixes, w_out_b[l], norm_mlp[l].reshape(1, d), w_up_b[l], w_down_b[l], gfinal,
                      final_norm=(l == depth - 1))
    return x2.reshape(b, s, d)
```

```python
import functools
import math

import ml_dtypes
import numpy as np
import jax
import jax.numpy as jnp
from jax import lax
from jax.experimental import pallas as pl
from jax.experimental.pallas import tpu as pltpu

F32 = jnp.float32
BF16 = jnp.bfloat16

D_MODEL = 1024
DEPTH = 4
HEAD_DIM = 64
GROUP_WIDTH = 256
N_HEADS = 4
D_FF = 4 * D_MODEL
EPS = 1e-6
NEG_INF = -1e30
LOG2E = math.log2(math.e)
LANES = 128

CMP_LEN = 32
CMP_STRIDE = 16
CMP_HIDDEN = 128
SEL_LEN = 64
SEL_TOPN = 8
FORCE_SCORE = 1e9
NSA_WINDOW = 512
SWA_WINDOW = 128
DIFF_QK_DIM = 32

VMEM_LIMIT = 48 * 1024 * 1024

_ORIG = dict(qa=(0, 256), ka=(256, 256), va=(512, 256), qn=(768, 256), kcn=(1024, 64), vcn=(1088, 64),
             ksn=(1152, 64), vsn=(1216, 64), kwn=(1280, 64), vwn=(1344, 64), gn=(1408, 12),
             qc=(1420, 256), kc=(1676, 256), vc=(1932, 256), qd=(2188, 256), kd=(2444, 128), vd=(2572, 128))


def _layout():
    segs, first = [], {}

    def put(name, lo=0, n=None):
        start, size = _ORIG[name]
        first.setdefault(name, sum(w for _, w in segs))
        segs.append((start + lo, size - lo if n is None else n))

    def zeros(n):
        segs.append((None, n))

    for nm in ("qa", "ka", "va", "qn", "kcn", "vcn"):
        put(nm)
    for nm in ("ksn", "vsn", "kwn", "vwn"):
        put(nm)
        put(nm)
    put("gn")
    zeros(LANES - 12)
    for nm in ("qc", "kc", "vc"):
        put(nm)
    for h in SWA_HEAD_ORDER:
        put("qd", h * HEAD_DIM, HEAD_DIM)
    put("kd")
    put("vd")
    return segs, first


SWA_HEAD_ORDER = (0, 2, 1, 3)
_SEGMENTS, _FIRST = _layout()
NP = sum(n for _, n in _SEGMENTS)


def _column_scales():
    cs = np.ones((1, NP), np.float32)
    for name, dim in (("qa", HEAD_DIM), ("qn", HEAD_DIM), ("qc", DIFF_QK_DIM), ("qd", HEAD_DIM)):
        cs[0, _FIRST[name]:_FIRST[name] + GROUP_WIDTH] = dim ** -0.5 * LOG2E
    return cs


def _col_block(name, width):
    block, rem = divmod(_FIRST[name], width)
    assert rem == 0, (name, width)
    return block


COL_QA, COL_KA, COL_VA, COL_QN = (_col_block(n, GROUP_WIDTH) for n in ("qa", "ka", "va", "qn"))
COL_QC, COL_KC, COL_VC, COL_QD = (_col_block(n, GROUP_WIDTH) for n in ("qc", "kc", "vc", "qd"))
COL_KVC, COL_KS, COL_VS, COL_KW, COL_VW, COL_GN, COL_KD, COL_VD = (
    _col_block(n, LANES) for n in ("kcn", "ksn", "vsn", "kwn", "vwn", "gn", "kd", "vd"))


def _alibi_slopes():
    m = 2.0 ** (-8.0 * np.arange(1, 13) / 12.0)
    m = m.astype(np.float32).reshape(4, 3)
    return [float(v) for v in m[:, 0]], [float(v) for v in m[:, 1]], [float(v) for v in m[:, 2]]


SLOPES_NSA, SLOPES_DIFF, SLOPES_SWA = _alibi_slopes()


def _nt(a, b):
    return lax.dot_general(a, b, (((1,), (1,)), ((), ())), preferred_element_type=F32)


def _nn(a, b):
    return jnp.dot(a, b, preferred_element_type=F32)


def _cparams(*sem):
    return pltpu.CompilerParams(dimension_semantics=sem, vmem_limit_bytes=VMEM_LIMIT)


def _rms(x):
    return x * lax.rsqrt(jnp.mean(x * x, axis=-1, keepdims=True) + EPS)


def _inproj_kernel(x_ref, g_ref, w_ref, cs_ref, o_ref):
    h = _rms(x_ref[...]) * g_ref[...]
    o_ref[...] = (_nn(h.astype(BF16), w_ref[...]) * cs_ref[...]).astype(BF16)


def _inproj(x2, gain, w, colscale, tm=512):
    m = x2.shape[0]
    return pl.pallas_call(
        _inproj_kernel,
        grid=(m // tm,),
        in_specs=[pl.BlockSpec((tm, D_MODEL), lambda i: (i, 0)),
                  pl.BlockSpec((1, D_MODEL), lambda i: (0, 0)),
                  pl.BlockSpec((D_MODEL, NP), lambda i: (0, 0)),
                  pl.BlockSpec((1, NP), lambda i: (0, 0))],
        out_specs=pl.BlockSpec((tm, NP), lambda i: (i, 0)),
        out_shape=jax.ShapeDtypeStruct((m, NP), BF16),
        compiler_params=_cparams("parallel"),
        name="inproj",
    )(x2, gain, w, colscale)


def _compress_kernel(c_ref, pea_ref, peb_ref, w1a_ref, w1b_ref, w2_ref, o_ref):
    c = c_ref[0].astype(F32)
    p = _nn((c + pea_ref[...]).astype(BF16), w1a_ref[...])
    r = _nn((c + peb_ref[...]).astype(BF16), w1b_ref[...])
    n = c.shape[0]
    r_next = pltpu.roll(r, n - 1, 0)
    hid = jax.nn.gelu(p + r_next)
    o_ref[0] = _nn(hid.astype(BF16), w2_ref[...]).astype(BF16)


def _compress(chunks, pea, peb, w1a, w1b, w2):
    b, n, w = chunks.shape
    full = lambda a: pl.BlockSpec(a.shape, lambda i: (0,) * a.ndim)
    return pl.pallas_call(
        _compress_kernel,
        grid=(b,),
        in_specs=[pl.BlockSpec((1, n, w), lambda i: (i, 0, 0)), full(pea), full(peb), full(w1a), full(w1b),
                  full(w2)],
        out_specs=pl.BlockSpec((1, n, 2 * LANES), lambda i: (i, 0, 0)),
        out_shape=jax.ShapeDtypeStruct((b, n, 2 * LANES), BF16),
        compiler_params=_cparams("parallel"),
        name="nsa_compress",
    )(chunks, pea, peb, w1a, w1b, w2)


def _half_mask(shape, half):
    lane = lax.broadcasted_iota(jnp.int32, shape, len(shape) - 1)
    return (lane // HEAD_DIM) == half


def _group_norm_store(o_ref, rows, gm_ref, o):
    o_ref[0, rows, :] = (_rms(o) * gm_ref[...]).astype(o_ref.dtype)


def _query_rows(i, blk):
    return pl.ds(pl.multiple_of(i * blk, blk), blk)


def _over_query_blocks(step, blk):
    def kernel(*refs):
        def body(i, c):
            step(i, *refs)
            return c
        lax.fori_loop(0, refs[0].shape[1] // blk, body, 0)
    return kernel


def _sb_step(i, q_ref, k_ref, v_ref, gm_ref, o_ref, acc_ref, carry_ref, lb_ref, lk_ref, tot_ref, *, blk):
    q = q_ref[0, _query_rows(i, blk), :]
    row = lax.broadcasted_iota(jnp.int32, (blk, blk), 0)
    col = lax.broadcasted_iota(jnp.int32, (blk, blk), 1)
    past = col < row
    u = jnp.where(row > col, 1.0, 0.0).astype(BF16)
    qm = []
    for h in range(N_HEADS):
        slab = q[:, (h // 2) * LANES:(h // 2 + 1) * LANES]
        qm.append(jnp.where(_half_mask(slab.shape, h % 2), slab, jnp.zeros_like(slab)))
    qm = [jnp.concatenate(qm[0:2], axis=0), jnp.concatenate(qm[2:4], axis=0)]
    acc_ref[...] = jnp.zeros_like(acc_ref)
    carry_ref[...] = jnp.zeros_like(carry_ref)

    def per_head(x, fn):
        return jnp.concatenate([fn(x[k * blk:(k + 1) * blk]) for k in range(x.shape[0] // blk)], axis=0)

    def a_logits(j):
        rows = pl.ds(pl.multiple_of(j * blk, blk), blk)
        return [_nt(qm[pr], k_ref[0, rows, pr * LANES:(pr + 1) * LANES]) for pr in range(2)]

    def a_finish(zs, slot, diag):
        for pr, z in enumerate(zs):
            r2 = slice(2 * pr * blk, 2 * (pr + 1) * blk)
            sp = jnp.log2(1.0 + jnp.exp2(-jnp.abs(z)))
            lb = jnp.minimum(z, 0.0) - sp
            lk = lb - z
            if diag:
                lk = per_head(lk, lambda t: jnp.where(past, t, 0.0))
                lb = per_head(lb, lambda t: jnp.where(past, t, NEG_INF))
            lb_ref[slot, r2, :] = lb
            lk_ref[slot, r2, :] = lk.astype(BF16)
            tot_ref[slot, r2, :] = jnp.broadcast_to(jnp.sum(lk, axis=-1, keepdims=True), (2 * blk, LANES))

    def b_suffix(slot):
        return _nn(lk_ref[slot], u)

    def b_finish(cs, j, slot):
        rows = pl.ds(pl.multiple_of(j * blk, blk), blk)
        carry = carry_ref[...]
        a = jnp.exp2(lb_ref[slot] + cs + jnp.concatenate([carry] * (blk // LANES), axis=1)).astype(BF16)
        for pr in range(2):
            r2 = slice(2 * pr * blk, 2 * (pr + 1) * blk)
            acc_ref[r2, :] += _nn(a[r2], v_ref[0, rows, pr * LANES:(pr + 1) * LANES])
        carry_ref[...] = carry + tot_ref[slot]

    a_finish(a_logits(i), 0, True)

    def step(t, slot):
        zs = a_logits(i - 1 - t)
        cs = b_suffix(slot)
        b_finish(cs, i - t, slot)
        a_finish(zs, 1 - slot, False)

    def body(tt, c):
        step(2 * tt, 0)
        step(2 * tt + 1, 1)
        return c

    lax.fori_loop(0, i // 2, body, 0)

    @pl.when(i % 2 == 1)
    def _():
        step(i - 1, 0)
        b_finish(b_suffix(1), 0, 1)

    @pl.when(i % 2 == 0)
    def _():
        b_finish(b_suffix(0), 0, 0)

    lo_half = _half_mask((blk, LANES), 0)
    acc = [acc_ref[h * blk:(h + 1) * blk, :] for h in range(N_HEADS)]
    o = jnp.concatenate([jnp.where(lo_half, acc[0], acc[1]), jnp.where(lo_half, acc[2], acc[3])], axis=1)
    _group_norm_store(o_ref, _query_rows(i, blk), gm_ref, o)


def _stick_breaking(proj, gm, blk=256):
    b, s, _ = proj.shape
    stage = lambda w, dt: pltpu.VMEM((2, N_HEADS * blk, w), dt)
    return pl.pallas_call(
        _over_query_blocks(functools.partial(_sb_step, blk=blk), blk),
        grid=(b,),
        in_specs=[pl.BlockSpec((1, s, GROUP_WIDTH), lambda bi: (bi, 0, COL_QA)),
                  pl.BlockSpec((1, s, GROUP_WIDTH), lambda bi: (bi, 0, COL_KA)),
                  pl.BlockSpec((1, s, GROUP_WIDTH), lambda bi: (bi, 0, COL_VA)),
                  pl.BlockSpec((1, GROUP_WIDTH), lambda bi: (0, 0))],
        out_specs=pl.BlockSpec((1, s, GROUP_WIDTH), lambda bi: (bi, 0, 0)),
        out_shape=jax.ShapeDtypeStruct((b, s, GROUP_WIDTH), BF16),
        scratch_shapes=[pltpu.VMEM((N_HEADS * blk, LANES), F32), pltpu.VMEM((N_HEADS * blk, LANES), F32),
                        stage(blk, F32), stage(blk, BF16), stage(LANES, F32)],
        compiler_params=_cparams("parallel"),
        name="stick_breaking",
    )(proj, proj, proj, gm)


POS_LOCAL, POS_BLOCK = 0, 3
SEL_LANE = 8
MASK_BIAS = -2.0 ** 100


def _bf16_pieces(x, n):
    out = []
    r = np.float32(x)
    for _ in range(n):
        p = np.float32(r.astype(ml_dtypes.bfloat16))
        out.append(float(p))
        r = np.float32(r - p)
    return out


def _position_tables(slopes, s, blk, sel_blocks=False):
    coef = np.zeros((N_HEADS, LANES), np.float32)
    for h in range(N_HEADS):
        pieces = _bf16_pieces(slopes[h] * LOG2E, 3)
        coef[h, POS_LOCAL:POS_LOCAL + 3] = pieces
        coef[h, POS_BLOCK:POS_BLOCK + 3] = [blk * p for p in pieces]
    pos = np.zeros((s, LANES), np.float32)
    idx = np.arange(s)
    pos[:, POS_LOCAL:POS_LOCAL + 3] = (idx % blk)[:, None]
    pos[:, POS_BLOCK:POS_BLOCK + 3] = (idx // blk)[:, None]
    if sel_blocks:
        pos[idx, SEL_LANE + idx // SEL_LEN] = 1.0
    return jnp.asarray(coef), jnp.asarray(pos, BF16)


def _diff_step(i, q_ref, k_ref, v_ref, pos_ref, coef_ref, dl_ref, gm_ref, o_ref, qa_ref, mx_ref, acc_ref,
               *, blk, lam_init):
    q = q_ref[0, _query_rows(i, blk), :]
    row = lax.broadcasted_iota(jnp.int32, (blk, blk), 0)
    col = lax.broadcasted_iota(jnp.int32, (blk, blk), 1)
    causal = col <= row
    lane = lax.broadcasted_iota(jnp.int32, (blk, LANES), 1)
    for h in range(N_HEADS):
        slab = q[:, (h // 2) * LANES:(h // 2 + 1) * LANES]
        coef = jnp.broadcast_to(coef_ref[h:h + 1, :], (blk, LANES)).astype(BF16)
        for c in range(2):
            r = slice(((2 * h + c) % 4) * blk, ((2 * h + c) % 4 + 1) * blk)
            qa_ref[h // 2, r, 0:LANES] = jnp.where((lane // DIFF_QK_DIM) == (h % 2) * 2 + c, slab,
                                                   jnp.zeros_like(slab))
            qa_ref[h // 2, r, LANES:2 * LANES] = coef
    ones = jnp.ones((blk, LANES), BF16)

    def per_map(x, fn):
        return jnp.concatenate([fn(x[k * blk:(k + 1) * blk]) for k in range(4)], axis=0)

    def sweep(blocks, first):
        chains = [(pr, pl.ds(pl.multiple_of(j * blk, blk), blk), diag) for pr in range(2) for j, diag in blocks]
        zs = []
        for pr, rows, diag in chains:
            ka = jnp.concatenate([k_ref[0, rows, pr * LANES:(pr + 1) * LANES], pos_ref[rows, :]], axis=1)
            z = _nt(qa_ref[pr], ka)
            if diag:
                z = per_map(z, lambda t: jnp.where(causal, t, NEG_INF))
            zs.append(z)
        m_new, alpha = [None, None], [None, None]
        for pr in range(2):
            zm = None
            for (cp, _, _), z in zip(chains, zs):
                if cp == pr:
                    for g in range(blk // LANES):
                        part = z[:, g * LANES:(g + 1) * LANES]
                        zm = part if zm is None else jnp.maximum(zm, part)
            cm = jnp.broadcast_to(jnp.max(zm, axis=-1, keepdims=True), (4 * blk, LANES))
            if first:
                m_new[pr] = cm
            else:
                m_old = mx_ref[pr]
                m_new[pr] = jnp.maximum(m_old, cm)
                alpha[pr] = jnp.exp2(m_old - m_new[pr])
            mx_ref[pr] = m_new[pr]
        ps = [jnp.exp2(z - jnp.concatenate([m_new[pr]] * (blk // LANES), axis=1)).astype(BF16)
              for (pr, _, _), z in zip(chains, zs)]
        outs = []
        for (pr, rows, diag), p in zip(chains, ps):
            vo = jnp.concatenate([v_ref[0, rows, pr * LANES:(pr + 1) * LANES], ones], axis=1)
            outs.append(_nn(p, vo))
        for pr in range(2):
            tot = functools.reduce(jnp.add, [o for (cp, _, _), o in zip(chains, outs) if cp == pr])
            acc_ref[pr] = tot if first else acc_ref[pr] * jnp.concatenate([alpha[pr]] * 2, axis=1) + tot

    base = (i // 4) * 4

    @pl.when(i % 2 == 1)
    def _():
        sweep([(i - 1, False), (i, True)], True)

    @pl.when(i % 2 == 0)
    def _():
        sweep([(i, True)], True)

    @pl.when(i % 4 >= 2)
    def _():
        sweep([(base, False), (base + 1, False)], False)

    def body(tt, c):
        sweep([(4 * tt + k, False) for k in range(4)], False)
        return c
    lax.fori_loop(0, i // 4, body, 0)

    dl = dl_ref[...]
    s1 = jnp.sum(dl[0:1] * dl[1:2], axis=-1, keepdims=True)
    s2 = jnp.sum(dl[2:3] * dl[3:4], axis=-1, keepdims=True)
    lam = jnp.exp(s1) - jnp.exp(s2) + lam_init
    outs = []
    for h in range(N_HEADS):
        k1 = (2 * h) % 4
        a1 = acc_ref[h // 2, k1 * blk:(k1 + 1) * blk, :]
        a2 = acc_ref[h // 2, (k1 + 1) * blk:(k1 + 2) * blk, :]
        d = a1[:, :LANES] / a1[:, LANES:] - lam * (a2[:, :LANES] / a2[:, LANES:])
        mine = _half_mask(d.shape, h % 2)
        ms = jnp.sum(jnp.where(mine, d * d, 0.0), axis=-1, keepdims=True) * (1.0 / HEAD_DIM)
        outs.append(d * lax.rsqrt(ms + EPS) * (1.0 - lam_init))
    lo_half = _half_mask((blk, LANES), 0)
    o = jnp.concatenate([jnp.where(lo_half, outs[0], outs[1]), jnp.where(lo_half, outs[2], outs[3])], axis=1)
    o_ref[0, _query_rows(i, blk), :] = (o * gm_ref[...]).astype(o_ref.dtype)


def _differential(proj, dl, gm, lam_init, blk=256):
    b, s, _ = proj.shape
    coef, pos = _position_tables(SLOPES_DIFF, s, blk)
    return pl.pallas_call(
        _over_query_blocks(functools.partial(_diff_step, blk=blk, lam_init=lam_init), blk),
        grid=(b,),
        in_specs=[pl.BlockSpec((1, s, GROUP_WIDTH), lambda bi: (bi, 0, COL_QC)),
                  pl.BlockSpec((1, s, GROUP_WIDTH), lambda bi: (bi, 0, COL_KC)),
                  pl.BlockSpec((1, s, GROUP_WIDTH), lambda bi: (bi, 0, COL_VC)),
                  pl.BlockSpec((s, LANES), lambda bi: (0, 0)),
                  pl.BlockSpec((N_HEADS, LANES), lambda bi: (0, 0)),
                  pl.BlockSpec((4, DIFF_QK_DIM), lambda bi: (0, 0)),
                  pl.BlockSpec((1, GROUP_WIDTH), lambda bi: (0, 0))],
        out_specs=pl.BlockSpec((1, s, GROUP_WIDTH), lambda bi: (bi, 0, 0)),
        out_shape=jax.ShapeDtypeStruct((b, s, GROUP_WIDTH), BF16),
        scratch_shapes=[pltpu.VMEM((2, 4 * blk, 2 * LANES), BF16), pltpu.VMEM((2, 4 * blk, LANES), F32),
                        pltpu.VMEM((2, 4 * blk, 2 * LANES), F32)],
        compiler_params=_cparams("parallel"),
        name="differential",
    )(proj, proj, proj, pos, coef, dl, gm)


def _swa_step(i, q_ref, k_ref, v_ref, sink_ref, gm_ref, o_ref, *, blk, sub):
    row = lax.broadcasted_iota(jnp.int32, (blk, 2 * blk), 0)
    col = lax.broadcasted_iota(jnp.int32, (blk, 2 * blk), 1)
    lo_half = _half_mask((blk, LANES), 0)
    blocks = []
    for s_ in range(sub):
        g = i * sub + s_
        first = jnp.maximum(g - 1, 0)
        start = pl.multiple_of(first * blk, blk)
        dist = (row - col) + (g - first) * blk
        mask = (dist >= 0) & (dist < SWA_WINDOW)
        distf = dist.astype(F32)
        q = q_ref[0, _query_rows(g, blk), :]
        kb = k_ref[0, pl.ds(start, 2 * blk), :]
        vb = v_ref[0, pl.ds(start, 2 * blk), :]
        outs = []
        for n, h in enumerate(SWA_HEAD_ORDER):
            slab = q[:, (n // 2) * LANES:(n // 2 + 1) * LANES]
            qm = jnp.where(_half_mask(slab.shape, n % 2), slab, jnp.zeros_like(slab))
            z = _nt(qm, kb) - (SLOPES_SWA[h] * LOG2E) * distf
            z = jnp.where(mask, z, NEG_INF)
            sink = sink_ref[h] * LOG2E
            m = jnp.maximum(jnp.max(z, axis=-1, keepdims=True), sink)
            p = jnp.exp2(z - m)
            l = jnp.sum(p, axis=-1, keepdims=True) + jnp.exp2(sink - m)
            outs.append(_nn(p.astype(BF16), vb) / l)
        blocks.append(jnp.concatenate([jnp.where(lo_half, outs[0], outs[1]), jnp.where(lo_half, outs[2], outs[3])],
                                      axis=1))
    _group_norm_store(o_ref, _query_rows(i, sub * blk), gm_ref, jnp.concatenate(blocks, axis=0))


def _sliding_window(proj, sinks, gm, blk=128, sub=4):
    b, s, _ = proj.shape
    rows = blk * sub
    return pl.pallas_call(
        _over_query_blocks(functools.partial(_swa_step, blk=blk, sub=sub), rows),
        grid=(b,),
        in_specs=[pl.BlockSpec((1, s, GROUP_WIDTH), lambda bi: (bi, 0, COL_QD)),
                  pl.BlockSpec((1, s, LANES), lambda bi: (bi, 0, COL_KD)),
                  pl.BlockSpec((1, s, LANES), lambda bi: (bi, 0, COL_VD)),
                  pl.BlockSpec(memory_space=pltpu.SMEM),
                  pl.BlockSpec((1, GROUP_WIDTH), lambda bi: (0, 0))],
        out_specs=pl.BlockSpec((1, s, GROUP_WIDTH), lambda bi: (bi, 0, 0)),
        out_shape=jax.ShapeDtypeStruct((b, s, GROUP_WIDTH), BF16),
        compiler_params=_cparams("parallel"),
        name="sliding_window",
    )(proj, proj, proj, sinks, gm)


def _nsa_step(i, q_ref, g_ref, cmp_ref, ks_ref, vs_ref, kw_ref, vw_ref, pos_ref, coef_ref, gm_ref, o_ref,
              qa_ref, mx_ref, acc_ref, *, blk, n_cmp, n_sel):
    q = q_ref[0, _query_rows(i, blk), :]
    lane = lax.broadcasted_iota(jnp.int32, (blk, LANES), 1)
    coefs = [jnp.broadcast_to(coef_ref[h:h + 1, :], (blk, LANES)) for h in range(N_HEADS)]
    for h in range(N_HEADS):
        slab = q[:, (h // 2) * LANES:(h // 2 + 1) * LANES]
        own = jnp.where(_half_mask(slab.shape, h % 2), slab, jnp.zeros_like(slab))
        qa_ref[0, h * blk:(h + 1) * blk, 0:LANES] = own
        qa_ref[1, h * blk:(h + 1) * blk, 0:LANES] = own
        qa_ref[0, h * blk:(h + 1) * blk, LANES:2 * LANES] = coefs[h].astype(BF16)
    qs = qa_ref[1, :, 0:LANES]

    tq = i * blk + lax.broadcasted_iota(jnp.int32, (blk, LANES), 0)

    row = lax.broadcasted_iota(jnp.int32, (blk, blk), 0)
    col = lax.broadcasted_iota(jnp.int32, (blk, blk), 1)
    causal = col <= row
    in_window = col > row
    ones = jnp.ones((blk, LANES), BF16)

    k_refs, v_refs = (kw_ref, ks_ref), (vw_ref, vs_ref)

    def process(blocks, first=False):
        groups = [(br, half) for br in sorted({b[0] for b in blocks}) for half in range(2)]
        chains = [(br, pl.ds(pl.multiple_of(j * blk, blk), blk), mask, (br, half))
                  for br, j, mask in blocks for half in range(2)]
        grows = lambda g: slice(2 * g[1] * blk, 2 * (g[1] + 1) * blk)
        zs = []
        for br, rows, mask, g in chains:
            ka = jnp.concatenate([k_refs[br][0, rows, :], pos_ref[rows, :]], axis=1)
            z = _nt(qa_ref[br, grows(g), :], ka)
            if mask is not None:
                z = jnp.concatenate([jnp.where(mask, z[r * blk:(r + 1) * blk], NEG_INF) for r in range(2)], axis=0)
            zs.append(z)
        m_new, alpha = {}, {}
        for g in groups:
            zm = None
            for (_, _, _, cg), z in zip(chains, zs):
                if cg == g:
                    for c in range(blk // LANES):
                        part = z[:, c * LANES:(c + 1) * LANES]
                        zm = part if zm is None else jnp.maximum(zm, part)
            cm = jnp.broadcast_to(jnp.max(zm, axis=-1, keepdims=True), (2 * blk, LANES))
            if first:
                m_new[g] = cm
            else:
                m_old = mx_ref[g[0], grows(g), :]
                m_new[g] = jnp.maximum(m_old, cm)
                alpha[g] = jnp.exp2(m_old - m_new[g])
            mx_ref[g[0], grows(g), :] = m_new[g]
        ps = [jnp.exp2(z - jnp.concatenate([m_new[g]] * (blk // LANES), axis=1)).astype(BF16)
              for (_, _, _, g), z in zip(chains, zs)]
        outs = []
        for (br, rows, mask, g), p in zip(chains, ps):
            vo = jnp.concatenate([v_refs[br][0, rows, :], ones], axis=1)
            outs.append(_nn(p, vo))
        for g in groups:
            tot = functools.reduce(jnp.add, [o for (_, _, _, cg), o in zip(chains, outs) if cg == g])
            if first:
                acc_ref[g[0], grows(g), :] = tot
            else:
                acc_ref[g[0], grows(g), :] = acc_ref[g[0], grows(g), :] * jnp.concatenate([alpha[g]] * 2, axis=1) + tot

    window = [(0, jnp.maximum(i - 2, 0), in_window & (i >= 2)),
              (0, jnp.maximum(i - 1, 0), jnp.broadcast_to(i >= 1, (blk, blk))),
              (0, i, causal)]

    kc = cmp_ref[0, :, 0:LANES]
    vc = cmp_ref[0, :, LANES:2 * LANES]
    dist_c = tq - (CMP_STRIDE * lane + CMP_LEN - 1)
    ok_c = (dist_c >= 0) & (lane < n_cmp)
    dist_cf = dist_c.astype(F32)
    bias_c = jnp.concatenate([jnp.where(ok_c, (-SLOPES_NSA[h] * LOG2E) * dist_cf, NEG_INF)
                              for h in range(N_HEADS)], axis=0)
    z = _nt(qs, kc) + bias_c
    p = jnp.exp2(z - jnp.max(z, axis=-1, keepdims=True))
    p = p / jnp.sum(p, axis=-1, keepdims=True)
    p = jnp.where(bias_c > 0.5 * NEG_INF, p, 0.0).astype(BF16)
    o_cmp = _nn(p, vc)

    sel_rows = -(-(SEL_LANE + n_sel) // 8) * 8
    rj = lax.broadcasted_iota(jnp.int32, (sel_rows, LANES), 0)
    ri = lax.broadcasted_iota(jnp.int32, (sel_rows, LANES), 1)
    to_sel = jnp.where((ri * CMP_STRIDE) // SEL_LEN + SEL_LANE == rj, 1.0, 0.0).astype(BF16)
    imp = _nt(to_sel, p[0:blk])
    for h in range(1, N_HEADS):
        imp = imp + _nt(to_sel, p[h * blk:(h + 1) * blk])
    srow = lax.broadcasted_iota(jnp.int32, (sel_rows, blk), 0)
    blk_id = srow - SEL_LANE
    cur = (i * blk + lax.broadcasted_iota(jnp.int32, (sel_rows, blk), 1)) // SEL_LEN
    forced = (blk_id == 0) | (blk_id == cur) | (blk_id == cur - 1)
    score = jnp.where(blk_id > cur, NEG_INF, jnp.where(forced, FORCE_SCORE, imp))
    score = jnp.where((blk_id >= 0) & (blk_id < n_sel), score, -3.0e38)
    sel = jnp.zeros((sel_rows, blk), F32)
    srowf = srow.astype(F32)
    for _ in range(min(SEL_TOPN, n_sel)):
        best = jnp.max(score, axis=0, keepdims=True)
        idx = jnp.min(jnp.where(score == best, srowf, float(LANES)), axis=0, keepdims=True)
        hit = srowf == idx
        sel = jnp.where(hit, 1.0, sel)
        score = jnp.where(hit, -3.4e38, score)
    sel_bias = jnp.where(sel > 0.5, 0.0, MASK_BIAS)
    sel_bias = jnp.concatenate([sel_bias, jnp.zeros((LANES - sel_rows, blk), F32)], axis=0).T
    in_sel = (lane >= SEL_LANE) & (lane < SEL_LANE + n_sel)

    process(window, first=True)
    o_win = acc_ref[0, :, 0:LANES] / acc_ref[0, :, LANES:2 * LANES]

    g = jax.nn.sigmoid(g_ref[0, _query_rows(i, blk), :].astype(F32))
    part = [g[:, 3 * h:3 * h + 1] * o_cmp[h * blk:(h + 1) * blk] + g[:, 3 * h + 2:3 * h + 3] * o_win[h * blk:(h + 1) * blk]
            for h in range(N_HEADS)]

    for h in range(N_HEADS):
        qa_ref[1, h * blk:(h + 1) * blk, LANES:2 * LANES] = jnp.where(in_sel, sel_bias, coefs[h]).astype(BF16)
    base = (i // 4) * 4
    process([(1, i, causal)], first=True)

    @pl.when(i % 2 == 1)
    def _():
        process([(1, i - 1, None)])

    @pl.when(i % 4 >= 2)
    def _():
        process([(1, base, None), (1, base + 1, None)])

    def body(tt, c):
        process([(1, 4 * tt + k, None) for k in range(4)])
        return c
    lax.fori_loop(0, i // 4, body, 0)
    o_sel = acc_ref[1, :, 0:LANES] / acc_ref[1, :, LANES:2 * LANES]

    outs = [part[h] + g[:, 3 * h + 1:3 * h + 2] * o_sel[h * blk:(h + 1) * blk] for h in range(N_HEADS)]
    lo_half = _half_mask((blk, LANES), 0)
    o = jnp.concatenate([jnp.where(lo_half, outs[0], outs[1]), jnp.where(lo_half, outs[2], outs[3])], axis=1)
    _group_norm_store(o_ref, _query_rows(i, blk), gm_ref, o)


def _native_sparse(proj, cmp_kv, gm, blk=256):
    b, s, _ = proj.shape
    n_cmp = (s - CMP_LEN) // CMP_STRIDE + 1
    n_sel = s // SEL_LEN
    assert cmp_kv.shape[1] == LANES and n_cmp <= LANES and SEL_LANE + n_sel <= LANES
    assert NSA_WINDOW == 2 * blk
    coef, pos = _position_tables(SLOPES_NSA, s, blk, sel_blocks=True)
    slab = lambda c: pl.BlockSpec((1, s, LANES), lambda bi: (bi, 0, c))
    rows = N_HEADS * blk
    return pl.pallas_call(
        _over_query_blocks(functools.partial(_nsa_step, blk=blk, n_cmp=n_cmp, n_sel=n_sel), blk),
        grid=(b,),
        in_specs=[pl.BlockSpec((1, s, GROUP_WIDTH), lambda bi: (bi, 0, COL_QN)),
                  slab(COL_GN),
                  pl.BlockSpec((1, LANES, 2 * LANES), lambda bi: (bi, 0, 0)),
                  slab(COL_KS), slab(COL_VS), slab(COL_KW), slab(COL_VW),
                  pl.BlockSpec((s, LANES), lambda bi: (0, 0)),
                  pl.BlockSpec((N_HEADS, LANES), lambda bi: (0, 0)),
                  pl.BlockSpec((1, GROUP_WIDTH), lambda bi: (0, 0))],
        out_specs=pl.BlockSpec((1, s, GROUP_WIDTH), lambda bi: (bi, 0, 0)),
        out_shape=jax.ShapeDtypeStruct((b, s, GROUP_WIDTH), BF16),
        scratch_shapes=[pltpu.VMEM((2, rows, 2 * LANES), BF16), pltpu.VMEM((2, rows, LANES), F32),
                        pltpu.VMEM((2, rows, 2 * LANES), F32)],
        compiler_params=_cparams("parallel"),
        name="native_sparse",
    )(proj, proj, cmp_kv, proj, proj, proj, proj, pos, coef, gm)


def _mix_mlp_kernel(x_ref, a_ref, b_ref, c_ref, d_ref, wo_ref, g_ref, wu_ref, wd_ref, gf_ref, o_ref, h_ref,
                    *, final_norm):
    f = pl.program_id(1)

    @pl.when(f == 0)
    def _():
        mix = jnp.concatenate([a_ref[...], b_ref[...], c_ref[...], d_ref[...]], axis=1)
        x1 = x_ref[...] + _nn(mix, wo_ref[...])
        o_ref[...] = x1
        h_ref[...] = (_rms(x1) * g_ref[...]).astype(BF16)

    u = jnp.maximum(_nn(h_ref[...], wu_ref[...]), 0.0)
    o_ref[...] += _nn((u * u).astype(BF16), wd_ref[...])

    if final_norm:
        @pl.when(f == pl.num_programs(1) - 1)
        def _():
            o_ref[...] = _rms(o_ref[...]) * gf_ref[...]


def _mix_mlp(x2, mixes, wo, gain, wu, wd, gfinal, final_norm, tm=1024, tf=1024):
    m = x2.shape[0]
    grp = pl.BlockSpec((tm, GROUP_WIDTH), lambda i, f: (i, 0))
    row = pl.BlockSpec((1, D_MODEL), lambda i, f: (0, 0))
    return pl.pallas_call(
        functools.partial(_mix_mlp_kernel, final_norm=final_norm),
        grid=(m // tm, D_FF // tf),
        in_specs=[pl.BlockSpec((tm, D_MODEL), lambda i, f: (i, 0)), grp, grp, grp, grp,
                  pl.BlockSpec((D_MODEL, D_MODEL), lambda i, f: (0, 0)), row,
                  pl.BlockSpec((D_MODEL, tf), lambda i, f: (0, f)),
                  pl.BlockSpec((tf, D_MODEL), lambda i, f: (f, 0)), row],
        out_specs=pl.BlockSpec((tm, D_MODEL), lambda i, f: (i, 0)),
        out_shape=jax.ShapeDtypeStruct((m, D_MODEL), F32),
        scratch_shapes=[pltpu.VMEM((tm, D_MODEL), BF16)],
        compiler_params=_cparams("parallel", "arbitrary"),
        name="mix_mlp",
    )(x2, *mixes, wo, gain, wu, wd, gfinal)


def _compress_weights(pe_k, w1_k, w2_k, pe_v, w1_v, w2_v):
    half = CMP_STRIDE
    d = HEAD_DIM

    def halves(w1):
        w = w1.reshape(CMP_LEN, d, CMP_HIDDEN)
        return w[:half], w[half:]

    ka, kb = halves(w1_k)
    va, vb = halves(w1_v)
    zero = jnp.zeros_like(ka)

    def merge(wk, wv):
        top = jnp.concatenate([wk, zero], axis=-1)
        bot = jnp.concatenate([zero, wv], axis=-1)
        return jnp.concatenate([top, bot], axis=1).reshape(half * 2 * d, 2 * CMP_HIDDEN).astype(BF16)

    w1a, w1b = merge(ka, va), merge(kb, vb)
    pe = jnp.concatenate([pe_k, pe_v], axis=-1)
    pea = pe[:half].reshape(1, half * 2 * d)
    peb = pe[half:].reshape(1, half * 2 * d)
    zk = jnp.zeros_like(w2_k)
    w2 = jnp.concatenate([jnp.concatenate([w2_k, w2_k, zk, zk], axis=1),
                          jnp.concatenate([zk, zk, w2_v, w2_v], axis=1)], axis=0).astype(BF16)
    return pea, peb, w1a, w1b, w2


def kernel(x, norm_attn, w_in, cmp_pe_k, cmp_w1_k, cmp_w2_k, cmp_pe_v, cmp_w1_v, cmp_w2_v, diff_lq1, diff_lk1,
           diff_lq2, diff_lk2, sinks, g_mix, w_out, norm_mlp, w_up, w_down, norm_final):
    b, s, d = x.shape
    m = b * s
    depth = w_in.shape[0]
    w_in_p = jnp.concatenate([jnp.zeros((depth, d, n), w_in.dtype) if a is None else w_in[:, :, a:a + n]
                              for a, n in _SEGMENTS], axis=2).astype(BF16)
    swa0 = 3 * GROUP_WIDTH
    mix_segs = [(0, swa0)] + [(swa0 + h * HEAD_DIM, HEAD_DIM) for h in SWA_HEAD_ORDER]
    g_mix = jnp.concatenate([g_mix[:, a:a + n] for a, n in mix_segs], axis=1)
    w_out_b = jnp.concatenate([w_out[:, a:a + n] for a, n in mix_segs], axis=1).astype(BF16)
    w_up_b = w_up.astype(BF16)
    w_down_b = w_down.astype(BF16)
    gfinal = norm_final.reshape(1, d)

    colscale = jnp.asarray(_column_scales())

    x2 = x.reshape(m, d)
    for l in range(depth):
        proj = _inproj(x2, norm_attn[l].reshape(1, d), w_in_p[l], colscale).reshape(b, s, NP)
        gm = g_mix[l].reshape(N_HEADS, 1, GROUP_WIDTH)

        chunks = proj[:, :, COL_KVC * LANES:(COL_KVC + 1) * LANES].reshape(b, s // CMP_STRIDE, CMP_STRIDE * LANES)
        cmp_kv = _compress(chunks, *_compress_weights(cmp_pe_k[l], cmp_w1_k[l], cmp_w2_k[l],
                                                      cmp_pe_v[l], cmp_w1_v[l], cmp_w2_v[l]))

        o_sb = _stick_breaking(proj, gm[0])
        o_nsa = _native_sparse(proj, cmp_kv, gm[1])
        lam_init = 0.8 - 0.6 * math.exp(-0.3 * l)
        dl = jnp.stack([diff_lq1[l], diff_lk1[l], diff_lq2[l], diff_lk2[l]]).astype(F32)
        o_diff = _differential(proj, dl, gm[2], lam_init)
        o_swa = _sliding_window(proj, sinks[l].astype(F32), gm[3])

        mixes = [o.reshape(m, GROUP_WIDTH) for o in (o_sb, o_nsa, o_diff, o_swa)]
        x2 = _mix_mlp(x2, mixes, w_out_b[l], norm_mlp[l].reshape(1, d), w_up_b[l], w_down_b[l], gfinal,
                      final_norm=(l == depth - 1))
    return x2.reshape(b, s, d)
```
